```python
import math
import jax
import jax.numpy as jnp
from jax import lax
import numpy as np

D_MODEL = 1024
BATCH = 8
SEQ = 8192
DEPTH = 1

CTX_LEN = 256
GRID_W = 64
EXPAND = 2
W_MIX = EXPAND * D_MODEL
W_MLSTM = W_MIX // 2
W_S5 = W_MIX - W_MLSTM
MLSTM_HEADS = 4
MLSTM_HEAD_DIM = W_MLSTM // MLSTM_HEADS
S5_GROUP = 16
S5_GROUPS = W_S5 // S5_GROUP
S5_STATE = 64
CONV_W = 3
CHUNK = 128
NORM_EPS = 1e-6

Q0 = 0
K0 = Q0 + W_MLSTM
V0 = K0 + W_MLSTM
O0 = V0 + W_MLSTM
ZM0 = O0 + W_MLSTM
G0 = ZM0 + W_MLSTM
U0 = G0 + 4 * MLSTM_HEADS
ZS0 = U0 + W_S5
N_IN = ZS0 + W_S5

kernel_name = "hybrid_mlstm_s5_prefix_block"


def rms_norm(a, g):
    af = a.astype(jnp.float32)
    af = af * lax.rsqrt(jnp.mean(af * af, axis=-1, keepdims=True) + NORM_EPS)
    return (af * g.astype(jnp.float32)).astype(a.dtype)


def ada_params(cvec, w, b):
    m = jax.nn.silu(cvec) @ w + b
    return jnp.split(m, 3, axis=-1)


def dw_conv(a, kern):
    return lax.conv_general_dilated(
        a, kern[:, None, :].astype(a.dtype), window_strides=(1,), padding='SAME',
        dimension_numbers=('NWC', 'WIO', 'NWC'), feature_group_count=a.shape[-1])


def branch_streams(h, w_in, b_gate, conv_qk, full):
    if full:
        z = h @ w_in
        pick = lambda s, e: z[..., s:e]
    else:
        pick = lambda s, e: h @ w_in[:, s:e]
    k = jax.nn.silu(dw_conv(pick(K0, V0), conv_qk[:, W_MLSTM:]))
    v = pick(V0, O0)
    gates = pick(G0, U0) + b_gate.reshape(-1)
    u = pick(U0, ZS0)
    if full:
        q = jax.nn.silu(dw_conv(pick(Q0, K0), conv_qk[:, :W_MLSTM]))
        return q, k, v, gates, u, pick(O0, ZM0), pick(ZM0, G0), pick(ZS0, N_IN)
    return None, k, v, gates, u, None, None, None


def mlstm_chunked(q, k, v, log_i, log_f, state, with_output):
    bsz, t_len, heads, dh = k.shape
    nc = t_len // CHUNK

    def blocks(a):
        a = a.reshape((bsz, nc, CHUNK, heads) + a.shape[3:])
        return jnp.moveaxis(a, (1, 3), (0, 2))

    lower = jnp.tril(jnp.ones((CHUNK, CHUNK), dtype=bool))

    def step(carry, inp):
        c_mat, n_vec, m_run = carry
        qc, kc, vc, li, lf = inp
        b = jnp.cumsum(lf, axis=-1)
        b_last = b[..., -1]
        g = b_last[..., None] - b + li
        m_new = jnp.maximum(b_last + m_run, jnp.max(g, axis=-1))
        decay = jnp.exp(b_last + m_run - m_new)
        kw = kc * jnp.exp(g - m_new[..., None])[..., None]
        c_new = decay[..., None, None] * c_mat + jnp.einsum('bhsd,bhse->bhde', kw, vc)
        n_new = decay[..., None] * n_vec + jnp.sum(kw, axis=2)
        if not with_output:
            return (c_new, n_new, m_new), None
        log_w = jnp.where(lower, b[..., :, None] - b[..., None, :] + li[..., None, :], -jnp.inf)
        inter = b + m_run[..., None]
        m_t = jnp.maximum(inter, jnp.max(log_w, axis=-1))
        w_inter = jnp.exp(inter - m_t)
        s = jnp.einsum('bhtd,bhsd->bhts', qc, kc) * jnp.exp(log_w - m_t[..., None])
        num = (w_inter[..., None] * jnp.einsum('bhtd,bhde->bhte', qc, c_mat)
               + jnp.einsum('bhts,bhse->bhte', s, vc))
        den = w_inter * jnp.einsum('bhtd,bhd->bht', qc, n_vec) + jnp.sum(s, axis=-1)
        h = num / jnp.maximum(jnp.abs(den), jnp.exp(-m_t))[..., None]
        return (c_new, n_new, m_new), h

    xs = (blocks(q) if with_output else None, blocks(k), blocks(v), blocks(log_i), blocks(log_f))
    state, hs = lax.scan(step, state, xs)
    if with_output:
        hs = jnp.moveaxis(hs, (0, 2), (1, 3)).reshape(bsz, t_len, heads, dh)
    return hs, state


def bidir_mlstm(q, k, v, gates, init_f, init_b, with_output):
    bsz, t_len, _ = k.shape
    heads = lambda a: None if a is None else a.astype(jnp.float32).reshape(
        bsz, t_len, MLSTM_HEADS, MLSTM_HEAD_DIM)
    flip = lambda a: None if a is None else jnp.flip(a, axis=1)
    qh, kh, vh = heads(q), heads(k) * MLSTM_HEAD_DIM ** -0.5, heads(v)
    gt = gates.astype(jnp.float32).reshape(bsz, t_len, 4, MLSTM_HEADS)
    li_f, lf_f = gt[:, :, 0], jax.nn.log_sigmoid(gt[:, :, 1])
    li_b, lf_b = gt[:, :, 2], jax.nn.log_sigmoid(gt[:, :, 3])
    h_f, st_f = mlstm_chunked(qh, kh, vh, li_f, lf_f, init_f, with_output)
    h_b, st_b = mlstm_chunked(flip(qh), flip(kh), flip(vh), flip(li_b), flip(lf_b), init_b, with_output)
    h = h_f + flip(h_b) if with_output else None
    return h, st_f, st_b


def s5_discretise(a_re, a_im, log_step, b_re, b_im):
    a_re, a_im = a_re.astype(jnp.float32), a_im.astype(jnp.float32)
    b_re, b_im = b_re.astype(jnp.float32), b_im.astype(jnp.float32)
    dt = jnp.exp(log_step.astype(jnp.float32))[:, None]
    mag = jnp.exp(a_re * dt)
    abar_re, abar_im = mag * jnp.cos(a_im * dt), mag * jnp.sin(a_im * dt)
    nr, ni = abar_re - 1.0, abar_im
    den = a_re * a_re + a_im * a_im
    cr = (nr * a_re + ni * a_im) / den
    ci = (ni * a_re - nr * a_im) / den
    bbar_re = cr[..., None] * b_re - ci[..., None] * b_im
    bbar_im = cr[..., None] * b_im + ci[..., None] * b_re
    return abar_re, abar_im, bbar_re, bbar_im


def _cplx_combine(e1, e2):
    a1r, a1i, b1r, b1i = e1
    a2r, a2i, b2r, b2i = e2
    return (a2r * a1r - a2i * a1i, a2r * a1i + a2i * a1r,
            a2r * b1r - a2i * b1i + b2r, a2r * b1i + a2i * b1r + b2i)


def s5_chunked(u, disc, c_re, c_im, state, with_output):
    abar_re, abar_im, bbar_re, bbar_im = disc
    c_re, c_im = c_re.astype(jnp.float32), c_im.astype(jnp.float32)
    bsz, t_len, groups, gc = u.shape
    nc = t_len // CHUNK
    ub = jnp.moveaxis(u.reshape(bsz, nc, CHUNK, groups, gc), 1, 0)
    shape = (bsz, CHUNK, groups, S5_STATE)
    a_re = jnp.broadcast_to(abar_re, shape)
    a_im = jnp.broadcast_to(abar_im, shape)

    def step(carry, uc):
        h_re0, h_im0 = carry
        bu_re = jnp.einsum('blgc,gpc->blgp', uc, bbar_re)
        bu_im = jnp.einsum('blgc,gpc->blgp', uc, bbar_im)
        acr, aci, bcr, bci = lax.associative_scan(_cplx_combine, (a_re, a_im, bu_re, bu_im), axis=1)
        h_re = acr * h_re0[:, None] - aci * h_im0[:, None] + bcr
        h_im = acr * h_im0[:, None] + aci * h_re0[:, None] + bci
        new = (h_re[:, -1], h_im[:, -1])
        if not with_output:
            return new, None
        y = jnp.einsum('blgp,gcp->blgc', h_re, c_re) - jnp.einsum('blgp,gcp->blgc', h_im, c_im)
        return new, y

    state, ys = lax.scan(step, state, ub)
    if with_output:
        ys = jnp.moveaxis(ys, 0, 1).reshape(bsz, t_len, groups, gc)
    return ys, state


def bidir_s5(u, discs, c_re, c_im, init_f, init_b, with_output):
    y_f, st_f = s5_chunked(u, discs[0], c_re[0], c_im[0], init_f, with_output)
    y_b, st_b = s5_chunked(jnp.flip(u, axis=1), discs[1], c_re[1], c_im[1], init_b, with_output)
    y = y_f + jnp.flip(y_b, axis=1) if with_output else None
    return y, st_f, st_b


def grid_to_colmajor(a):
    bsz, t_len = a.shape[:2]
    rows = t_len // GRID_W
    a = a.reshape((bsz, rows, GRID_W) + a.shape[2:])
    return jnp.swapaxes(a, 1, 2).reshape((bsz, t_len) + a.shape[3:])


def colmajor_to_grid(a):
    bsz, t_len = a.shape[:2]
    rows = t_len // GRID_W
    a = a.reshape((bsz, GRID_W, rows) + a.shape[2:])
    return jnp.swapaxes(a, 1, 2).reshape((bsz, t_len) + a.shape[3:])


def merge_branches(hm, o, zm, ys, u, zs, mh_g, s5_d, glu_w, glu_b, w_out):
    bsz, t_len, _ = o.shape
    hm = hm * jax.nn.sigmoid(o.astype(jnp.float32)).reshape(bsz, t_len, MLSTM_HEADS, MLSTM_HEAD_DIM)
    mu = jnp.mean(hm, axis=-1, keepdims=True)
    var = jnp.mean(jnp.square(hm - mu), axis=-1, keepdims=True)
    hm = ((hm - mu) * lax.rsqrt(var + NORM_EPS)).reshape(bsz, t_len, W_MLSTM) * mh_g.astype(jnp.float32)
    m_out = hm.astype(o.dtype) * jax.nn.silu(zm)
    y = (ys.reshape(bsz, t_len, W_S5) + s5_d.astype(jnp.float32) * u.astype(jnp.float32)).astype(u.dtype)
    g = jax.nn.gelu(y)
    s_out = g * jax.nn.sigmoid(g @ glu_w + glu_b) * jax.nn.silu(zs)
    return jnp.concatenate([m_out, s_out], axis=-1) @ w_out


def setup_inputs(seed: int = 0) -> dict:
    key = jax.random.key(seed)
    ks = jax.random.split(key, 24)
    f32 = jnp.float32
    nrm = lambda k, shape, s: jax.random.normal(k, shape, f32) * s
    x = nrm(ks[0], (BATCH, SEQ, D_MODEL), 1.0)
    c = nrm(ks[1], (BATCH, D_MODEL), 1.0)
    ctx = nrm(ks[2], (BATCH, CTX_LEN, D_MODEL), 1.0)
    c_ctx = nrm(ks[3], (D_MODEL,), 1.0)
    norm_g = 1.0 + nrm(ks[4], (DEPTH, D_MODEL), 0.02)
    ada_w = nrm(ks[5], (DEPTH, D_MODEL, 3 * D_MODEL), 0.02)
    ada_b = nrm(ks[6], (DEPTH, 3 * D_MODEL), 0.02)
    w_in = nrm(ks[7], (DEPTH, D_MODEL, N_IN), D_MODEL ** -0.5)
    w_in = w_in.at[:, :, G0:U0].multiply(0.1)
    fgate_bias = jnp.linspace(3.0, 6.0, MLSTM_HEADS, dtype=f32)
    b_gate = (nrm(ks[8], (DEPTH, 4, MLSTM_HEADS), 0.1)
              + jnp.array([0.0, 1.0, 0.0, 1.0], f32)[:, None] * fgate_bias[None, :])
    conv_qk = nrm(ks[9], (DEPTH, CONV_W, 2 * W_MLSTM), CONV_W ** -0.5)
    mh_g = 1.0 + nrm(ks[10], (DEPTH, W_MLSTM), 0.02)
    s5_a_re = -0.5 + nrm(ks[11], (DEPTH, 2, S5_GROUPS, S5_STATE), 0.01)
    s5_a_im = (math.pi * jnp.arange(S5_STATE, dtype=f32)
               + nrm(ks[12], (DEPTH, 2, S5_GROUPS, S5_STATE), 0.01))
    s5_log_step = jax.random.uniform(ks[13], (DEPTH, 2, S5_GROUPS), f32,
                                     math.log(1e-3), math.log(1e-1))
    s5_b_re = nrm(ks[14], (DEPTH, 2, S5_GROUPS, S5_STATE, S5_GROUP), (2 * S5_GROUP) ** -0.5)
    s5_b_im = nrm(ks[15], (DEPTH, 2, S5_GROUPS, S5_STATE, S5_GROUP), (2 * S5_GROUP) ** -0.5)
    s5_c_re = nrm(ks[16], (DEPTH, 2, S5_GROUPS, S5_GROUP, S5_STATE), 0.5)
    s5_c_im = nrm(ks[17], (DEPTH, 2, S5_GROUPS, S5_GROUP, S5_STATE), 0.5)
    s5_d = nrm(ks[18], (DEPTH, W_S5), 1.0)
    glu_w = nrm(ks[19], (DEPTH, W_S5, W_S5), W_S5 ** -0.5)
    glu_b = nrm(ks[20], (DEPTH, W_S5), 0.02)
    w_out = nrm(ks[21], (DEPTH, W_MIX, D_MODEL), W_MIX ** -0.5)
    final_g = 1.0 + nrm(ks[22], (D_MODEL,), 0.02)
    return {"x": x, "c": c, "ctx": ctx, "c_ctx": c_ctx, "norm_g": norm_g, "ada_w": ada_w,
            "ada_b": ada_b, "w_in": w_in, "b_gate": b_gate, "conv_qk": conv_qk, "mh_g": mh_g,
            "s5_a_re": s5_a_re, "s5_a_im": s5_a_im, "s5_log_step": s5_log_step,
            "s5_b_re": s5_b_re, "s5_b_im": s5_b_im, "s5_c_re": s5_c_re, "s5_c_im": s5_c_im,
            "s5_d": s5_d, "glu_w": glu_w, "glu_b": glu_b, "w_out": w_out, "final_g": final_g}


def reference(x, c, ctx, c_ctx, norm_g, ada_w, ada_b, w_in, b_gate, conv_qk, mh_g,
              s5_a_re, s5_a_im, s5_log_step, s5_b_re, s5_b_im, s5_c_re, s5_c_im,
              s5_d, glu_w, glu_b, w_out, final_g):
    f32 = jnp.float32
    bsz = x.shape[0]
    for l in range(DEPTH):
        last = l == DEPTH - 1
        sh_x, sc_x, gt_x = ada_params(c, ada_w[l], ada_b[l])
        sh_c, sc_c, gt_c = ada_params(c_ctx, ada_w[l], ada_b[l])
        hx = rms_norm(x, norm_g[l]) * (1.0 + sc_x[:, None]) + sh_x[:, None]
        hc = rms_norm(ctx, norm_g[l]) * (1.0 + sc_c) + sh_c
        discs = [s5_discretise(s5_a_re[l, d], s5_a_im[l, d], s5_log_step[l, d],
                               s5_b_re[l, d], s5_b_im[l, d]) for d in range(2)]

        zero_m = (jnp.zeros((bsz, MLSTM_HEADS, MLSTM_HEAD_DIM, MLSTM_HEAD_DIM), f32),
                  jnp.zeros((bsz, MLSTM_HEADS, MLSTM_HEAD_DIM), f32),
                  jnp.zeros((bsz, MLSTM_HEADS), f32))
        zero_s = (jnp.zeros((bsz, S5_GROUPS, S5_STATE), f32),
                  jnp.zeros((bsz, S5_GROUPS, S5_STATE), f32))
        qc_, kc_, vc_, gc_, uc_, oc_, zmc_, zsc_ = branch_streams(hc, w_in[l], b_gate[l], conv_qk[l], not last)
        ctx_len = kc_.shape[1]
        hm_c, mst_f, mst_b = bidir_mlstm(qc_, kc_, vc_, gc_, zero_m, zero_m, not last)
        uc_g = uc_.astype(f32).reshape(bsz, ctx_len, S5_GROUPS, S5_GROUP)
        ys_c, sst_f, sst_b = bidir_s5(uc_g, discs, s5_c_re[l], s5_c_im[l], zero_s, zero_s, not last)

        q_, k_, v_, g_, u_, o_, zm_, zs_ = branch_streams(hx, w_in[l], b_gate[l], conv_qk[l], True)
        t_len = k_.shape[1]
        hm_x, _, _ = bidir_mlstm(q_, k_, v_, g_, mst_f, mst_b, True)
        u_g = grid_to_colmajor(u_.astype(f32).reshape(bsz, t_len, S5_GROUPS, S5_GROUP))
        ys_x, _, _ = bidir_s5(u_g, discs, s5_c_re[l], s5_c_im[l], sst_f, sst_b, True)
        ys_x = colmajor_to_grid(ys_x)
        out_x = merge_branches(hm_x, o_, zm_, ys_x, u_, zs_, mh_g[l], s5_d[l], glu_w[l], glu_b[l], w_out[l])
        if not last:
            out_c = merge_branches(hm_c, oc_, zmc_, ys_c, uc_, zsc_, mh_g[l], s5_d[l],
                                   glu_w[l], glu_b[l], w_out[l])
            ctx = ctx + gt_c * out_c
        x = x + gt_x[:, None] * out_x
    return rms_norm(x, final_g)
```

```python
import functools

import jax
import jax.numpy as jnp
from jax import lax
from jax.experimental import pallas as pl
from jax.experimental.pallas import tpu as pltpu

F32 = jnp.float32
BF16 = jnp.bfloat16

D_MODEL = 1024
HEADS = 4
HEAD_DIM = 256
W_BRANCH = 1024
S5_GROUPS = 64
S5_GC = 16
S5_STATE = 64
S5_SUB = 16
GRID_W = 64
N_GATES = 16
GATE_PAD = 128
CONV_W = 3
NORM_EPS = 1e-6
MLSTM_CHUNK = 128
NEG_BIG = -1e30

VMEM_LIMIT = 56 * 1024 * 1024


def _silu(a):
    return a * jax.nn.sigmoid(a)


def _log_sigmoid(a):
    return jnp.minimum(a, 0.0) - jnp.log1p(jnp.exp(-jnp.abs(a)))


def _dot(a, b):
    return jnp.dot(a, b, preferred_element_type=F32)


def _split_bf16(a):
    hi = a.astype(BF16)
    lo = (a - hi.astype(F32)).astype(BF16)
    return hi, lo


def _ada_kernel(c_ref, w_ref, b_ref, o_ref):
    s = _silu(c_ref[...])
    o_ref[...] = jnp.dot(s, w_ref[...], preferred_element_type=F32,
                         precision=lax.Precision.HIGHEST) + b_ref[...]


def _ada(cc, ada_w, ada_b):
    rows = cc.shape[0]
    n_out = ada_w.shape[1]
    tn = 1024
    return pl.pallas_call(
        _ada_kernel,
        grid=(n_out // tn,),
        in_specs=[pl.BlockSpec((rows, D_MODEL), lambda j: (0, 0)),
                  pl.BlockSpec((D_MODEL, tn), lambda j: (0, j)),
                  pl.BlockSpec((1, tn), lambda j: (0, j))],
        out_specs=pl.BlockSpec((rows, tn), lambda j: (0, j)),
        out_shape=jax.ShapeDtypeStruct((rows, n_out), F32),
        compiler_params=pltpu.CompilerParams(dimension_semantics=("arbitrary",),
                                             vmem_limit_bytes=VMEM_LIMIT),
        name="ada",
    )(cc, ada_w, ada_b.reshape(1, n_out))


HALO = 16


def _inproj_kernel(x_ref, xp_ref, xn_ref, mod_ref, ng_ref, wqk_ref, wm_ref, conv_ref, bg_ref,
                   q_ref, k_ref, v_ref, o_ref, zm_ref, u_ref, zs_ref, g_ref, gt_ref,
                   lhs_sc, z_sc, *, tm):
    i = pl.program_id(1)
    nt = pl.num_programs(1)
    shift = mod_ref[0:1, :]
    scale = mod_ref[1:2, :]
    ng = ng_ref[...]

    def norm_mod(a):
        ms = jnp.mean(a * a, axis=-1, keepdims=True)
        return (a * lax.rsqrt(ms + NORM_EPS) * ng) * (1.0 + scale) + shift

    lhs_sc[0:tm, :] = norm_mod(x_ref[...]).astype(BF16)
    lhs_sc[tm:tm + HALO, :] = norm_mod(xp_ref[...]).astype(BF16)
    lhs_sc[tm + HALO:tm + 2 * HALO, :] = norm_mod(xn_ref[...]).astype(BF16)

    valid_prev = (i > 0).astype(F32)
    valid_next = (i < nt - 1).astype(F32)
    for half, out_ref, out_scale in ((0, q_ref, 1.0), (1, k_ref, HEAD_DIM ** -0.5)):
        cols = slice(half * W_BRANCH, (half + 1) * W_BRANCH)
        z = _dot(lhs_sc[...], wqk_ref[:, cols])
        z_sc[8:8 + tm, :] = z[0:tm]
        z_sc[0:8, :] = z[tm + HALO - 8:tm + HALO] * valid_prev
        z_sc[8 + tm:16 + tm, :] = z[tm + HALO:tm + HALO + 8] * valid_next
        cw = conv_ref[:, cols]
        conv = (cw[0:1] * z_sc[7:7 + tm, :] + cw[1:2] * z_sc[8:8 + tm, :]
                + cw[2:3] * z_sc[9:9 + tm, :])
        out_ref[...] = (_silu(conv) * out_scale).astype(BF16)

    h = lhs_sc[0:tm, :]
    v_ref[...] = _dot(h, wm_ref[:, 0:1024]).astype(BF16)
    o_ref[...] = _dot(h, wm_ref[:, 1024:2048])
    zm_ref[...] = _dot(h, wm_ref[:, 2048:3072])
    u_ref[...] = _dot(h, wm_ref[:, 3072:4096]).astype(BF16)
    zs_ref[...] = _dot(h, wm_ref[:, 4096:5120])
    zg = _dot(h, wm_ref[:, 5120:5120 + GATE_PAD]) + bg_ref[...]
    lane = lax.broadcasted_iota(jnp.int32, zg.shape, 1)
    is_forget = jnp.logical_and(lane < N_GATES, (lane % 8) >= 4)
    gates = jnp.where(is_forget, _log_sigmoid(zg), zg)
    g_ref[...] = gates[:, 0:N_GATES]
    gt_ref[...] = gates.T[0:N_GATES, :]


def _inproj(x, mod, norm_g, w_qk, w_main, conv_qk, b_gate_pad, tm):
    bsz, t_len, _ = x.shape
    nt = t_len // tm
    nhb = t_len // HALO
    tok = lambda w, dt: jax.ShapeDtypeStruct((bsz, t_len, w), dt)
    tile = lambda w: pl.BlockSpec((None, tm, w), lambda b, i: (b, i, 0))
    const = lambda shape: pl.BlockSpec(shape, lambda b, i: (0,) * len(shape))
    return pl.pallas_call(
        functools.partial(_inproj_kernel, tm=tm),
        grid=(bsz, nt),
        in_specs=[
            tile(D_MODEL),
            pl.BlockSpec((None, HALO, D_MODEL),
                         lambda b, i: (b, jnp.maximum(i * (tm // HALO) - 1, 0), 0)),
            pl.BlockSpec((None, HALO, D_MODEL),
                         lambda b, i: (b, jnp.minimum((i + 1) * (tm // HALO), nhb - 1), 0)),
            pl.BlockSpec((None, 3, D_MODEL), lambda b, i: (b, 0, 0)),
            const((1, D_MODEL)),
            const((D_MODEL, 2 * W_BRANCH)),
            const((D_MODEL, 5 * W_BRANCH + GATE_PAD)),
            const((CONV_W, 2 * W_BRANCH)),
            const((1, GATE_PAD)),
        ],
        out_specs=[tile(W_BRANCH)] * 7 + [
            tile(N_GATES),
            pl.BlockSpec((None, N_GATES, tm), lambda b, i: (b, 0, i)),
        ],
        out_shape=[tok(W_BRANCH, BF16), tok(W_BRANCH, BF16), tok(W_BRANCH, BF16),
                   tok(W_BRANCH, F32), tok(W_BRANCH, F32), tok(W_BRANCH, BF16),
                   tok(W_BRANCH, F32), tok(N_GATES, F32),
                   jax.ShapeDtypeStruct((bsz, N_GATES, t_len), F32)],
        scratch_shapes=[pltpu.VMEM((tm + 2 * HALO, D_MODEL), BF16),
                        pltpu.VMEM((tm + 16, W_BRANCH), F32)],
        compiler_params=pltpu.CompilerParams(dimension_semantics=("parallel", "arbitrary"),
                                             vmem_limit_bytes=VMEM_LIMIT),
        name="inproj",
    )(x, x, x, mod, norm_g.reshape(1, D_MODEL), w_qk, w_main, conv_qk, b_gate_pad)


def _mlstm_kernel(*refs, reverse, with_output, chunk):
    if with_output:
        (q_ref, k_ref, v_ref, g_ref, gt_ref, c0_ref, n0_ref, m0_ref,
         h_ref, co_ref, no_ref, mo_ref, c_sc, n_sc, m_sc) = refs
    else:
        (k_ref, v_ref, g_ref, gt_ref, c0_ref, n0_ref, m0_ref,
         co_ref, no_ref, mo_ref, c_sc, n_sc, m_sc) = refs
    i = pl.program_id(1)
    nc = pl.num_programs(1)

    @pl.when(i == 0)
    def _():
        c_sc[...] = c0_ref[...]
        n_sc[...] = n0_ref[...]
        m_sc[...] = m0_ref[...]

    row = lax.broadcasted_iota(jnp.int32, (chunk, chunk), 0)
    col = lax.broadcasted_iota(jnp.int32, (chunk, chunk), 1)
    if reverse:
        mask_ts, mask_st = col >= row, row >= col
    else:
        mask_ts, mask_st = col <= row, row <= col
    tri_ts = jnp.where(mask_ts, 1.0, 0.0).astype(BF16)
    tri_st = jnp.where(mask_st, 1.0, 0.0).astype(BF16)

    g = g_ref[...]
    gt = gt_ref[...]
    g_hi, g_lo = _split_bf16(g)
    gt_hi, gt_lo = _split_bf16(gt)
    cum_col = _dot(tri_ts, g_hi) + _dot(tri_ts, g_lo)
    cum_row = _dot(gt_hi, tri_st) + _dot(gt_lo, tri_st)
    li0 = 8 if reverse else 0
    lf0 = li0 + 4
    last = 0 if reverse else chunk - 1

    for hd in range(HEADS):
        sl = slice(hd * HEAD_DIM, (hd + 1) * HEAD_DIM)
        b_c = cum_col[:, lf0 + hd:lf0 + hd + 1]
        li_c = g[:, li0 + hd:li0 + hd + 1]
        b_r = cum_row[lf0 + hd:lf0 + hd + 1, :]
        li_r = gt[li0 + hd:li0 + hd + 1, :]
        b_last = b_r[:, last:last + 1]
        m_old = m_sc[hd:hd + 1, 0:1]
        k = k_ref[:, sl]
        v = v_ref[:, sl]
        c_old = c_sc[hd]
        n_old = n_sc[hd:hd + 1, :]

        if with_output:
            q = q_ref[:, sl]
            log_w = jnp.where(mask_ts, b_c - b_r + li_r, NEG_BIG)
            inter = b_c + m_old
            m_t = jnp.maximum(inter, jnp.max(log_w, axis=1, keepdims=True))
            w_inter = jnp.exp(inter - m_t)
            qk = lax.dot_general(q, k, (((1,), (1,)), ((), ())), preferred_element_type=F32)
            s = qk * jnp.exp(log_w - m_t)
            num = w_inter * _dot(q, c_old.astype(BF16)) + _dot(s.astype(BF16), v)
            den = (w_inter * jnp.sum(q.astype(F32) * n_old, axis=1, keepdims=True)
                   + jnp.sum(s, axis=1, keepdims=True))
            h_ref[:, sl] = num / jnp.maximum(jnp.abs(den), jnp.exp(-m_t))

        g_c = b_last - b_c + li_c
        g_r = b_last - b_r + li_r
        m_new = jnp.maximum(b_last + m_old, jnp.max(g_r, axis=1, keepdims=True))
        decay = jnp.exp(b_last + m_old - m_new)
        kw = k.astype(F32) * jnp.exp(g_c - m_new)
        c_sc[hd] = decay * c_old + lax.dot_general(
            kw.astype(BF16), v, (((0,), (0,)), ((), ())), preferred_element_type=F32)
        n_sc[hd:hd + 1, :] = decay * n_old + jnp.sum(kw, axis=0, keepdims=True)
        m_sc[hd:hd + 1, :] = jnp.broadcast_to(m_new, (1, 128))

    @pl.when(i == nc - 1)
    def _():
        co_ref[...] = c_sc[...]
        no_ref[...] = n_sc[...]
        mo_ref[...] = m_sc[...]


def _mlstm(q, k, v, g, gt, state, reverse, with_output):
    bsz, t_len, _ = k.shape
    chunk = MLSTM_CHUNK
    nc = t_len // chunk
    cidx = (lambda i: nc - 1 - i) if reverse else (lambda i: i)
    tile = lambda w: pl.BlockSpec((None, chunk, w), lambda b, i: (b, cidx(i), 0))
    st_specs = [pl.BlockSpec((None, HEADS, HEAD_DIM, HEAD_DIM), lambda b, i: (b, 0, 0, 0)),
                pl.BlockSpec((None, HEADS, HEAD_DIM), lambda b, i: (b, 0, 0)),
                pl.BlockSpec((None, HEADS, 128), lambda b, i: (b, 0, 0))]
    st_shapes = [jax.ShapeDtypeStruct((bsz, HEADS, HEAD_DIM, HEAD_DIM), F32),
                 jax.ShapeDtypeStruct((bsz, HEADS, HEAD_DIM), F32),
                 jax.ShapeDtypeStruct((bsz, HEADS, 128), F32)]
    in_specs = [tile(W_BRANCH), tile(W_BRANCH), tile(N_GATES),
                pl.BlockSpec((None, N_GATES, chunk), lambda b, i: (b, 0, cidx(i)))] + st_specs
    args = [k, v, g, gt] + list(state)
    out_specs, out_shape = list(st_specs), list(st_shapes)
    if with_output:
        in_specs = [tile(W_BRANCH)] + in_specs
        args = [q] + args
        out_specs = [tile(W_BRANCH)] + out_specs
        out_shape = [jax.ShapeDtypeStruct((bsz, t_len, W_BRANCH), F32)] + out_shape
    outs = pl.pallas_call(
        functools.partial(_mlstm_kernel, reverse=reverse, with_output=with_output, chunk=chunk),
        grid=(bsz, nc),
        in_specs=in_specs,
        out_specs=out_specs,
        out_shape=out_shape,
        scratch_shapes=[pltpu.VMEM((HEADS, HEAD_DIM, HEAD_DIM), F32),
                        pltpu.VMEM((HEADS, HEAD_DIM), F32),
                        pltpu.VMEM((HEADS, 128), F32)],
        compiler_params=pltpu.CompilerParams(dimension_semantics=("parallel", "arbitrary"),
                                             vmem_limit_bytes=VMEM_LIMIT),
        name="mlstm_" + ("bwd" if reverse else "fwd") + ("_out" if with_output else "_state"),
    )(*args)
    if with_output:
        return outs[0], tuple(outs[1:])
    return None, tuple(outs)


def _s5_kernel(v_ref, g_ref, m_ref, p_ref, a_ref, y_ref, gu_sc, s_sc, *, nk_ctx, nk, bsz, rblk):
    rows = nk * bsz
    rows_ctx = nk_ctx * bsz
    for r0 in range(0, rows, rblk):
        rs = slice(r0, min(r0 + rblk, rows))
        gu_sc[rs, :] = _dot(v_ref[0, rs, :], g_ref[0]) + _dot(v_ref[1, rs, :], g_ref[1])

    a = a_ref[...]
    a_fr = jnp.broadcast_to(a[:, 0:128], (bsz, 128))
    a_fi = jnp.broadcast_to(a[:, 128:256], (bsz, 128))
    a_br = jnp.broadcast_to(a[:, 256:384], (bsz, 128))
    a_bi = jnp.broadcast_to(a[:, 384:512], (bsz, 128))
    zero = jnp.zeros((bsz, 128), F32)

    def step(k, carry, c_re, c_im, a_r, a_i):
        s_r, s_i = carry
        r0 = pl.multiple_of(k * bsz, bsz)
        inc_r = gu_sc[pl.ds(r0, bsz), c_re:c_re + 128]
        inc_i = gu_sc[pl.ds(r0, bsz), c_im:c_im + 128]
        gu_sc[pl.ds(r0, bsz), c_re:c_re + 128] = s_r
        gu_sc[pl.ds(r0, bsz), c_im:c_im + 128] = s_i
        return (a_r * s_r - a_i * s_i + inc_r, a_r * s_i + a_i * s_r + inc_i)

    fwd = functools.partial(step, c_re=0, c_im=128, a_r=a_fr, a_i=a_fi)
    bwd = functools.partial(step, c_re=256, c_im=384, a_r=a_br, a_i=a_bi)
    lax.fori_loop(0, nk, fwd, (zero, zero))
    carry = lax.fori_loop(0, nk_ctx, lambda j, c: bwd(nk_ctx - 1 - j, c), (zero, zero))
    lax.fori_loop(0, nk - nk_ctx, lambda j, c: bwd(nk - 1 - j, c), carry)

    for r0 in range(rows_ctx, rows, rblk):
        rs = slice(r0, min(r0 + rblk, rows))
        s_sc[rs, :] = gu_sc[rs, :].astype(BF16)
        ys = slice(r0 - rows_ctx, min(r0 + rblk, rows) - rows_ctx)
        for gg in range(2):
            y_ref[gg, ys, :] = _dot(v_ref[gg, rs, :], m_ref[gg]) + _dot(s_sc[rs, :], p_ref[gg])


def _s5(v_all, g_all, m_all, p_all, a16, nk_ctx, nk, bsz):
    rows = nk * bsz
    rows_x = (nk - nk_ctx) * bsz
    lanes = S5_SUB * S5_GC
    return pl.pallas_call(
        functools.partial(_s5_kernel, nk_ctx=nk_ctx, nk=nk, bsz=bsz, rblk=512),
        grid=(S5_GROUPS // 2,),
        in_specs=[pl.BlockSpec((2, rows, lanes), lambda j: (j, 0, 0)),
                  pl.BlockSpec((2, lanes, 512), lambda j: (j, 0, 0)),
                  pl.BlockSpec((2, lanes, lanes), lambda j: (j, 0, 0)),
                  pl.BlockSpec((2, 512, lanes), lambda j: (j, 0, 0)),
                  pl.BlockSpec((None, 1, 512), lambda j: (j, 0, 0))],
        out_specs=pl.BlockSpec((2, rows_x, lanes), lambda j: (j, 0, 0)),
        out_shape=jax.ShapeDtypeStruct((S5_GROUPS, rows_x, lanes), F32),
        scratch_shapes=[pltpu.VMEM((rows, 512), F32), pltpu.VMEM((rows, 512), BF16)],
        compiler_params=pltpu.CompilerParams(dimension_semantics=("parallel",),
                                             vmem_limit_bytes=VMEM_LIMIT),
        name="s5",
    )(v_all, g_all, m_all, p_all, a16)


def _s5_params(a_re, a_im, log_step, b_re, b_im, c_re, c_im, d_skip):
    hp = lax.Precision.HIGHEST
    n_g, n_p, n_c, n_s = S5_GROUPS, S5_STATE, S5_GC, S5_SUB
    a_re, a_im = a_re.astype(F32), a_im.astype(F32)
    dt = jnp.exp(log_step.astype(F32))[..., None]
    lam_re, lam_im = a_re * dt, a_im * dt

    def a_pow(n):
        mag = jnp.exp(lam_re * n)
        return mag * jnp.cos(lam_im * n), mag * jnp.sin(lam_im * n)

    ab_re, ab_im = a_pow(1.0)
    nr, ni = ab_re - 1.0, ab_im
    den = a_re * a_re + a_im * a_im
    cr = (nr * a_re + ni * a_im) / den
    ci = (ni * a_re - nr * a_im) / den
    bb_re = cr[..., None] * b_re - ci[..., None] * b_im
    bb_im = cr[..., None] * b_im + ci[..., None] * b_re

    steps = jnp.arange(n_s + 1, dtype=F32)[:, None, None, None]
    pw_re, pw_im = a_pow(steps)
    ca_re = c_re[None] * pw_re[:, :, :, None, :] - c_im[None] * pw_im[:, :, :, None, :]
    ca_im = c_re[None] * pw_im[:, :, :, None, :] + c_im[None] * pw_re[:, :, :, None, :]
    kern = (jnp.einsum('ndgop,dgpc->ndgoc', ca_re[:n_s], bb_re, precision=hp)
            - jnp.einsum('ndgop,dgpc->ndgoc', ca_im[:n_s], bb_im, precision=hp))
    idx = jnp.arange(n_s)
    delta = idx[None, :] - idx[:, None]
    kf = kern[jnp.clip(delta, 0, n_s - 1), 0]
    kb = kern[jnp.clip(-delta, 0, n_s - 1), 1]
    blk = (jnp.where((delta >= 0)[:, :, None, None, None], kf, 0.0)
           + jnp.where((delta <= 0)[:, :, None, None, None], kb, 0.0))
    m_all = jnp.transpose(blk, (2, 0, 4, 1, 3)).reshape(n_g, n_s * n_c, n_s * n_c)
    m_all = m_all + jnp.eye(n_s * n_c, dtype=F32)[None] * jnp.tile(
        d_skip.astype(F32).reshape(n_g, 1, n_c), (1, n_s, 1)).reshape(n_g, 1, n_s * n_c)

    def cmul(xr, xi, yr, yi):
        return xr * yr - xi * yi, xr * yi + xi * yr

    gf_re, gf_im = cmul(pw_re[n_s - 1 - idx, 0][..., None], pw_im[n_s - 1 - idx, 0][..., None],
                        bb_re[0][None], bb_im[0][None])
    gb_re, gb_im = cmul(pw_re[idx, 1][..., None], pw_im[idx, 1][..., None],
                        bb_re[1][None], bb_im[1][None])
    g4 = jnp.stack([gf_re, gf_im, gb_re, gb_im], axis=0)
    g4 = jnp.transpose(g4, (2, 1, 4, 0, 3)).reshape(n_g, n_s * n_c, 4, n_p)
    parity = (jnp.arange(n_g) % 2)[:, None, None, None, None]
    half = jnp.arange(2)[None, None, None, :, None]
    g_all = jnp.where(parity == half, g4[:, :, :, None, :], 0.0).reshape(n_g, n_s * n_c, 8 * n_p)

    p4 = jnp.stack([ca_re[idx + 1, 0], -ca_im[idx + 1, 0],
                    ca_re[n_s - idx, 1], -ca_im[n_s - idx, 1]], axis=0)
    p4 = jnp.transpose(p4, (2, 0, 4, 1, 3)).reshape(n_g, 4, n_p, n_s * n_c)
    parity = (jnp.arange(n_g) % 2)[:, None, None, None, None]
    half = jnp.arange(2)[None, None, :, None, None]
    p_all = jnp.where(parity == half, p4[:, :, None, :, :], 0.0).reshape(n_g, 8 * n_p, n_s * n_c)

    a16 = jnp.stack([pw_re[n_s, 0], pw_im[n_s, 0], pw_re[n_s, 1], pw_im[n_s, 1]], axis=0)
    a16 = jnp.transpose(a16.reshape(4, n_g // 2, 2 * n_p), (1, 0, 2)).reshape(n_g // 2, 1, 8 * n_p)
    return m_all.astype(BF16), g_all.astype(BF16), p_all.astype(BF16), a16


def _merge_kernel(hf_ref, hb_ref, o_ref, zm_ref, y_ref, zs_ref, x_ref, mod_ref, mhg_ref,
                  gluw_ref, glub_ref, wout_ref, fg_ref, out_ref):
    hm = (hf_ref[...] + hb_ref[...]) * jax.nn.sigmoid(o_ref[...])
    mhg = mhg_ref[...]
    parts = []
    for hd in range(HEADS):
        sl = slice(hd * HEAD_DIM, (hd + 1) * HEAD_DIM)
        seg = hm[:, sl]
        mu = jnp.mean(seg, axis=-1, keepdims=True)
        dev = seg - mu
        var = jnp.mean(dev * dev, axis=-1, keepdims=True)
        parts.append(dev * lax.rsqrt(var + NORM_EPS) * mhg[:, sl])
    m_out = jnp.concatenate(parts, axis=-1) * _silu(zm_ref[...])

    y = y_ref[...]
    gl = 0.5 * y * (1.0 + jnp.tanh(0.7978845608028654 * (y + 0.044715 * (y * y * y))))
    gate = jax.nn.sigmoid(_dot(gl.astype(BF16), gluw_ref[...]) + glub_ref[...])
    s_out = gl * gate * _silu(zs_ref[...])

    mixed = (_dot(m_out.astype(BF16), wout_ref[0:W_BRANCH, :])
             + _dot(s_out.astype(BF16), wout_ref[W_BRANCH:2 * W_BRANCH, :]))
    xo = x_ref[...] + mod_ref[2:3, :] * mixed
    ms = jnp.mean(xo * xo, axis=-1, keepdims=True)
    out_ref[...] = xo * lax.rsqrt(ms + NORM_EPS) * fg_ref[...]


def _merge(hf, hb, o, zm, y, zs, x, mod, mh_g, glu_w, glu_b, w_out, final_g, tm):
    bsz, t_len, _ = x.shape
    tile = pl.BlockSpec((None, tm, D_MODEL), lambda b, i: (b, i, 0))
    const = lambda shape: pl.BlockSpec(shape, lambda b, i: (0,) * len(shape))
    return pl.pallas_call(
        _merge_kernel,
        grid=(bsz, t_len // tm),
        in_specs=[tile] * 7 + [pl.BlockSpec((None, 3, D_MODEL), lambda b, i: (b, 0, 0)),
                               const((1, W_BRANCH)), const((W_BRANCH, W_BRANCH)),
                               const((1, W_BRANCH)), const((2 * W_BRANCH, D_MODEL)),
                               const((1, D_MODEL))],
        out_specs=tile,
        out_shape=jax.ShapeDtypeStruct((bsz, t_len, D_MODEL), F32),
        compiler_params=pltpu.CompilerParams(dimension_semantics=("parallel", "arbitrary"),
                                             vmem_limit_bytes=VMEM_LIMIT),
        name="merge",
    )(hf, hb, o, zm, y, zs, x, mod, mh_g.reshape(1, -1), glu_w, glu_b.reshape(1, -1), w_out,
      final_g.reshape(1, -1))


def _s5_rows_x(u):
    bsz, t_len, _ = u.shape
    rows = t_len // GRID_W
    a = u.reshape(bsz, rows // S5_SUB, S5_SUB, GRID_W, S5_GROUPS, S5_GC)
    a = jnp.transpose(a, (4, 3, 1, 0, 2, 5))
    return a.reshape(S5_GROUPS, GRID_W * (rows // S5_SUB) * bsz, S5_SUB * S5_GC)


def _s5_rows_ctx(u):
    bsz, t_len, _ = u.shape
    a = u.reshape(bsz, t_len // S5_SUB, S5_SUB, S5_GROUPS, S5_GC)
    a = jnp.transpose(a, (3, 1, 0, 2, 4))
    return a.reshape(S5_GROUPS, (t_len // S5_SUB) * bsz, S5_SUB * S5_GC)


def _s5_tokens_x(y, bsz, t_len):
    rows = t_len // GRID_W
    a = y.reshape(S5_GROUPS, GRID_W, rows // S5_SUB, bsz, S5_SUB, S5_GC)
    a = jnp.transpose(a, (3, 2, 4, 1, 0, 5))
    return a.reshape(bsz, t_len, W_BRANCH)


def kernel(x, c, ctx, c_ctx, norm_g, ada_w, ada_b, w_in, b_gate, conv_qk, mh_g, s5_a_re, s5_a_im,
           s5_log_step, s5_b_re, s5_b_im, s5_c_re, s5_c_im, s5_d, glu_w, glu_b, w_out, final_g):
    bsz, t_len, _ = x.shape
    ctx_len = ctx.shape[1]
    layer = 0

    cc = jnp.zeros((16, D_MODEL), F32).at[:bsz].set(c).at[bsz].set(c_ctx)
    mod = _ada(cc, ada_w[layer], ada_b[layer]).reshape(16, 3, D_MODEL)
    mod_x = mod[:bsz]
    mod_c = jnp.broadcast_to(mod[bsz][None], (bsz, 3, D_MODEL))

    w = w_in[layer]
    wb = W_BRANCH
    w_qk = w[:, 0:2 * wb].astype(BF16)
    gate0 = 5 * wb
    w_gate = jnp.pad(w[:, gate0:gate0 + N_GATES], ((0, 0), (0, GATE_PAD - N_GATES)))
    w_main = jnp.concatenate([w[:, 2 * wb:5 * wb], w[:, gate0 + N_GATES:], w_gate], axis=1).astype(BF16)
    b_gate_pad = jnp.pad(b_gate[layer].reshape(1, N_GATES), ((0, 0), (0, GATE_PAD - N_GATES)))

    proj = functools.partial(_inproj, norm_g=norm_g[layer], w_qk=w_qk, w_main=w_main,
                             conv_qk=conv_qk[layer], b_gate_pad=b_gate_pad)
    _, k_c, v_c, _, _, u_c, _, g_c, gt_c = proj(ctx, mod_c, tm=256)
    q_x, k_x, v_x, o_x, zm_x, u_x, zs_x, g_x, gt_x = proj(x, mod_x, tm=256)

    zero_state = (jnp.zeros((bsz, HEADS, HEAD_DIM, HEAD_DIM), F32),
                  jnp.zeros((bsz, HEADS, HEAD_DIM), F32),
                  jnp.zeros((bsz, HEADS, 128), F32))
    _, st_f = _mlstm(None, k_c, v_c, g_c, gt_c, zero_state, reverse=False, with_output=False)
    _, st_b = _mlstm(None, k_c, v_c, g_c, gt_c, zero_state, reverse=True, with_output=False)
    h_f, _ = _mlstm(q_x, k_x, v_x, g_x, gt_x, st_f, reverse=False, with_output=True)
    h_b, _ = _mlstm(q_x, k_x, v_x, g_x, gt_x, st_b, reverse=True, with_output=True)

    m_all, g_all, p_all, a16 = _s5_params(
        s5_a_re[layer], s5_a_im[layer], s5_log_step[layer], s5_b_re[layer], s5_b_im[layer],
        s5_c_re[layer], s5_c_im[layer], s5_d[layer])
    v_all = jnp.concatenate([_s5_rows_ctx(u_c), _s5_rows_x(u_x)], axis=1)
    nk_ctx = ctx_len // S5_SUB
    nk = nk_ctx + t_len // S5_SUB
    y_rows = _s5(v_all, g_all, m_all, p_all, a16, nk_ctx, nk, bsz)
    y_x = _s5_tokens_x(y_rows, bsz, t_len)

    return _merge(h_f, h_b, o_x, zm_x, y_x, zs_x, x, mod_x, mh_g[layer], glu_w[layer].astype(BF16),
                  glu_b[layer], w_out[layer].astype(BF16), final_g, tm=256)
```

```python
import functools

import jax
import jax.numpy as jnp
from jax import lax
from jax.experimental import pallas as pl
from jax.experimental.pallas import tpu as pltpu

F32 = jnp.float32
BF16 = jnp.bfloat16

D_MODEL = 1024
HEADS = 4
HEAD_DIM = 256
W_BRANCH = 1024
S5_GROUPS = 64
S5_GC = 16
S5_STATE = 64
S5_SUB = 16
GRID_W = 64
N_GATES = 16
GATE_PAD = 128
CONV_W = 3
NORM_EPS = 1e-6
MLSTM_CHUNK = 128
NEG_BIG = -1e30

VMEM_LIMIT = 56 * 1024 * 1024


def _silu(a):
    return a * jax.nn.sigmoid(a)


def _log_sigmoid(a):
    return jnp.minimum(a, 0.0) - jnp.log1p(jnp.exp(-jnp.abs(a)))


def _dot(a, b):
    return jnp.dot(a, b, preferred_element_type=F32)


def _split_bf16(a):
    hi = a.astype(BF16)
    lo = (a - hi.astype(F32)).astype(BF16)
    return hi, lo


def _block_transpose8(rows):
    lane = lax.broadcasted_iota(jnp.int32, rows[0].shape, 1)
    blk = lane // S5_GC
    for d in (4, 2, 1):
        keep = (blk & d) == 0
        new = list(rows)
        for i in range(8):
            if i & d == 0:
                a, b = rows[i], rows[i + d]
                new[i] = jnp.where(keep, a, pltpu.roll(b, d * S5_GC, axis=1))
                new[i + d] = jnp.where(keep, pltpu.roll(a, 128 - d * S5_GC, axis=1), b)
        rows = new
    return rows


def _ada_kernel(c_ref, w_ref, b_ref, o_ref):
    s = _silu(c_ref[...])
    o_ref[...] = jnp.dot(s, w_ref[...], preferred_element_type=F32,
                         precision=lax.Precision.HIGHEST) + b_ref[...]


def _ada(cc, ada_w, ada_b):
    rows = cc.shape[0]
    n_out = ada_w.shape[1]
    tn = 1024
    return pl.pallas_call(
        _ada_kernel,
        grid=(n_out // tn,),
        in_specs=[pl.BlockSpec((rows, D_MODEL), lambda j: (0, 0)),
                  pl.BlockSpec((D_MODEL, tn), lambda j: (0, j)),
                  pl.BlockSpec((1, tn), lambda j: (0, j))],
        out_specs=pl.BlockSpec((rows, tn), lambda j: (0, j)),
        out_shape=jax.ShapeDtypeStruct((rows, n_out), F32),
        compiler_params=pltpu.CompilerParams(dimension_semantics=("arbitrary",),
                                             vmem_limit_bytes=VMEM_LIMIT),
        name="ada",
    )(cc, ada_w, ada_b.reshape(1, n_out))


HALO = 16


def _inproj_kernel(x_ref, xp_ref, xn_ref, mod_ref, ng_ref, wqk_ref, wm_ref, conv_ref, bg_ref,
                   q_ref, k_ref, v_ref, o_ref, zm_ref, u_ref, zs_ref, g_ref, gt_ref,
                   lhs_sc, z_sc, *, tm, s5_layout):
    i = pl.program_id(1)
    nt = pl.num_programs(1)
    shift = mod_ref[0:1, :]
    scale = mod_ref[1:2, :]
    ng = ng_ref[...]

    def norm_mod(a):
        ms = jnp.mean(a * a, axis=-1, keepdims=True)
        return (a * lax.rsqrt(ms + NORM_EPS) * ng) * (1.0 + scale) + shift

    lhs_sc[0:tm, :] = norm_mod(x_ref[...]).astype(BF16)
    lhs_sc[tm:tm + HALO, :] = norm_mod(xp_ref[...]).astype(BF16)
    lhs_sc[tm + HALO:tm + 2 * HALO, :] = norm_mod(xn_ref[...]).astype(BF16)

    valid_prev = (i > 0).astype(F32)
    valid_next = (i < nt - 1).astype(F32)
    for half, out_ref, out_scale in ((0, q_ref, 1.0), (1, k_ref, HEAD_DIM ** -0.5)):
        cols = slice(half * W_BRANCH, (half + 1) * W_BRANCH)
        z = _dot(lhs_sc[...], wqk_ref[:, cols])
        z_sc[8:8 + tm, :] = z[0:tm]
        z_sc[0:8, :] = z[tm + HALO - 8:tm + HALO] * valid_prev
        z_sc[8 + tm:16 + tm, :] = z[tm + HALO:tm + HALO + 8] * valid_next
        cw = conv_ref[:, cols]
        conv = (cw[0:1] * z_sc[7:7 + tm, :] + cw[1:2] * z_sc[8:8 + tm, :]
                + cw[2:3] * z_sc[9:9 + tm, :])
        out_ref[...] = (_silu(conv) * out_scale).astype(BF16)

    h = lhs_sc[0:tm, :]
    v_ref[...] = _dot(h, wm_ref[:, 0:1024]).astype(BF16)
    o_ref[...] = _dot(h, wm_ref[:, 1024:2048]).astype(BF16)
    zm_ref[...] = _dot(h, wm_ref[:, 2048:3072]).astype(BF16)
    zs_ref[...] = _dot(h, wm_ref[:, 4096:5120]).astype(BF16)
    u = _dot(h, wm_ref[:, 3072:4096])
    if s5_layout:
        for g8 in range(W_BRANCH // 128):
            rows = [u[i * GRID_W:(i + 1) * GRID_W, g8 * 128:(g8 + 1) * 128] for i in range(8)]
            for gp, blk in enumerate(_block_transpose8(rows)):
                u_ref[g8 * 8 + gp] = blk.astype(BF16)
    else:
        u_ref[...] = u.astype(BF16)
    zg = _dot(h, wm_ref[:, 5120:5120 + GATE_PAD]) + bg_ref[...]
    lane = lax.broadcasted_iota(jnp.int32, zg.shape, 1)
    is_forget = jnp.logical_and(lane < N_GATES, (lane % 8) >= 4)
    gates = jnp.where(is_forget, _log_sigmoid(zg), zg)
    g_ref[...] = gates[:, 0:N_GATES]
    gt_ref[...] = gates.T[0:N_GATES, :]


def _inproj(x, mod, norm_g, w_qk, w_main, conv_qk, b_gate_pad, tm, s5_layout):
    bsz, t_len, _ = x.shape
    nt = t_len // tm
    nhb = t_len // HALO
    tok = lambda w, dt: jax.ShapeDtypeStruct((bsz, t_len, w), dt)
    tile = lambda w: pl.BlockSpec((None, tm, w), lambda b, i: (b, i, 0))
    const = lambda shape: pl.BlockSpec(shape, lambda b, i: (0,) * len(shape),
                                       pipeline_mode=pl.Buffered(1))
    if s5_layout:
        assert tm == 8 * GRID_W and t_len % (2 * tm) == 0
        u_spec = pl.BlockSpec((S5_GROUPS, GRID_W, 128), lambda b, i: (0, (i // 2) * bsz + b, i % 2))
        u_shape = jax.ShapeDtypeStruct((S5_GROUPS, (t_len // (2 * tm)) * bsz * GRID_W, 256), BF16)
    else:
        u_spec, u_shape = tile(W_BRANCH), tok(W_BRANCH, BF16)
    return pl.pallas_call(
        functools.partial(_inproj_kernel, tm=tm, s5_layout=s5_layout),
        grid=(bsz, nt),
        in_specs=[
            tile(D_MODEL),
            pl.BlockSpec((None, HALO, D_MODEL),
                         lambda b, i: (b, jnp.maximum(i * (tm // HALO) - 1, 0), 0)),
            pl.BlockSpec((None, HALO, D_MODEL),
                         lambda b, i: (b, jnp.minimum((i + 1) * (tm // HALO), nhb - 1), 0)),
            pl.BlockSpec((None, 3, D_MODEL), lambda b, i: (b, 0, 0)),
            const((1, D_MODEL)),
            const((D_MODEL, 2 * W_BRANCH)),
            const((D_MODEL, 5 * W_BRANCH + GATE_PAD)),
            const((CONV_W, 2 * W_BRANCH)),
            const((1, GATE_PAD)),
        ],
        out_specs=[tile(W_BRANCH)] * 5 + [u_spec, tile(W_BRANCH)] + [
            tile(N_GATES),
            pl.BlockSpec((None, N_GATES, tm), lambda b, i: (b, 0, i)),
        ],
        out_shape=[tok(W_BRANCH, BF16)] * 5 + [u_shape, tok(W_BRANCH, BF16), tok(N_GATES, F32),
                   jax.ShapeDtypeStruct((bsz, N_GATES, t_len), F32)],
        scratch_shapes=[pltpu.VMEM((tm + 2 * HALO, D_MODEL), BF16),
                        pltpu.VMEM((tm + 16, W_BRANCH), F32)],
        compiler_params=pltpu.CompilerParams(dimension_semantics=("parallel", "arbitrary"),
                                             vmem_limit_bytes=VMEM_LIMIT),
        name="inproj",
    )(x, x, x, mod, norm_g.reshape(1, D_MODEL), w_qk, w_main, conv_qk, b_gate_pad)


def _mlstm_kernel(*refs, reverse, with_output, chunk):
    if with_output:
        (q_ref, k_ref, v_ref, g_ref, gt_ref, c0_ref, n0_ref, m0_ref,
         h_ref, co_ref, no_ref, mo_ref, c_sc, n_sc, m_sc) = refs
    else:
        (k_ref, v_ref, g_ref, gt_ref, c0_ref, n0_ref, m0_ref,
         co_ref, no_ref, mo_ref, c_sc, n_sc, m_sc) = refs
    i = pl.program_id(1)
    nc = pl.num_programs(1)

    @pl.when(i == 0)
    def _():
        c_sc[...] = c0_ref[...]
        n_sc[...] = n0_ref[...]
        m_sc[...] = m0_ref[...]

    row = lax.broadcasted_iota(jnp.int32, (chunk, chunk), 0)
    col = lax.broadcasted_iota(jnp.int32, (chunk, chunk), 1)
    if reverse:
        mask_ts, mask_st = col >= row, row >= col
    else:
        mask_ts, mask_st = col <= row, row <= col
    tri_ts = jnp.where(mask_ts, 1.0, 0.0).astype(BF16)
    tri_st = jnp.where(mask_st, 1.0, 0.0).astype(BF16)

    g = g_ref[...]
    gt = gt_ref[...]
    g_hi, g_lo = _split_bf16(g)
    gt_hi, gt_lo = _split_bf16(gt)
    cum_col = _dot(tri_ts, g_hi) + _dot(tri_ts, g_lo)
    cum_row = _dot(gt_hi, tri_st) + _dot(gt_lo, tri_st)
    li0 = 8 if reverse else 0
    lf0 = li0 + 4
    last = 0 if reverse else chunk - 1

    for hd in range(HEADS):
        sl = slice(hd * HEAD_DIM, (hd + 1) * HEAD_DIM)
        b_c = cum_col[:, lf0 + hd:lf0 + hd + 1]
        li_c = g[:, li0 + hd:li0 + hd + 1]
        b_r = cum_row[lf0 + hd:lf0 + hd + 1, :]
        li_r = gt[li0 + hd:li0 + hd + 1, :]
        b_last = b_r[:, last:last + 1]
        m_old = m_sc[hd:hd + 1, 0:1]
        k = k_ref[:, sl]
        v = v_ref[:, sl]
        c_old = c_sc[hd]
        n_old = n_sc[hd:hd + 1, :]

        if with_output:
            q = q_ref[:, sl]
            log_w = jnp.where(mask_ts, b_c - b_r + li_r, NEG_BIG)
            inter = b_c + m_old
            m_t = jnp.maximum(inter, jnp.max(log_w, axis=1, keepdims=True))
            w_inter = jnp.exp(inter - m_t)
            qk = lax.dot_general(q, k, (((1,), (1,)), ((), ())), preferred_element_type=F32)
            s = qk * jnp.exp(log_w - m_t)
            num = w_inter * _dot(q, c_old.astype(BF16)) + _dot(s.astype(BF16), v)
            den = (w_inter * jnp.sum(q.astype(F32) * n_old, axis=1, keepdims=True)
                   + jnp.sum(s, axis=1, keepdims=True))
            h_ref[:, sl] = (num / jnp.maximum(jnp.abs(den), jnp.exp(-m_t))).astype(BF16)

        g_c = b_last - b_c + li_c
        g_r = b_last - b_r + li_r
        m_new = jnp.maximum(b_last + m_old, jnp.max(g_r, axis=1, keepdims=True))
        decay = jnp.exp(b_last + m_old - m_new)
        kw = k.astype(F32) * jnp.exp(g_c - m_new)
        c_sc[hd] = decay * c_old + lax.dot_general(
            kw.astype(BF16), v, (((0,), (0,)), ((), ())), preferred_element_type=F32)
        n_sc[hd:hd + 1, :] = decay * n_old + jnp.sum(kw, axis=0, keepdims=True)
        m_sc[hd:hd + 1, :] = jnp.broadcast_to(m_new, (1, 128))

    @pl.when(i == nc - 1)
    def _():
        co_ref[...] = c_sc[...]
        no_ref[...] = n_sc[...]
        mo_ref[...] = m_sc[...]


def _mlstm(q, k, v, g, gt, state, reverse, with_output):
    bsz, t_len, _ = k.shape
    chunk = MLSTM_CHUNK
    nc = t_len // chunk
    cidx = (lambda i: nc - 1 - i) if reverse else (lambda i: i)
    tile = lambda w: pl.BlockSpec((None, chunk, w), lambda b, i: (b, cidx(i), 0))
    st_specs = [pl.BlockSpec((None, HEADS, HEAD_DIM, HEAD_DIM), lambda b, i: (b, 0, 0, 0)),
                pl.BlockSpec((None, HEADS, HEAD_DIM), lambda b, i: (b, 0, 0)),
                pl.BlockSpec((None, HEADS, 128), lambda b, i: (b, 0, 0))]
    st_shapes = [jax.ShapeDtypeStruct((bsz, HEADS, HEAD_DIM, HEAD_DIM), F32),
                 jax.ShapeDtypeStruct((bsz, HEADS, HEAD_DIM), F32),
                 jax.ShapeDtypeStruct((bsz, HEADS, 128), F32)]
    in_specs = [tile(W_BRANCH), tile(W_BRANCH), tile(N_GATES),
                pl.BlockSpec((None, N_GATES, chunk), lambda b, i: (b, 0, cidx(i)))] + st_specs
    args = [k, v, g, gt] + list(state)
    out_specs, out_shape = list(st_specs), list(st_shapes)
    if with_output:
        in_specs = [tile(W_BRANCH)] + in_specs
        args = [q] + args
        out_specs = [tile(W_BRANCH)] + out_specs
        out_shape = [jax.ShapeDtypeStruct((bsz, t_len, W_BRANCH), BF16)] + out_shape
    outs = pl.pallas_call(
        functools.partial(_mlstm_kernel, reverse=reverse, with_output=with_output, chunk=chunk),
        grid=(bsz, nc),
        in_specs=in_specs,
        out_specs=out_specs,
        out_shape=out_shape,
        scratch_shapes=[pltpu.VMEM((HEADS, HEAD_DIM, HEAD_DIM), F32),
                        pltpu.VMEM((HEADS, HEAD_DIM), F32),
                        pltpu.VMEM((HEADS, 128), F32)],
        compiler_params=pltpu.CompilerParams(dimension_semantics=("parallel", "arbitrary"),
                                             vmem_limit_bytes=VMEM_LIMIT),
        name="mlstm_" + ("bwd" if reverse else "fwd") + ("_out" if with_output else "_state"),
    )(*args)
    if with_output:
        return outs[0], tuple(outs[1:])
    return None, tuple(outs)


def _s5_kernel(vc_ref, vx_ref, g_ref, m_ref, p_ref, a_ref, y_ref, gu_sc, s_sc, *,
               nk_ctx, n_rc, bsz, rblk):
    rows_ctx = nk_ctx * bsz
    rows_x = n_rc * bsz * GRID_W
    nk_x = n_rc * GRID_W

    def increments(v_ref, r0, r1, base):
        inc = _dot(v_ref[0, r0:r1, :], g_ref[0]) + _dot(v_ref[1, r0:r1, :], g_ref[1])
        for comp in range(4):
            gu_sc[comp, base + r0:base + r1, :] = inc[:, comp * 128:(comp + 1) * 128]

    increments(vc_ref, 0, rows_ctx, 0)
    for r0 in range(0, rows_x, rblk):
        increments(vx_ref, r0, min(r0 + rblk, rows_x), rows_ctx)

    a = a_ref[...]
    a_pow = [jnp.broadcast_to(a[:, comp * 128:(comp + 1) * 128], (bsz, 128)) for comp in range(4)]
    zero = jnp.zeros((bsz, 128), F32)

    def step(rows, carry, direction):
        s_r, s_i = carry
        a_r, a_i = a_pow[2 * direction], a_pow[2 * direction + 1]
        inc_r = gu_sc[2 * direction, rows, :]
        inc_i = gu_sc[2 * direction + 1, rows, :]
        gu_sc[2 * direction, rows, :] = s_r
        gu_sc[2 * direction + 1, rows, :] = s_i
        return (a_r * s_r - a_i * s_i + inc_r, a_r * s_i + a_i * s_r + inc_i)

    def ctx_rows(k):
        return pl.ds(pl.multiple_of(k * bsz, bsz), bsz)

    def x_rows(j):
        w, rc = j // n_rc, j % n_rc
        return pl.ds(rows_ctx + rc * (bsz * GRID_W) + w, bsz, stride=GRID_W)

    carry = lax.fori_loop(0, nk_ctx, lambda k, c: step(ctx_rows(k), c, 0), (zero, zero))
    lax.fori_loop(0, nk_x, lambda j, c: step(x_rows(j), c, 0), carry)
    carry = lax.fori_loop(0, nk_ctx, lambda k, c: step(ctx_rows(nk_ctx - 1 - k), c, 1), (zero, zero))
    lax.fori_loop(0, nk_x, lambda j, c: step(x_rows(nk_x - 1 - j), c, 1), carry)

    for r0 in range(0, rows_x, rblk):
        r1 = min(r0 + rblk, rows_x)
        for comp in range(4):
            s_sc[r0:r1, comp * 128:(comp + 1) * 128] = (
                gu_sc[comp, rows_ctx + r0:rows_ctx + r1, :].astype(BF16))
        for gg in range(2):
            y_ref[gg, r0:r1, :] = (_dot(vx_ref[gg, r0:r1, :], m_ref[gg])
                                   + _dot(s_sc[r0:r1, :], p_ref[gg])).astype(BF16)


def _s5(v_ctx, v_x, g_all, m_all, p_all, a16, bsz):
    rows_ctx, rows_x = v_ctx.shape[1], v_x.shape[1]
    lanes = S5_SUB * S5_GC
    return pl.pallas_call(
        functools.partial(_s5_kernel, nk_ctx=rows_ctx // bsz, n_rc=rows_x // (bsz * GRID_W),
                          bsz=bsz, rblk=512),
        grid=(S5_GROUPS // 2,),
        in_specs=[pl.BlockSpec((2, rows_ctx, lanes), lambda j: (j, 0, 0)),
                  pl.BlockSpec((2, rows_x, lanes), lambda j: (j, 0, 0)),
                  pl.BlockSpec((2, lanes, 512), lambda j: (j, 0, 0)),
                  pl.BlockSpec((2, lanes, lanes), lambda j: (j, 0, 0)),
                  pl.BlockSpec((2, 512, lanes), lambda j: (j, 0, 0)),
                  pl.BlockSpec((None, 1, 512), lambda j: (j, 0, 0))],
        out_specs=pl.BlockSpec((2, rows_x, lanes), lambda j: (j, 0, 0)),
        out_shape=jax.ShapeDtypeStruct((S5_GROUPS, rows_x, lanes), BF16),
        scratch_shapes=[pltpu.VMEM((4, rows_ctx + rows_x, 128), F32),
                        pltpu.VMEM((rows_x, 512), BF16)],
        compiler_params=pltpu.CompilerParams(dimension_semantics=("parallel",),
                                             vmem_limit_bytes=VMEM_LIMIT),
        name="s5",
    )(v_ctx, v_x, g_all, m_all, p_all, a16)


def _s5_params(a_re, a_im, log_step, b_re, b_im, c_re, c_im, d_skip):
    hp = lax.Precision.HIGHEST
    n_g, n_p, n_c, n_s = S5_GROUPS, S5_STATE, S5_GC, S5_SUB
    a_re, a_im = a_re.astype(F32), a_im.astype(F32)
    dt = jnp.exp(log_step.astype(F32))[..., None]
    lam_re, lam_im = a_re * dt, a_im * dt

    def a_pow(n):
        mag = jnp.exp(lam_re * n)
        return mag * jnp.cos(lam_im * n), mag * jnp.sin(lam_im * n)

    ab_re, ab_im = a_pow(1.0)
    nr, ni = ab_re - 1.0, ab_im
    den = a_re * a_re + a_im * a_im
    cr = (nr * a_re + ni * a_im) / den
    ci = (ni * a_re - nr * a_im) / den
    bb_re = cr[..., None] * b_re - ci[..., None] * b_im
    bb_im = cr[..., None] * b_im + ci[..., None] * b_re

    steps = jnp.arange(n_s + 1, dtype=F32)[:, None, None, None]
    pw_re, pw_im = a_pow(steps)
    ca_re = c_re[None] * pw_re[:, :, :, None, :] - c_im[None] * pw_im[:, :, :, None, :]
    ca_im = c_re[None] * pw_im[:, :, :, None, :] + c_im[None] * pw_re[:, :, :, None, :]
    kern = (jnp.einsum('ndgop,dgpc->ndgoc', ca_re[:n_s], bb_re, precision=hp)
            - jnp.einsum('ndgop,dgpc->ndgoc', ca_im[:n_s], bb_im, precision=hp))
    idx = jnp.arange(n_s)
    delta = idx[None, :] - idx[:, None]
    kf = kern[jnp.clip(delta, 0, n_s - 1), 0]
    kb = kern[jnp.clip(-delta, 0, n_s - 1), 1]
    blk = (jnp.where((delta >= 0)[:, :, None, None, None], kf, 0.0)
           + jnp.where((delta <= 0)[:, :, None, None, None], kb, 0.0))
    m_all = jnp.transpose(blk, (2, 0, 4, 1, 3)).reshape(n_g, n_s * n_c, n_s * n_c)
    m_all = m_all + jnp.eye(n_s * n_c, dtype=F32)[None] * jnp.tile(
        d_skip.astype(F32).reshape(n_g, 1, n_c), (1, n_s, 1)).reshape(n_g, 1, n_s * n_c)

    def cmul(xr, xi, yr, yi):
        return xr * yr - xi * yi, xr * yi + xi * yr

    gf_re, gf_im = cmul(pw_re[n_s - 1 - idx, 0][..., None], pw_im[n_s - 1 - idx, 0][..., None],
                        bb_re[0][None], bb_im[0][None])
    gb_re, gb_im = cmul(pw_re[idx, 1][..., None], pw_im[idx, 1][..., None],
                        bb_re[1][None], bb_im[1][None])
    g4 = jnp.stack([gf_re, gf_im, gb_re, gb_im], axis=0)
    g4 = jnp.transpose(g4, (2, 1, 4, 0, 3)).reshape(n_g, n_s * n_c, 4, n_p)
    parity = (jnp.arange(n_g) % 2)[:, None, None, None, None]
    half = jnp.arange(2)[None, None, None, :, None]
    g_all = jnp.where(parity == half, g4[:, :, :, None, :], 0.0).reshape(n_g, n_s * n_c, 8 * n_p)

    p4 = jnp.stack([ca_re[idx + 1, 0], -ca_im[idx + 1, 0],
                    ca_re[n_s - idx, 1], -ca_im[n_s - idx, 1]], axis=0)
    p4 = jnp.transpose(p4, (2, 0, 4, 1, 3)).reshape(n_g, 4, n_p, n_s * n_c)
    parity = (jnp.arange(n_g) % 2)[:, None, None, None, None]
    half = jnp.arange(2)[None, None, :, None, None]
    p_all = jnp.where(parity == half, p4[:, :, None, :, :], 0.0).reshape(n_g, 8 * n_p, n_s * n_c)

    a16 = jnp.stack([pw_re[n_s, 0], pw_im[n_s, 0], pw_re[n_s, 1], pw_im[n_s, 1]], axis=0)
    a16 = jnp.transpose(a16.reshape(4, n_g // 2, 2 * n_p), (1, 0, 2)).reshape(n_g // 2, 1, 8 * n_p)
    return m_all.astype(BF16), g_all.astype(BF16), p_all.astype(BF16), a16


def _merge_kernel(hf_ref, hb_ref, o_ref, zm_ref, y_ref, zs_ref, x_ref, mod_ref, mhg_ref,
                  gluw_ref, glub_ref, wout_ref, fg_ref, out_ref, y_sc):
    for g8 in range(W_BRANCH // 128):
        rows = [y_ref[g8 * 8 + gp].astype(F32) for gp in range(8)]
        for i, blk in enumerate(_block_transpose8(rows)):
            y_sc[i * GRID_W:(i + 1) * GRID_W, g8 * 128:(g8 + 1) * 128] = blk

    f32 = lambda ref: ref[...].astype(F32)
    hm = (f32(hf_ref) + f32(hb_ref)) * jax.nn.sigmoid(f32(o_ref))
    mhg = mhg_ref[...]
    parts = []
    for hd in range(HEADS):
        sl = slice(hd * HEAD_DIM, (hd + 1) * HEAD_DIM)
        seg = hm[:, sl]
        mu = jnp.mean(seg, axis=-1, keepdims=True)
        dev = seg - mu
        var = jnp.mean(dev * dev, axis=-1, keepdims=True)
        parts.append(dev * lax.rsqrt(var + NORM_EPS) * mhg[:, sl])
    m_out = jnp.concatenate(parts, axis=-1) * _silu(f32(zm_ref))

    y = y_sc[...]
    gl = 0.5 * y * (1.0 + jnp.tanh(0.7978845608028654 * (y + 0.044715 * (y * y * y))))
    gate = jax.nn.sigmoid(_dot(gl.astype(BF16), gluw_ref[...]) + glub_ref[...])
    s_out = gl * gate * _silu(f32(zs_ref))

    mixed = (_dot(m_out.astype(BF16), wout_ref[0:W_BRANCH, :])
             + _dot(s_out.astype(BF16), wout_ref[W_BRANCH:2 * W_BRANCH, :]))
    xo = x_ref[...] + mod_ref[2:3, :] * mixed
    ms = jnp.mean(xo * xo, axis=-1, keepdims=True)
    out_ref[...] = xo * lax.rsqrt(ms + NORM_EPS) * fg_ref[...]


def _merge(hf, hb, o, zm, y, zs, x, mod, mh_g, glu_w, glu_b, w_out, final_g):
    bsz, t_len, _ = x.shape
    tm = 8 * GRID_W
    tile = pl.BlockSpec((None, tm, D_MODEL), lambda b, i: (b, i, 0))
    y_spec = pl.BlockSpec((S5_GROUPS, GRID_W, 128), lambda b, i: (0, (i // 2) * bsz + b, i % 2))
    const = lambda shape: pl.BlockSpec(shape, lambda b, i: (0,) * len(shape),
                                       pipeline_mode=pl.Buffered(1))
    return pl.pallas_call(
        _merge_kernel,
        grid=(bsz, t_len // tm),
        in_specs=[tile] * 4 + [y_spec, tile, tile,
                               pl.BlockSpec((None, 3, D_MODEL), lambda b, i: (b, 0, 0)),
                               const((1, W_BRANCH)), const((W_BRANCH, W_BRANCH)),
                               const((1, W_BRANCH)), const((2 * W_BRANCH, D_MODEL)),
                               const((1, D_MODEL))],
        out_specs=tile,
        out_shape=jax.ShapeDtypeStruct((bsz, t_len, D_MODEL), F32),
        scratch_shapes=[pltpu.VMEM((tm, W_BRANCH), F32)],
        compiler_params=pltpu.CompilerParams(dimension_semantics=("parallel", "arbitrary"),
                                             vmem_limit_bytes=VMEM_LIMIT),
        name="merge",
    )(hf, hb, o, zm, y, zs, x, mod, mh_g.reshape(1, -1), glu_w, glu_b.reshape(1, -1), w_out,
      final_g.reshape(1, -1))


def _s5_rows_ctx(u):
    bsz, t_len, _ = u.shape
    a = u.reshape(bsz, t_len // S5_SUB, S5_SUB, S5_GROUPS, S5_GC)
    a = jnp.transpose(a, (3, 1, 0, 2, 4))
    return a.reshape(S5_GROUPS, (t_len // S5_SUB) * bsz, S5_SUB * S5_GC)


def kernel(x, c, ctx, c_ctx, norm_g, ada_w, ada_b, w_in, b_gate, conv_qk, mh_g, s5_a_re, s5_a_im,
           s5_log_step, s5_b_re, s5_b_im, s5_c_re, s5_c_im, s5_d, glu_w, glu_b, w_out, final_g):
    bsz, t_len, _ = x.shape
    ctx_len = ctx.shape[1]
    layer = 0

    cc = jnp.zeros((16, D_MODEL), F32).at[:bsz].set(c).at[bsz].set(c_ctx)
    mod = _ada(cc, ada_w[layer], ada_b[layer]).reshape(16, 3, D_MODEL)
    mod_x = mod[:bsz]
    mod_c = jnp.broadcast_to(mod[bsz][None], (bsz, 3, D_MODEL))

    w = w_in[layer]
    wb = W_BRANCH
    w_qk = w[:, 0:2 * wb].astype(BF16)
    gate0 = 5 * wb
    w_gate = jnp.pad(w[:, gate0:gate0 + N_GATES], ((0, 0), (0, GATE_PAD - N_GATES)))
    w_main = jnp.concatenate([w[:, 2 * wb:5 * wb], w[:, gate0 + N_GATES:], w_gate], axis=1).astype(BF16)
    b_gate_pad = jnp.pad(b_gate[layer].reshape(1, N_GATES), ((0, 0), (0, GATE_PAD - N_GATES)))

    proj = functools.partial(_inproj, norm_g=norm_g[layer], w_qk=w_qk, w_main=w_main,
                             conv_qk=conv_qk[layer], b_gate_pad=b_gate_pad)
    _, k_c, v_c, _, _, u_c, _, g_c, gt_c = proj(ctx, mod_c, tm=256, s5_layout=False)
    q_x, k_x, v_x, o_x, zm_x, u_x, zs_x, g_x, gt_x = proj(x, mod_x, tm=8 * GRID_W, s5_layout=True)

    zero_state = (jnp.zeros((bsz, HEADS, HEAD_DIM, HEAD_DIM), F32),
                  jnp.zeros((bsz, HEADS, HEAD_DIM), F32),
                  jnp.zeros((bsz, HEADS, 128), F32))
    _, st_f = _mlstm(None, k_c, v_c, g_c, gt_c, zero_state, reverse=False, with_output=False)
    _, st_b = _mlstm(None, k_c, v_c, g_c, gt_c, zero_state, reverse=True, with_output=False)
    h_f, _ = _mlstm(q_x, k_x, v_x, g_x, gt_x, st_f, reverse=False, with_output=True)
    h_b, _ = _mlstm(q_x, k_x, v_x, g_x, gt_x, st_b, reverse=True, with_output=True)

    m_all, g_all, p_all, a16 = _s5_params(
        s5_a_re[layer], s5_a_im[layer], s5_log_step[layer], s5_b_re[layer], s5_b_im[layer],
        s5_c_re[layer], s5_c_im[layer], s5_d[layer])
    y_x = _s5(_s5_rows_ctx(u_c), u_x, g_all, m_all, p_all, a16, bsz)

    return _merge(h_f, h_b, o_x, zm_x, y_x, zs_x, x, mod_x, mh_g[layer], glu_w[layer].astype(BF16),
                  glu_b[layer], w_out[layer].astype(BF16), final_g)
```

```python
import functools

import jax
import jax.numpy as jnp
from jax import lax
from jax.experimental import pallas as pl
from jax.experimental.pallas import tpu as pltpu

F32 = jnp.float32
BF16 = jnp.bfloat16

D_MODEL = 1024
HEADS = 4
HEAD_DIM = 256
W_BRANCH = 1024
S5_GROUPS = 64
S5_GC = 16
S5_STATE = 64
S5_SUB = 16
GRID_W = 64
S5_ROW_PITCH = 72
N_GATES = 16
GATE_PAD = 128
CONV_W = 3
NORM_EPS = 1e-6
MLSTM_CHUNK = 256
NEG_BIG = -1e30

VMEM_LIMIT = 56 * 1024 * 1024


def _silu(a):
    return a * jax.nn.sigmoid(a)


def _log_sigmoid(a):
    return jnp.minimum(a, 0.0) - jnp.log1p(jnp.exp(-jnp.abs(a)))


def _dot(a, b):
    return jnp.dot(a, b, preferred_element_type=F32)


def _split_bf16(a):
    hi = a.astype(BF16)
    lo = (a - hi.astype(F32)).astype(BF16)
    return hi, lo


def _block_transpose8(rows):
    lane = lax.broadcasted_iota(jnp.int32, rows[0].shape, 1)
    blk = lane // S5_GC
    for d in (4, 2, 1):
        keep = (blk & d) == 0
        new = list(rows)
        for i in range(8):
            if i & d == 0:
                a, b = rows[i], rows[i + d]
                new[i] = jnp.where(keep, a, pltpu.roll(b, d * S5_GC, axis=1))
                new[i + d] = jnp.where(keep, pltpu.roll(a, 128 - d * S5_GC, axis=1), b)
        rows = new
    return rows


def _ada_kernel(c_ref, w_ref, b_ref, o_ref):
    s = _silu(c_ref[...])
    o_ref[...] = jnp.dot(s, w_ref[...], preferred_element_type=F32,
                         precision=lax.Precision.HIGHEST) + b_ref[...]


def _ada(cc, ada_w, ada_b):
    rows = cc.shape[0]
    n_out = ada_w.shape[1]
    tn = 1024
    return pl.pallas_call(
        _ada_kernel,
        grid=(n_out // tn,),
        in_specs=[pl.BlockSpec((rows, D_MODEL), lambda j: (0, 0)),
                  pl.BlockSpec((D_MODEL, tn), lambda j: (0, j)),
                  pl.BlockSpec((1, tn), lambda j: (0, j))],
        out_specs=pl.BlockSpec((rows, tn), lambda j: (0, j)),
        out_shape=jax.ShapeDtypeStruct((rows, n_out), F32),
        compiler_params=pltpu.CompilerParams(dimension_semantics=("arbitrary",),
                                             vmem_limit_bytes=VMEM_LIMIT),
        name="ada",
    )(cc, ada_w, ada_b.reshape(1, n_out))


HALO = 16


def _inproj_kernel(x_ref, xp_ref, xn_ref, mod_ref, ng_ref, wqk_ref, wm_ref, conv_ref, bg_ref,
                   q_ref, k_ref, v_ref, o_ref, zm_ref, u_ref, zs_ref, g_ref, gt_ref,
                   lhs_sc, z_sc, *, tm, s5_layout):
    i = pl.program_id(1)
    nt = pl.num_programs(1)
    shift = mod_ref[0:1, :]
    scale = mod_ref[1:2, :]
    ng = ng_ref[...]

    def norm_mod(a):
        ms = jnp.mean(a * a, axis=-1, keepdims=True)
        return (a * lax.rsqrt(ms + NORM_EPS) * ng) * (1.0 + scale) + shift

    lhs_sc[0:tm, :] = norm_mod(x_ref[...]).astype(BF16)
    lhs_sc[tm:tm + HALO, :] = norm_mod(xp_ref[...]).astype(BF16)
    lhs_sc[tm + HALO:tm + 2 * HALO, :] = norm_mod(xn_ref[...]).astype(BF16)

    valid_prev = (i > 0).astype(F32)
    valid_next = (i < nt - 1).astype(F32)
    for half, out_ref, out_scale in ((0, q_ref, 1.0), (1, k_ref, HEAD_DIM ** -0.5)):
        cols = slice(half * W_BRANCH, (half + 1) * W_BRANCH)
        z = _dot(lhs_sc[...], wqk_ref[:, cols])
        z_sc[8:8 + tm, :] = z[0:tm]
        z_sc[0:8, :] = z[tm + HALO - 8:tm + HALO] * valid_prev
        z_sc[8 + tm:16 + tm, :] = z[tm + HALO:tm + HALO + 8] * valid_next
        cw = conv_ref[:, cols]
        conv = (cw[0:1] * z_sc[7:7 + tm, :] + cw[1:2] * z_sc[8:8 + tm, :]
                + cw[2:3] * z_sc[9:9 + tm, :])
        out_ref[...] = (_silu(conv) * out_scale).astype(BF16)

    h = lhs_sc[0:tm, :]
    v_ref[...] = _dot(h, wm_ref[:, 0:1024]).astype(BF16)
    o_ref[...] = _dot(h, wm_ref[:, 1024:2048]).astype(BF16)
    zm_ref[...] = _dot(h, wm_ref[:, 2048:3072]).astype(BF16)
    zs_ref[...] = _dot(h, wm_ref[:, 4096:5120]).astype(BF16)
    u = _dot(h, wm_ref[:, 3072:4096])
    if s5_layout:
        for g8 in range(W_BRANCH // 128):
            rows = [u[i * GRID_W:(i + 1) * GRID_W, g8 * 128:(g8 + 1) * 128] for i in range(8)]
            for gp, blk in enumerate(_block_transpose8(rows)):
                u_ref[g8 * 8 + gp] = blk.astype(BF16)
    else:
        u_ref[...] = u.astype(BF16)
    zg = _dot(h, wm_ref[:, 5120:5120 + GATE_PAD]) + bg_ref[...]
    lane = lax.broadcasted_iota(jnp.int32, zg.shape, 1)
    is_forget = jnp.logical_and(lane < N_GATES, (lane % 8) >= 4)
    gates = jnp.where(is_forget, _log_sigmoid(zg), zg)
    g_ref[...] = gates[:, 0:N_GATES]
    gt_ref[...] = gates.T[0:N_GATES, :]


def _inproj(x, mod, norm_g, w_qk, w_main, conv_qk, b_gate_pad, tm, s5_layout):
    bsz, t_len, _ = x.shape
    nt = t_len // tm
    nhb = t_len // HALO
    tok = lambda w, dt: jax.ShapeDtypeStruct((bsz, t_len, w), dt)
    tile = lambda w: pl.BlockSpec((None, tm, w), lambda b, i: (b, i, 0))
    const = lambda shape: pl.BlockSpec(shape, lambda b, i: (0,) * len(shape),
                                       pipeline_mode=pl.Buffered(1))
    if s5_layout:
        assert tm == 8 * GRID_W and t_len % (2 * tm) == 0
        u_spec = pl.BlockSpec((S5_GROUPS, GRID_W, 128), lambda b, i: (0, (i // 2) * bsz + b, i % 2))
        u_shape = jax.ShapeDtypeStruct((S5_GROUPS, (t_len // (2 * tm)) * bsz * GRID_W, 256), BF16)
    else:
        u_spec, u_shape = tile(W_BRANCH), tok(W_BRANCH, BF16)
    return pl.pallas_call(
        functools.partial(_inproj_kernel, tm=tm, s5_layout=s5_layout),
        grid=(bsz, nt),
        in_specs=[
            tile(D_MODEL),
            pl.BlockSpec((None, HALO, D_MODEL),
                         lambda b, i: (b, jnp.maximum(i * (tm // HALO) - 1, 0), 0)),
            pl.BlockSpec((None, HALO, D_MODEL),
                         lambda b, i: (b, jnp.minimum((i + 1) * (tm // HALO), nhb - 1), 0)),
            pl.BlockSpec((None, 3, D_MODEL), lambda b, i: (b, 0, 0)),
            const((1, D_MODEL)),
            const((D_MODEL, 2 * W_BRANCH)),
            const((D_MODEL, 5 * W_BRANCH + GATE_PAD)),
            const((CONV_W, 2 * W_BRANCH)),
            const((1, GATE_PAD)),
        ],
        out_specs=[tile(W_BRANCH)] * 5 + [u_spec, tile(W_BRANCH)] + [
            tile(N_GATES),
            pl.BlockSpec((None, N_GATES, tm), lambda b, i: (b, 0, i)),
        ],
        out_shape=[tok(W_BRANCH, BF16)] * 5 + [u_shape, tok(W_BRANCH, BF16), tok(N_GATES, F32),
                   jax.ShapeDtypeStruct((bsz, N_GATES, t_len), F32)],
        scratch_shapes=[pltpu.VMEM((tm + 2 * HALO, D_MODEL), BF16),
                        pltpu.VMEM((tm + 16, W_BRANCH), F32)],
        compiler_params=pltpu.CompilerParams(dimension_semantics=("parallel", "arbitrary"),
                                             vmem_limit_bytes=VMEM_LIMIT),
        name="inproj",
    )(x, x, x, mod, norm_g.reshape(1, D_MODEL), w_qk, w_main, conv_qk, b_gate_pad)


def _mlstm_direction(q_ref, k_ref, v_ref, g_ref, gt_ref, h_ref, c_sc, n_sc, m_sc, *,
                     reverse, chunk):
    with_output = h_ref is not None
    row = lax.broadcasted_iota(jnp.int32, (chunk, chunk), 0)
    col = lax.broadcasted_iota(jnp.int32, (chunk, chunk), 1)
    if reverse:
        mask_ts, mask_st = col >= row, row >= col
    else:
        mask_ts, mask_st = col <= row, row <= col
    tri_ts = jnp.where(mask_ts, 1.0, 0.0).astype(BF16)
    tri_st = jnp.where(mask_st, 1.0, 0.0).astype(BF16)

    g = g_ref[...]
    gt = gt_ref[...]
    g_hi, g_lo = _split_bf16(g)
    gt_hi, gt_lo = _split_bf16(gt)
    cum_col = _dot(tri_ts, g_hi) + _dot(tri_ts, g_lo)
    cum_row = _dot(gt_hi, tri_st) + _dot(gt_lo, tri_st)
    li0 = 8 if reverse else 0
    lf0 = li0 + 4
    last = 0 if reverse else chunk - 1

    for hd in range(HEADS):
        sl = slice(hd * HEAD_DIM, (hd + 1) * HEAD_DIM)
        b_c = cum_col[:, lf0 + hd:lf0 + hd + 1]
        li_c = g[:, li0 + hd:li0 + hd + 1]
        b_r = cum_row[lf0 + hd:lf0 + hd + 1, :]
        li_r = gt[li0 + hd:li0 + hd + 1, :]
        b_last = b_r[:, last:last + 1]
        m_old = m_sc[hd:hd + 1, 0:1]
        k = k_ref[:, sl]
        v = v_ref[:, sl]
        c_old = c_sc[hd]
        n_old = n_sc[hd:hd + 1, :]

        if with_output:
            q = q_ref[:, sl]
            log_w = jnp.where(mask_ts, b_c - b_r + li_r, NEG_BIG)
            inter = b_c + m_old
            m_t = jnp.maximum(inter, jnp.max(log_w, axis=1, keepdims=True))
            w_inter = jnp.exp(inter - m_t)
            qk = lax.dot_general(q, k, (((1,), (1,)), ((), ())), preferred_element_type=F32)
            s = qk * jnp.exp(log_w - m_t)
            num = w_inter * _dot(q, c_old.astype(BF16)) + _dot(s.astype(BF16), v)
            den = (w_inter * jnp.sum(q.astype(F32) * n_old, axis=1, keepdims=True)
                   + jnp.sum(s, axis=1, keepdims=True))
            h_ref[:, sl] = (num / jnp.maximum(jnp.abs(den), jnp.exp(-m_t))).astype(BF16)

        g_c = b_last - b_c + li_c
        g_r = b_last - b_r + li_r
        m_new = jnp.maximum(b_last + m_old, jnp.max(g_r, axis=1, keepdims=True))
        decay = jnp.exp(b_last + m_old - m_new)
        kw = k.astype(F32) * jnp.exp(g_c - m_new)
        c_sc[hd] = decay * c_old + lax.dot_general(
            kw.astype(BF16), v, (((0,), (0,)), ((), ())), preferred_element_type=F32)
        n_sc[hd:hd + 1, :] = decay * n_old + jnp.sum(kw, axis=0, keepdims=True)
        m_sc[hd:hd + 1, :] = jnp.broadcast_to(m_new, (1, 128))


def _mlstm_kernel(*refs, with_output, chunk):
    n_in = 5 if with_output else 4
    ins = [refs[0:n_in], refs[n_in:2 * n_in]]
    c0_ref, n0_ref, m0_ref = refs[2 * n_in:2 * n_in + 3]
    rest = refs[2 * n_in + 3:]
    if with_output:
        h_refs, rest = rest[0:2], rest[2:]
    else:
        h_refs = (None, None)
        ins = [(None,) + tuple(r) for r in ins]
    co_ref, no_ref, mo_ref, c_sc, n_sc, m_sc = rest
    i = pl.program_id(1)
    nc = pl.num_programs(1)

    @pl.when(i == 0)
    def _():
        c_sc[...] = c0_ref[...]
        n_sc[...] = n0_ref[...]
        m_sc[...] = m0_ref[...]

    for d in range(2):
        _mlstm_direction(*ins[d], h_refs[d], c_sc.at[d], n_sc.at[d], m_sc.at[d],
                         reverse=bool(d), chunk=chunk)

    @pl.when(i == nc - 1)
    def _():
        co_ref[...] = c_sc[...]
        no_ref[...] = n_sc[...]
        mo_ref[...] = m_sc[...]


def _mlstm(q, k, v, g, gt, state, with_output):
    bsz, t_len, _ = k.shape
    chunk = MLSTM_CHUNK
    nc = t_len // chunk
    cidx = (lambda i: i, lambda i: nc - 1 - i)
    tile = lambda w, d: pl.BlockSpec((None, chunk, w), lambda b, i: (b, cidx[d](i), 0))
    st_specs = [pl.BlockSpec((None, 2, HEADS, HEAD_DIM, HEAD_DIM), lambda b, i: (b, 0, 0, 0, 0)),
                pl.BlockSpec((None, 2, HEADS, HEAD_DIM), lambda b, i: (b, 0, 0, 0)),
                pl.BlockSpec((None, 2, HEADS, 128), lambda b, i: (b, 0, 0, 0))]
    st_shapes = [jax.ShapeDtypeStruct((bsz, 2, HEADS, HEAD_DIM, HEAD_DIM), F32),
                 jax.ShapeDtypeStruct((bsz, 2, HEADS, HEAD_DIM), F32),
                 jax.ShapeDtypeStruct((bsz, 2, HEADS, 128), F32)]
    in_specs, args = [], []
    for d in range(2):
        gt_spec = pl.BlockSpec((None, N_GATES, chunk), lambda b, i, d=d: (b, 0, cidx[d](i)))
        in_specs += ([tile(W_BRANCH, d)] if with_output else []) + [
            tile(W_BRANCH, d), tile(W_BRANCH, d), tile(N_GATES, d), gt_spec]
        args += ([q] if with_output else []) + [k, v, g, gt]
    out_specs, out_shape = list(st_specs), list(st_shapes)
    if with_output:
        out_specs = [tile(W_BRANCH, 0), tile(W_BRANCH, 1)] + out_specs
        out_shape = [jax.ShapeDtypeStruct((bsz, t_len, W_BRANCH), BF16)] * 2 + out_shape
    outs = pl.pallas_call(
        functools.partial(_mlstm_kernel, with_output=with_output, chunk=chunk),
        grid=(bsz, nc),
        in_specs=in_specs + st_specs,
        out_specs=out_specs,
        out_shape=out_shape,
        scratch_shapes=[pltpu.VMEM((2, HEADS, HEAD_DIM, HEAD_DIM), F32),
                        pltpu.VMEM((2, HEADS, HEAD_DIM), F32),
                        pltpu.VMEM((2, HEADS, 128), F32)],
        compiler_params=pltpu.CompilerParams(dimension_semantics=("parallel", "arbitrary"),
                                             vmem_limit_bytes=VMEM_LIMIT),
        name="mlstm_out" if with_output else "mlstm_state",
    )(*args, *state)
    if with_output:
        return (outs[0], outs[1]), tuple(outs[2:])
    return None, tuple(outs)


def _s5_kernel(vc_ref, vx_ref, g_ref, m_ref, p_ref, a_ref, y_ref, gu_sc, s_sc, *,
               nk_ctx, n_rc, bsz, rblk):
    rows_ctx = nk_ctx * bsz
    rows_x = n_rc * bsz * GRID_W

    def increments(v_ref, r0, r1):
        return _dot(v_ref[0, r0:r1, :], g_ref[0]) + _dot(v_ref[1, r0:r1, :], g_ref[1])

    inc = increments(vc_ref, 0, rows_ctx)
    for comp in range(4):
        gu_sc[comp, 0:rows_ctx, :] = inc[:, comp * 128:(comp + 1) * 128]
    for r0 in range(0, rows_x, rblk):
        inc = increments(vx_ref, r0, r0 + rblk)
        for run in range(rblk // GRID_W):
            dst = rows_ctx + (r0 // GRID_W + run) * S5_ROW_PITCH
            for comp in range(4):
                gu_sc[comp, dst:dst + GRID_W, :] = (
                    inc[run * GRID_W:(run + 1) * GRID_W, comp * 128:(comp + 1) * 128])

    a = a_ref[...]
    a_pow = [jnp.broadcast_to(a[:, comp * 128:(comp + 1) * 128], (bsz, 128)) for comp in range(4)]
    zero = jnp.zeros((bsz, 128), F32)

    def step(rows, carry, direction):
        s_r, s_i = carry
        a_r, a_i = a_pow[2 * direction], a_pow[2 * direction + 1]
        inc_r = gu_sc[2 * direction, rows, :]
        inc_i = gu_sc[2 * direction + 1, rows, :]
        gu_sc[2 * direction, rows, :] = s_r
        gu_sc[2 * direction + 1, rows, :] = s_i
        return (a_r * s_r - a_i * s_i + inc_r, a_r * s_i + a_i * s_r + inc_i)

    def ctx_rows(k):
        return pl.ds(pl.multiple_of(k * bsz, bsz), bsz)

    def x_rows(w, rc):
        return pl.ds(rows_ctx + rc * (bsz * S5_ROW_PITCH) + w, bsz, stride=S5_ROW_PITCH)

    def ctx_body(k, carry):
        return step(ctx_rows(k), carry[0], 0), step(ctx_rows(nk_ctx - 1 - k), carry[1], 1)

    def x_body(w, carry):
        c_f, c_b = carry
        for rc in range(n_rc):
            c_f = step(x_rows(w, rc), c_f, 0)
            c_b = step(x_rows(GRID_W - 1 - w, n_rc - 1 - rc), c_b, 1)
        return c_f, c_b

    carry = lax.fori_loop(0, nk_ctx, ctx_body, ((zero, zero), (zero, zero)))
    lax.fori_loop(0, GRID_W, x_body, carry)

    for r0 in range(0, rows_x, rblk):
        r1 = r0 + rblk
        for run in range(rblk // GRID_W):
            src = rows_ctx + (r0 // GRID_W + run) * S5_ROW_PITCH
            for comp in range(4):
                s_sc[r0 + run * GRID_W:r0 + (run + 1) * GRID_W, comp * 128:(comp + 1) * 128] = (
                    gu_sc[comp, src:src + GRID_W, :].astype(BF16))
        for gg in range(2):
            y_ref[gg, r0:r1, :] = (_dot(vx_ref[gg, r0:r1, :], m_ref[gg])
                                   + _dot(s_sc[r0:r1, :], p_ref[gg])).astype(BF16)


def _s5(v_ctx, v_x, g_all, m_all, p_all, a16, bsz):
    rows_ctx, rows_x = v_ctx.shape[1], v_x.shape[1]
    lanes = S5_SUB * S5_GC
    return pl.pallas_call(
        functools.partial(_s5_kernel, nk_ctx=rows_ctx // bsz, n_rc=rows_x // (bsz * GRID_W),
                          bsz=bsz, rblk=512),
        grid=(S5_GROUPS // 2,),
        in_specs=[pl.BlockSpec((2, rows_ctx, lanes), lambda j: (j, 0, 0)),
                  pl.BlockSpec((2, rows_x, lanes), lambda j: (j, 0, 0)),
                  pl.BlockSpec((2, lanes, 512), lambda j: (j, 0, 0)),
                  pl.BlockSpec((2, lanes, lanes), lambda j: (j, 0, 0)),
                  pl.BlockSpec((2, 512, lanes), lambda j: (j, 0, 0)),
                  pl.BlockSpec((None, 1, 512), lambda j: (j, 0, 0))],
        out_specs=pl.BlockSpec((2, rows_x, lanes), lambda j: (j, 0, 0)),
        out_shape=jax.ShapeDtypeStruct((S5_GROUPS, rows_x, lanes), BF16),
        scratch_shapes=[pltpu.VMEM((4, rows_ctx + (rows_x // GRID_W) * S5_ROW_PITCH, 128), F32),
                        pltpu.VMEM((rows_x, 512), BF16)],
        compiler_params=pltpu.CompilerParams(dimension_semantics=("parallel",),
                                             vmem_limit_bytes=VMEM_LIMIT),
        name="s5",
    )(v_ctx, v_x, g_all, m_all, p_all, a16)


def _s5_params(a_re, a_im, log_step, b_re, b_im, c_re, c_im, d_skip):
    hp = lax.Precision.HIGHEST
    n_g, n_p, n_c, n_s = S5_GROUPS, S5_STATE, S5_GC, S5_SUB
    a_re, a_im = a_re.astype(F32), a_im.astype(F32)
    dt = jnp.exp(log_step.astype(F32))[..., None]
    lam_re, lam_im = a_re * dt, a_im * dt

    def a_pow(n):
        mag = jnp.exp(lam_re * n)
        return mag * jnp.cos(lam_im * n), mag * jnp.sin(lam_im * n)

    ab_re, ab_im = a_pow(1.0)
    nr, ni = ab_re - 1.0, ab_im
    den = a_re * a_re + a_im * a_im
    cr = (nr * a_re + ni * a_im) / den
    ci = (ni * a_re - nr * a_im) / den
    bb_re = cr[..., None] * b_re - ci[..., None] * b_im
    bb_im = cr[..., None] * b_im + ci[..., None] * b_re

    steps = jnp.arange(n_s + 1, dtype=F32)[:, None, None, None]
    pw_re, pw_im = a_pow(steps)
    ca_re = c_re[None] * pw_re[:, :, :, None, :] - c_im[None] * pw_im[:, :, :, None, :]
    ca_im = c_re[None] * pw_im[:, :, :, None, :] + c_im[None] * pw_re[:, :, :, None, :]
    kern = (jnp.einsum('ndgop,dgpc->ndgoc', ca_re[:n_s], bb_re, precision=hp)
            - jnp.einsum('ndgop,dgpc->ndgoc', ca_im[:n_s], bb_im, precision=hp))
    idx = jnp.arange(n_s)
    delta = idx[None, :] - idx[:, None]
    kf = kern[jnp.clip(delta, 0, n_s - 1), 0]
    kb = kern[jnp.clip(-delta, 0, n_s - 1), 1]
    blk = (jnp.where((delta >= 0)[:, :, None, None, None], kf, 0.0)
           + jnp.where((delta <= 0)[:, :, None, None, None], kb, 0.0))
    m_all = jnp.transpose(blk, (2, 0, 4, 1, 3)).reshape(n_g, n_s * n_c, n_s * n_c)
    m_all = m_all + jnp.eye(n_s * n_c, dtype=F32)[None] * jnp.tile(
        d_skip.astype(F32).reshape(n_g, 1, n_c), (1, n_s, 1)).reshape(n_g, 1, n_s * n_c)

    def cmul(xr, xi, yr, yi):
        return xr * yr - xi * yi, xr * yi + xi * yr

    gf_re, gf_im = cmul(pw_re[n_s - 1 - idx, 0][..., None], pw_im[n_s - 1 - idx, 0][..., None],
                        bb_re[0][None], bb_im[0][None])
    gb_re, gb_im = cmul(pw_re[idx, 1][..., None], pw_im[idx, 1][..., None],
                        bb_re[1][None], bb_im[1][None])
    g4 = jnp.stack([gf_re, gf_im, gb_re, gb_im], axis=0)
    g4 = jnp.transpose(g4, (2, 1, 4, 0, 3)).reshape(n_g, n_s * n_c, 4, n_p)
    parity = (jnp.arange(n_g) % 2)[:, None, None, None, None]
    half = jnp.arange(2)[None, None, None, :, None]
    g_all = jnp.where(parity == half, g4[:, :, :, None, :], 0.0).reshape(n_g, n_s * n_c, 8 * n_p)

    p4 = jnp.stack([ca_re[idx + 1, 0], -ca_im[idx + 1, 0],
                    ca_re[n_s - idx, 1], -ca_im[n_s - idx, 1]], axis=0)
    p4 = jnp.transpose(p4, (2, 0, 4, 1, 3)).reshape(n_g, 4, n_p, n_s * n_c)
    parity = (jnp.arange(n_g) % 2)[:, None, None, None, None]
    half = jnp.arange(2)[None, None, :, None, None]
    p_all = jnp.where(parity == half, p4[:, :, None, :, :], 0.0).reshape(n_g, 8 * n_p, n_s * n_c)

    a16 = jnp.stack([pw_re[n_s, 0], pw_im[n_s, 0], pw_re[n_s, 1], pw_im[n_s, 1]], axis=0)
    a16 = jnp.transpose(a16.reshape(4, n_g // 2, 2 * n_p), (1, 0, 2)).reshape(n_g // 2, 1, 8 * n_p)
    return m_all.astype(BF16), g_all.astype(BF16), p_all.astype(BF16), a16


def _merge_kernel(hf_ref, hb_ref, o_ref, zm_ref, y_ref, zs_ref, x_ref, mod_ref, mhg_ref,
                  gluw_ref, glub_ref, wout_ref, fg_ref, out_ref, y_sc):
    for g8 in range(W_BRANCH // 128):
        rows = [y_ref[g8 * 8 + gp].astype(F32) for gp in range(8)]
        for i, blk in enumerate(_block_transpose8(rows)):
            y_sc[i * GRID_W:(i + 1) * GRID_W, g8 * 128:(g8 + 1) * 128] = blk

    f32 = lambda ref: ref[...].astype(F32)
    hm = (f32(hf_ref) + f32(hb_ref)) * jax.nn.sigmoid(f32(o_ref))
    mhg = mhg_ref[...]
    parts = []
    for hd in range(HEADS):
        sl = slice(hd * HEAD_DIM, (hd + 1) * HEAD_DIM)
        seg = hm[:, sl]
        mu = jnp.mean(seg, axis=-1, keepdims=True)
        dev = seg - mu
        var = jnp.mean(dev * dev, axis=-1, keepdims=True)
        parts.append(dev * lax.rsqrt(var + NORM_EPS) * mhg[:, sl])
    m_out = jnp.concatenate(parts, axis=-1) * _silu(f32(zm_ref))

    y = y_sc[...]
    gl = 0.5 * y * (1.0 + jnp.tanh(0.7978845608028654 * (y + 0.044715 * (y * y * y))))
    gate = jax.nn.sigmoid(_dot(gl.astype(BF16), gluw_ref[...]) + glub_ref[...])
    s_out = gl * gate * _silu(f32(zs_ref))

    mixed = (_dot(m_out.astype(BF16), wout_ref[0:W_BRANCH, :])
             + _dot(s_out.astype(BF16), wout_ref[W_BRANCH:2 * W_BRANCH, :]))
    xo = x_ref[...] + mod_ref[2:3, :] * mixed
    ms = jnp.mean(xo * xo, axis=-1, keepdims=True)
    out_ref[...] = xo * lax.rsqrt(ms + NORM_EPS) * fg_ref[...]


def _merge(hf, hb, o, zm, y, zs, x, mod, mh_g, glu_w, glu_b, w_out, final_g):
    bsz, t_len, _ = x.shape
    tm = 8 * GRID_W
    tile = pl.BlockSpec((None, tm, D_MODEL), lambda b, i: (b, i, 0))
    y_spec = pl.BlockSpec((S5_GROUPS, GRID_W, 128), lambda b, i: (0, (i // 2) * bsz + b, i % 2))
    const = lambda shape: pl.BlockSpec(shape, lambda b, i: (0,) * len(shape),
                                       pipeline_mode=pl.Buffered(1))
    return pl.pallas_call(
        _merge_kernel,
        grid=(bsz, t_len // tm),
        in_specs=[tile] * 4 + [y_spec, tile, tile,
                               pl.BlockSpec((None, 3, D_MODEL), lambda b, i: (b, 0, 0)),
                               const((1, W_BRANCH)), const((W_BRANCH, W_BRANCH)),
                               const((1, W_BRANCH)), const((2 * W_BRANCH, D_MODEL)),
                               const((1, D_MODEL))],
        out_specs=tile,
        out_shape=jax.ShapeDtypeStruct((bsz, t_len, D_MODEL), F32),
        scratch_shapes=[pltpu.VMEM((tm, W_BRANCH), F32)],
        compiler_params=pltpu.CompilerParams(dimension_semantics=("parallel", "arbitrary"),
                                             vmem_limit_bytes=VMEM_LIMIT),
        name="merge",
    )(hf, hb, o, zm, y, zs, x, mod, mh_g.reshape(1, -1), glu_w, glu_b.reshape(1, -1), w_out,
      final_g.reshape(1, -1))


def _s5_rows_ctx(u):
    bsz, t_len, _ = u.shape
    a = u.reshape(bsz, t_len // S5_SUB, S5_SUB, S5_GROUPS, S5_GC)
    a = jnp.transpose(a, (3, 1, 0, 2, 4))
    return a.reshape(S5_GROUPS, (t_len // S5_SUB) * bsz, S5_SUB * S5_GC)


def kernel(x, c, ctx, c_ctx, norm_g, ada_w, ada_b, w_in, b_gate, conv_qk, mh_g, s5_a_re, s5_a_im,
           s5_log_step, s5_b_re, s5_b_im, s5_c_re, s5_c_im, s5_d, glu_w, glu_b, w_out, final_g):
    bsz, t_len, _ = x.shape
    layer = 0

    cc = jnp.zeros((16, D_MODEL), F32).at[:bsz].set(c).at[bsz].set(c_ctx)
    mod = _ada(cc, ada_w[layer], ada_b[layer]).reshape(16, 3, D_MODEL)
    mod_x = mod[:bsz]
    mod_c = jnp.broadcast_to(mod[bsz][None], (bsz, 3, D_MODEL))

    w = w_in[layer]
    wb = W_BRANCH
    w_qk = w[:, 0:2 * wb].astype(BF16)
    gate0 = 5 * wb
    w_gate = jnp.pad(w[:, gate0:gate0 + N_GATES], ((0, 0), (0, GATE_PAD - N_GATES)))
    w_main = jnp.concatenate([w[:, 2 * wb:5 * wb], w[:, gate0 + N_GATES:], w_gate], axis=1).astype(BF16)
    b_gate_pad = jnp.pad(b_gate[layer].reshape(1, N_GATES), ((0, 0), (0, GATE_PAD - N_GATES)))

    proj = functools.partial(_inproj, norm_g=norm_g[layer], w_qk=w_qk, w_main=w_main,
                             conv_qk=conv_qk[layer], b_gate_pad=b_gate_pad)
    _, k_c, v_c, _, _, u_c, _, g_c, gt_c = proj(ctx, mod_c, tm=256, s5_layout=False)
    q_x, k_x, v_x, o_x, zm_x, u_x, zs_x, g_x, gt_x = proj(x, mod_x, tm=8 * GRID_W, s5_layout=True)

    zero_state = (jnp.zeros((bsz, 2, HEADS, HEAD_DIM, HEAD_DIM), F32),
                  jnp.zeros((bsz, 2, HEADS, HEAD_DIM), F32),
                  jnp.zeros((bsz, 2, HEADS, 128), F32))
    _, ctx_state = _mlstm(None, k_c, v_c, g_c, gt_c, zero_state, with_output=False)
    (h_f, h_b), _ = _mlstm(q_x, k_x, v_x, g_x, gt_x, ctx_state, with_output=True)

    m_all, g_all, p_all, a16 = _s5_params(
        s5_a_re[layer], s5_a_im[layer], s5_log_step[layer], s5_b_re[layer], s5_b_im[layer],
        s5_c_re[layer], s5_c_im[layer], s5_d[layer])
    y_x = _s5(_s5_rows_ctx(u_c), u_x, g_all, m_all, p_all, a16, bsz)

    return _merge(h_f, h_b, o_x, zm_x, y_x, zs_x, x, mod_x, mh_g[layer], glu_w[layer].astype(BF16),
                  glu_b[layer], w_out[layer].astype(BF16), final_g)
```

```python
import functools

import jax
import jax.numpy as jnp
from jax import lax
from jax.experimental import pallas as pl
from jax.experimental.pallas import tpu as pltpu

F32 = jnp.float32
BF16 = jnp.bfloat16

D_MODEL = 1024
HEADS = 4
HEAD_DIM = 256
W_BRANCH = 1024
S5_GROUPS = 64
S5_GC = 16
S5_STATE = 64
S5_SUB = 16
GRID_W = 64
S5_ROW_PITCH = 72
N_GATES = 16
GATE_PAD = 128
CONV_W = 3
NORM_EPS = 1e-6
MLSTM_CHUNK = 256
NEG_BIG = -1e30

VMEM_LIMIT = 56 * 1024 * 1024


def _silu(a):
    return a * jax.nn.sigmoid(a)


def _log_sigmoid(a):
    return jnp.minimum(a, 0.0) - jnp.log1p(jnp.exp(-jnp.abs(a)))


def _dot(a, b):
    return jnp.dot(a, b, preferred_element_type=F32)


def _split_bf16(a):
    hi = a.astype(BF16)
    lo = (a - hi.astype(F32)).astype(BF16)
    return hi, lo


def _block_transpose8(rows):
    lane = lax.broadcasted_iota(jnp.int32, rows[0].shape, 1)
    blk = lane // S5_GC
    for d in (4, 2, 1):
        keep = (blk & d) == 0
        new = list(rows)
        for i in range(8):
            if i & d == 0:
                a, b = rows[i], rows[i + d]
                new[i] = jnp.where(keep, a, pltpu.roll(b, d * S5_GC, axis=1))
                new[i + d] = jnp.where(keep, pltpu.roll(a, 128 - d * S5_GC, axis=1), b)
        rows = new
    return rows


def _ada_kernel(c_ref, w_ref, b_ref, o_ref):
    s = _silu(c_ref[...])
    o_ref[...] = jnp.dot(s, w_ref[...], preferred_element_type=F32,
                         precision=lax.Precision.HIGHEST) + b_ref[...]


def _ada(cc, ada_w, ada_b):
    rows = cc.shape[0]
    n_out = ada_w.shape[1]
    tn = 1024
    return pl.pallas_call(
        _ada_kernel,
        grid=(n_out // tn,),
        in_specs=[pl.BlockSpec((rows, D_MODEL), lambda j: (0, 0)),
                  pl.BlockSpec((D_MODEL, tn), lambda j: (0, j)),
                  pl.BlockSpec((1, tn), lambda j: (0, j))],
        out_specs=pl.BlockSpec((rows, tn), lambda j: (0, j)),
        out_shape=jax.ShapeDtypeStruct((rows, n_out), F32),
        compiler_params=pltpu.CompilerParams(dimension_semantics=("arbitrary",),
                                             vmem_limit_bytes=VMEM_LIMIT),
        name="ada",
    )(cc, ada_w, ada_b.reshape(1, n_out))


HALO = 16


def _inproj_kernel(x_ref, xp_ref, xn_ref, mod_ref, ng_ref, wqk_ref, wm_ref, conv_ref, bg_ref,
                   q_ref, kt_ref, v_ref, o_ref, zm_ref, u_ref, zs_ref, gt_ref,
                   lhs_sc, z_sc, *, tm, s5_layout):
    i = pl.program_id(1)
    nt = pl.num_programs(1)
    shift = mod_ref[0:1, :]
    scale = mod_ref[1:2, :]
    ng = ng_ref[...]

    def norm_mod(a):
        ms = jnp.mean(a * a, axis=-1, keepdims=True)
        return (a * lax.rsqrt(ms + NORM_EPS) * ng) * (1.0 + scale) + shift

    lhs_sc[0:tm, :] = norm_mod(x_ref[...]).astype(BF16)
    lhs_sc[tm:tm + HALO, :] = norm_mod(xp_ref[...]).astype(BF16)
    lhs_sc[tm + HALO:tm + 2 * HALO, :] = norm_mod(xn_ref[...]).astype(BF16)

    valid_prev = (i > 0).astype(F32)
    valid_next = (i < nt - 1).astype(F32)
    for half in range(2):
        cols = slice(half * W_BRANCH, (half + 1) * W_BRANCH)
        z = _dot(lhs_sc[...], wqk_ref[:, cols])
        z_sc[8:8 + tm, :] = z[0:tm]
        z_sc[0:8, :] = z[tm + HALO - 8:tm + HALO] * valid_prev
        z_sc[8 + tm:16 + tm, :] = z[tm + HALO:tm + HALO + 8] * valid_next
        cw = conv_ref[:, cols]
        conv = (cw[0:1] * z_sc[7:7 + tm, :] + cw[1:2] * z_sc[8:8 + tm, :]
                + cw[2:3] * z_sc[9:9 + tm, :])
        if half == 0:
            q_ref[...] = _silu(conv).astype(BF16)
        else:
            kt_ref[...] = (_silu(conv) * HEAD_DIM ** -0.5).T.astype(BF16)

    h = lhs_sc[0:tm, :]
    v_ref[...] = _dot(h, wm_ref[:, 0:1024]).astype(BF16)
    o_ref[...] = _dot(h, wm_ref[:, 1024:2048]).astype(BF16)
    zm_ref[...] = _dot(h, wm_ref[:, 2048:3072]).astype(BF16)
    zs_ref[...] = _dot(h, wm_ref[:, 4096:5120]).astype(BF16)
    u = _dot(h, wm_ref[:, 3072:4096])
    if s5_layout:
        for g8 in range(W_BRANCH // 128):
            rows = [u[i * GRID_W:(i + 1) * GRID_W, g8 * 128:(g8 + 1) * 128] for i in range(8)]
            for gp, blk in enumerate(_block_transpose8(rows)):
                u_ref[g8 * 8 + gp] = blk.astype(BF16)
    else:
        u_ref[...] = u.astype(BF16)
    zg = _dot(h, wm_ref[:, 5120:5120 + GATE_PAD]) + bg_ref[...]
    lane = lax.broadcasted_iota(jnp.int32, zg.shape, 1)
    is_forget = jnp.logical_and(lane < N_GATES, (lane % 8) >= 4)
    gates = jnp.where(is_forget, _log_sigmoid(zg), zg)
    gt_ref[...] = gates.T[0:N_GATES, :]


def _inproj(x, mod, norm_g, w_qk, w_main, conv_qk, b_gate_pad, tm, s5_layout):
    bsz, t_len, _ = x.shape
    nt = t_len // tm
    nhb = t_len // HALO
    tok = lambda w, dt: jax.ShapeDtypeStruct((bsz, t_len, w), dt)
    tile = lambda w: pl.BlockSpec((None, tm, w), lambda b, i: (b, i, 0))
    tile_t = lambda w: pl.BlockSpec((None, w, tm), lambda b, i: (b, 0, i))
    const = lambda shape: pl.BlockSpec(shape, lambda b, i: (0,) * len(shape),
                                       pipeline_mode=pl.Buffered(1))
    if s5_layout:
        assert tm == 8 * GRID_W and t_len % (2 * tm) == 0
        u_spec = pl.BlockSpec((S5_GROUPS, GRID_W, 128), lambda b, i: (0, (i // 2) * bsz + b, i % 2))
        u_shape = jax.ShapeDtypeStruct((S5_GROUPS, (t_len // (2 * tm)) * bsz * GRID_W, 256), BF16)
    else:
        u_spec, u_shape = tile(W_BRANCH), tok(W_BRANCH, BF16)
    return pl.pallas_call(
        functools.partial(_inproj_kernel, tm=tm, s5_layout=s5_layout),
        grid=(bsz, nt),
        in_specs=[
            tile(D_MODEL),
            pl.BlockSpec((None, HALO, D_MODEL),
                         lambda b, i: (b, jnp.maximum(i * (tm // HALO) - 1, 0), 0)),
            pl.BlockSpec((None, HALO, D_MODEL),
                         lambda b, i: (b, jnp.minimum((i + 1) * (tm // HALO), nhb - 1), 0)),
            pl.BlockSpec((None, 3, D_MODEL), lambda b, i: (b, 0, 0)),
            const((1, D_MODEL)),
            const((D_MODEL, 2 * W_BRANCH)),
            const((D_MODEL, 5 * W_BRANCH + GATE_PAD)),
            const((CONV_W, 2 * W_BRANCH)),
            const((1, GATE_PAD)),
        ],
        out_specs=[tile(W_BRANCH), tile_t(W_BRANCH)] + [tile(W_BRANCH)] * 3 + [
            u_spec, tile(W_BRANCH), tile_t(N_GATES)],
        out_shape=[tok(W_BRANCH, BF16), jax.ShapeDtypeStruct((bsz, W_BRANCH, t_len), BF16)]
        + [tok(W_BRANCH, BF16)] * 3 + [u_shape, tok(W_BRANCH, BF16),
                                       jax.ShapeDtypeStruct((bsz, N_GATES, t_len), F32)],
        scratch_shapes=[pltpu.VMEM((tm + 2 * HALO, D_MODEL), BF16),
                        pltpu.VMEM((tm + 16, W_BRANCH), F32)],
        compiler_params=pltpu.CompilerParams(dimension_semantics=("parallel", "arbitrary"),
                                             vmem_limit_bytes=VMEM_LIMIT),
        name="inproj",
    )(x, x, x, mod, norm_g.reshape(1, D_MODEL), w_qk, w_main, conv_qk, b_gate_pad)


def _mlstm_direction(q_ref, kt_ref, v_ref, gt_ref, h_ref, c_sc, n_sc, m_sc, *, reverse, chunk):
    with_output = h_ref is not None
    row = lax.broadcasted_iota(jnp.int32, (chunk, chunk), 0)
    col = lax.broadcasted_iota(jnp.int32, (chunk, chunk), 1)
    if reverse:
        mask_ts, mask_st = col >= row, row >= col
    else:
        mask_ts, mask_st = col <= row, row <= col
    tri_st = jnp.where(mask_st, 1.0, 0.0).astype(BF16)

    gt = gt_ref[...]
    gt_hi, gt_lo = _split_bf16(gt)
    cum_row = _dot(gt_hi, tri_st) + _dot(gt_lo, tri_st)
    li0 = 8 if reverse else 0
    lf0 = li0 + HEADS
    last = 0 if reverse else chunk - 1
    li_r = gt[li0:li0 + HEADS, :]
    b_r = cum_row[lf0:lf0 + HEADS, :]
    b_last = b_r[:, last:last + 1]
    m_old = m_sc[:, 0:1]
    g_r = b_last - b_r + li_r
    m_new = jnp.maximum(b_last + m_old, jnp.max(g_r, axis=1, keepdims=True))
    decay = jnp.exp(b_last + m_old - m_new)
    k_scale = jnp.exp(g_r - m_new)
    ones_rep = jnp.ones((chunk, 128), BF16)

    if with_output:
        a_r = li_r - b_r
        lane = lax.broadcasted_iota(jnp.int32, a_r.shape, 1)
        run_max = a_r
        shift = 1
        while shift < chunk:
            if reverse:
                moved = jnp.where(lane < chunk - shift, pltpu.roll(run_max, chunk - shift, axis=1), NEG_BIG)
            else:
                moved = jnp.where(lane >= shift, pltpu.roll(run_max, shift, axis=1), NEG_BIG)
            run_max = jnp.maximum(run_max, moved)
            shift *= 2
        mm_r = jnp.maximum(run_max, m_old)
        rows = jnp.concatenate([mm_r, b_r + mm_r], axis=0)
        sel_r = lax.broadcasted_iota(jnp.int32, (2 * HEADS, 2 * HEADS * 128), 0)
        sel_c = lax.broadcasted_iota(jnp.int32, (2 * HEADS, 2 * HEADS * 128), 1) // 128
        sel = jnp.where(sel_r == sel_c, 1.0, 0.0).astype(BF16)
        cols = jnp.zeros((chunk, 2 * HEADS * 128), F32)
        rest = rows
        for _ in range(3):
            part = rest.astype(BF16)
            rest = rest - part.astype(F32)
            cols = cols + lax.dot_general(part, sel, (((0,), (0,)), ((), ())),
                                          preferred_element_type=F32)
        twice = lambda a: jnp.concatenate([a] * (chunk // 128), axis=1)

    sls = [slice(hd * HEAD_DIM, (hd + 1) * HEAD_DIM) for hd in range(HEADS)]
    kts = [kt_ref[sl, :] for sl in sls]
    vs = [v_ref[:, sl] for sl in sls]
    c_olds = [c_sc[hd] for hd in range(HEADS)]
    n_olds = [n_sc[hd] for hd in range(HEADS)]
    kw_ts = [(kts[hd].astype(F32) * k_scale[hd:hd + 1, :]).astype(BF16) for hd in range(HEADS)]
    if with_output:
        qs = [q_ref[:, sl] for sl in sls]
        qks = [_dot(qs[hd], kts[hd]) for hd in range(HEADS)]
        qcs = [_dot(qs[hd], c_olds[hd].astype(BF16)) for hd in range(HEADS)]
        qns = [_dot(qs[hd], n_olds[hd].astype(BF16)) for hd in range(HEADS)]
    cus = [_dot(kw_ts[hd], vs[hd]) for hd in range(HEADS)]
    nus = [_dot(kw_ts[hd], ones_rep) for hd in range(HEADS)]
    if with_output:
        ss = []
        for hd in range(HEADS):
            mm_c = cols[:, hd * 128:(hd + 1) * 128]
            decay_mat = jnp.exp(jnp.where(mask_ts, a_r[hd:hd + 1, :] - twice(mm_c), NEG_BIG))
            ss.append((qks[hd] * decay_mat).astype(BF16))
        svs = [_dot(ss[hd], vs[hd]) for hd in range(HEADS)]
        sos = [_dot(ss[hd], ones_rep) for hd in range(HEADS)]
        for hd in range(HEADS):
            mm_c = cols[:, hd * 128:(hd + 1) * 128]
            bm_c = cols[:, (HEADS + hd) * 128:(HEADS + hd + 1) * 128]
            w_inter = jnp.exp(m_old[hd:hd + 1, :] - mm_c)
            num = jnp.concatenate([w_inter] * (HEAD_DIM // 128), axis=1) * qcs[hd] + svs[hd]
            den = w_inter * qns[hd] + sos[hd]
            inv = 1.0 / jnp.maximum(jnp.abs(den), jnp.exp(-bm_c))
            h_ref[:, sls[hd]] = (num * jnp.concatenate([inv] * (HEAD_DIM // 128), axis=1)).astype(BF16)
    for hd in range(HEADS):
        c_sc[hd] = decay[hd:hd + 1, :] * c_olds[hd] + cus[hd]
        n_sc[hd] = decay[hd:hd + 1, :] * n_olds[hd] + nus[hd]
        m_sc[hd:hd + 1, :] = jnp.broadcast_to(m_new[hd:hd + 1, :], (1, 128))


def _mlstm_kernel(*refs, with_output, chunk):
    n_in = 4 if with_output else 3
    ins = [refs[0:n_in], refs[n_in:2 * n_in]]
    c0_ref, n0_ref, m0_ref = refs[2 * n_in:2 * n_in + 3]
    rest = refs[2 * n_in + 3:]
    if with_output:
        h_refs, rest = rest[0:2], rest[2:]
    else:
        h_refs = (None, None)
        ins = [(None,) + tuple(r) for r in ins]
    co_ref, no_ref, mo_ref, c_sc, n_sc, m_sc = rest
    i = pl.program_id(1)
    nc = pl.num_programs(1)

    @pl.when(i == 0)
    def _():
        c_sc[...] = c0_ref[...]
        n_sc[...] = n0_ref[...]
        m_sc[...] = m0_ref[...]

    for d in range(2):
        _mlstm_direction(*ins[d], h_refs[d], c_sc.at[d], n_sc.at[d], m_sc.at[d],
                         reverse=bool(d), chunk=chunk)

    @pl.when(i == nc - 1)
    def _():
        co_ref[...] = c_sc[...]
        no_ref[...] = n_sc[...]
        mo_ref[...] = m_sc[...]


def _mlstm(q, kt, v, gt, state, with_output):
    bsz, t_len, _ = v.shape
    chunk = MLSTM_CHUNK
    nc = t_len // chunk
    cidx = (lambda i: i, lambda i: nc - 1 - i)
    tile = lambda w, d: pl.BlockSpec((None, chunk, w), lambda b, i: (b, cidx[d](i), 0))
    tile_t = lambda w, d: pl.BlockSpec((None, w, chunk), lambda b, i: (b, 0, cidx[d](i)))
    st_dims = [(2, HEADS, HEAD_DIM, HEAD_DIM), (2, HEADS, HEAD_DIM, 128), (2, HEADS, 128)]
    st_specs = [pl.BlockSpec((None,) + s, lambda b, i, n=len(s): (b,) + (0,) * n) for s in st_dims]
    st_shapes = [jax.ShapeDtypeStruct((bsz,) + s, F32) for s in st_dims]
    in_specs, args = [], []
    for d in range(2):
        in_specs += ([tile(W_BRANCH, d)] if with_output else []) + [
            tile_t(W_BRANCH, d), tile(W_BRANCH, d), tile_t(N_GATES, d)]
        args += ([q] if with_output else []) + [kt, v, gt]
    out_specs, out_shape = list(st_specs), list(st_shapes)
    if with_output:
        out_specs = [tile(W_BRANCH, 0), tile(W_BRANCH, 1)] + out_specs
        out_shape = [jax.ShapeDtypeStruct((bsz, t_len, W_BRANCH), BF16)] * 2 + out_shape
    outs = pl.pallas_call(
        functools.partial(_mlstm_kernel, with_output=with_output, chunk=chunk),
        grid=(bsz, nc),
        in_specs=in_specs + st_specs,
        out_specs=out_specs,
        out_shape=out_shape,
        scratch_shapes=[pltpu.VMEM(s, F32) for s in st_dims],
        compiler_params=pltpu.CompilerParams(dimension_semantics=("parallel", "arbitrary"),
                                             vmem_limit_bytes=VMEM_LIMIT),
        name="mlstm_out" if with_output else "mlstm_state",
    )(*args, *state)
    if with_output:
        return (outs[0], outs[1]), tuple(outs[2:])
    return None, tuple(outs)


def _s5_kernel(vc_ref, vx_ref, g_ref, m_ref, p_ref, a_ref, y_ref, gu_sc, s_sc, *,
               nk_ctx, n_rc, bsz, rblk):
    rows_ctx = nk_ctx * bsz
    rows_x = n_rc * bsz * GRID_W

    def increments(v_ref, r0, r1):
        return _dot(v_ref[0, r0:r1, :], g_ref[0]) + _dot(v_ref[1, r0:r1, :], g_ref[1])

    inc = increments(vc_ref, 0, rows_ctx)
    for comp in range(4):
        gu_sc[comp, 0:rows_ctx, :] = inc[:, comp * 128:(comp + 1) * 128]
    for r0 in range(0, rows_x, rblk):
        inc = increments(vx_ref, r0, r0 + rblk)
        for run in range(rblk // GRID_W):
            dst = rows_ctx + (r0 // GRID_W + run) * S5_ROW_PITCH
            for comp in range(4):
                gu_sc[comp, dst:dst + GRID_W, :] = (
                    inc[run * GRID_W:(run + 1) * GRID_W, comp * 128:(comp + 1) * 128])

    a = a_ref[...]
    a_pow = [jnp.broadcast_to(a[:, comp * 128:(comp + 1) * 128], (bsz, 128)) for comp in range(4)]
    zero = jnp.zeros((bsz, 128), F32)

    def step(rows, carry, direction):
        s_r, s_i = carry
        a_r, a_i = a_pow[2 * direction], a_pow[2 * direction + 1]
        inc_r = gu_sc[2 * direction, rows, :]
        inc_i = gu_sc[2 * direction + 1, rows, :]
        gu_sc[2 * direction, rows, :] = s_r
        gu_sc[2 * direction + 1, rows, :] = s_i
        return (a_r * s_r - a_i * s_i + inc_r, a_r * s_i + a_i * s_r + inc_i)

    def ctx_rows(k):
        return pl.ds(pl.multiple_of(k * bsz, bsz), bsz)

    def x_rows(w, rc):
        return pl.ds(rows_ctx + rc * (bsz * S5_ROW_PITCH) + w, bsz, stride=S5_ROW_PITCH)

    def ctx_body(k, carry):
        return step(ctx_rows(k), carry[0], 0), step(ctx_rows(nk_ctx - 1 - k), carry[1], 1)

    def x_body(w, carry):
        c_f, c_b = carry
        for rc in range(n_rc):
            c_f = step(x_rows(w, rc), c_f, 0)
            c_b = step(x_rows(GRID_W - 1 - w, n_rc - 1 - rc), c_b, 1)
        return c_f, c_b

    carry = lax.fori_loop(0, nk_ctx, ctx_body, ((zero, zero), (zero, zero)))
    lax.fori_loop(0, GRID_W, x_body, carry)

    for r0 in range(0, rows_x, rblk):
        r1 = r0 + rblk
        for run in range(rblk // GRID_W):
            src = rows_ctx + (r0 // GRID_W + run) * S5_ROW_PITCH
            for comp in range(4):
                s_sc[r0 + run * GRID_W:r0 + (run + 1) * GRID_W, comp * 128:(comp + 1) * 128] = (
                    gu_sc[comp, src:src + GRID_W, :].astype(BF16))
        for gg in range(2):
            y_ref[gg, r0:r1, :] = (_dot(vx_ref[gg, r0:r1, :], m_ref[gg])
                                   + _dot(s_sc[r0:r1, :], p_ref[gg])).astype(BF16)


def _s5(v_ctx, v_x, g_all, m_all, p_all, a16, bsz):
    rows_ctx, rows_x = v_ctx.shape[1], v_x.shape[1]
    lanes = S5_SUB * S5_GC
    return pl.pallas_call(
        functools.partial(_s5_kernel, nk_ctx=rows_ctx // bsz, n_rc=rows_x // (bsz * GRID_W),
                          bsz=bsz, rblk=512),
        grid=(S5_GROUPS // 2,),
        in_specs=[pl.BlockSpec((2, rows_ctx, lanes), lambda j: (j, 0, 0)),
                  pl.BlockSpec((2, rows_x, lanes), lambda j: (j, 0, 0)),
                  pl.BlockSpec((2, lanes, 512), lambda j: (j, 0, 0)),
                  pl.BlockSpec((2, lanes, lanes), lambda j: (j, 0, 0)),
                  pl.BlockSpec((2, 512, lanes), lambda j: (j, 0, 0)),
                  pl.BlockSpec((None, 1, 512), lambda j: (j, 0, 0))],
        out_specs=pl.BlockSpec((2, rows_x, lanes), lambda j: (j, 0, 0)),
        out_shape=jax.ShapeDtypeStruct((S5_GROUPS, rows_x, lanes), BF16),
        scratch_shapes=[pltpu.VMEM((4, rows_ctx + (rows_x // GRID_W) * S5_ROW_PITCH, 128), F32),
                        pltpu.VMEM((rows_x, 512), BF16)],
        compiler_params=pltpu.CompilerParams(dimension_semantics=("parallel",),
                                             vmem_limit_bytes=VMEM_LIMIT),
        name="s5",
    )(v_ctx, v_x, g_all, m_all, p_all, a16)


def _s5_prep_kernel(lr_ref, lc_ref, bt_ref, ct_ref, d_ref, m_ref, g_ref, p_ref, a_ref):
    hp = lax.Precision.HIGHEST
    n_s, lanes = S5_SUB, S5_SUB * S5_GC
    lane128 = lax.broadcasted_iota(jnp.int32, (n_s, 128), 1)
    blk_of_lane = lax.broadcasted_iota(jnp.int32, (128, lanes), 1) // S5_GC
    g_types, p_types, a16, k_rows = [], [], [], []
    for d in range(2):
        a_r, a_i, log_dt = lr_ref[d, 0:1, :], lr_ref[d, 1:2, :], lr_ref[d, 2:3, :]
        dt = jnp.exp(log_dt)
        lam_r, lam_i = a_r * dt, a_i * dt
        steps = lax.broadcasted_iota(jnp.int32, (24, 128), 0).astype(F32)
        mag = jnp.exp(lam_r * steps)
        pw_r, pw_i = mag * jnp.cos(lam_i * steps), mag * jnp.sin(lam_i * steps)
        nr, ni = pw_r[1:2] - 1.0, pw_i[1:2]
        den = a_r * a_r + a_i * a_i
        co_r, co_i = (nr * a_r + ni * a_i) / den, (ni * a_r - nr * a_i) / den
        b_r = jnp.concatenate([bt_ref[d, 0]] * n_s, axis=0)
        b_i = jnp.concatenate([bt_ref[d, 1]] * n_s, axis=0)
        bb_r, bb_i = co_r * b_r - co_i * b_i, co_r * b_i + co_i * b_r
        order = [n_s - 1 - i for i in range(n_s)] if d == 0 else list(range(n_s))
        pg_r = jnp.concatenate([jnp.broadcast_to(pw_r[n:n + 1], (S5_GC, 128)) for n in order], axis=0)
        pg_i = jnp.concatenate([jnp.broadcast_to(pw_i[n:n + 1], (S5_GC, 128)) for n in order], axis=0)
        g_types += [bb_r * pg_r - bb_i * pg_i, bb_r * pg_i + bb_i * pg_r]
        a16 += [pw_r[n_s:n_s + 1], pw_i[n_s:n_s + 1]]
        x0 = slice((n_s - 1) * S5_GC, n_s * S5_GC) if d == 0 else slice(0, S5_GC)
        x_r, x_i = bb_r[x0], bb_i[x0]
        lhs_r = jnp.concatenate([jnp.where(lane128 < S5_STATE, x_r, 0.0),
                                 jnp.where(lane128 < S5_STATE, 0.0, x_r)], axis=0)
        lhs_i = jnp.concatenate([jnp.where(lane128 < S5_STATE, x_i, 0.0),
                                 jnp.where(lane128 < S5_STATE, 0.0, x_i)], axis=0)
        dt_c = jnp.exp(lc_ref[d, 2])
        lam_rc, lam_ic = lc_ref[d, 0] * dt_c, lc_ref[d, 1] * dt_c
        n_y = (blk_of_lane if d == 0 else n_s - 1 - blk_of_lane).astype(F32)
        mag_y = jnp.exp(lam_rc * n_y)
        ypw_r, ypw_i = mag_y * jnp.cos(lam_ic * n_y), mag_y * jnp.sin(lam_ic * n_y)
        c_r, c_i = ct_ref[d, 0], ct_ref[d, 1]
        y_r, y_i = c_r * ypw_r - c_i * ypw_i, c_r * ypw_i + c_i * ypw_r
        k_rows.append(jnp.dot(lhs_r, y_r, preferred_element_type=F32, precision=hp)
                      - jnp.dot(lhs_i, y_i, preferred_element_type=F32, precision=hp))
        mag_1 = jnp.exp(lam_rc[:, 0:128])
        a1_r, a1_i = mag_1 * jnp.cos(lam_ic[:, 0:128]), mag_1 * jnp.sin(lam_ic[:, 0:128])
        a1_r = jnp.concatenate([a1_r, a1_r], axis=1)
        a1_i = jnp.concatenate([a1_i, a1_i], axis=1)
        p_types += [y_r * a1_r - y_i * a1_i, -(y_r * a1_i + y_i * a1_r)]

    a_ref[...] = jnp.concatenate(a16, axis=1)
    lane_g = lax.broadcasted_iota(jnp.int32, (lanes, 128), 1)
    row_p = lax.broadcasted_iota(jnp.int32, (128, lanes), 0)
    lane_k = lax.broadcasted_iota(jnp.int32, (S5_GC, lanes), 1)
    row_m = lax.broadcasted_iota(jnp.int32, (lanes, lanes), 0)
    lane_m = lax.broadcasted_iota(jnp.int32, (lanes, lanes), 1)
    for h in range(2):
        mine_l = (lane_g >= S5_STATE) == bool(h)
        g_ref[h] = jnp.concatenate([jnp.where(mine_l, t, 0.0) for t in g_types], axis=1).astype(BF16)
        mine_r = (row_p >= S5_STATE) == bool(h)
        p_ref[h] = jnp.concatenate([jnp.where(mine_r, t, 0.0) for t in p_types], axis=0).astype(BF16)
        k_f = k_rows[0][h * S5_GC:(h + 1) * S5_GC]
        k_b = k_rows[1][h * S5_GC:(h + 1) * S5_GC]
        blocks = []
        for i in range(n_s):
            up, down = S5_GC * i, S5_GC * (n_s - 1 - i)
            f = k_f if up == 0 else jnp.where(lane_k >= up, pltpu.roll(k_f, up, axis=1), 0.0)
            b = k_b if down == 0 else jnp.where(lane_k < lanes - down,
                                                pltpu.roll(k_b, lanes - down, axis=1), 0.0)
            blocks.append(f + b)
        m = jnp.concatenate(blocks, axis=0) + jnp.where(row_m == lane_m, d_ref[h], 0.0)
        m_ref[h] = m.astype(BF16)


def _s5_prep(a_re, a_im, log_step, b_re, b_im, c_re, c_im, d_skip):
    n_g, n_p, n_c, n_s = S5_GROUPS, S5_STATE, S5_GC, S5_SUB
    lanes = n_s * n_c
    pair = lambda a: jnp.transpose(a.astype(F32).reshape(2, n_g // 2, 2 * n_p), (1, 0, 2))
    lam_row = jnp.stack([pair(a_re), pair(a_im),
                         pair(jnp.broadcast_to(log_step[..., None], a_re.shape))], axis=2)
    lam_col = jnp.broadcast_to(lam_row[..., None], lam_row.shape + (lanes,))
    bt = lambda b: jnp.transpose(b.astype(F32).reshape(2, n_g // 2, 2, n_p, n_c),
                                 (1, 0, 4, 2, 3)).reshape(n_g // 2, 2, n_c, 2 * n_p)
    b_t = jnp.stack([bt(b_re), bt(b_im)], axis=2)
    ct = lambda c: jnp.tile(jnp.transpose(c.astype(F32).reshape(2, n_g // 2, 2, n_c, n_p),
                                          (1, 0, 2, 4, 3)).reshape(n_g // 2, 2, 2 * n_p, n_c),
                            (1, 1, 1, n_s))
    c_t = jnp.stack([ct(c_re), ct(c_im)], axis=2)
    d_row = jnp.tile(d_skip.astype(F32).reshape(n_g // 2, 2, 1, n_c), (1, 1, 1, n_s))
    blk = lambda *s: pl.BlockSpec((None,) + s, lambda j: (j,) + (0,) * len(s))
    grp = lambda *s: pl.BlockSpec((2,) + s, lambda j: (j,) + (0,) * len(s))
    return pl.pallas_call(
        _s5_prep_kernel,
        grid=(n_g // 2,),
        in_specs=[blk(2, 3, 2 * n_p), blk(2, 3, 2 * n_p, lanes), blk(2, 2, n_c, 2 * n_p),
                  blk(2, 2, 2 * n_p, lanes), blk(2, 1, lanes)],
        out_specs=[grp(lanes, lanes), grp(lanes, 8 * n_p), grp(8 * n_p, lanes), blk(1, 8 * n_p)],
        out_shape=[jax.ShapeDtypeStruct((n_g, lanes, lanes), BF16),
                   jax.ShapeDtypeStruct((n_g, lanes, 8 * n_p), BF16),
                   jax.ShapeDtypeStruct((n_g, 8 * n_p, lanes), BF16),
                   jax.ShapeDtypeStruct((n_g // 2, 1, 8 * n_p), F32)],
        compiler_params=pltpu.CompilerParams(dimension_semantics=("parallel",),
                                             vmem_limit_bytes=VMEM_LIMIT),
        name="s5_prep",
    )(lam_row, lam_col, b_t, c_t, d_row)


def _merge_kernel(hf_ref, hb_ref, o_ref, zm_ref, y_ref, zs_ref, x_ref, mod_ref, mhg_ref,
                  gluw_ref, glub_ref, wout_ref, fg_ref, out_ref, y_sc):
    for g8 in range(W_BRANCH // 128):
        rows = [y_ref[g8 * 8 + gp].astype(F32) for gp in range(8)]
        for i, blk in enumerate(_block_transpose8(rows)):
            y_sc[i * GRID_W:(i + 1) * GRID_W, g8 * 128:(g8 + 1) * 128] = blk

    f32 = lambda ref: ref[...].astype(F32)
    hm = (f32(hf_ref) + f32(hb_ref)) * jax.nn.sigmoid(f32(o_ref))
    mhg = mhg_ref[...]
    parts = []
    for hd in range(HEADS):
        sl = slice(hd * HEAD_DIM, (hd + 1) * HEAD_DIM)
        seg = hm[:, sl]
        mu = jnp.mean(seg, axis=-1, keepdims=True)
        dev = seg - mu
        var = jnp.mean(dev * dev, axis=-1, keepdims=True)
        parts.append(dev * lax.rsqrt(var + NORM_EPS) * mhg[:, sl])
    m_out = jnp.concatenate(parts, axis=-1) * _silu(f32(zm_ref))

    y = y_sc[...]
    gl = 0.5 * y * (1.0 + jnp.tanh(0.7978845608028654 * (y + 0.044715 * (y * y * y))))
    gate = jax.nn.sigmoid(_dot(gl.astype(BF16), gluw_ref[...]) + glub_ref[...])
    s_out = gl * gate * _silu(f32(zs_ref))

    mixed = (_dot(m_out.astype(BF16), wout_ref[0:W_BRANCH, :])
             + _dot(s_out.astype(BF16), wout_ref[W_BRANCH:2 * W_BRANCH, :]))
    xo = x_ref[...] + mod_ref[2:3, :] * mixed
    ms = jnp.mean(xo * xo, axis=-1, keepdims=True)
    out_ref[...] = xo * lax.rsqrt(ms + NORM_EPS) * fg_ref[...]


def _merge(hf, hb, o, zm, y, zs, x, mod, mh_g, glu_w, glu_b, w_out, final_g):
    bsz, t_len, _ = x.shape
    tm = 8 * GRID_W
    tile = pl.BlockSpec((None, tm, D_MODEL), lambda b, i: (b, i, 0))
    y_spec = pl.BlockSpec((S5_GROUPS, GRID_W, 128), lambda b, i: (0, (i // 2) * bsz + b, i % 2))
    const = lambda shape: pl.BlockSpec(shape, lambda b, i: (0,) * len(shape),
                                       pipeline_mode=pl.Buffered(1))
    return pl.pallas_call(
        _merge_kernel,
        grid=(bsz, t_len // tm),
        in_specs=[tile] * 4 + [y_spec, tile, tile,
                               pl.BlockSpec((None, 3, D_MODEL), lambda b, i: (b, 0, 0)),
                               const((1, W_BRANCH)), const((W_BRANCH, W_BRANCH)),
                               const((1, W_BRANCH)), const((2 * W_BRANCH, D_MODEL)),
                               const((1, D_MODEL))],
        out_specs=tile,
        out_shape=jax.ShapeDtypeStruct((bsz, t_len, D_MODEL), F32),
        scratch_shapes=[pltpu.VMEM((tm, W_BRANCH), F32)],
        compiler_params=pltpu.CompilerParams(dimension_semantics=("parallel", "arbitrary"),
                                             vmem_limit_bytes=VMEM_LIMIT),
        name="merge",
    )(hf, hb, o, zm, y, zs, x, mod, mh_g.reshape(1, -1), glu_w, glu_b.reshape(1, -1), w_out,
      final_g.reshape(1, -1))


def _s5_rows_ctx(u):
    bsz, t_len, _ = u.shape
    a = u.reshape(bsz, t_len // S5_SUB, S5_SUB, S5_GROUPS, S5_GC)
    a = jnp.transpose(a, (3, 1, 0, 2, 4))
    return a.reshape(S5_GROUPS, (t_len // S5_SUB) * bsz, S5_SUB * S5_GC)


def kernel(x, c, ctx, c_ctx, norm_g, ada_w, ada_b, w_in, b_gate, conv_qk, mh_g, s5_a_re, s5_a_im,
           s5_log_step, s5_b_re, s5_b_im, s5_c_re, s5_c_im, s5_d, glu_w, glu_b, w_out, final_g):
    bsz, t_len, _ = x.shape
    layer = 0

    cc = jnp.zeros((16, D_MODEL), F32).at[:bsz].set(c).at[bsz].set(c_ctx)
    mod = _ada(cc, ada_w[layer], ada_b[layer]).reshape(16, 3, D_MODEL)
    mod_x = mod[:bsz]
    mod_c = jnp.broadcast_to(mod[bsz][None], (bsz, 3, D_MODEL))

    w = w_in[layer]
    wb = W_BRANCH
    w_qk = w[:, 0:2 * wb].astype(BF16)
    gate0 = 5 * wb
    w_gate = jnp.pad(w[:, gate0:gate0 + N_GATES], ((0, 0), (0, GATE_PAD - N_GATES)))
    w_main = jnp.concatenate([w[:, 2 * wb:5 * wb], w[:, gate0 + N_GATES:], w_gate], axis=1).astype(BF16)
    b_gate_pad = jnp.pad(b_gate[layer].reshape(1, N_GATES), ((0, 0), (0, GATE_PAD - N_GATES)))

    proj = functools.partial(_inproj, norm_g=norm_g[layer], w_qk=w_qk, w_main=w_main,
                             conv_qk=conv_qk[layer], b_gate_pad=b_gate_pad)
    _, kt_c, v_c, _, _, u_c, _, gt_c = proj(ctx, mod_c, tm=256, s5_layout=False)
    q_x, kt_x, v_x, o_x, zm_x, u_x, zs_x, gt_x = proj(x, mod_x, tm=8 * GRID_W, s5_layout=True)

    zero_state = (jnp.zeros((bsz, 2, HEADS, HEAD_DIM, HEAD_DIM), F32),
                  jnp.zeros((bsz, 2, HEADS, HEAD_DIM, 128), F32),
                  jnp.zeros((bsz, 2, HEADS, 128), F32))
    _, ctx_state = _mlstm(None, kt_c, v_c, gt_c, zero_state, with_output=False)
    (h_f, h_b), _ = _mlstm(q_x, kt_x, v_x, gt_x, ctx_state, with_output=True)

    m_all, g_all, p_all, a16 = _s5_prep(
        s5_a_re[layer], s5_a_im[layer], s5_log_step[layer], s5_b_re[layer], s5_b_im[layer],
        s5_c_re[layer], s5_c_im[layer], s5_d[layer])
    y_x = _s5(_s5_rows_ctx(u_c), u_x, g_all, m_all, p_all, a16, bsz)

    return _merge(h_f, h_b, o_x, zm_x, y_x, zs_x, x, mod_x, mh_g[layer], glu_w[layer].astype(BF16),
                  glu_b[layer], w_out[layer].astype(BF16), final_g)
```

```python
import functools

import jax
import jax.numpy as jnp
from jax import lax
from jax.experimental import pallas as pl
from jax.experimental.pallas import tpu as pltpu

F32 = jnp.float32
BF16 = jnp.bfloat16

D_MODEL = 1024
HEADS = 4
HEAD_DIM = 256
W_BRANCH = 1024
S5_GROUPS = 64
S5_GC = 16
S5_STATE = 64
S5_SUB = 16
GRID_W = 64
S5_ROW_PITCH = 72
N_GATES = 16
GATE_PAD = 128
CONV_W = 3
NORM_EPS = 1e-6
MLSTM_CHUNK = 256
NEG_BIG = -1e30

VMEM_LIMIT = 56 * 1024 * 1024


def _silu(a):
    return a * jax.nn.sigmoid(a)


def _log_sigmoid(a):
    return jnp.minimum(a, 0.0) - jnp.log1p(jnp.exp(-jnp.abs(a)))


def _dot(a, b):
    return jnp.dot(a, b, preferred_element_type=F32)


def _split_bf16(a):
    hi = a.astype(BF16)
    lo = (a - hi.astype(F32)).astype(BF16)
    return hi, lo


def _block_transpose8(rows):
    lane = lax.broadcasted_iota(jnp.int32, rows[0].shape, 1)
    blk = lane // S5_GC
    for d in (4, 2, 1):
        keep = (blk & d) == 0
        new = list(rows)
        for i in range(8):
            if i & d == 0:
                a, b = rows[i], rows[i + d]
                new[i] = jnp.where(keep, a, pltpu.roll(b, d * S5_GC, axis=1))
                new[i + d] = jnp.where(keep, pltpu.roll(a, 128 - d * S5_GC, axis=1), b)
        rows = new
    return rows


def _ada_kernel(c_ref, w_ref, b_ref, o_ref):
    s = _silu(c_ref[...])
    o_ref[...] = jnp.dot(s, w_ref[...], preferred_element_type=F32,
                         precision=lax.Precision.HIGHEST) + b_ref[...]


def _ada(cc, ada_w, ada_b):
    rows = cc.shape[0]
    n_out = ada_w.shape[1]
    tn = 1024
    return pl.pallas_call(
        _ada_kernel,
        grid=(n_out // tn,),
        in_specs=[pl.BlockSpec((rows, D_MODEL), lambda j: (0, 0)),
                  pl.BlockSpec((D_MODEL, tn), lambda j: (0, j)),
                  pl.BlockSpec((1, tn), lambda j: (0, j))],
        out_specs=pl.BlockSpec((rows, tn), lambda j: (0, j)),
        out_shape=jax.ShapeDtypeStruct((rows, n_out), F32),
        compiler_params=pltpu.CompilerParams(dimension_semantics=("arbitrary",),
                                             vmem_limit_bytes=VMEM_LIMIT),
        name="ada",
    )(cc, ada_w, ada_b.reshape(1, n_out))


HALO = 16


def _inproj_kernel(x_ref, xp_ref, xn_ref, mod_ref, ng_ref, wqk_ref, wm_ref, conv_ref, bg_ref,
                   *rest, tm, s5_layout, state_only):
    if state_only:
        kt_ref, v_ref, u_ref, gt_ref, lhs_sc, z_sc = rest
    else:
        q_ref, kt_ref, v_ref, o_ref, zm_ref, u_ref, zs_ref, gt_ref, lhs_sc, z_sc = rest
    i = pl.program_id(1)
    nt = pl.num_programs(1)
    shift = mod_ref[0:1, :]
    scale = mod_ref[1:2, :]
    ng = ng_ref[...]

    def norm_mod(a):
        ms = jnp.mean(a * a, axis=-1, keepdims=True)
        return (a * lax.rsqrt(ms + NORM_EPS) * ng) * (1.0 + scale) + shift

    lhs_sc[0:tm, :] = norm_mod(x_ref[...]).astype(BF16)
    lhs_sc[tm:tm + HALO, :] = norm_mod(xp_ref[...]).astype(BF16)
    lhs_sc[tm + HALO:tm + 2 * HALO, :] = norm_mod(xn_ref[...]).astype(BF16)

    valid_prev = (i > 0).astype(F32)
    valid_next = (i < nt - 1).astype(F32)
    for half in range(1 if state_only else 0, 2):
        cols = slice(half * W_BRANCH, (half + 1) * W_BRANCH)
        z = _dot(lhs_sc[...], wqk_ref[:, cols])
        z_sc[8:8 + tm, :] = z[0:tm]
        z_sc[0:8, :] = z[tm + HALO - 8:tm + HALO] * valid_prev
        z_sc[8 + tm:16 + tm, :] = z[tm + HALO:tm + HALO + 8] * valid_next
        cw = conv_ref[:, cols]
        conv = (cw[0:1] * z_sc[7:7 + tm, :] + cw[1:2] * z_sc[8:8 + tm, :]
                + cw[2:3] * z_sc[9:9 + tm, :])
        if half == 0:
            q_ref[...] = _silu(conv).astype(BF16)
        else:
            kt_ref[...] = (_silu(conv) * HEAD_DIM ** -0.5).T.astype(BF16)

    h = lhs_sc[0:tm, :]
    v_ref[...] = _dot(h, wm_ref[:, 0:1024]).astype(BF16)
    if not state_only:
        o_ref[...] = _dot(h, wm_ref[:, 1024:2048]).astype(BF16)
        zm_ref[...] = _dot(h, wm_ref[:, 2048:3072]).astype(BF16)
        zs_ref[...] = _dot(h, wm_ref[:, 4096:5120]).astype(BF16)
    u = _dot(h, wm_ref[:, 3072:4096])
    if s5_layout:
        for g8 in range(W_BRANCH // 128):
            rows = [u[i * GRID_W:(i + 1) * GRID_W, g8 * 128:(g8 + 1) * 128] for i in range(8)]
            for gp, blk in enumerate(_block_transpose8(rows)):
                u_ref[g8 * 8 + gp] = blk.astype(BF16)
    else:
        u_ref[...] = u.astype(BF16)
    zg = _dot(h, wm_ref[:, 5120:5120 + GATE_PAD]) + bg_ref[...]
    lane = lax.broadcasted_iota(jnp.int32, zg.shape, 1)
    is_forget = jnp.logical_and(lane < N_GATES, (lane % 8) >= 4)
    gates = jnp.where(is_forget, _log_sigmoid(zg), zg)
    gt_ref[...] = gates.T[0:N_GATES, :]


def _inproj(x, mod, norm_g, w_qk, w_main, conv_qk, b_gate_pad, tm, s5_layout, state_only=False):
    bsz, t_len, _ = x.shape
    nt = t_len // tm
    nhb = t_len // HALO
    tok = lambda w, dt: jax.ShapeDtypeStruct((bsz, t_len, w), dt)
    tile = lambda w: pl.BlockSpec((None, tm, w), lambda b, i: (b, i, 0))
    tile_t = lambda w: pl.BlockSpec((None, w, tm), lambda b, i: (b, 0, i))
    const = lambda shape: pl.BlockSpec(shape, lambda b, i: (0,) * len(shape),
                                       pipeline_mode=pl.Buffered(1))
    if s5_layout:
        assert tm == 8 * GRID_W and t_len % (2 * tm) == 0
        u_spec = pl.BlockSpec((S5_GROUPS, GRID_W, 128), lambda b, i: (0, (i // 2) * bsz + b, i % 2))
        u_shape = jax.ShapeDtypeStruct((S5_GROUPS, (t_len // (2 * tm)) * bsz * GRID_W, 256), BF16)
    else:
        u_spec, u_shape = tile(W_BRANCH), tok(W_BRANCH, BF16)
    kt_shape = jax.ShapeDtypeStruct((bsz, W_BRANCH, t_len), BF16)
    gt_shape = jax.ShapeDtypeStruct((bsz, N_GATES, t_len), F32)
    if state_only:
        out_specs = [tile_t(W_BRANCH), tile(W_BRANCH), u_spec, tile_t(N_GATES)]
        out_shape = [kt_shape, tok(W_BRANCH, BF16), u_shape, gt_shape]
    else:
        out_specs = [tile(W_BRANCH), tile_t(W_BRANCH)] + [tile(W_BRANCH)] * 3 + [
            u_spec, tile(W_BRANCH), tile_t(N_GATES)]
        out_shape = [tok(W_BRANCH, BF16), kt_shape] + [tok(W_BRANCH, BF16)] * 3 + [
            u_shape, tok(W_BRANCH, BF16), gt_shape]
    return pl.pallas_call(
        functools.partial(_inproj_kernel, tm=tm, s5_layout=s5_layout, state_only=state_only),
        grid=(bsz, nt),
        in_specs=[
            tile(D_MODEL),
            pl.BlockSpec((None, HALO, D_MODEL),
                         lambda b, i: (b, jnp.maximum(i * (tm // HALO) - 1, 0), 0)),
            pl.BlockSpec((None, HALO, D_MODEL),
                         lambda b, i: (b, jnp.minimum((i + 1) * (tm // HALO), nhb - 1), 0)),
            pl.BlockSpec((None, 3, D_MODEL), lambda b, i: (b, 0, 0)),
            const((1, D_MODEL)),
            const((D_MODEL, 2 * W_BRANCH)),
            const((D_MODEL, 5 * W_BRANCH + GATE_PAD)),
            const((CONV_W, 2 * W_BRANCH)),
            const((1, GATE_PAD)),
        ],
        out_specs=out_specs,
        out_shape=out_shape,
        scratch_shapes=[pltpu.VMEM((tm + 2 * HALO, D_MODEL), BF16),
                        pltpu.VMEM((tm + 16, W_BRANCH), F32)],
        compiler_params=pltpu.CompilerParams(dimension_semantics=("parallel", "arbitrary"),
                                             vmem_limit_bytes=VMEM_LIMIT),
        name="inproj",
    )(x, x, x, mod, norm_g.reshape(1, D_MODEL), w_qk, w_main, conv_qk, b_gate_pad)


def _mlstm_direction(q_ref, kt_ref, v_ref, gt_ref, h_ref, c_sc, n_sc, m_sc, *, reverse, chunk):
    with_output = h_ref is not None
    row = lax.broadcasted_iota(jnp.int32, (chunk, chunk), 0)
    col = lax.broadcasted_iota(jnp.int32, (chunk, chunk), 1)
    if reverse:
        mask_ts, mask_st = col >= row, row >= col
    else:
        mask_ts, mask_st = col <= row, row <= col
    tri_st = jnp.where(mask_st, 1.0, 0.0).astype(BF16)

    gt = gt_ref[...]
    gt_hi, gt_lo = _split_bf16(gt)
    cum_row = _dot(gt_hi, tri_st) + _dot(gt_lo, tri_st)
    li0 = 8 if reverse else 0
    lf0 = li0 + HEADS
    last = 0 if reverse else chunk - 1
    li_r = gt[li0:li0 + HEADS, :]
    b_r = cum_row[lf0:lf0 + HEADS, :]
    b_last = b_r[:, last:last + 1]
    m_old = m_sc[:, 0:1]
    g_r = b_last - b_r + li_r
    m_new = jnp.maximum(b_last + m_old, jnp.max(g_r, axis=1, keepdims=True))
    decay = jnp.exp(b_last + m_old - m_new)
    k_scale = jnp.exp(g_r - m_new).astype(BF16)
    ones_rows = jnp.ones((8, chunk), BF16)
    wide = lambda a, n: jnp.concatenate([a] * (n // 128), axis=1)

    if with_output:
        a_r = li_r - b_r
        lane = lax.broadcasted_iota(jnp.int32, a_r.shape, 1)
        run_max = a_r
        shift = 1
        while shift < chunk:
            if reverse:
                moved = jnp.where(lane < chunk - shift, pltpu.roll(run_max, chunk - shift, axis=1), NEG_BIG)
            else:
                moved = jnp.where(lane >= shift, pltpu.roll(run_max, shift, axis=1), NEG_BIG)
            run_max = jnp.maximum(run_max, moved)
            shift *= 2
        mm_r = jnp.maximum(run_max, m_old).astype(BF16)
        b_hi, b_lo = _split_bf16(b_r)
        rows = jnp.concatenate([mm_r, b_hi, b_lo, jnp.zeros_like(b_hi)], axis=0)
        sel_r = lax.broadcasted_iota(jnp.int32, (4 * HEADS, 2 * HEADS * 128), 0)
        sel_c = lax.broadcasted_iota(jnp.int32, (4 * HEADS, 2 * HEADS * 128), 1) // 128
        head_r = sel_r % HEADS
        pick = jnp.logical_or(jnp.logical_and(sel_r < HEADS, sel_c % HEADS == head_r),
                              jnp.logical_and(jnp.logical_and(sel_r >= HEADS, sel_r < 3 * HEADS),
                                              sel_c == HEADS + head_r))
        sel = jnp.where(pick, 1.0, 0.0).astype(BF16)
        cols = lax.dot_general(rows, sel, (((0,), (0,)), ((), ())), preferred_element_type=F32)

    for hd in range(HEADS):
        sl = slice(hd * HEAD_DIM, (hd + 1) * HEAD_DIM)
        kt = kt_ref[sl, :]
        v = v_ref[:, sl]
        c_old = c_sc[hd]
        n_old = n_sc[hd]
        kw_t = kt * k_scale[hd:hd + 1, :]
        if with_output:
            q = q_ref[:, sl]
            mm_c = cols[:, hd * 128:(hd + 1) * 128]
            bm_c = cols[:, (HEADS + hd) * 128:(HEADS + hd + 1) * 128]
            decay_mat = jnp.exp(jnp.where(mask_ts, a_r[hd:hd + 1, :] - wide(mm_c, chunk), NEG_BIG))
            s_f = _dot(q, kt) * decay_mat
            w_inter = jnp.exp(m_old[hd:hd + 1, :] - mm_c)
            den = (w_inter * jnp.sum(q.astype(F32) * n_old[0:1, :], axis=1, keepdims=True)
                   + jnp.sum(s_f, axis=1, keepdims=True))
            q_w = q * wide(w_inter, HEAD_DIM).astype(BF16)
            num = _dot(jnp.concatenate([q_w, s_f.astype(BF16)], axis=1),
                       jnp.concatenate([c_old.astype(BF16), v], axis=0))
            inv = 1.0 / jnp.maximum(jnp.abs(den), jnp.exp(-bm_c))
            h_ref[:, sl] = (num * wide(inv, HEAD_DIM)).astype(BF16)
        c_sc[hd] = decay[hd:hd + 1, :] * c_old + _dot(kw_t, v)
        n_sc[hd] = decay[hd:hd + 1, :] * n_old + lax.dot_general(
            ones_rows, kw_t, (((1,), (1,)), ((), ())), preferred_element_type=F32)
        m_sc[hd:hd + 1, :] = jnp.broadcast_to(m_new[hd:hd + 1, :], (1, 128))


def _mlstm_kernel(*refs, with_output, chunk):
    n_in = 4 if with_output else 3
    ins = [refs[0:n_in], refs[n_in:2 * n_in]]
    c0_ref, n0_ref, m0_ref = refs[2 * n_in:2 * n_in + 3]
    rest = refs[2 * n_in + 3:]
    if with_output:
        h_refs, rest = rest[0:2], rest[2:]
    else:
        h_refs = (None, None)
        ins = [(None,) + tuple(r) for r in ins]
    co_ref, no_ref, mo_ref, c_sc, n_sc, m_sc = rest
    i = pl.program_id(1)
    nc = pl.num_programs(1)

    @pl.when(i == 0)
    def _():
        c_sc[...] = c0_ref[...]
        n_sc[...] = n0_ref[...]
        m_sc[...] = m0_ref[...]

    for d in range(2):
        _mlstm_direction(*ins[d], h_refs[d], c_sc.at[d], n_sc.at[d], m_sc.at[d],
                         reverse=bool(d), chunk=chunk)

    @pl.when(i == nc - 1)
    def _():
        co_ref[...] = c_sc[...]
        no_ref[...] = n_sc[...]
        mo_ref[...] = m_sc[...]


def _mlstm(q, kt, v, gt, state, with_output):
    bsz, t_len, _ = v.shape
    chunk = MLSTM_CHUNK
    nc = t_len // chunk
    cidx = (lambda i: i, lambda i: nc - 1 - i)
    tile = lambda w, d: pl.BlockSpec((None, chunk, w), lambda b, i: (b, cidx[d](i), 0))
    tile_t = lambda w, d: pl.BlockSpec((None, w, chunk), lambda b, i: (b, 0, cidx[d](i)))
    st_dims = [(2, HEADS, HEAD_DIM, HEAD_DIM), (2, HEADS, 8, HEAD_DIM), (2, HEADS, 128)]
    st_specs = [pl.BlockSpec((None,) + s, lambda b, i, n=len(s): (b,) + (0,) * n) for s in st_dims]
    st_shapes = [jax.ShapeDtypeStruct((bsz,) + s, F32) for s in st_dims]
    in_specs, args = [], []
    for d in range(2):
        in_specs += ([tile(W_BRANCH, d)] if with_output else []) + [
            tile_t(W_BRANCH, d), tile(W_BRANCH, d), tile_t(N_GATES, d)]
        args += ([q] if with_output else []) + [kt, v, gt]
    out_specs, out_shape = list(st_specs), list(st_shapes)
    if with_output:
        out_specs = [tile(W_BRANCH, 0), tile(W_BRANCH, 1)] + out_specs
        out_shape = [jax.ShapeDtypeStruct((bsz, t_len, W_BRANCH), BF16)] * 2 + out_shape
    outs = pl.pallas_call(
        functools.partial(_mlstm_kernel, with_output=with_output, chunk=chunk),
        grid=(bsz, nc),
        in_specs=in_specs + st_specs,
        out_specs=out_specs,
        out_shape=out_shape,
        scratch_shapes=[pltpu.VMEM(s, F32) for s in st_dims],
        compiler_params=pltpu.CompilerParams(dimension_semantics=("parallel", "arbitrary"),
                                             vmem_limit_bytes=VMEM_LIMIT),
        name="mlstm_out" if with_output else "mlstm_state",
    )(*args, *state)
    if with_output:
        return (outs[0], outs[1]), tuple(outs[2:])
    return None, tuple(outs)


def _s5_kernel(vc_ref, vx_ref, g_ref, m_ref, p_ref, a_ref, y_ref, gu_sc, s_sc, *,
               nk_ctx, n_rc, bsz, rblk):
    rows_ctx = nk_ctx * bsz
    rows_x = n_rc * bsz * GRID_W

    def increments(v_ref, r0, r1):
        return _dot(v_ref[0, r0:r1, :], g_ref[0]) + _dot(v_ref[1, r0:r1, :], g_ref[1])

    inc = increments(vc_ref, 0, rows_ctx)
    for comp in range(4):
        gu_sc[comp, 0:rows_ctx, :] = inc[:, comp * 128:(comp + 1) * 128]
    for r0 in range(0, rows_x, rblk):
        inc = increments(vx_ref, r0, r0 + rblk)
        for run in range(rblk // GRID_W):
            dst = rows_ctx + (r0 // GRID_W + run) * S5_ROW_PITCH
            for comp in range(4):
                gu_sc[comp, dst:dst + GRID_W, :] = (
                    inc[run * GRID_W:(run + 1) * GRID_W, comp * 128:(comp + 1) * 128])

    a = a_ref[...]
    a_pow = [jnp.broadcast_to(a[:, comp * 128:(comp + 1) * 128], (bsz, 128)) for comp in range(4)]
    zero = jnp.zeros((bsz, 128), F32)

    def cmul(x_r, x_i, y_r, y_i):
        return x_r * y_r - x_i * y_i, x_r * y_i + x_i * y_r

    a_sq = [cmul(a_pow[2 * d], a_pow[2 * d + 1], a_pow[2 * d], a_pow[2 * d + 1]) for d in range(2)]

    def step(rows, carry, direction):
        s_r, s_i = carry
        inc_r = gu_sc[2 * direction, rows, :]
        inc_i = gu_sc[2 * direction + 1, rows, :]
        gu_sc[2 * direction, rows, :] = s_r
        gu_sc[2 * direction + 1, rows, :] = s_i
        p_r, p_i = cmul(a_pow[2 * direction], a_pow[2 * direction + 1], s_r, s_i)
        return p_r + inc_r, p_i + inc_i

    def step2(rows0, rows1, carry, direction):
        s_r, s_i = carry
        a_r, a_i = a_pow[2 * direction], a_pow[2 * direction + 1]
        inc0_r, inc0_i = gu_sc[2 * direction, rows0, :], gu_sc[2 * direction + 1, rows0, :]
        inc1_r, inc1_i = gu_sc[2 * direction, rows1, :], gu_sc[2 * direction + 1, rows1, :]
        gu_sc[2 * direction, rows0, :] = s_r
        gu_sc[2 * direction + 1, rows0, :] = s_i
        m_r, m_i = cmul(a_r, a_i, s_r, s_i)
        gu_sc[2 * direction, rows1, :] = m_r + inc0_r
        gu_sc[2 * direction + 1, rows1, :] = m_i + inc0_i
        c_r, c_i = cmul(a_r, a_i, inc0_r, inc0_i)
        q_r, q_i = cmul(a_sq[direction][0], a_sq[direction][1], s_r, s_i)
        return q_r + (c_r + inc1_r), q_i + (c_i + inc1_i)

    def ctx_rows(k):
        return pl.ds(pl.multiple_of(k * bsz, bsz), bsz)

    def x_rows(w, rc):
        return pl.ds(rows_ctx + rc * (bsz * S5_ROW_PITCH) + w, bsz, stride=S5_ROW_PITCH)

    def ctx_body(k, carry):
        return step(ctx_rows(k), carry[0], 0), step(ctx_rows(nk_ctx - 1 - k), carry[1], 1)

    def x_body(w, carry):
        c_f, c_b = carry
        w_b = GRID_W - 1 - w
        for rc in range(0, n_rc - 1, 2):
            c_f = step2(x_rows(w, rc), x_rows(w, rc + 1), c_f, 0)
            c_b = step2(x_rows(w_b, n_rc - 1 - rc), x_rows(w_b, n_rc - 2 - rc), c_b, 1)
        if n_rc % 2:
            c_f = step(x_rows(w, n_rc - 1), c_f, 0)
            c_b = step(x_rows(w_b, 0), c_b, 1)
        return c_f, c_b

    carry = lax.fori_loop(0, nk_ctx, ctx_body, ((zero, zero), (zero, zero)))
    lax.fori_loop(0, GRID_W, x_body, carry)

    for r0 in range(0, rows_x, rblk):
        r1 = r0 + rblk
        for run in range(rblk // GRID_W):
            src = rows_ctx + (r0 // GRID_W + run) * S5_ROW_PITCH
            for comp in range(4):
                s_sc[r0 + run * GRID_W:r0 + (run + 1) * GRID_W, comp * 128:(comp + 1) * 128] = (
                    gu_sc[comp, src:src + GRID_W, :].astype(BF16))
        for gg in range(2):
            y_ref[gg, r0:r1, :] = (_dot(vx_ref[gg, r0:r1, :], m_ref[gg])
                                   + _dot(s_sc[r0:r1, :], p_ref[gg])).astype(BF16)


def _s5(v_ctx, v_x, g_all, m_all, p_all, a16, bsz):
    rows_ctx, rows_x = v_ctx.shape[1], v_x.shape[1]
    lanes = S5_SUB * S5_GC
    return pl.pallas_call(
        functools.partial(_s5_kernel, nk_ctx=rows_ctx // bsz, n_rc=rows_x // (bsz * GRID_W),
                          bsz=bsz, rblk=512),
        grid=(S5_GROUPS // 2,),
        in_specs=[pl.BlockSpec((2, rows_ctx, lanes), lambda j: (j, 0, 0)),
                  pl.BlockSpec((2, rows_x, lanes), lambda j: (j, 0, 0)),
                  pl.BlockSpec((2, lanes, 512), lambda j: (j, 0, 0)),
                  pl.BlockSpec((2, lanes, lanes), lambda j: (j, 0, 0)),
                  pl.BlockSpec((2, 512, lanes), lambda j: (j, 0, 0)),
                  pl.BlockSpec((None, 1, 512), lambda j: (j, 0, 0))],
        out_specs=pl.BlockSpec((2, rows_x, lanes), lambda j: (j, 0, 0)),
        out_shape=jax.ShapeDtypeStruct((S5_GROUPS, rows_x, lanes), BF16),
        scratch_shapes=[pltpu.VMEM((4, rows_ctx + (rows_x // GRID_W) * S5_ROW_PITCH, 128), F32),
                        pltpu.VMEM((rows_x, 512), BF16)],
        compiler_params=pltpu.CompilerParams(dimension_semantics=("parallel",),
                                             vmem_limit_bytes=VMEM_LIMIT),
        name="s5",
    )(v_ctx, v_x, g_all, m_all, p_all, a16)


def _s5_prep_kernel(lr_ref, lc_ref, bt_ref, ct_ref, d_ref, m_ref, g_ref, p_ref, a_ref):
    hp = lax.Precision.HIGHEST
    n_s, lanes = S5_SUB, S5_SUB * S5_GC
    lane128 = lax.broadcasted_iota(jnp.int32, (n_s, 128), 1)
    blk_of_lane = lax.broadcasted_iota(jnp.int32, (128, lanes), 1) // S5_GC
    g_types, p_types, a16, k_rows = [], [], [], []
    for d in range(2):
        a_r, a_i, log_dt = lr_ref[d, 0:1, :], lr_ref[d, 1:2, :], lr_ref[d, 2:3, :]
        dt = jnp.exp(log_dt)
        lam_r, lam_i = a_r * dt, a_i * dt
        steps = lax.broadcasted_iota(jnp.int32, (24, 128), 0).astype(F32)
        mag = jnp.exp(lam_r * steps)
        pw_r, pw_i = mag * jnp.cos(lam_i * steps), mag * jnp.sin(lam_i * steps)
        nr, ni = pw_r[1:2] - 1.0, pw_i[1:2]
        den = a_r * a_r + a_i * a_i
        co_r, co_i = (nr * a_r + ni * a_i) / den, (ni * a_r - nr * a_i) / den
        b_r = jnp.concatenate([bt_ref[d, 0]] * n_s, axis=0)
        b_i = jnp.concatenate([bt_ref[d, 1]] * n_s, axis=0)
        bb_r, bb_i = co_r * b_r - co_i * b_i, co_r * b_i + co_i * b_r
        order = [n_s - 1 - i for i in range(n_s)] if d == 0 else list(range(n_s))
        pg_r = jnp.concatenate([jnp.broadcast_to(pw_r[n:n + 1], (S5_GC, 128)) for n in order], axis=0)
        pg_i = jnp.concatenate([jnp.broadcast_to(pw_i[n:n + 1], (S5_GC, 128)) for n in order], axis=0)
        g_types += [bb_r * pg_r - bb_i * pg_i, bb_r * pg_i + bb_i * pg_r]
        a16 += [pw_r[n_s:n_s + 1], pw_i[n_s:n_s + 1]]
        x0 = slice((n_s - 1) * S5_GC, n_s * S5_GC) if d == 0 else slice(0, S5_GC)
        x_r, x_i = bb_r[x0], bb_i[x0]
        lhs_r = jnp.concatenate([jnp.where(lane128 < S5_STATE, x_r, 0.0),
                                 jnp.where(lane128 < S5_STATE, 0.0, x_r)], axis=0)
        lhs_i = jnp.concatenate([jnp.where(lane128 < S5_STATE, x_i, 0.0),
                                 jnp.where(lane128 < S5_STATE, 0.0, x_i)], axis=0)
        dt_c = jnp.exp(lc_ref[d, 2])
        lam_rc, lam_ic = lc_ref[d, 0] * dt_c, lc_ref[d, 1] * dt_c
        n_y = (blk_of_lane if d == 0 else n_s - 1 - blk_of_lane).astype(F32)
        mag_y = jnp.exp(lam_rc * n_y)
        ypw_r, ypw_i = mag_y * jnp.cos(lam_ic * n_y), mag_y * jnp.sin(lam_ic * n_y)
        c_r, c_i = ct_ref[d, 0], ct_ref[d, 1]
        y_r, y_i = c_r * ypw_r - c_i * ypw_i, c_r * ypw_i + c_i * ypw_r
        k_rows.append(jnp.dot(lhs_r, y_r, preferred_element_type=F32, precision=hp)
                      - jnp.dot(lhs_i, y_i, preferred_element_type=F32, precision=hp))
        mag_1 = jnp.exp(lam_rc[:, 0:128])
        a1_r, a1_i = mag_1 * jnp.cos(lam_ic[:, 0:128]), mag_1 * jnp.sin(lam_ic[:, 0:128])
        a1_r = jnp.concatenate([a1_r, a1_r], axis=1)
        a1_i = jnp.concatenate([a1_i, a1_i], axis=1)
        p_types += [y_r * a1_r - y_i * a1_i, -(y_r * a1_i + y_i * a1_r)]

    a_ref[...] = jnp.concatenate(a16, axis=1)
    lane_g = lax.broadcasted_iota(jnp.int32, (lanes, 128), 1)
    row_p = lax.broadcasted_iota(jnp.int32, (128, lanes), 0)
    lane_k = lax.broadcasted_iota(jnp.int32, (S5_GC, lanes), 1)
    row_m = lax.broadcasted_iota(jnp.int32, (lanes, lanes), 0)
    lane_m = lax.broadcasted_iota(jnp.int32, (lanes, lanes), 1)
    for h in range(2):
        mine_l = (lane_g >= S5_STATE) == bool(h)
        g_ref[h] = jnp.concatenate([jnp.where(mine_l, t, 0.0) for t in g_types], axis=1).astype(BF16)
        mine_r = (row_p >= S5_STATE) == bool(h)
        p_ref[h] = jnp.concatenate([jnp.where(mine_r, t, 0.0) for t in p_types], axis=0).astype(BF16)
        k_f = k_rows[0][h * S5_GC:(h + 1) * S5_GC]
        k_b = k_rows[1][h * S5_GC:(h + 1) * S5_GC]
        blocks = []
        for i in range(n_s):
            up, down = S5_GC * i, S5_GC * (n_s - 1 - i)
            f = k_f if up == 0 else jnp.where(lane_k >= up, pltpu.roll(k_f, up, axis=1), 0.0)
            b = k_b if down == 0 else jnp.where(lane_k < lanes - down,
                                                pltpu.roll(k_b, lanes - down, axis=1), 0.0)
            blocks.append(f + b)
        m = jnp.concatenate(blocks, axis=0) + jnp.where(row_m == lane_m, d_ref[h], 0.0)
        m_ref[h] = m.astype(BF16)


def _s5_prep(a_re, a_im, log_step, b_re, b_im, c_re, c_im, d_skip):
    n_g, n_p, n_c, n_s = S5_GROUPS, S5_STATE, S5_GC, S5_SUB
    lanes = n_s * n_c
    pair = lambda a: jnp.transpose(a.astype(F32).reshape(2, n_g // 2, 2 * n_p), (1, 0, 2))
    lam_row = jnp.stack([pair(a_re), pair(a_im),
                         pair(jnp.broadcast_to(log_step[..., None], a_re.shape))], axis=2)
    lam_col = jnp.broadcast_to(lam_row[..., None], lam_row.shape + (lanes,))
    bt = lambda b: jnp.transpose(b.astype(F32).reshape(2, n_g // 2, 2, n_p, n_c),
                                 (1, 0, 4, 2, 3)).reshape(n_g // 2, 2, n_c, 2 * n_p)
    b_t = jnp.stack([bt(b_re), bt(b_im)], axis=2)
    ct = lambda c: jnp.tile(jnp.transpose(c.astype(F32).reshape(2, n_g // 2, 2, n_c, n_p),
                                          (1, 0, 2, 4, 3)).reshape(n_g // 2, 2, 2 * n_p, n_c),
                            (1, 1, 1, n_s))
    c_t = jnp.stack([ct(c_re), ct(c_im)], axis=2)
    d_row = jnp.tile(d_skip.astype(F32).reshape(n_g // 2, 2, 1, n_c), (1, 1, 1, n_s))
    blk = lambda *s: pl.BlockSpec((None,) + s, lambda j: (j,) + (0,) * len(s))
    grp = lambda *s: pl.BlockSpec((2,) + s, lambda j: (j,) + (0,) * len(s))
    return pl.pallas_call(
        _s5_prep_kernel,
        grid=(n_g // 2,),
        in_specs=[blk(2, 3, 2 * n_p), blk(2, 3, 2 * n_p, lanes), blk(2, 2, n_c, 2 * n_p),
                  blk(2, 2, 2 * n_p, lanes), blk(2, 1, lanes)],
        out_specs=[grp(lanes, lanes), grp(lanes, 8 * n_p), grp(8 * n_p, lanes), blk(1, 8 * n_p)],
        out_shape=[jax.ShapeDtypeStruct((n_g, lanes, lanes), BF16),
                   jax.ShapeDtypeStruct((n_g, lanes, 8 * n_p), BF16),
                   jax.ShapeDtypeStruct((n_g, 8 * n_p, lanes), BF16),
                   jax.ShapeDtypeStruct((n_g // 2, 1, 8 * n_p), F32)],
        compiler_params=pltpu.CompilerParams(dimension_semantics=("parallel",),
                                             vmem_limit_bytes=VMEM_LIMIT),
        name="s5_prep",
    )(lam_row, lam_col, b_t, c_t, d_row)


def _merge_kernel(hf_ref, hb_ref, o_ref, zm_ref, y_ref, zs_ref, x_ref, mod_ref, mhg_ref,
                  gluw_ref, glub_ref, wout_ref, fg_ref, out_ref, y_sc):
    for g8 in range(W_BRANCH // 128):
        rows = [y_ref[g8 * 8 + gp].astype(F32) for gp in range(8)]
        for i, blk in enumerate(_block_transpose8(rows)):
            y_sc[i * GRID_W:(i + 1) * GRID_W, g8 * 128:(g8 + 1) * 128] = blk

    f32 = lambda ref: ref[...].astype(F32)
    hm = (f32(hf_ref) + f32(hb_ref)) * jax.nn.sigmoid(f32(o_ref))
    mhg = mhg_ref[...]
    parts = []
    for hd in range(HEADS):
        sl = slice(hd * HEAD_DIM, (hd + 1) * HEAD_DIM)
        seg = hm[:, sl]
        mu = jnp.mean(seg, axis=-1, keepdims=True)
        dev = seg - mu
        var = jnp.mean(dev * dev, axis=-1, keepdims=True)
        parts.append(dev * lax.rsqrt(var + NORM_EPS) * mhg[:, sl])
    m_out = jnp.concatenate(parts, axis=-1) * _silu(f32(zm_ref))

    y = y_sc[...]
    gl = 0.5 * y * (1.0 + jnp.tanh(0.7978845608028654 * (y + 0.044715 * (y * y * y))))
    gate = jax.nn.sigmoid(_dot(gl.astype(BF16), gluw_ref[...]) + glub_ref[...])
    s_out = gl * gate * _silu(f32(zs_ref))

    mixed = (_dot(m_out.astype(BF16), wout_ref[0:W_BRANCH, :])
             + _dot(s_out.astype(BF16), wout_ref[W_BRANCH:2 * W_BRANCH, :]))
    xo = x_ref[...] + mod_ref[2:3, :] * mixed
    ms = jnp.mean(xo * xo, axis=-1, keepdims=True)
    out_ref[...] = xo * lax.rsqrt(ms + NORM_EPS) * fg_ref[...]


def _merge(hf, hb, o, zm, y, zs, x, mod, mh_g, glu_w, glu_b, w_out, final_g):
    bsz, t_len, _ = x.shape
    tm = 8 * GRID_W
    tile = pl.BlockSpec((None, tm, D_MODEL), lambda b, i: (b, i, 0))
    y_spec = pl.BlockSpec((S5_GROUPS, GRID_W, 128), lambda b, i: (0, (i // 2) * bsz + b, i % 2))
    const = lambda shape: pl.BlockSpec(shape, lambda b, i: (0,) * len(shape),
                                       pipeline_mode=pl.Buffered(1))
    return pl.pallas_call(
        _merge_kernel,
        grid=(bsz, t_len // tm),
        in_specs=[tile] * 4 + [y_spec, tile, tile,
                               pl.BlockSpec((None, 3, D_MODEL), lambda b, i: (b, 0, 0)),
                               const((1, W_BRANCH)), const((W_BRANCH, W_BRANCH)),
                               const((1, W_BRANCH)), const((2 * W_BRANCH, D_MODEL)),
                               const((1, D_MODEL))],
        out_specs=tile,
        out_shape=jax.ShapeDtypeStruct((bsz, t_len, D_MODEL), F32),
        scratch_shapes=[pltpu.VMEM((tm, W_BRANCH), F32)],
        compiler_params=pltpu.CompilerParams(dimension_semantics=("parallel", "arbitrary"),
                                             vmem_limit_bytes=VMEM_LIMIT),
        name="merge",
    )(hf, hb, o, zm, y, zs, x, mod, mh_g.reshape(1, -1), glu_w, glu_b.reshape(1, -1), w_out,
      final_g.reshape(1, -1))


def _s5_rows_ctx(u):
    bsz, t_len, _ = u.shape
    a = u.reshape(bsz, t_len // S5_SUB, S5_SUB, S5_GROUPS, S5_GC)
    a = jnp.transpose(a, (3, 1, 0, 2, 4))
    return a.reshape(S5_GROUPS, (t_len // S5_SUB) * bsz, S5_SUB * S5_GC)


def kernel(x, c, ctx, c_ctx, norm_g, ada_w, ada_b, w_in, b_gate, conv_qk, mh_g, s5_a_re, s5_a_im,
           s5_log_step, s5_b_re, s5_b_im, s5_c_re, s5_c_im, s5_d, glu_w, glu_b, w_out, final_g):
    bsz, t_len, _ = x.shape
    layer = 0

    cc = jnp.zeros((16, D_MODEL), F32).at[:bsz].set(c).at[bsz].set(c_ctx)
    mod = _ada(cc, ada_w[layer], ada_b[layer]).reshape(16, 3, D_MODEL)
    mod_x = mod[:bsz]
    mod_c = jnp.broadcast_to(mod[bsz][None], (bsz, 3, D_MODEL))

    w = w_in[layer]
    wb = W_BRANCH
    w_qk = w[:, 0:2 * wb].astype(BF16)
    gate0 = 5 * wb
    w_gate = jnp.pad(w[:, gate0:gate0 + N_GATES], ((0, 0), (0, GATE_PAD - N_GATES)))
    w_main = jnp.concatenate([w[:, 2 * wb:5 * wb], w[:, gate0 + N_GATES:], w_gate], axis=1).astype(BF16)
    b_gate_pad = jnp.pad(b_gate[layer].reshape(1, N_GATES), ((0, 0), (0, GATE_PAD - N_GATES)))

    proj = functools.partial(_inproj, norm_g=norm_g[layer], w_qk=w_qk, w_main=w_main,
                             conv_qk=conv_qk[layer], b_gate_pad=b_gate_pad)
    kt_c, v_c, u_c, gt_c = proj(ctx, mod_c, tm=256, s5_layout=False, state_only=True)
    q_x, kt_x, v_x, o_x, zm_x, u_x, zs_x, gt_x = proj(x, mod_x, tm=8 * GRID_W, s5_layout=True)

    zero_state = (jnp.zeros((bsz, 2, HEADS, HEAD_DIM, HEAD_DIM), F32),
                  jnp.zeros((bsz, 2, HEADS, 8, HEAD_DIM), F32),
                  jnp.zeros((bsz, 2, HEADS, 128), F32))
    _, ctx_state = _mlstm(None, kt_c, v_c, gt_c, zero_state, with_output=False)
    (h_f, h_b), _ = _mlstm(q_x, kt_x, v_x, gt_x, ctx_state, with_output=True)

    m_all, g_all, p_all, a16 = _s5_prep(
        s5_a_re[layer], s5_a_im[layer], s5_log_step[layer], s5_b_re[layer], s5_b_im[layer],
        s5_c_re[layer], s5_c_im[layer], s5_d[layer])
    y_x = _s5(_s5_rows_ctx(u_c), u_x, g_all, m_all, p_all, a16, bsz)

    return _merge(h_f, h_b, o_x, zm_x, y_x, zs_x, x, mod_x, mh_g[layer], glu_w[layer].astype(BF16),
                  glu_b[layer], w_out[layer].astype(BF16), final_g)
```

```python
import functools

import jax
import jax.numpy as jnp
from jax import lax
from jax.experimental import pallas as pl
from jax.experimental.pallas import tpu as pltpu

F32 = jnp.float32
BF16 = jnp.bfloat16

D_MODEL = 1024
HEADS = 4
HEAD_DIM = 256
W_BRANCH = 1024
S5_GROUPS = 64
S5_GC = 16
S5_STATE = 64
S5_SUB = 16
GRID_W = 64
S5_ROW_PITCH = 72
N_GATES = 16
GATE_PAD = 128
CONV_W = 3
NORM_EPS = 1e-6
MLSTM_CHUNK = 256
NEG_BIG = -1e30

VMEM_LIMIT = 56 * 1024 * 1024


def _silu(a):
    return a * jax.nn.sigmoid(a)


def _log_sigmoid(a):
    return jnp.minimum(a, 0.0) - jnp.log1p(jnp.exp(-jnp.abs(a)))


def _dot(a, b):
    return jnp.dot(a, b, preferred_element_type=F32)


def _split_bf16(a):
    hi = a.astype(BF16)
    lo = (a - hi.astype(F32)).astype(BF16)
    return hi, lo


def _block_transpose8(rows):
    lane = lax.broadcasted_iota(jnp.int32, rows[0].shape, 1)
    blk = lane // S5_GC
    for d in (4, 2, 1):
        keep = (blk & d) == 0
        new = list(rows)
        for i in range(8):
            if i & d == 0:
                a, b = rows[i], rows[i + d]
                new[i] = jnp.where(keep, a, pltpu.roll(b, d * S5_GC, axis=1))
                new[i + d] = jnp.where(keep, pltpu.roll(a, 128 - d * S5_GC, axis=1), b)
        rows = new
    return rows


def _chunk_gate_rows(gt_blk, reverse):
    chunk = gt_blk.shape[1]
    row = lax.broadcasted_iota(jnp.int32, (chunk, chunk), 0)
    col = lax.broadcasted_iota(jnp.int32, (chunk, chunk), 1)
    mask_st = (row >= col) if reverse else (row <= col)
    tri_st = jnp.where(mask_st, 1.0, 0.0).astype(BF16)
    hi, lo = _split_bf16(gt_blk)
    cum = _dot(hi, tri_st) + _dot(lo, tri_st)
    li0 = 8 if reverse else 0
    b_r = cum[li0 + HEADS:li0 + 2 * HEADS, :]
    run_max = gt_blk[li0:li0 + HEADS, :] - b_r
    lane = lax.broadcasted_iota(jnp.int32, run_max.shape, 1)
    shift = 1
    while shift < chunk:
        if reverse:
            moved = jnp.where(lane < chunk - shift, pltpu.roll(run_max, chunk - shift, axis=1), NEG_BIG)
        else:
            moved = jnp.where(lane >= shift, pltpu.roll(run_max, shift, axis=1), NEG_BIG)
        run_max = jnp.maximum(run_max, moved)
        shift *= 2
    return b_r, run_max


def _ada_kernel(c_ref, w_ref, b_ref, o_ref):
    s = _silu(c_ref[...])
    o_ref[...] = jnp.dot(s, w_ref[...], preferred_element_type=F32,
                         precision=lax.Precision.HIGHEST) + b_ref[...]


def _ada(cc, ada_w, ada_b):
    rows = cc.shape[0]
    n_out = ada_w.shape[1]
    tn = 1024
    return pl.pallas_call(
        _ada_kernel,
        grid=(n_out // tn,),
        in_specs=[pl.BlockSpec((rows, D_MODEL), lambda j: (0, 0)),
                  pl.BlockSpec((D_MODEL, tn), lambda j: (0, j)),
                  pl.BlockSpec((1, tn), lambda j: (0, j))],
        out_specs=pl.BlockSpec((rows, tn), lambda j: (0, j)),
        out_shape=jax.ShapeDtypeStruct((rows, n_out), F32),
        compiler_params=pltpu.CompilerParams(dimension_semantics=("arbitrary",),
                                             vmem_limit_bytes=VMEM_LIMIT),
        name="ada",
    )(cc, ada_w, ada_b.reshape(1, n_out))


HALO = 16


def _inproj_kernel(x_ref, xp_ref, xn_ref, mod_ref, ng_ref, wqk_ref, wm_ref, conv_ref, bg_ref,
                   *rest, tm, s5_layout, state_only):
    if state_only:
        kt_ref, v_ref, u_ref, gt_ref, gx_ref, lhs_sc, z_sc = rest
    else:
        q_ref, kt_ref, v_ref, o_ref, zm_ref, u_ref, zs_ref, gt_ref, gx_ref, lhs_sc, z_sc = rest
    i = pl.program_id(1)
    nt = pl.num_programs(1)
    shift = mod_ref[0:1, :]
    scale = mod_ref[1:2, :]
    ng = ng_ref[...]

    def norm_mod(a):
        ms = jnp.mean(a * a, axis=-1, keepdims=True)
        return (a * lax.rsqrt(ms + NORM_EPS) * ng) * (1.0 + scale) + shift

    lhs_sc[0:tm, :] = norm_mod(x_ref[...]).astype(BF16)
    lhs_sc[tm:tm + HALO, :] = norm_mod(xp_ref[...]).astype(BF16)
    lhs_sc[tm + HALO:tm + 2 * HALO, :] = norm_mod(xn_ref[...]).astype(BF16)

    h = lhs_sc[0:tm, :]
    zg = _dot(h, wm_ref[:, 5120:5120 + GATE_PAD]) + bg_ref[...]
    lane = lax.broadcasted_iota(jnp.int32, zg.shape, 1)
    is_forget = jnp.logical_and(lane < N_GATES, (lane % 8) >= 4)
    gates = jnp.where(is_forget, _log_sigmoid(zg), zg)
    gt = gates.T[0:N_GATES, :]
    gt_ref[...] = gt
    for c0 in range(0, tm, MLSTM_CHUNK):
        parts = []
        for reverse in (False, True):
            parts += list(_chunk_gate_rows(gt[:, c0:c0 + MLSTM_CHUNK], reverse))
        gx_ref[:, c0:c0 + MLSTM_CHUNK] = jnp.concatenate(parts, axis=0)

    valid_prev = (i > 0).astype(F32)
    valid_next = (i < nt - 1).astype(F32)
    for half in range(1 if state_only else 0, 2):
        cols = slice(half * W_BRANCH, (half + 1) * W_BRANCH)
        z = _dot(lhs_sc[...], wqk_ref[:, cols])
        z_sc[8:8 + tm, :] = z[0:tm]
        z_sc[0:8, :] = z[tm + HALO - 8:tm + HALO] * valid_prev
        z_sc[8 + tm:16 + tm, :] = z[tm + HALO:tm + HALO + 8] * valid_next
        cw = conv_ref[:, cols]
        conv = (cw[0:1] * z_sc[7:7 + tm, :] + cw[1:2] * z_sc[8:8 + tm, :]
                + cw[2:3] * z_sc[9:9 + tm, :])
        if half == 0:
            q_ref[...] = _silu(conv).astype(BF16)
        else:
            kt_ref[...] = (_silu(conv) * HEAD_DIM ** -0.5).T.astype(BF16)

    v_ref[...] = _dot(h, wm_ref[:, 0:1024]).astype(BF16)
    if not state_only:
        o_ref[...] = _dot(h, wm_ref[:, 1024:2048]).astype(BF16)
        zm_ref[...] = _dot(h, wm_ref[:, 2048:3072]).astype(BF16)
        zs_ref[...] = _dot(h, wm_ref[:, 4096:5120]).astype(BF16)
    u = _dot(h, wm_ref[:, 3072:4096])
    if s5_layout:
        for g8 in range(W_BRANCH // 128):
            rows = [u[i * GRID_W:(i + 1) * GRID_W, g8 * 128:(g8 + 1) * 128] for i in range(8)]
            for gp, blk in enumerate(_block_transpose8(rows)):
                u_ref[g8 * 8 + gp] = blk.astype(BF16)
    else:
        u_ref[...] = u.astype(BF16)


def _inproj(x, mod, norm_g, w_qk, w_main, conv_qk, b_gate_pad, tm, s5_layout, state_only=False):
    bsz, t_len, _ = x.shape
    nt = t_len // tm
    nhb = t_len // HALO
    tok = lambda w, dt: jax.ShapeDtypeStruct((bsz, t_len, w), dt)
    tile = lambda w: pl.BlockSpec((None, tm, w), lambda b, i: (b, i, 0))
    tile_t = lambda w: pl.BlockSpec((None, w, tm), lambda b, i: (b, 0, i))
    const = lambda shape: pl.BlockSpec(shape, lambda b, i: (0,) * len(shape),
                                       pipeline_mode=pl.Buffered(1))
    if s5_layout:
        assert tm == 8 * GRID_W and t_len % (2 * tm) == 0
        u_spec = pl.BlockSpec((S5_GROUPS, GRID_W, 128), lambda b, i: (0, (i // 2) * bsz + b, i % 2))
        u_shape = jax.ShapeDtypeStruct((S5_GROUPS, (t_len // (2 * tm)) * bsz * GRID_W, 256), BF16)
    else:
        u_spec, u_shape = tile(W_BRANCH), tok(W_BRANCH, BF16)
    kt_shape = jax.ShapeDtypeStruct((bsz, W_BRANCH, t_len), BF16)
    gt_shape = jax.ShapeDtypeStruct((bsz, N_GATES, t_len), F32)
    if state_only:
        out_specs = [tile_t(W_BRANCH), tile(W_BRANCH), u_spec, tile_t(N_GATES), tile_t(N_GATES)]
        out_shape = [kt_shape, tok(W_BRANCH, BF16), u_shape, gt_shape, gt_shape]
    else:
        out_specs = [tile(W_BRANCH), tile_t(W_BRANCH)] + [tile(W_BRANCH)] * 3 + [
            u_spec, tile(W_BRANCH), tile_t(N_GATES), tile_t(N_GATES)]
        out_shape = [tok(W_BRANCH, BF16), kt_shape] + [tok(W_BRANCH, BF16)] * 3 + [
            u_shape, tok(W_BRANCH, BF16), gt_shape, gt_shape]
    return pl.pallas_call(
        functools.partial(_inproj_kernel, tm=tm, s5_layout=s5_layout, state_only=state_only),
        grid=(bsz, nt),
        in_specs=[
            tile(D_MODEL),
            pl.BlockSpec((None, HALO, D_MODEL),
                         lambda b, i: (b, jnp.maximum(i * (tm // HALO) - 1, 0), 0)),
            pl.BlockSpec((None, HALO, D_MODEL),
                         lambda b, i: (b, jnp.minimum((i + 1) * (tm // HALO), nhb - 1), 0)),
            pl.BlockSpec((None, 3, D_MODEL), lambda b, i: (b, 0, 0)),
            const((1, D_MODEL)),
            const((D_MODEL, 2 * W_BRANCH)),
            const((D_MODEL, 5 * W_BRANCH + GATE_PAD)),
            const((CONV_W, 2 * W_BRANCH)),
            const((1, GATE_PAD)),
        ],
        out_specs=out_specs,
        out_shape=out_shape,
        scratch_shapes=[pltpu.VMEM((tm + 2 * HALO, D_MODEL), BF16),
                        pltpu.VMEM((tm + 16, W_BRANCH), F32)],
        compiler_params=pltpu.CompilerParams(dimension_semantics=("parallel", "arbitrary"),
                                             vmem_limit_bytes=VMEM_LIMIT),
        name="inproj",
    )(x, x, x, mod, norm_g.reshape(1, D_MODEL), w_qk, w_main, conv_qk, b_gate_pad)


def _mlstm_gates(gt_ref, gx_ref, m_sc, *, reverse, chunk, with_output):
    li0 = 8 if reverse else 0
    last = 0 if reverse else chunk - 1
    li_r = gt_ref[li0:li0 + HEADS, :]
    b_r = gx_ref[li0:li0 + HEADS, :]
    b_last = b_r[:, last:last + 1]
    m_old = m_sc[:, 0:1]
    g_r = b_last - b_r + li_r
    m_new = jnp.maximum(b_last + m_old, jnp.max(g_r, axis=1, keepdims=True))
    out = dict(m_old=m_old, m_new=m_new, decay=jnp.exp(b_last + m_old - m_new),
               k_scale=jnp.exp(g_r - m_new).astype(BF16))
    if not with_output:
        return out
    row = lax.broadcasted_iota(jnp.int32, (chunk, chunk), 0)
    col = lax.broadcasted_iota(jnp.int32, (chunk, chunk), 1)
    mask_ts = (col >= row) if reverse else (col <= row)
    mm_r = jnp.maximum(gx_ref[li0 + HEADS:li0 + 2 * HEADS, :], m_old).astype(BF16)
    b_hi, b_lo = _split_bf16(b_r)
    rows = jnp.concatenate([mm_r, b_hi, b_lo, jnp.zeros_like(b_hi)], axis=0)
    sel_r = lax.broadcasted_iota(jnp.int32, (4 * HEADS, 2 * HEADS * 128), 0)
    sel_c = lax.broadcasted_iota(jnp.int32, (4 * HEADS, 2 * HEADS * 128), 1) // 128
    head_r = sel_r % HEADS
    pick = jnp.logical_or(jnp.logical_and(sel_r < HEADS, sel_c % HEADS == head_r),
                          jnp.logical_and(jnp.logical_and(sel_r >= HEADS, sel_r < 3 * HEADS),
                                          sel_c == HEADS + head_r))
    sel = jnp.where(pick, 1.0, 0.0).astype(BF16)
    cols = lax.dot_general(rows, sel, (((0,), (0,)), ((), ())), preferred_element_type=F32)
    out.update(a_r=li_r - b_r, cols=cols, mask_ts=mask_ts)
    return out


def _mlstm_heads(q_ref, kt_ref, v_ref, h_ref, c_sc, n_sc, m_sc, gates, *, chunk):
    with_output = h_ref is not None
    m_old, m_new, decay, k_scale = gates["m_old"], gates["m_new"], gates["decay"], gates["k_scale"]
    if with_output:
        a_r, cols, mask_ts = gates["a_r"], gates["cols"], gates["mask_ts"]
    ones_rows = jnp.ones((8, chunk), BF16)
    wide = lambda a, n: jnp.concatenate([a] * (n // 128), axis=1)
    for hd in range(HEADS):
        sl = slice(hd * HEAD_DIM, (hd + 1) * HEAD_DIM)
        kt = kt_ref[sl, :]
        v = v_ref[:, sl]
        c_old = c_sc[hd]
        n_old = n_sc[hd]
        kw_t = kt * k_scale[hd:hd + 1, :]
        if with_output:
            q = q_ref[:, sl]
            mm_c = cols[:, hd * 128:(hd + 1) * 128]
            bm_c = cols[:, (HEADS + hd) * 128:(HEADS + hd + 1) * 128]
            decay_mat = jnp.exp(jnp.where(mask_ts, a_r[hd:hd + 1, :] - wide(mm_c, chunk), NEG_BIG))
            s_f = _dot(q, kt) * decay_mat
            w_inter = jnp.exp(m_old[hd:hd + 1, :] - mm_c)
            den = (w_inter * jnp.sum(q.astype(F32) * n_old[0:1, :], axis=1, keepdims=True)
                   + jnp.sum(s_f, axis=1, keepdims=True))
            q_w = q * wide(w_inter, HEAD_DIM).astype(BF16)
            num = _dot(jnp.concatenate([q_w, s_f.astype(BF16)], axis=1),
                       jnp.concatenate([c_old.astype(BF16), v], axis=0))
            inv = 1.0 / jnp.maximum(jnp.abs(den), jnp.exp(-bm_c))
            h_ref[:, sl] = (num * wide(inv, HEAD_DIM)).astype(BF16)
        c_sc[hd] = decay[hd:hd + 1, :] * c_old + _dot(kw_t, v)
        n_sc[hd] = decay[hd:hd + 1, :] * n_old + lax.dot_general(
            ones_rows, kw_t, (((1,), (1,)), ((), ())), preferred_element_type=F32)
        m_sc[hd:hd + 1, :] = jnp.broadcast_to(m_new[hd:hd + 1, :], (1, 128))


def _mlstm_kernel(*refs, with_output, chunk):
    n_in = 5 if with_output else 4
    ins = [refs[0:n_in], refs[n_in:2 * n_in]]
    c0_ref, n0_ref, m0_ref = refs[2 * n_in:2 * n_in + 3]
    rest = refs[2 * n_in + 3:]
    if with_output:
        h_refs, rest = rest[0:2], rest[2:]
    else:
        h_refs = (None, None)
        ins = [(None,) + tuple(r) for r in ins]
    co_ref, no_ref, mo_ref, c_sc, n_sc, m_sc = rest
    i = pl.program_id(1)
    nc = pl.num_programs(1)

    @pl.when(i == 0)
    def _():
        c_sc[...] = c0_ref[...]
        n_sc[...] = n0_ref[...]
        m_sc[...] = m0_ref[...]

    gates = [_mlstm_gates(ins[d][3], ins[d][4], m_sc.at[d], reverse=bool(d), chunk=chunk,
                          with_output=with_output) for d in range(2)]
    for d in range(2):
        _mlstm_heads(ins[d][0], ins[d][1], ins[d][2], h_refs[d], c_sc.at[d], n_sc.at[d], m_sc.at[d],
                     gates[d], chunk=chunk)

    @pl.when(i == nc - 1)
    def _():
        co_ref[...] = c_sc[...]
        no_ref[...] = n_sc[...]
        mo_ref[...] = m_sc[...]


def _mlstm(q, kt, v, gt, gx, state, with_output):
    bsz, t_len, _ = v.shape
    chunk = MLSTM_CHUNK
    nc = t_len // chunk
    cidx = (lambda i: i, lambda i: nc - 1 - i)
    tile = lambda w, d: pl.BlockSpec((None, chunk, w), lambda b, i: (b, cidx[d](i), 0))
    tile_t = lambda w, d: pl.BlockSpec((None, w, chunk), lambda b, i: (b, 0, cidx[d](i)))
    st_dims = [(2, HEADS, HEAD_DIM, HEAD_DIM), (2, HEADS, 8, HEAD_DIM), (2, HEADS, 128)]
    st_specs = [pl.BlockSpec((None,) + s, lambda b, i, n=len(s): (b,) + (0,) * n) for s in st_dims]
    st_shapes = [jax.ShapeDtypeStruct((bsz,) + s, F32) for s in st_dims]
    in_specs, args = [], []
    for d in range(2):
        in_specs += ([tile(W_BRANCH, d)] if with_output else []) + [
            tile_t(W_BRANCH, d), tile(W_BRANCH, d), tile_t(N_GATES, d), tile_t(N_GATES, d)]
        args += ([q] if with_output else []) + [kt, v, gt, gx]
    out_specs, out_shape = list(st_specs), list(st_shapes)
    if with_output:
        out_specs = [tile(W_BRANCH, 0), tile(W_BRANCH, 1)] + out_specs
        out_shape = [jax.ShapeDtypeStruct((bsz, t_len, W_BRANCH), BF16)] * 2 + out_shape
    outs = pl.pallas_call(
        functools.partial(_mlstm_kernel, with_output=with_output, chunk=chunk),
        grid=(bsz, nc),
        in_specs=in_specs + st_specs,
        out_specs=out_specs,
        out_shape=out_shape,
        scratch_shapes=[pltpu.VMEM(s, F32) for s in st_dims],
        compiler_params=pltpu.CompilerParams(dimension_semantics=("parallel", "arbitrary"),
                                             vmem_limit_bytes=VMEM_LIMIT),
        name="mlstm_out" if with_output else "mlstm_state",
    )(*args, *state)
    if with_output:
        return (outs[0], outs[1]), tuple(outs[2:])
    return None, tuple(outs)


def _s5_kernel(vc_ref, vx_ref, g_ref, m_ref, p_ref, a_ref, y_ref, gu_sc, s_sc, *,
               nk_ctx, n_rc, bsz, rblk):
    rows_ctx = nk_ctx * bsz
    rows_x = n_rc * bsz * GRID_W

    def increments(v_ref, r0, r1):
        return _dot(v_ref[0, r0:r1, :], g_ref[0]) + _dot(v_ref[1, r0:r1, :], g_ref[1])

    inc = increments(vc_ref, 0, rows_ctx)
    for comp in range(4):
        gu_sc[comp, 0:rows_ctx, :] = inc[:, comp * 128:(comp + 1) * 128]
    for r0 in range(0, rows_x, rblk):
        inc = increments(vx_ref, r0, r0 + rblk)
        for run in range(rblk // GRID_W):
            dst = rows_ctx + (r0 // GRID_W + run) * S5_ROW_PITCH
            for comp in range(4):
                gu_sc[comp, dst:dst + GRID_W, :] = (
                    inc[run * GRID_W:(run + 1) * GRID_W, comp * 128:(comp + 1) * 128])

    a = a_ref[...]
    a_pow = [jnp.broadcast_to(a[:, comp * 128:(comp + 1) * 128], (bsz, 128)) for comp in range(4)]
    zero = jnp.zeros((bsz, 128), F32)

    def cmul(x_r, x_i, y_r, y_i):
        return x_r * y_r - x_i * y_i, x_r * y_i + x_i * y_r

    a_sq = [cmul(a_pow[2 * d], a_pow[2 * d + 1], a_pow[2 * d], a_pow[2 * d + 1]) for d in range(2)]

    def step(rows, carry, direction):
        s_r, s_i = carry
        inc_r = gu_sc[2 * direction, rows, :]
        inc_i = gu_sc[2 * direction + 1, rows, :]
        gu_sc[2 * direction, rows, :] = s_r
        gu_sc[2 * direction + 1, rows, :] = s_i
        p_r, p_i = cmul(a_pow[2 * direction], a_pow[2 * direction + 1], s_r, s_i)
        return p_r + inc_r, p_i + inc_i

    def step2(rows0, rows1, carry, direction):
        s_r, s_i = carry
        a_r, a_i = a_pow[2 * direction], a_pow[2 * direction + 1]
        inc0_r, inc0_i = gu_sc[2 * direction, rows0, :], gu_sc[2 * direction + 1, rows0, :]
        inc1_r, inc1_i = gu_sc[2 * direction, rows1, :], gu_sc[2 * direction + 1, rows1, :]
        gu_sc[2 * direction, rows0, :] = s_r
        gu_sc[2 * direction + 1, rows0, :] = s_i
        m_r, m_i = cmul(a_r, a_i, s_r, s_i)
        gu_sc[2 * direction, rows1, :] = m_r + inc0_r
        gu_sc[2 * direction + 1, rows1, :] = m_i + inc0_i
        c_r, c_i = cmul(a_r, a_i, inc0_r, inc0_i)
        q_r, q_i = cmul(a_sq[direction][0], a_sq[direction][1], s_r, s_i)
        return q_r + (c_r + inc1_r), q_i + (c_i + inc1_i)

    def ctx_rows(k):
        return pl.ds(pl.multiple_of(k * bsz, bsz), bsz)

    def x_rows(w, rc):
        return pl.ds(rows_ctx + rc * (bsz * S5_ROW_PITCH) + w, bsz, stride=S5_ROW_PITCH)

    def ctx_body(k, carry):
        return step(ctx_rows(k), carry[0], 0), step(ctx_rows(nk_ctx - 1 - k), carry[1], 1)

    def x_body(w, carry):
        c_f, c_b = carry
        w_b = GRID_W - 1 - w
        for rc in range(0, n_rc - 1, 2):
            c_f = step2(x_rows(w, rc), x_rows(w, rc + 1), c_f, 0)
            c_b = step2(x_rows(w_b, n_rc - 1 - rc), x_rows(w_b, n_rc - 2 - rc), c_b, 1)
        if n_rc % 2:
            c_f = step(x_rows(w, n_rc - 1), c_f, 0)
            c_b = step(x_rows(w_b, 0), c_b, 1)
        return c_f, c_b

    carry = lax.fori_loop(0, nk_ctx, ctx_body, ((zero, zero), (zero, zero)))
    lax.fori_loop(0, GRID_W, x_body, carry)

    for r0 in range(0, rows_x, rblk):
        r1 = r0 + rblk
        for run in range(rblk // GRID_W):
            src = rows_ctx + (r0 // GRID_W + run) * S5_ROW_PITCH
            for comp in range(4):
                s_sc[r0 + run * GRID_W:r0 + (run + 1) * GRID_W, comp * 128:(comp + 1) * 128] = (
                    gu_sc[comp, src:src + GRID_W, :].astype(BF16))
        for gg in range(2):
            y_ref[gg, r0:r1, :] = (_dot(vx_ref[gg, r0:r1, :], m_ref[gg])
                                   + _dot(s_sc[r0:r1, :], p_ref[gg])).astype(BF16)


def _s5(v_ctx, v_x, g_all, m_all, p_all, a16, bsz):
    rows_ctx, rows_x = v_ctx.shape[1], v_x.shape[1]
    lanes = S5_SUB * S5_GC
    return pl.pallas_call(
        functools.partial(_s5_kernel, nk_ctx=rows_ctx // bsz, n_rc=rows_x // (bsz * GRID_W),
                          bsz=bsz, rblk=512),
        grid=(S5_GROUPS // 2,),
        in_specs=[pl.BlockSpec((2, rows_ctx, lanes), lambda j: (j, 0, 0)),
                  pl.BlockSpec((2, rows_x, lanes), lambda j: (j, 0, 0)),
                  pl.BlockSpec((2, lanes, 512), lambda j: (j, 0, 0)),
                  pl.BlockSpec((2, lanes, lanes), lambda j: (j, 0, 0)),
                  pl.BlockSpec((2, 512, lanes), lambda j: (j, 0, 0)),
                  pl.BlockSpec((None, 1, 512), lambda j: (j, 0, 0))],
        out_specs=pl.BlockSpec((2, rows_x, lanes), lambda j: (j, 0, 0)),
        out_shape=jax.ShapeDtypeStruct((S5_GROUPS, rows_x, lanes), BF16),
        scratch_shapes=[pltpu.VMEM((4, rows_ctx + (rows_x // GRID_W) * S5_ROW_PITCH, 128), F32),
                        pltpu.VMEM((rows_x, 512), BF16)],
        compiler_params=pltpu.CompilerParams(dimension_semantics=("parallel",),
                                             vmem_limit_bytes=VMEM_LIMIT),
        name="s5",
    )(v_ctx, v_x, g_all, m_all, p_all, a16)


def _s5_prep_kernel(lr_ref, lc_ref, bt_ref, ct_ref, d_ref, m_ref, g_ref, p_ref, a_ref):
    hp = lax.Precision.HIGHEST
    n_s, lanes = S5_SUB, S5_SUB * S5_GC
    lane128 = lax.broadcasted_iota(jnp.int32, (n_s, 128), 1)
    blk_of_lane = lax.broadcasted_iota(jnp.int32, (128, lanes), 1) // S5_GC
    g_types, p_types, a16, k_rows = [], [], [], []
    for d in range(2):
        a_r, a_i, log_dt = lr_ref[d, 0:1, :], lr_ref[d, 1:2, :], lr_ref[d, 2:3, :]
        dt = jnp.exp(log_dt)
        lam_r, lam_i = a_r * dt, a_i * dt
        steps = lax.broadcasted_iota(jnp.int32, (24, 128), 0).astype(F32)
        mag = jnp.exp(lam_r * steps)
        pw_r, pw_i = mag * jnp.cos(lam_i * steps), mag * jnp.sin(lam_i * steps)
        nr, ni = pw_r[1:2] - 1.0, pw_i[1:2]
        den = a_r * a_r + a_i * a_i
        co_r, co_i = (nr * a_r + ni * a_i) / den, (ni * a_r - nr * a_i) / den
        b_r = jnp.concatenate([bt_ref[d, 0]] * n_s, axis=0)
        b_i = jnp.concatenate([bt_ref[d, 1]] * n_s, axis=0)
        bb_r, bb_i = co_r * b_r - co_i * b_i, co_r * b_i + co_i * b_r
        order = [n_s - 1 - i for i in range(n_s)] if d == 0 else list(range(n_s))
        pg_r = jnp.concatenate([jnp.broadcast_to(pw_r[n:n + 1], (S5_GC, 128)) for n in order], axis=0)
        pg_i = jnp.concatenate([jnp.broadcast_to(pw_i[n:n + 1], (S5_GC, 128)) for n in order], axis=0)
        g_types += [bb_r * pg_r - bb_i * pg_i, bb_r * pg_i + bb_i * pg_r]
        a16 += [pw_r[n_s:n_s + 1], pw_i[n_s:n_s + 1]]
        x0 = slice((n_s - 1) * S5_GC, n_s * S5_GC) if d == 0 else slice(0, S5_GC)
        x_r, x_i = bb_r[x0], bb_i[x0]
        lhs_r = jnp.concatenate([jnp.where(lane128 < S5_STATE, x_r, 0.0),
                                 jnp.where(lane128 < S5_STATE, 0.0, x_r)], axis=0)
        lhs_i = jnp.concatenate([jnp.where(lane128 < S5_STATE, x_i, 0.0),
                                 jnp.where(lane128 < S5_STATE, 0.0, x_i)], axis=0)
        dt_c = jnp.exp(lc_ref[d, 2])
        lam_rc, lam_ic = lc_ref[d, 0] * dt_c, lc_ref[d, 1] * dt_c
        n_y = (blk_of_lane if d == 0 else n_s - 1 - blk_of_lane).astype(F32)
        mag_y = jnp.exp(lam_rc * n_y)
        ypw_r, ypw_i = mag_y * jnp.cos(lam_ic * n_y), mag_y * jnp.sin(lam_ic * n_y)
        c_r, c_i = ct_ref[d, 0], ct_ref[d, 1]
        y_r, y_i = c_r * ypw_r - c_i * ypw_i, c_r * ypw_i + c_i * ypw_r
        k_rows.append(jnp.dot(lhs_r, y_r, preferred_element_type=F32, precision=hp)
                      - jnp.dot(lhs_i, y_i, preferred_element_type=F32, precision=hp))
        mag_1 = jnp.exp(lam_rc[:, 0:128])
        a1_r, a1_i = mag_1 * jnp.cos(lam_ic[:, 0:128]), mag_1 * jnp.sin(lam_ic[:, 0:128])
        a1_r = jnp.concatenate([a1_r, a1_r], axis=1)
        a1_i = jnp.concatenate([a1_i, a1_i], axis=1)
        p_types += [y_r * a1_r - y_i * a1_i, -(y_r * a1_i + y_i * a1_r)]

    a_ref[...] = jnp.concatenate(a16, axis=1)
    lane_g = lax.broadcasted_iota(jnp.int32, (lanes, 128), 1)
    row_p = lax.broadcasted_iota(jnp.int32, (128, lanes), 0)
    lane_k = lax.broadcasted_iota(jnp.int32, (S5_GC, lanes), 1)
    row_m = lax.broadcasted_iota(jnp.int32, (lanes, lanes), 0)
    lane_m = lax.broadcasted_iota(jnp.int32, (lanes, lanes), 1)
    for h in range(2):
        mine_l = (lane_g >= S5_STATE) == bool(h)
        g_ref[h] = jnp.concatenate([jnp.where(mine_l, t, 0.0) for t in g_types], axis=1).astype(BF16)
        mine_r = (row_p >= S5_STATE) == bool(h)
        p_ref[h] = jnp.concatenate([jnp.where(mine_r, t, 0.0) for t in p_types], axis=0).astype(BF16)
        k_f = k_rows[0][h * S5_GC:(h + 1) * S5_GC]
        k_b = k_rows[1][h * S5_GC:(h + 1) * S5_GC]
        blocks = []
        for i in range(n_s):
            up, down = S5_GC * i, S5_GC * (n_s - 1 - i)
            f = k_f if up == 0 else jnp.where(lane_k >= up, pltpu.roll(k_f, up, axis=1), 0.0)
            b = k_b if down == 0 else jnp.where(lane_k < lanes - down,
                                                pltpu.roll(k_b, lanes - down, axis=1), 0.0)
            blocks.append(f + b)
        m = jnp.concatenate(blocks, axis=0) + jnp.where(row_m == lane_m, d_ref[h], 0.0)
        m_ref[h] = m.astype(BF16)


def _s5_prep(a_re, a_im, log_step, b_re, b_im, c_re, c_im, d_skip):
    n_g, n_p, n_c, n_s = S5_GROUPS, S5_STATE, S5_GC, S5_SUB
    lanes = n_s * n_c
    pair = lambda a: jnp.transpose(a.astype(F32).reshape(2, n_g // 2, 2 * n_p), (1, 0, 2))
    lam_row = jnp.stack([pair(a_re), pair(a_im),
                         pair(jnp.broadcast_to(log_step[..., None], a_re.shape))], axis=2)
    lam_col = jnp.broadcast_to(lam_row[..., None], lam_row.shape + (lanes,))
    bt = lambda b: jnp.transpose(b.astype(F32).reshape(2, n_g // 2, 2, n_p, n_c),
                                 (1, 0, 4, 2, 3)).reshape(n_g // 2, 2, n_c, 2 * n_p)
    b_t = jnp.stack([bt(b_re), bt(b_im)], axis=2)
    ct = lambda c: jnp.tile(jnp.transpose(c.astype(F32).reshape(2, n_g // 2, 2, n_c, n_p),
                                          (1, 0, 2, 4, 3)).reshape(n_g // 2, 2, 2 * n_p, n_c),
                            (1, 1, 1, n_s))
    c_t = jnp.stack([ct(c_re), ct(c_im)], axis=2)
    d_row = jnp.tile(d_skip.astype(F32).reshape(n_g // 2, 2, 1, n_c), (1, 1, 1, n_s))
    blk = lambda *s: pl.BlockSpec((None,) + s, lambda j: (j,) + (0,) * len(s))
    grp = lambda *s: pl.BlockSpec((2,) + s, lambda j: (j,) + (0,) * len(s))
    return pl.pallas_call(
        _s5_prep_kernel,
        grid=(n_g // 2,),
        in_specs=[blk(2, 3, 2 * n_p), blk(2, 3, 2 * n_p, lanes), blk(2, 2, n_c, 2 * n_p),
                  blk(2, 2, 2 * n_p, lanes), blk(2, 1, lanes)],
        out_specs=[grp(lanes, lanes), grp(lanes, 8 * n_p), grp(8 * n_p, lanes), blk(1, 8 * n_p)],
        out_shape=[jax.ShapeDtypeStruct((n_g, lanes, lanes), BF16),
                   jax.ShapeDtypeStruct((n_g, lanes, 8 * n_p), BF16),
                   jax.ShapeDtypeStruct((n_g, 8 * n_p, lanes), BF16),
                   jax.ShapeDtypeStruct((n_g // 2, 1, 8 * n_p), F32)],
        compiler_params=pltpu.CompilerParams(dimension_semantics=("parallel",),
                                             vmem_limit_bytes=VMEM_LIMIT),
        name="s5_prep",
    )(lam_row, lam_col, b_t, c_t, d_row)


def _merge_kernel(hf_ref, hb_ref, o_ref, zm_ref, y_ref, zs_ref, x_ref, mod_ref, mhg_ref,
                  gluw_ref, glub_ref, wout_ref, fg_ref, out_ref, y_sc):
    for g8 in range(W_BRANCH // 128):
        rows = [y_ref[g8 * 8 + gp].astype(F32) for gp in range(8)]
        for i, blk in enumerate(_block_transpose8(rows)):
            y_sc[i * GRID_W:(i + 1) * GRID_W, g8 * 128:(g8 + 1) * 128] = blk

    f32 = lambda ref: ref[...].astype(F32)
    hm = (f32(hf_ref) + f32(hb_ref)) * jax.nn.sigmoid(f32(o_ref))
    mhg = mhg_ref[...]
    parts = []
    for hd in range(HEADS):
        sl = slice(hd * HEAD_DIM, (hd + 1) * HEAD_DIM)
        seg = hm[:, sl]
        mu = jnp.mean(seg, axis=-1, keepdims=True)
        dev = seg - mu
        var = jnp.mean(dev * dev, axis=-1, keepdims=True)
        parts.append(dev * lax.rsqrt(var + NORM_EPS) * mhg[:, sl])
    m_out = jnp.concatenate(parts, axis=-1) * _silu(f32(zm_ref))

    y = y_sc[...]
    gl = 0.5 * y * (1.0 + jnp.tanh(0.7978845608028654 * (y + 0.044715 * (y * y * y))))
    gate = jax.nn.sigmoid(_dot(gl.astype(BF16), gluw_ref[...]) + glub_ref[...])
    s_out = gl * gate * _silu(f32(zs_ref))

    mixed = (_dot(m_out.astype(BF16), wout_ref[0:W_BRANCH, :])
             + _dot(s_out.astype(BF16), wout_ref[W_BRANCH:2 * W_BRANCH, :]))
    xo = x_ref[...] + mod_ref[2:3, :] * mixed
    ms = jnp.mean(xo * xo, axis=-1, keepdims=True)
    out_ref[...] = xo * lax.rsqrt(ms + NORM_EPS) * fg_ref[...]


def _merge(hf, hb, o, zm, y, zs, x, mod, mh_g, glu_w, glu_b, w_out, final_g):
    bsz, t_len, _ = x.shape
    tm = 8 * GRID_W
    tile = pl.BlockSpec((None, tm, D_MODEL), lambda b, i: (b, i, 0))
    y_spec = pl.BlockSpec((S5_GROUPS, GRID_W, 128), lambda b, i: (0, (i // 2) * bsz + b, i % 2))
    const = lambda shape: pl.BlockSpec(shape, lambda b, i: (0,) * len(shape),
                                       pipeline_mode=pl.Buffered(1))
    return pl.pallas_call(
        _merge_kernel,
        grid=(bsz, t_len // tm),
        in_specs=[tile] * 4 + [y_spec, tile, tile,
                               pl.BlockSpec((None, 3, D_MODEL), lambda b, i: (b, 0, 0)),
                               const((1, W_BRANCH)), const((W_BRANCH, W_BRANCH)),
                               const((1, W_BRANCH)), const((2 * W_BRANCH, D_MODEL)),
                               const((1, D_MODEL))],
        out_specs=tile,
        out_shape=jax.ShapeDtypeStruct((bsz, t_len, D_MODEL), F32),
        scratch_shapes=[pltpu.VMEM((tm, W_BRANCH), F32)],
        compiler_params=pltpu.CompilerParams(dimension_semantics=("parallel", "arbitrary"),
                                             vmem_limit_bytes=VMEM_LIMIT),
        name="merge",
    )(hf, hb, o, zm, y, zs, x, mod, mh_g.reshape(1, -1), glu_w, glu_b.reshape(1, -1), w_out,
      final_g.reshape(1, -1))


def _s5_rows_ctx(u):
    bsz, t_len, _ = u.shape
    a = u.reshape(bsz, t_len // S5_SUB, S5_SUB, S5_GROUPS, S5_GC)
    a = jnp.transpose(a, (3, 1, 0, 2, 4))
    return a.reshape(S5_GROUPS, (t_len // S5_SUB) * bsz, S5_SUB * S5_GC)


def kernel(x, c, ctx, c_ctx, norm_g, ada_w, ada_b, w_in, b_gate, conv_qk, mh_g, s5_a_re, s5_a_im,
           s5_log_step, s5_b_re, s5_b_im, s5_c_re, s5_c_im, s5_d, glu_w, glu_b, w_out, final_g):
    bsz, t_len, _ = x.shape
    layer = 0

    cc = jnp.zeros((16, D_MODEL), F32).at[:bsz].set(c).at[bsz].set(c_ctx)
    mod = _ada(cc, ada_w[layer], ada_b[layer]).reshape(16, 3, D_MODEL)
    mod_x = mod[:bsz]
    mod_c = jnp.broadcast_to(mod[bsz][None], (bsz, 3, D_MODEL))

    w = w_in[layer]
    wb = W_BRANCH
    w_qk = w[:, 0:2 * wb].astype(BF16)
    gate0 = 5 * wb
    w_gate = jnp.pad(w[:, gate0:gate0 + N_GATES], ((0, 0), (0, GATE_PAD - N_GATES)))
    w_main = jnp.concatenate([w[:, 2 * wb:5 * wb], w[:, gate0 + N_GATES:], w_gate], axis=1).astype(BF16)
    b_gate_pad = jnp.pad(b_gate[layer].reshape(1, N_GATES), ((0, 0), (0, GATE_PAD - N_GATES)))

    proj = functools.partial(_inproj, norm_g=norm_g[layer], w_qk=w_qk, w_main=w_main,
                             conv_qk=conv_qk[layer], b_gate_pad=b_gate_pad)
    kt_c, v_c, u_c, gt_c, gx_c = proj(ctx, mod_c, tm=256, s5_layout=False, state_only=True)
    q_x, kt_x, v_x, o_x, zm_x, u_x, zs_x, gt_x, gx_x = proj(x, mod_x, tm=8 * GRID_W, s5_layout=True)

    zero_state = (jnp.zeros((bsz, 2, HEADS, HEAD_DIM, HEAD_DIM), F32),
                  jnp.zeros((bsz, 2, HEADS, 8, HEAD_DIM), F32),
                  jnp.zeros((bsz, 2, HEADS, 128), F32))
    _, ctx_state = _mlstm(None, kt_c, v_c, gt_c, gx_c, zero_state, with_output=False)
    (h_f, h_b), _ = _mlstm(q_x, kt_x, v_x, gt_x, gx_x, ctx_state, with_output=True)

    m_all, g_all, p_all, a16 = _s5_prep(
        s5_a_re[layer], s5_a_im[layer], s5_log_step[layer], s5_b_re[layer], s5_b_im[layer],
        s5_c_re[layer], s5_c_im[layer], s5_d[layer])
    y_x = _s5(_s5_rows_ctx(u_c), u_x, g_all, m_all, p_all, a16, bsz)

    return _merge(h_f, h_b, o_x, zm_x, y_x, zs_x, x, mod_x, mh_g[layer], glu_w[layer].astype(BF16),
                  glu_b[layer], w_out[layer].astype(BF16), final_g)
```

```python
import functools

import jax
import jax.numpy as jnp
from jax import lax
from jax.experimental import pallas as pl
from jax.experimental.pallas import tpu as pltpu

F32 = jnp.float32
BF16 = jnp.bfloat16

D_MODEL = 1024
HEADS = 4
HEAD_DIM = 256
W_BRANCH = 1024
S5_GROUPS = 64
S5_GC = 16
S5_STATE = 64
S5_SUB = 16
GRID_W = 64
S5_ROW_PITCH = 72
N_GATES = 16
GATE_PAD = 128
CONV_W = 3
NORM_EPS = 1e-6
MLSTM_CHUNK = 256
NEG_BIG = -1e30

VMEM_LIMIT = 56 * 1024 * 1024


def _silu(a):
    return a * jax.nn.sigmoid(a)


def _log_sigmoid(a):
    return jnp.minimum(a, 0.0) - jnp.log1p(jnp.exp(-jnp.abs(a)))


def _dot(a, b):
    return jnp.dot(a, b, preferred_element_type=F32)


def _split_bf16(a):
    hi = a.astype(BF16)
    lo = (a - hi.astype(F32)).astype(BF16)
    return hi, lo


def _block_transpose8(rows):
    lane = lax.broadcasted_iota(jnp.int32, rows[0].shape, 1)
    blk = lane // S5_GC
    for d in (4, 2, 1):
        keep = (blk & d) == 0
        new = list(rows)
        for i in range(8):
            if i & d == 0:
                a, b = rows[i], rows[i + d]
                new[i] = jnp.where(keep, a, pltpu.roll(b, d * S5_GC, axis=1))
                new[i + d] = jnp.where(keep, pltpu.roll(a, 128 - d * S5_GC, axis=1), b)
        rows = new
    return rows


def _chunk_gate_rows(gt_blk, reverse):
    chunk = gt_blk.shape[1]
    row = lax.broadcasted_iota(jnp.int32, (chunk, chunk), 0)
    col = lax.broadcasted_iota(jnp.int32, (chunk, chunk), 1)
    mask_st = (row >= col) if reverse else (row <= col)
    tri_st = jnp.where(mask_st, 1.0, 0.0).astype(BF16)
    hi, lo = _split_bf16(gt_blk)
    cum = _dot(hi, tri_st) + _dot(lo, tri_st)
    li0 = 8 if reverse else 0
    b_r = cum[li0 + HEADS:li0 + 2 * HEADS, :]
    run_max = gt_blk[li0:li0 + HEADS, :] - b_r
    lane = lax.broadcasted_iota(jnp.int32, run_max.shape, 1)
    shift = 1
    while shift < chunk:
        if reverse:
            moved = jnp.where(lane < chunk - shift, pltpu.roll(run_max, chunk - shift, axis=1), NEG_BIG)
        else:
            moved = jnp.where(lane >= shift, pltpu.roll(run_max, shift, axis=1), NEG_BIG)
        run_max = jnp.maximum(run_max, moved)
        shift *= 2
    return b_r, run_max


def _ada_kernel(c_ref, w_ref, b_ref, o_ref):
    s = _silu(c_ref[...])
    o_ref[...] = jnp.dot(s, w_ref[...], preferred_element_type=F32,
                         precision=lax.Precision.HIGHEST) + b_ref[...]


def _ada(cc, ada_w, ada_b):
    rows = cc.shape[0]
    n_out = ada_w.shape[1]
    tn = 1024
    return pl.pallas_call(
        _ada_kernel,
        grid=(n_out // tn,),
        in_specs=[pl.BlockSpec((rows, D_MODEL), lambda j: (0, 0)),
                  pl.BlockSpec((D_MODEL, tn), lambda j: (0, j)),
                  pl.BlockSpec((1, tn), lambda j: (0, j))],
        out_specs=pl.BlockSpec((rows, tn), lambda j: (0, j)),
        out_shape=jax.ShapeDtypeStruct((rows, n_out), F32),
        compiler_params=pltpu.CompilerParams(dimension_semantics=("arbitrary",),
                                             vmem_limit_bytes=VMEM_LIMIT),
        name="ada",
    )(cc, ada_w, ada_b.reshape(1, n_out))


HALO = 16


def _inproj_kernel(x_ref, xp_ref, xn_ref, mod_ref, ng_ref, wqk_ref, wm_ref, conv_ref, bg_ref,
                   *rest, tm, s5_layout, state_only):
    if state_only:
        kt_ref, v_ref, u_ref, gt_ref, gx_ref, lhs_sc = rest
    else:
        q_ref, kt_ref, v_ref, o_ref, zm_ref, u_ref, zs_ref, gt_ref, gx_ref, lhs_sc = rest
    i = pl.program_id(1)
    nt = pl.num_programs(1)
    shift = mod_ref[0:1, :]
    gain = ng_ref[...] * (1.0 + mod_ref[1:2, :])

    def norm_mod(a):
        ms = jnp.mean(a * a, axis=-1, keepdims=True)
        return a * lax.rsqrt(ms + NORM_EPS) * gain + shift

    lhs_sc[0:tm, :] = norm_mod(x_ref[...]).astype(BF16)
    lhs_sc[tm:tm + HALO, :] = norm_mod(xp_ref[...]).astype(BF16)
    lhs_sc[tm + HALO:tm + 2 * HALO, :] = norm_mod(xn_ref[...]).astype(BF16)

    h = lhs_sc[0:tm, :]
    zg = _dot(h, wm_ref[:, 5120:5120 + GATE_PAD]) + bg_ref[...]
    lane = lax.broadcasted_iota(jnp.int32, zg.shape, 1)
    is_forget = jnp.logical_and(lane < N_GATES, (lane % 8) >= 4)
    gates = jnp.where(is_forget, _log_sigmoid(zg), zg)
    gt = gates.T[0:N_GATES, :]
    gt_ref[...] = gt
    for c0 in range(0, tm, MLSTM_CHUNK):
        parts = []
        for reverse in (False, True):
            parts += list(_chunk_gate_rows(gt[:, c0:c0 + MLSTM_CHUNK], reverse))
        gx_ref[:, c0:c0 + MLSTM_CHUNK] = jnp.concatenate(parts, axis=0)

    valid_prev = (i > 0).astype(F32)
    valid_next = (i < nt - 1).astype(F32)
    for half in range(1 if state_only else 0, 2):
        cols = slice(half * W_BRANCH, (half + 1) * W_BRANCH)
        z = _dot(lhs_sc[...], wqk_ref[:, cols])
        zc = z[0:tm]
        row8 = lax.broadcasted_iota(jnp.int32, (8, W_BRANCH), 0)
        before = pltpu.roll(zc, 1, axis=0)
        first = jnp.where(row8 == 0, z[tm + HALO - 1:tm + HALO] * valid_prev, before[0:8])
        before = jnp.concatenate([first, before[8:]], axis=0)
        after = pltpu.roll(zc, tm - 1, axis=0)
        final = jnp.where(row8 == 7, z[tm + HALO:tm + HALO + 1] * valid_next, after[tm - 8:tm])
        after = jnp.concatenate([after[:tm - 8], final], axis=0)
        cw = conv_ref[:, cols]
        conv = cw[0:1] * before + cw[1:2] * zc + cw[2:3] * after
        if half == 0:
            q_ref[...] = _silu(conv.astype(BF16))
        else:
            kt_ref[...] = (_silu(conv.astype(BF16)) * HEAD_DIM ** -0.5).T

    v_ref[...] = _dot(h, wm_ref[:, 0:1024]).astype(BF16)
    if not state_only:
        o_ref[...] = _dot(h, wm_ref[:, 1024:2048]).astype(BF16)
        zm_ref[...] = _dot(h, wm_ref[:, 2048:3072]).astype(BF16)
        zs_ref[...] = _dot(h, wm_ref[:, 4096:5120]).astype(BF16)
    u = _dot(h, wm_ref[:, 3072:4096])
    if s5_layout:
        for g8 in range(W_BRANCH // 128):
            rows = [u[i * GRID_W:(i + 1) * GRID_W, g8 * 128:(g8 + 1) * 128] for i in range(8)]
            for gp, blk in enumerate(_block_transpose8(rows)):
                u_ref[g8 * 8 + gp] = blk.astype(BF16)
    else:
        u_ref[...] = u.astype(BF16)


def _inproj(x, mod, norm_g, w_qk, w_main, conv_qk, b_gate_pad, tm, s5_layout, state_only=False):
    bsz, t_len, _ = x.shape
    nt = t_len // tm
    nhb = t_len // HALO
    tok = lambda w, dt: jax.ShapeDtypeStruct((bsz, t_len, w), dt)
    tile = lambda w: pl.BlockSpec((None, tm, w), lambda b, i: (b, i, 0))
    tile_t = lambda w: pl.BlockSpec((None, w, tm), lambda b, i: (b, 0, i))
    const = lambda shape: pl.BlockSpec(shape, lambda b, i: (0,) * len(shape),
                                       pipeline_mode=pl.Buffered(1))
    if s5_layout:
        assert tm == 8 * GRID_W and t_len % (2 * tm) == 0
        u_spec = pl.BlockSpec((S5_GROUPS, GRID_W, 128), lambda b, i: (0, (i // 2) * bsz + b, i % 2))
        u_shape = jax.ShapeDtypeStruct((S5_GROUPS, (t_len // (2 * tm)) * bsz * GRID_W, 256), BF16)
    else:
        u_spec, u_shape = tile(W_BRANCH), tok(W_BRANCH, BF16)
    kt_shape = jax.ShapeDtypeStruct((bsz, W_BRANCH, t_len), BF16)
    gt_shape = jax.ShapeDtypeStruct((bsz, N_GATES, t_len), F32)
    if state_only:
        out_specs = [tile_t(W_BRANCH), tile(W_BRANCH), u_spec, tile_t(N_GATES), tile_t(N_GATES)]
        out_shape = [kt_shape, tok(W_BRANCH, BF16), u_shape, gt_shape, gt_shape]
    else:
        out_specs = [tile(W_BRANCH), tile_t(W_BRANCH)] + [tile(W_BRANCH)] * 3 + [
            u_spec, tile(W_BRANCH), tile_t(N_GATES), tile_t(N_GATES)]
        out_shape = [tok(W_BRANCH, BF16), kt_shape] + [tok(W_BRANCH, BF16)] * 3 + [
            u_shape, tok(W_BRANCH, BF16), gt_shape, gt_shape]
    return pl.pallas_call(
        functools.partial(_inproj_kernel, tm=tm, s5_layout=s5_layout, state_only=state_only),
        grid=(bsz, nt),
        in_specs=[
            tile(D_MODEL),
            pl.BlockSpec((None, HALO, D_MODEL),
                         lambda b, i: (b, jnp.maximum(i * (tm // HALO) - 1, 0), 0)),
            pl.BlockSpec((None, HALO, D_MODEL),
                         lambda b, i: (b, jnp.minimum((i + 1) * (tm // HALO), nhb - 1), 0)),
            pl.BlockSpec((None, 3, D_MODEL), lambda b, i: (b, 0, 0)),
            const((1, D_MODEL)),
            const((D_MODEL, 2 * W_BRANCH)),
            const((D_MODEL, 5 * W_BRANCH + GATE_PAD)),
            const((CONV_W, 2 * W_BRANCH)),
            const((1, GATE_PAD)),
        ],
        out_specs=out_specs,
        out_shape=out_shape,
        scratch_shapes=[pltpu.VMEM((tm + 2 * HALO, D_MODEL), BF16)],
        compiler_params=pltpu.CompilerParams(dimension_semantics=("parallel", "arbitrary"),
                                             vmem_limit_bytes=VMEM_LIMIT),
        name="inproj",
    )(x, x, x, mod, norm_g.reshape(1, D_MODEL), w_qk, w_main, conv_qk, b_gate_pad)


def _mlstm_gates(gt_ref, gx_ref, m_sc, *, reverse, chunk, with_output):
    li0 = 8 if reverse else 0
    last = 0 if reverse else chunk - 1
    li_r = gt_ref[li0:li0 + HEADS, :]
    b_r = gx_ref[li0:li0 + HEADS, :]
    b_last = b_r[:, last:last + 1]
    m_old = m_sc[:, 0:1]
    g_r = b_last - b_r + li_r
    m_new = jnp.maximum(b_last + m_old, jnp.max(g_r, axis=1, keepdims=True))
    out = dict(m_old=m_old, m_new=m_new, decay=jnp.exp(b_last + m_old - m_new),
               k_scale=jnp.exp(g_r - m_new).astype(BF16))
    if not with_output:
        return out
    row = lax.broadcasted_iota(jnp.int32, (chunk, chunk), 0)
    col = lax.broadcasted_iota(jnp.int32, (chunk, chunk), 1)
    mask_ts = (col >= row) if reverse else (col <= row)
    mm_r = jnp.maximum(gx_ref[li0 + HEADS:li0 + 2 * HEADS, :], m_old).astype(BF16)
    b_hi, b_lo = _split_bf16(b_r)
    rows = jnp.concatenate([mm_r, b_hi, b_lo, jnp.zeros_like(b_hi)], axis=0)
    sel_r = lax.broadcasted_iota(jnp.int32, (4 * HEADS, 2 * HEADS * 128), 0)
    sel_c = lax.broadcasted_iota(jnp.int32, (4 * HEADS, 2 * HEADS * 128), 1) // 128
    head_r = sel_r % HEADS
    pick = jnp.logical_or(jnp.logical_and(sel_r < HEADS, sel_c % HEADS == head_r),
                          jnp.logical_and(jnp.logical_and(sel_r >= HEADS, sel_r < 3 * HEADS),
                                          sel_c == HEADS + head_r))
    sel = jnp.where(pick, 1.0, 0.0).astype(BF16)
    cols = lax.dot_general(rows, sel, (((0,), (0,)), ((), ())), preferred_element_type=F32)
    out.update(a_r=li_r - b_r, cols=cols, mask_ts=mask_ts)
    return out


def _mlstm_heads(q_ref, kt_ref, v_ref, h_ref, c_sc, n_sc, m_sc, gates, *, chunk):
    with_output = h_ref is not None
    m_old, m_new, decay, k_scale = gates["m_old"], gates["m_new"], gates["decay"], gates["k_scale"]
    if with_output:
        a_r, cols, mask_ts = gates["a_r"], gates["cols"], gates["mask_ts"]
    ones_rows = jnp.ones((8, chunk), BF16)
    wide = lambda a, n: jnp.concatenate([a] * (n // 128), axis=1)
    for hd in range(HEADS):
        sl = slice(hd * HEAD_DIM, (hd + 1) * HEAD_DIM)
        kt = kt_ref[sl, :]
        v = v_ref[:, sl]
        c_old = c_sc[hd]
        n_old = n_sc[hd]
        kw_t = kt * k_scale[hd:hd + 1, :]
        if with_output:
            q = q_ref[:, sl]
            mm_c = cols[:, hd * 128:(hd + 1) * 128]
            bm_c = cols[:, (HEADS + hd) * 128:(HEADS + hd + 1) * 128]
            decay_mat = jnp.exp(jnp.where(mask_ts, a_r[hd:hd + 1, :] - wide(mm_c, chunk), NEG_BIG))
            s_f = _dot(q, kt) * decay_mat
            w_inter = jnp.exp(m_old[hd:hd + 1, :] - mm_c)
            den = (w_inter * jnp.sum(q.astype(F32) * n_old[0:1, :], axis=1, keepdims=True)
                   + jnp.sum(s_f, axis=1, keepdims=True))
            q_w = q * wide(w_inter, HEAD_DIM).astype(BF16)
            num = _dot(jnp.concatenate([q_w, s_f.astype(BF16)], axis=1),
                       jnp.concatenate([c_old.astype(BF16), v], axis=0))
            inv = 1.0 / jnp.maximum(jnp.abs(den), jnp.exp(-bm_c))
            h_ref[:, sl] = (num * wide(inv, HEAD_DIM)).astype(BF16)
        c_sc[hd] = decay[hd:hd + 1, :] * c_old + _dot(kw_t, v)
        n_sc[hd] = decay[hd:hd + 1, :] * n_old + lax.dot_general(
            ones_rows, kw_t, (((1,), (1,)), ((), ())), preferred_element_type=F32)
        m_sc[hd:hd + 1, :] = jnp.broadcast_to(m_new[hd:hd + 1, :], (1, 128))


def _mlstm_kernel(*refs, with_output, chunk):
    n_in = 5 if with_output else 4
    ins = [refs[0:n_in], refs[n_in:2 * n_in]]
    c0_ref, n0_ref, m0_ref = refs[2 * n_in:2 * n_in + 3]
    rest = refs[2 * n_in + 3:]
    if with_output:
        h_refs, rest = rest[0:2], rest[2:]
    else:
        h_refs = (None, None)
        ins = [(None,) + tuple(r) for r in ins]
    co_ref, no_ref, mo_ref, c_sc, n_sc, m_sc = rest
    i = pl.program_id(1)
    nc = pl.num_programs(1)

    @pl.when(i == 0)
    def _():
        c_sc[...] = c0_ref[...]
        n_sc[...] = n0_ref[...]
        m_sc[...] = m0_ref[...]

    gates = [_mlstm_gates(ins[d][3], ins[d][4], m_sc.at[d], reverse=bool(d), chunk=chunk,
                          with_output=with_output) for d in range(2)]
    for d in range(2):
        _mlstm_heads(ins[d][0], ins[d][1], ins[d][2], h_refs[d], c_sc.at[d], n_sc.at[d], m_sc.at[d],
                     gates[d], chunk=chunk)

    @pl.when(i == nc - 1)
    def _():
        co_ref[...] = c_sc[...]
        no_ref[...] = n_sc[...]
        mo_ref[...] = m_sc[...]


def _mlstm(q, kt, v, gt, gx, state, with_output):
    bsz, t_len, _ = v.shape
    chunk = MLSTM_CHUNK
    nc = t_len // chunk
    cidx = (lambda i: i, lambda i: nc - 1 - i)
    tile = lambda w, d: pl.BlockSpec((None, chunk, w), lambda b, i: (b, cidx[d](i), 0))
    tile_t = lambda w, d: pl.BlockSpec((None, w, chunk), lambda b, i: (b, 0, cidx[d](i)))
    st_dims = [(2, HEADS, HEAD_DIM, HEAD_DIM), (2, HEADS, 8, HEAD_DIM), (2, HEADS, 128)]
    st_specs = [pl.BlockSpec((None,) + s, lambda b, i, n=len(s): (b,) + (0,) * n) for s in st_dims]
    st_shapes = [jax.ShapeDtypeStruct((bsz,) + s, F32) for s in st_dims]
    in_specs, args = [], []
    for d in range(2):
        in_specs += ([tile(W_BRANCH, d)] if with_output else []) + [
            tile_t(W_BRANCH, d), tile(W_BRANCH, d), tile_t(N_GATES, d), tile_t(N_GATES, d)]
        args += ([q] if with_output else []) + [kt, v, gt, gx]
    out_specs, out_shape = list(st_specs), list(st_shapes)
    if with_output:
        out_specs = [tile(W_BRANCH, 0), tile(W_BRANCH, 1)] + out_specs
        out_shape = [jax.ShapeDtypeStruct((bsz, t_len, W_BRANCH), BF16)] * 2 + out_shape
    outs = pl.pallas_call(
        functools.partial(_mlstm_kernel, with_output=with_output, chunk=chunk),
        grid=(bsz, nc),
        in_specs=in_specs + st_specs,
        out_specs=out_specs,
        out_shape=out_shape,
        scratch_shapes=[pltpu.VMEM(s, F32) for s in st_dims],
        compiler_params=pltpu.CompilerParams(dimension_semantics=("parallel", "arbitrary"),
                                             vmem_limit_bytes=VMEM_LIMIT),
        name="mlstm_out" if with_output else "mlstm_state",
    )(*args, *state)
    if with_output:
        return (outs[0], outs[1]), tuple(outs[2:])
    return None, tuple(outs)


def _s5_kernel(vc_ref, vx_ref, g_ref, m_ref, p_ref, a_ref, y_ref, gu_sc, s_sc, *,
               nk_ctx, n_rc, bsz, rblk):
    rows_ctx = nk_ctx * bsz
    rows_x = n_rc * bsz * GRID_W

    def increments(v_ref, r0, r1):
        return _dot(v_ref[0, r0:r1, :], g_ref[0]) + _dot(v_ref[1, r0:r1, :], g_ref[1])

    inc = increments(vc_ref, 0, rows_ctx)
    for comp in range(4):
        gu_sc[comp, 0:rows_ctx, :] = inc[:, comp * 128:(comp + 1) * 128]
    for r0 in range(0, rows_x, rblk):
        inc = increments(vx_ref, r0, r0 + rblk)
        for run in range(rblk // GRID_W):
            dst = rows_ctx + (r0 // GRID_W + run) * S5_ROW_PITCH
            for comp in range(4):
                gu_sc[comp, dst:dst + GRID_W, :] = (
                    inc[run * GRID_W:(run + 1) * GRID_W, comp * 128:(comp + 1) * 128])

    a = a_ref[...]
    a_pow = [jnp.broadcast_to(a[:, comp * 128:(comp + 1) * 128], (bsz, 128)) for comp in range(4)]
    zero = jnp.zeros((bsz, 128), F32)

    def cmul(x_r, x_i, y_r, y_i):
        return x_r * y_r - x_i * y_i, x_r * y_i + x_i * y_r

    a_sq = [cmul(a_pow[2 * d], a_pow[2 * d + 1], a_pow[2 * d], a_pow[2 * d + 1]) for d in range(2)]

    def step(rows, carry, direction):
        s_r, s_i = carry
        inc_r = gu_sc[2 * direction, rows, :]
        inc_i = gu_sc[2 * direction + 1, rows, :]
        gu_sc[2 * direction, rows, :] = s_r
        gu_sc[2 * direction + 1, rows, :] = s_i
        p_r, p_i = cmul(a_pow[2 * direction], a_pow[2 * direction + 1], s_r, s_i)
        return p_r + inc_r, p_i + inc_i

    def step2(rows0, rows1, carry, direction):
        s_r, s_i = carry
        a_r, a_i = a_pow[2 * direction], a_pow[2 * direction + 1]
        inc0_r, inc0_i = gu_sc[2 * direction, rows0, :], gu_sc[2 * direction + 1, rows0, :]
        inc1_r, inc1_i = gu_sc[2 * direction, rows1, :], gu_sc[2 * direction + 1, rows1, :]
        gu_sc[2 * direction, rows0, :] = s_r
        gu_sc[2 * direction + 1, rows0, :] = s_i
        m_r, m_i = cmul(a_r, a_i, s_r, s_i)
        gu_sc[2 * direction, rows1, :] = m_r + inc0_r
        gu_sc[2 * direction + 1, rows1, :] = m_i + inc0_i
        c_r, c_i = cmul(a_r, a_i, inc0_r, inc0_i)
        q_r, q_i = cmul(a_sq[direction][0], a_sq[direction][1], s_r, s_i)
        return q_r + (c_r + inc1_r), q_i + (c_i + inc1_i)

    def ctx_rows(k):
        return pl.ds(pl.multiple_of(k * bsz, bsz), bsz)

    def x_rows(w, rc):
        return pl.ds(rows_ctx + rc * (bsz * S5_ROW_PITCH) + w, bsz, stride=S5_ROW_PITCH)

    def ctx_body(k, carry):
        return step(ctx_rows(k), carry[0], 0), step(ctx_rows(nk_ctx - 1 - k), carry[1], 1)

    def x_body(w, carry):
        c_f, c_b = carry
        w_b = GRID_W - 1 - w
        for rc in range(0, n_rc - 1, 2):
            c_f = step2(x_rows(w, rc), x_rows(w, rc + 1), c_f, 0)
            c_b = step2(x_rows(w_b, n_rc - 1 - rc), x_rows(w_b, n_rc - 2 - rc), c_b, 1)
        if n_rc % 2:
            c_f = step(x_rows(w, n_rc - 1), c_f, 0)
            c_b = step(x_rows(w_b, 0), c_b, 1)
        return c_f, c_b

    carry = lax.fori_loop(0, nk_ctx, ctx_body, ((zero, zero), (zero, zero)))
    lax.fori_loop(0, GRID_W, x_body, carry)

    for r0 in range(0, rows_x, rblk):
        r1 = r0 + rblk
        for run in range(rblk // GRID_W):
            src = rows_ctx + (r0 // GRID_W + run) * S5_ROW_PITCH
            for comp in range(4):
                s_sc[r0 + run * GRID_W:r0 + (run + 1) * GRID_W, comp * 128:(comp + 1) * 128] = (
                    gu_sc[comp, src:src + GRID_W, :].astype(BF16))
        for gg in range(2):
            y_ref[gg, r0:r1, :] = (_dot(vx_ref[gg, r0:r1, :], m_ref[gg])
                                   + _dot(s_sc[r0:r1, :], p_ref[gg])).astype(BF16)


def _s5(v_ctx, v_x, g_all, m_all, p_all, a16, bsz):
    rows_ctx, rows_x = v_ctx.shape[1], v_x.shape[1]
    lanes = S5_SUB * S5_GC
    return pl.pallas_call(
        functools.partial(_s5_kernel, nk_ctx=rows_ctx // bsz, n_rc=rows_x // (bsz * GRID_W),
                          bsz=bsz, rblk=512),
        grid=(S5_GROUPS // 2,),
        in_specs=[pl.BlockSpec((2, rows_ctx, lanes), lambda j: (j, 0, 0)),
                  pl.BlockSpec((2, rows_x, lanes), lambda j: (j, 0, 0)),
                  pl.BlockSpec((2, lanes, 512), lambda j: (j, 0, 0)),
                  pl.BlockSpec((2, lanes, lanes), lambda j: (j, 0, 0)),
                  pl.BlockSpec((2, 512, lanes), lambda j: (j, 0, 0)),
                  pl.BlockSpec((None, 1, 512), lambda j: (j, 0, 0))],
        out_specs=pl.BlockSpec((2, rows_x, lanes), lambda j: (j, 0, 0)),
        out_shape=jax.ShapeDtypeStruct((S5_GROUPS, rows_x, lanes), BF16),
        scratch_shapes=[pltpu.VMEM((4, rows_ctx + (rows_x // GRID_W) * S5_ROW_PITCH, 128), F32),
                        pltpu.VMEM((rows_x, 512), BF16)],
        compiler_params=pltpu.CompilerParams(dimension_semantics=("parallel",),
                                             vmem_limit_bytes=VMEM_LIMIT),
        name="s5",
    )(v_ctx, v_x, g_all, m_all, p_all, a16)


def _s5_prep_kernel(lr_ref, lc_ref, bt_ref, ct_ref, d_ref, m_ref, g_ref, p_ref, a_ref):
    hp = lax.Precision.HIGHEST
    n_s, lanes = S5_SUB, S5_SUB * S5_GC
    lane128 = lax.broadcasted_iota(jnp.int32, (n_s, 128), 1)
    blk_of_lane = lax.broadcasted_iota(jnp.int32, (128, lanes), 1) // S5_GC
    g_types, p_types, a16, k_rows = [], [], [], []
    for d in range(2):
        a_r, a_i, log_dt = lr_ref[d, 0:1, :], lr_ref[d, 1:2, :], lr_ref[d, 2:3, :]
        dt = jnp.exp(log_dt)
        lam_r, lam_i = a_r * dt, a_i * dt
        steps = lax.broadcasted_iota(jnp.int32, (24, 128), 0).astype(F32)
        mag = jnp.exp(lam_r * steps)
        pw_r, pw_i = mag * jnp.cos(lam_i * steps), mag * jnp.sin(lam_i * steps)
        nr, ni = pw_r[1:2] - 1.0, pw_i[1:2]
        den = a_r * a_r + a_i * a_i
        co_r, co_i = (nr * a_r + ni * a_i) / den, (ni * a_r - nr * a_i) / den
        b_r = jnp.concatenate([bt_ref[d, 0]] * n_s, axis=0)
        b_i = jnp.concatenate([bt_ref[d, 1]] * n_s, axis=0)
        bb_r, bb_i = co_r * b_r - co_i * b_i, co_r * b_i + co_i * b_r
        order = [n_s - 1 - i for i in range(n_s)] if d == 0 else list(range(n_s))
        pg_r = jnp.concatenate([jnp.broadcast_to(pw_r[n:n + 1], (S5_GC, 128)) for n in order], axis=0)
        pg_i = jnp.concatenate([jnp.broadcast_to(pw_i[n:n + 1], (S5_GC, 128)) for n in order], axis=0)
        g_types += [bb_r * pg_r - bb_i * pg_i, bb_r * pg_i + bb_i * pg_r]
        a16 += [pw_r[n_s:n_s + 1], pw_i[n_s:n_s + 1]]
        x0 = slice((n_s - 1) * S5_GC, n_s * S5_GC) if d == 0 else slice(0, S5_GC)
        x_r, x_i = bb_r[x0], bb_i[x0]
        lhs_r = jnp.concatenate([jnp.where(lane128 < S5_STATE, x_r, 0.0),
                                 jnp.where(lane128 < S5_STATE, 0.0, x_r)], axis=0)
        lhs_i = jnp.concatenate([jnp.where(lane128 < S5_STATE, x_i, 0.0),
                                 jnp.where(lane128 < S5_STATE, 0.0, x_i)], axis=0)
        dt_c = jnp.exp(lc_ref[d, 2])
        lam_rc, lam_ic = lc_ref[d, 0] * dt_c, lc_ref[d, 1] * dt_c
        n_y = (blk_of_lane if d == 0 else n_s - 1 - blk_of_lane).astype(F32)
        mag_y = jnp.exp(lam_rc * n_y)
        ypw_r, ypw_i = mag_y * jnp.cos(lam_ic * n_y), mag_y * jnp.sin(lam_ic * n_y)
        c_r, c_i = ct_ref[d, 0], ct_ref[d, 1]
        y_r, y_i = c_r * ypw_r - c_i * ypw_i, c_r * ypw_i + c_i * ypw_r
        k_rows.append(jnp.dot(lhs_r, y_r, preferred_element_type=F32, precision=hp)
                      - jnp.dot(lhs_i, y_i, preferred_element_type=F32, precision=hp))
        mag_1 = jnp.exp(lam_rc[:, 0:128])
        a1_r, a1_i = mag_1 * jnp.cos(lam_ic[:, 0:128]), mag_1 * jnp.sin(lam_ic[:, 0:128])
        a1_r = jnp.concatenate([a1_r, a1_r], axis=1)
        a1_i = jnp.concatenate([a1_i, a1_i], axis=1)
        p_types += [y_r * a1_r - y_i * a1_i, -(y_r * a1_i + y_i * a1_r)]

    a_ref[...] = jnp.concatenate(a16, axis=1)
    lane_g = lax.broadcasted_iota(jnp.int32, (lanes, 128), 1)
    row_p = lax.broadcasted_iota(jnp.int32, (128, lanes), 0)
    lane_k = lax.broadcasted_iota(jnp.int32, (S5_GC, lanes), 1)
    row_m = lax.broadcasted_iota(jnp.int32, (lanes, lanes), 0)
    lane_m = lax.broadcasted_iota(jnp.int32, (lanes, lanes), 1)
    for h in range(2):
        mine_l = (lane_g >= S5_STATE) == bool(h)
        g_ref[h] = jnp.concatenate([jnp.where(mine_l, t, 0.0) for t in g_types], axis=1).astype(BF16)
        mine_r = (row_p >= S5_STATE) == bool(h)
        p_ref[h] = jnp.concatenate([jnp.where(mine_r, t, 0.0) for t in p_types], axis=0).astype(BF16)
        k_f = k_rows[0][h * S5_GC:(h + 1) * S5_GC]
        k_b = k_rows[1][h * S5_GC:(h + 1) * S5_GC]
        blocks = []
        for i in range(n_s):
            up, down = S5_GC * i, S5_GC * (n_s - 1 - i)
            f = k_f if up == 0 else jnp.where(lane_k >= up, pltpu.roll(k_f, up, axis=1), 0.0)
            b = k_b if down == 0 else jnp.where(lane_k < lanes - down,
                                                pltpu.roll(k_b, lanes - down, axis=1), 0.0)
            blocks.append(f + b)
        m = jnp.concatenate(blocks, axis=0) + jnp.where(row_m == lane_m, d_ref[h], 0.0)
        m_ref[h] = m.astype(BF16)


def _s5_prep(a_re, a_im, log_step, b_re, b_im, c_re, c_im, d_skip):
    n_g, n_p, n_c, n_s = S5_GROUPS, S5_STATE, S5_GC, S5_SUB
    lanes = n_s * n_c
    pair = lambda a: jnp.transpose(a.astype(F32).reshape(2, n_g // 2, 2 * n_p), (1, 0, 2))
    lam_row = jnp.stack([pair(a_re), pair(a_im),
                         pair(jnp.broadcast_to(log_step[..., None], a_re.shape))], axis=2)
    lam_col = jnp.broadcast_to(lam_row[..., None], lam_row.shape + (lanes,))
    bt = lambda b: jnp.transpose(b.astype(F32).reshape(2, n_g // 2, 2, n_p, n_c),
                                 (1, 0, 4, 2, 3)).reshape(n_g // 2, 2, n_c, 2 * n_p)
    b_t = jnp.stack([bt(b_re), bt(b_im)], axis=2)
    ct = lambda c: jnp.tile(jnp.transpose(c.astype(F32).reshape(2, n_g // 2, 2, n_c, n_p),
                                          (1, 0, 2, 4, 3)).reshape(n_g // 2, 2, 2 * n_p, n_c),
                            (1, 1, 1, n_s))
    c_t = jnp.stack([ct(c_re), ct(c_im)], axis=2)
    d_row = jnp.tile(d_skip.astype(F32).reshape(n_g // 2, 2, 1, n_c), (1, 1, 1, n_s))
    blk = lambda *s: pl.BlockSpec((None,) + s, lambda j: (j,) + (0,) * len(s))
    grp = lambda *s: pl.BlockSpec((2,) + s, lambda j: (j,) + (0,) * len(s))
    return pl.pallas_call(
        _s5_prep_kernel,
        grid=(n_g // 2,),
        in_specs=[blk(2, 3, 2 * n_p), blk(2, 3, 2 * n_p, lanes), blk(2, 2, n_c, 2 * n_p),
                  blk(2, 2, 2 * n_p, lanes), blk(2, 1, lanes)],
        out_specs=[grp(lanes, lanes), grp(lanes, 8 * n_p), grp(8 * n_p, lanes), blk(1, 8 * n_p)],
        out_shape=[jax.ShapeDtypeStruct((n_g, lanes, lanes), BF16),
                   jax.ShapeDtypeStruct((n_g, lanes, 8 * n_p), BF16),
                   jax.ShapeDtypeStruct((n_g, 8 * n_p, lanes), BF16),
                   jax.ShapeDtypeStruct((n_g // 2, 1, 8 * n_p), F32)],
        compiler_params=pltpu.CompilerParams(dimension_semantics=("parallel",),
                                             vmem_limit_bytes=VMEM_LIMIT),
        name="s5_prep",
    )(lam_row, lam_col, b_t, c_t, d_row)


def _merge_kernel(hf_ref, hb_ref, o_ref, zm_ref, y_ref, zs_ref, x_ref, mod_ref, mhg_ref,
                  gluw_ref, glub_ref, wout_ref, fg_ref, out_ref, y_sc):
    for g8 in range(W_BRANCH // 128):
        rows = [y_ref[g8 * 8 + gp].astype(F32) for gp in range(8)]
        for i, blk in enumerate(_block_transpose8(rows)):
            y_sc[i * GRID_W:(i + 1) * GRID_W, g8 * 128:(g8 + 1) * 128] = blk

    hm = ((hf_ref[...] + hb_ref[...]) * jax.nn.sigmoid(o_ref[...])).astype(F32)
    mhg = mhg_ref[...]
    parts = []
    for hd in range(HEADS):
        sl = slice(hd * HEAD_DIM, (hd + 1) * HEAD_DIM)
        seg = hm[:, sl]
        mu = jnp.mean(seg, axis=-1, keepdims=True)
        dev = seg - mu
        var = jnp.mean(dev * dev, axis=-1, keepdims=True)
        parts.append(dev * lax.rsqrt(var + NORM_EPS) * mhg[:, sl])
    m_out = jnp.concatenate(parts, axis=-1).astype(BF16) * _silu(zm_ref[...])

    y = y_sc[...].astype(BF16)
    gl = 0.5 * y * (1.0 + jnp.tanh(0.7978845608028654 * (y + 0.044715 * (y * y * y))))
    gate = jax.nn.sigmoid((_dot(gl, gluw_ref[...]) + glub_ref[...]).astype(BF16))
    s_out = gl * gate * _silu(zs_ref[...])

    mixed = _dot(m_out, wout_ref[0:W_BRANCH, :]) + _dot(s_out, wout_ref[W_BRANCH:2 * W_BRANCH, :])
    xo = x_ref[...] + mod_ref[2:3, :] * mixed
    ms = jnp.mean(xo * xo, axis=-1, keepdims=True)
    out_ref[...] = xo * lax.rsqrt(ms + NORM_EPS) * fg_ref[...]


def _merge(hf, hb, o, zm, y, zs, x, mod, mh_g, glu_w, glu_b, w_out, final_g):
    bsz, t_len, _ = x.shape
    tm = 8 * GRID_W
    tile = pl.BlockSpec((None, tm, D_MODEL), lambda b, i: (b, i, 0))
    y_spec = pl.BlockSpec((S5_GROUPS, GRID_W, 128), lambda b, i: (0, (i // 2) * bsz + b, i % 2))
    const = lambda shape: pl.BlockSpec(shape, lambda b, i: (0,) * len(shape),
                                       pipeline_mode=pl.Buffered(1))
    return pl.pallas_call(
        _merge_kernel,
        grid=(bsz, t_len // tm),
        in_specs=[tile] * 4 + [y_spec, tile, tile,
                               pl.BlockSpec((None, 3, D_MODEL), lambda b, i: (b, 0, 0)),
                               const((1, W_BRANCH)), const((W_BRANCH, W_BRANCH)),
                               const((1, W_BRANCH)), const((2 * W_BRANCH, D_MODEL)),
                               const((1, D_MODEL))],
        out_specs=tile,
        out_shape=jax.ShapeDtypeStruct((bsz, t_len, D_MODEL), F32),
        scratch_shapes=[pltpu.VMEM((tm, W_BRANCH), F32)],
        compiler_params=pltpu.CompilerParams(dimension_semantics=("parallel", "arbitrary"),
                                             vmem_limit_bytes=VMEM_LIMIT),
        name="merge",
    )(hf, hb, o, zm, y, zs, x, mod, mh_g.reshape(1, -1), glu_w, glu_b.reshape(1, -1), w_out,
      final_g.reshape(1, -1))


def _s5_rows_ctx(u):
    bsz, t_len, _ = u.shape
    a = u.reshape(bsz, t_len // S5_SUB, S5_SUB, S5_GROUPS, S5_GC)
    a = jnp.transpose(a, (3, 1, 0, 2, 4))
    return a.reshape(S5_GROUPS, (t_len // S5_SUB) * bsz, S5_SUB * S5_GC)


def kernel(x, c, ctx, c_ctx, norm_g, ada_w, ada_b, w_in, b_gate, conv_qk, mh_g, s5_a_re, s5_a_im,
           s5_log_step, s5_b_re, s5_b_im, s5_c_re, s5_c_im, s5_d, glu_w, glu_b, w_out, final_g):
    bsz, t_len, _ = x.shape
    layer = 0

    cc = jnp.zeros((16, D_MODEL), F32).at[:bsz].set(c).at[bsz].set(c_ctx)
    mod = _ada(cc, ada_w[layer], ada_b[layer]).reshape(16, 3, D_MODEL)
    mod_x = mod[:bsz]
    mod_c = jnp.broadcast_to(mod[bsz][None], (bsz, 3, D_MODEL))

    w = w_in[layer]
    wb = W_BRANCH
    w_qk = w[:, 0:2 * wb].astype(BF16)
    gate0 = 5 * wb
    w_gate = jnp.pad(w[:, gate0:gate0 + N_GATES], ((0, 0), (0, GATE_PAD - N_GATES)))
    w_main = jnp.concatenate([w[:, 2 * wb:5 * wb], w[:, gate0 + N_GATES:], w_gate], axis=1).astype(BF16)
    b_gate_pad = jnp.pad(b_gate[layer].reshape(1, N_GATES), ((0, 0), (0, GATE_PAD - N_GATES)))

    proj = functools.partial(_inproj, norm_g=norm_g[layer], w_qk=w_qk, w_main=w_main,
                             conv_qk=conv_qk[layer], b_gate_pad=b_gate_pad)
    kt_c, v_c, u_c, gt_c, gx_c = proj(ctx, mod_c, tm=256, s5_layout=False, state_only=True)
    q_x, kt_x, v_x, o_x, zm_x, u_x, zs_x, gt_x, gx_x = proj(x, mod_x, tm=8 * GRID_W, s5_layout=True)

    zero_state = (jnp.zeros((bsz, 2, HEADS, HEAD_DIM, HEAD_DIM), F32),
                  jnp.zeros((bsz, 2, HEADS, 8, HEAD_DIM), F32),
                  jnp.zeros((bsz, 2, HEADS, 128), F32))
    _, ctx_state = _mlstm(None, kt_c, v_c, gt_c, gx_c, zero_state, with_output=False)
    (h_f, h_b), _ = _mlstm(q_x, kt_x, v_x, gt_x, gx_x, ctx_state, with_output=True)

    m_all, g_all, p_all, a16 = _s5_prep(
        s5_a_re[layer], s5_a_im[layer], s5_log_step[layer], s5_b_re[layer], s5_b_im[layer],
        s5_c_re[layer], s5_c_im[layer], s5_d[layer])
    y_x = _s5(_s5_rows_ctx(u_c), u_x, g_all, m_all, p_all, a16, bsz)

    return _merge(h_f, h_b, o_x, zm_x, y_x, zs_x, x, mod_x, mh_g[layer], glu_w[layer].astype(BF16),
                  glu_b[layer], w_out[layer].astype(BF16), final_g)
```

```python
import functools

import jax
import jax.numpy as jnp
from jax import lax
from jax.experimental import pallas as pl
from jax.experimental.pallas import tpu as pltpu

F32 = jnp.float32
BF16 = jnp.bfloat16

D_MODEL = 1024
HEADS = 4
HEAD_DIM = 256
W_BRANCH = 1024
S5_GROUPS = 64
S5_GC = 16
S5_STATE = 64
S5_SUB = 16
GRID_W = 64
S5_ROW_PITCH = 72
N_GATES = 16
GATE_PAD = 128
CONV_W = 3
NORM_EPS = 1e-6
MLSTM_CHUNK = 256
NEG_BIG = -1e30

VMEM_LIMIT = 56 * 1024 * 1024


def _silu(a):
    return a * jax.nn.sigmoid(a)


def _log_sigmoid(a):
    return jnp.minimum(a, 0.0) - jnp.log1p(jnp.exp(-jnp.abs(a)))


def _dot(a, b):
    return jnp.dot(a, b, preferred_element_type=F32)


def _split_bf16(a):
    hi = a.astype(BF16)
    lo = (a - hi.astype(F32)).astype(BF16)
    return hi, lo


def _block_transpose8(rows):
    lane = lax.broadcasted_iota(jnp.int32, rows[0].shape, 1)
    blk = lane // S5_GC
    for d in (4, 2, 1):
        keep = (blk & d) == 0
        new = list(rows)
        for i in range(8):
            if i & d == 0:
                a, b = rows[i], rows[i + d]
                new[i] = jnp.where(keep, a, pltpu.roll(b, d * S5_GC, axis=1))
                new[i + d] = jnp.where(keep, pltpu.roll(a, 128 - d * S5_GC, axis=1), b)
        rows = new
    return rows


def _chunk_gate_rows(gt_blk, reverse):
    chunk = gt_blk.shape[1]
    row = lax.broadcasted_iota(jnp.int32, (chunk, chunk), 0)
    col = lax.broadcasted_iota(jnp.int32, (chunk, chunk), 1)
    mask_st = (row >= col) if reverse else (row <= col)
    tri_st = jnp.where(mask_st, 1.0, 0.0).astype(BF16)
    hi, lo = _split_bf16(gt_blk)
    cum = _dot(hi, tri_st) + _dot(lo, tri_st)
    li0 = 8 if reverse else 0
    b_r = cum[li0 + HEADS:li0 + 2 * HEADS, :]
    run_max = gt_blk[li0:li0 + HEADS, :] - b_r
    lane = lax.broadcasted_iota(jnp.int32, run_max.shape, 1)
    shift = 1
    while shift < chunk:
        if reverse:
            moved = jnp.where(lane < chunk - shift, pltpu.roll(run_max, chunk - shift, axis=1), NEG_BIG)
        else:
            moved = jnp.where(lane >= shift, pltpu.roll(run_max, shift, axis=1), NEG_BIG)
        run_max = jnp.maximum(run_max, moved)
        shift *= 2
    return b_r, run_max


def _ada_kernel(c_ref, w_ref, b_ref, o_ref):
    s = _silu(c_ref[...])
    o_ref[...] = jnp.dot(s, w_ref[...], preferred_element_type=F32,
                         precision=lax.Precision.HIGHEST) + b_ref[...]


def _ada(cc, ada_w, ada_b):
    rows = cc.shape[0]
    n_out = ada_w.shape[1]
    tn = 1024
    return pl.pallas_call(
        _ada_kernel,
        grid=(n_out // tn,),
        in_specs=[pl.BlockSpec((rows, D_MODEL), lambda j: (0, 0)),
                  pl.BlockSpec((D_MODEL, tn), lambda j: (0, j)),
                  pl.BlockSpec((1, tn), lambda j: (0, j))],
        out_specs=pl.BlockSpec((rows, tn), lambda j: (0, j)),
        out_shape=jax.ShapeDtypeStruct((rows, n_out), F32),
        compiler_params=pltpu.CompilerParams(dimension_semantics=("arbitrary",),
                                             vmem_limit_bytes=VMEM_LIMIT),
        name="ada",
    )(cc, ada_w, ada_b.reshape(1, n_out))


HALO = 16


def _inproj_kernel(x_ref, xp_ref, xn_ref, mod_ref, ng_ref, wqk_ref, wm_ref, conv_ref, bg_ref,
                   *rest, tm, s5_layout, state_only):
    if state_only:
        kt_ref, v_ref, u_ref, gt_ref, gx_ref, lhs_sc = rest
    else:
        q_ref, kt_ref, v_ref, o_ref, zm_ref, u_ref, zs_ref, gt_ref, gx_ref, lhs_sc = rest
    i = pl.program_id(1)
    nt = pl.num_programs(1)
    shift = mod_ref[0:1, :]
    gain = ng_ref[...] * (1.0 + mod_ref[1:2, :])

    def norm_mod(a):
        ms = jnp.mean(a * a, axis=-1, keepdims=True)
        return a * lax.rsqrt(ms + NORM_EPS) * gain + shift

    lhs_sc[0:tm, :] = norm_mod(x_ref[...]).astype(BF16)
    lhs_sc[tm:tm + HALO, :] = norm_mod(xp_ref[...]).astype(BF16)
    lhs_sc[tm + HALO:tm + 2 * HALO, :] = norm_mod(xn_ref[...]).astype(BF16)

    h = lhs_sc[0:tm, :]
    zg = _dot(h, wm_ref[:, 5120:5120 + GATE_PAD]) + bg_ref[...]
    lane = lax.broadcasted_iota(jnp.int32, zg.shape, 1)
    is_forget = jnp.logical_and(lane < N_GATES, (lane % 8) >= 4)
    gates = jnp.where(is_forget, _log_sigmoid(zg), zg)
    gt = gates.T[0:N_GATES, :]
    gt_ref[...] = gt
    for c0 in range(0, tm, MLSTM_CHUNK):
        parts = []
        for reverse in (False, True):
            parts += list(_chunk_gate_rows(gt[:, c0:c0 + MLSTM_CHUNK], reverse))
        gx_ref[:, c0:c0 + MLSTM_CHUNK] = jnp.concatenate(parts, axis=0)

    u = _dot(h, wm_ref[:, 3072:4096])
    if s5_layout:
        for g8 in range(W_BRANCH // 128):
            rows = [u[i * GRID_W:(i + 1) * GRID_W, g8 * 128:(g8 + 1) * 128] for i in range(8)]
            for gp, blk in enumerate(_block_transpose8(rows)):
                u_ref[g8 * 8 + gp] = blk.astype(BF16)
    else:
        u_ref[...] = u.astype(BF16)

    valid_prev = (i > 0).astype(F32)
    valid_next = (i < nt - 1).astype(F32)

    def conv_stage(half):
        cols = slice(half * W_BRANCH, (half + 1) * W_BRANCH)
        z = _dot(lhs_sc[...], wqk_ref[:, cols])
        zc = z[0:tm]
        row8 = lax.broadcasted_iota(jnp.int32, (8, W_BRANCH), 0)
        before = pltpu.roll(zc, 1, axis=0)
        first = jnp.where(row8 == 0, z[tm + HALO - 1:tm + HALO] * valid_prev, before[0:8])
        before = jnp.concatenate([first, before[8:]], axis=0)
        after = pltpu.roll(zc, tm - 1, axis=0)
        final = jnp.where(row8 == 7, z[tm + HALO:tm + HALO + 1] * valid_next, after[tm - 8:tm])
        after = jnp.concatenate([after[:tm - 8], final], axis=0)
        cw = conv_ref[:, cols]
        conv = cw[0:1] * before + cw[1:2] * zc + cw[2:3] * after
        if half == 0:
            q_ref[...] = _silu(conv.astype(BF16))
        else:
            kt_ref[...] = (_silu(conv.astype(BF16)) * HEAD_DIM ** -0.5).T

    v_ref[...] = _dot(h, wm_ref[:, 0:1024]).astype(BF16)
    if not state_only:
        conv_stage(0)
        o_ref[...] = _dot(h, wm_ref[:, 1024:2048]).astype(BF16)
    conv_stage(1)
    if not state_only:
        zm_ref[...] = _dot(h, wm_ref[:, 2048:3072]).astype(BF16)
        zs_ref[...] = _dot(h, wm_ref[:, 4096:5120]).astype(BF16)


def _inproj(x, mod, norm_g, w_qk, w_main, conv_qk, b_gate_pad, tm, s5_layout, state_only=False):
    bsz, t_len, _ = x.shape
    nt = t_len // tm
    nhb = t_len // HALO
    tok = lambda w, dt: jax.ShapeDtypeStruct((bsz, t_len, w), dt)
    tile = lambda w: pl.BlockSpec((None, tm, w), lambda b, i: (b, i, 0))
    tile_t = lambda w: pl.BlockSpec((None, w, tm), lambda b, i: (b, 0, i))
    const = lambda shape: pl.BlockSpec(shape, lambda b, i: (0,) * len(shape),
                                       pipeline_mode=pl.Buffered(1))
    if s5_layout:
        assert tm == 8 * GRID_W and t_len % (2 * tm) == 0
        u_spec = pl.BlockSpec((S5_GROUPS, GRID_W, 128), lambda b, i: (0, (i // 2) * bsz + b, i % 2))
        u_shape = jax.ShapeDtypeStruct((S5_GROUPS, (t_len // (2 * tm)) * bsz * GRID_W, 256), BF16)
    else:
        u_spec, u_shape = tile(W_BRANCH), tok(W_BRANCH, BF16)
    kt_shape = jax.ShapeDtypeStruct((bsz, W_BRANCH, t_len), BF16)
    gt_shape = jax.ShapeDtypeStruct((bsz, N_GATES, t_len), F32)
    if state_only:
        out_specs = [tile_t(W_BRANCH), tile(W_BRANCH), u_spec, tile_t(N_GATES), tile_t(N_GATES)]
        out_shape = [kt_shape, tok(W_BRANCH, BF16), u_shape, gt_shape, gt_shape]
    else:
        out_specs = [tile(W_BRANCH), tile_t(W_BRANCH)] + [tile(W_BRANCH)] * 3 + [
            u_spec, tile(W_BRANCH), tile_t(N_GATES), tile_t(N_GATES)]
        out_shape = [tok(W_BRANCH, BF16), kt_shape] + [tok(W_BRANCH, BF16)] * 3 + [
            u_shape, tok(W_BRANCH, BF16), gt_shape, gt_shape]
    return pl.pallas_call(
        functools.partial(_inproj_kernel, tm=tm, s5_layout=s5_layout, state_only=state_only),
        grid=(bsz, nt),
        in_specs=[
            tile(D_MODEL),
            pl.BlockSpec((None, HALO, D_MODEL),
                         lambda b, i: (b, jnp.maximum(i * (tm // HALO) - 1, 0), 0)),
            pl.BlockSpec((None, HALO, D_MODEL),
                         lambda b, i: (b, jnp.minimum((i + 1) * (tm // HALO), nhb - 1), 0)),
            pl.BlockSpec((None, 3, D_MODEL), lambda b, i: (b, 0, 0)),
            const((1, D_MODEL)),
            const((D_MODEL, 2 * W_BRANCH)),
            const((D_MODEL, 5 * W_BRANCH + GATE_PAD)),
            const((CONV_W, 2 * W_BRANCH)),
            const((1, GATE_PAD)),
        ],
        out_specs=out_specs,
        out_shape=out_shape,
        scratch_shapes=[pltpu.VMEM((tm + 2 * HALO, D_MODEL), BF16)],
        compiler_params=pltpu.CompilerParams(dimension_semantics=("parallel", "arbitrary"),
                                             vmem_limit_bytes=VMEM_LIMIT),
        name="inproj",
    )(x, x, x, mod, norm_g.reshape(1, D_MODEL), w_qk, w_main, conv_qk, b_gate_pad)


def _mlstm_gates(gt_ref, gx_ref, m_sc, *, reverse, chunk, with_output):
    li0 = 8 if reverse else 0
    last = 0 if reverse else chunk - 1
    li_r = gt_ref[li0:li0 + HEADS, :]
    b_r = gx_ref[li0:li0 + HEADS, :]
    b_last = b_r[:, last:last + 1]
    m_old = m_sc[:, 0:1]
    g_r = b_last - b_r + li_r
    m_new = jnp.maximum(b_last + m_old, jnp.max(g_r, axis=1, keepdims=True))
    out = dict(m_old=m_old, m_new=m_new, decay=jnp.exp(b_last + m_old - m_new),
               k_scale=jnp.exp(g_r - m_new).astype(BF16))
    if not with_output:
        return out
    row = lax.broadcasted_iota(jnp.int32, (chunk, chunk), 0)
    col = lax.broadcasted_iota(jnp.int32, (chunk, chunk), 1)
    mask_ts = (col >= row) if reverse else (col <= row)
    mm_r = jnp.maximum(gx_ref[li0 + HEADS:li0 + 2 * HEADS, :], m_old).astype(BF16)
    b_hi, b_lo = _split_bf16(b_r)
    rows = jnp.concatenate([mm_r, b_hi, b_lo, jnp.zeros_like(b_hi)], axis=0)
    sel_r = lax.broadcasted_iota(jnp.int32, (4 * HEADS, 2 * HEADS * 128), 0)
    sel_c = lax.broadcasted_iota(jnp.int32, (4 * HEADS, 2 * HEADS * 128), 1) // 128
    head_r = sel_r % HEADS
    pick = jnp.logical_or(jnp.logical_and(sel_r < HEADS, sel_c % HEADS == head_r),
                          jnp.logical_and(jnp.logical_and(sel_r >= HEADS, sel_r < 3 * HEADS),
                                          sel_c == HEADS + head_r))
    sel = jnp.where(pick, 1.0, 0.0).astype(BF16)
    cols = lax.dot_general(rows, sel, (((0,), (0,)), ((), ())), preferred_element_type=F32)
    out.update(a_r=li_r - b_r, cols=cols, mask_ts=mask_ts)
    return out


def _mlstm_heads(q_ref, kt_ref, v_ref, h_ref, c_sc, n_sc, m_sc, gates, *, chunk):
    with_output = h_ref is not None
    m_old, m_new, decay, k_scale = gates["m_old"], gates["m_new"], gates["decay"], gates["k_scale"]
    if with_output:
        a_r, cols, mask_ts = gates["a_r"], gates["cols"], gates["mask_ts"]
    ones_rows = jnp.ones((8, chunk), BF16)
    wide = lambda a, n: jnp.concatenate([a] * (n // 128), axis=1)
    for hd in range(HEADS):
        sl = slice(hd * HEAD_DIM, (hd + 1) * HEAD_DIM)
        kt = kt_ref[sl, :]
        v = v_ref[:, sl]
        c_old = c_sc[hd]
        n_old = n_sc[hd]
        kw_t = kt * k_scale[hd:hd + 1, :]
        if with_output:
            q = q_ref[:, sl]
            mm_c = cols[:, hd * 128:(hd + 1) * 128]
            bm_c = cols[:, (HEADS + hd) * 128:(HEADS + hd + 1) * 128]
            decay_mat = jnp.exp(jnp.where(mask_ts, a_r[hd:hd + 1, :] - wide(mm_c, chunk), NEG_BIG))
            s_f = _dot(q, kt) * decay_mat
            w_inter = jnp.exp(m_old[hd:hd + 1, :] - mm_c)
            den = (w_inter * jnp.sum(q.astype(F32) * n_old[0:1, :], axis=1, keepdims=True)
                   + jnp.sum(s_f, axis=1, keepdims=True))
            q_w = q * wide(w_inter, HEAD_DIM).astype(BF16)
            num = _dot(jnp.concatenate([q_w, s_f.astype(BF16)], axis=1),
                       jnp.concatenate([c_old.astype(BF16), v], axis=0))
            inv = 1.0 / jnp.maximum(jnp.abs(den), jnp.exp(-bm_c))
            h_ref[:, sl] = (num * wide(inv, HEAD_DIM)).astype(BF16)
        c_sc[hd] = decay[hd:hd + 1, :] * c_old + _dot(kw_t, v)
        n_sc[hd] = decay[hd:hd + 1, :] * n_old + lax.dot_general(
            ones_rows, kw_t, (((1,), (1,)), ((), ())), preferred_element_type=F32)
        m_sc[hd:hd + 1, :] = jnp.broadcast_to(m_new[hd:hd + 1, :], (1, 128))


def _mlstm_kernel(*refs, with_output, chunk):
    n_in = 5 if with_output else 4
    ins = [refs[0:n_in], refs[n_in:2 * n_in]]
    c0_ref, n0_ref, m0_ref = refs[2 * n_in:2 * n_in + 3]
    rest = refs[2 * n_in + 3:]
    if with_output:
        h_refs, rest = rest[0:2], rest[2:]
    else:
        h_refs = (None, None)
        ins = [(None,) + tuple(r) for r in ins]
    co_ref, no_ref, mo_ref, c_sc, n_sc, m_sc = rest
    i = pl.program_id(1)
    nc = pl.num_programs(1)

    @pl.when(i == 0)
    def _():
        c_sc[...] = c0_ref[...]
        n_sc[...] = n0_ref[...]
        m_sc[...] = m0_ref[...]

    gates = [_mlstm_gates(ins[d][3], ins[d][4], m_sc.at[d], reverse=bool(d), chunk=chunk,
                          with_output=with_output) for d in range(2)]
    for d in range(2):
        _mlstm_heads(ins[d][0], ins[d][1], ins[d][2], h_refs[d], c_sc.at[d], n_sc.at[d], m_sc.at[d],
                     gates[d], chunk=chunk)

    @pl.when(i == nc - 1)
    def _():
        co_ref[...] = c_sc[...]
        no_ref[...] = n_sc[...]
        mo_ref[...] = m_sc[...]


def _mlstm(q, kt, v, gt, gx, state, with_output):
    bsz, t_len, _ = v.shape
    chunk = MLSTM_CHUNK
    nc = t_len // chunk
    cidx = (lambda i: i, lambda i: nc - 1 - i)
    tile = lambda w, d: pl.BlockSpec((None, chunk, w), lambda b, i: (b, cidx[d](i), 0))
    tile_t = lambda w, d: pl.BlockSpec((None, w, chunk), lambda b, i: (b, 0, cidx[d](i)))
    st_dims = [(2, HEADS, HEAD_DIM, HEAD_DIM), (2, HEADS, 8, HEAD_DIM), (2, HEADS, 128)]
    st_specs = [pl.BlockSpec((None,) + s, lambda b, i, n=len(s): (b,) + (0,) * n) for s in st_dims]
    st_shapes = [jax.ShapeDtypeStruct((bsz,) + s, F32) for s in st_dims]
    in_specs, args = [], []
    for d in range(2):
        in_specs += ([tile(W_BRANCH, d)] if with_output else []) + [
            tile_t(W_BRANCH, d), tile(W_BRANCH, d), tile_t(N_GATES, d), tile_t(N_GATES, d)]
        args += ([q] if with_output else []) + [kt, v, gt, gx]
    out_specs, out_shape = list(st_specs), list(st_shapes)
    if with_output:
        out_specs = [tile(W_BRANCH, 0), tile(W_BRANCH, 1)] + out_specs
        out_shape = [jax.ShapeDtypeStruct((bsz, t_len, W_BRANCH), BF16)] * 2 + out_shape
    outs = pl.pallas_call(
        functools.partial(_mlstm_kernel, with_output=with_output, chunk=chunk),
        grid=(bsz, nc),
        in_specs=in_specs + st_specs,
        out_specs=out_specs,
        out_shape=out_shape,
        scratch_shapes=[pltpu.VMEM(s, F32) for s in st_dims],
        compiler_params=pltpu.CompilerParams(dimension_semantics=("parallel", "arbitrary"),
                                             vmem_limit_bytes=VMEM_LIMIT),
        name="mlstm_out" if with_output else "mlstm_state",
    )(*args, *state)
    if with_output:
        return (outs[0], outs[1]), tuple(outs[2:])
    return None, tuple(outs)


def _s5_kernel(vc_ref, vx_ref, g_ref, m_ref, p_ref, a_ref, y_ref, gu_sc, s_sc, *,
               nk_ctx, n_rc, bsz, rblk):
    rows_ctx = nk_ctx * bsz
    rows_x = n_rc * bsz * GRID_W

    def increments(v_ref, r0, r1):
        return _dot(v_ref[0, r0:r1, :], g_ref[0]) + _dot(v_ref[1, r0:r1, :], g_ref[1])

    inc = increments(vc_ref, 0, rows_ctx)
    for comp in range(4):
        gu_sc[comp, 0:rows_ctx, :] = inc[:, comp * 128:(comp + 1) * 128]
    for r0 in range(0, rows_x, rblk):
        inc = increments(vx_ref, r0, r0 + rblk)
        for run in range(rblk // GRID_W):
            dst = rows_ctx + (r0 // GRID_W + run) * S5_ROW_PITCH
            for comp in range(4):
                gu_sc[comp, dst:dst + GRID_W, :] = (
                    inc[run * GRID_W:(run + 1) * GRID_W, comp * 128:(comp + 1) * 128])

    a = a_ref[...]
    a_pow = [jnp.broadcast_to(a[:, comp * 128:(comp + 1) * 128], (bsz, 128)) for comp in range(4)]
    zero = jnp.zeros((bsz, 128), F32)

    def cmul(x_r, x_i, y_r, y_i):
        return x_r * y_r - x_i * y_i, x_r * y_i + x_i * y_r

    a_sq = [cmul(a_pow[2 * d], a_pow[2 * d + 1], a_pow[2 * d], a_pow[2 * d + 1]) for d in range(2)]

    def step(rows, carry, direction):
        s_r, s_i = carry
        inc_r = gu_sc[2 * direction, rows, :]
        inc_i = gu_sc[2 * direction + 1, rows, :]
        gu_sc[2 * direction, rows, :] = s_r
        gu_sc[2 * direction + 1, rows, :] = s_i
        p_r, p_i = cmul(a_pow[2 * direction], a_pow[2 * direction + 1], s_r, s_i)
        return p_r + inc_r, p_i + inc_i

    def step2(rows0, rows1, carry, direction):
        s_r, s_i = carry
        a_r, a_i = a_pow[2 * direction], a_pow[2 * direction + 1]
        inc0_r, inc0_i = gu_sc[2 * direction, rows0, :], gu_sc[2 * direction + 1, rows0, :]
        inc1_r, inc1_i = gu_sc[2 * direction, rows1, :], gu_sc[2 * direction + 1, rows1, :]
        gu_sc[2 * direction, rows0, :] = s_r
        gu_sc[2 * direction + 1, rows0, :] = s_i
        m_r, m_i = cmul(a_r, a_i, s_r, s_i)
        gu_sc[2 * direction, rows1, :] = m_r + inc0_r
        gu_sc[2 * direction + 1, rows1, :] = m_i + inc0_i
        c_r, c_i = cmul(a_r, a_i, inc0_r, inc0_i)
        q_r, q_i = cmul(a_sq[direction][0], a_sq[direction][1], s_r, s_i)
        return q_r + (c_r + inc1_r), q_i + (c_i + inc1_i)

    def ctx_rows(k):
        return pl.ds(pl.multiple_of(k * bsz, bsz), bsz)

    def x_rows(w, rc):
        return pl.ds(rows_ctx + rc * (bsz * S5_ROW_PITCH) + w, bsz, stride=S5_ROW_PITCH)

    def ctx_body(k, carry):
        return step(ctx_rows(k), carry[0], 0), step(ctx_rows(nk_ctx - 1 - k), carry[1], 1)

    def x_body(w, carry):
        c_f, c_b = carry
        w_b = GRID_W - 1 - w
        for rc in range(0, n_rc - 1, 2):
            c_f = step2(x_rows(w, rc), x_rows(w, rc + 1), c_f, 0)
            c_b = step2(x_rows(w_b, n_rc - 1 - rc), x_rows(w_b, n_rc - 2 - rc), c_b, 1)
        if n_rc % 2:
            c_f = step(x_rows(w, n_rc - 1), c_f, 0)
            c_b = step(x_rows(w_b, 0), c_b, 1)
        return c_f, c_b

    carry = lax.fori_loop(0, nk_ctx, ctx_body, ((zero, zero), (zero, zero)))
    lax.fori_loop(0, GRID_W, x_body, carry)

    for r0 in range(0, rows_x, rblk):
        r1 = r0 + rblk
        for run in range(rblk // GRID_W):
            src = rows_ctx + (r0 // GRID_W + run) * S5_ROW_PITCH
            for comp in range(4):
                s_sc[r0 + run * GRID_W:r0 + (run + 1) * GRID_W, comp * 128:(comp + 1) * 128] = (
                    gu_sc[comp, src:src + GRID_W, :].astype(BF16))
        for gg in range(2):
            y_ref[gg, r0:r1, :] = (_dot(vx_ref[gg, r0:r1, :], m_ref[gg])
                                   + _dot(s_sc[r0:r1, :], p_ref[gg])).astype(BF16)


def _s5(v_ctx, v_x, g_all, m_all, p_all, a16, bsz):
    rows_ctx, rows_x = v_ctx.shape[1], v_x.shape[1]
    lanes = S5_SUB * S5_GC
    return pl.pallas_call(
        functools.partial(_s5_kernel, nk_ctx=rows_ctx // bsz, n_rc=rows_x // (bsz * GRID_W),
                          bsz=bsz, rblk=512),
        grid=(S5_GROUPS // 2,),
        in_specs=[pl.BlockSpec((2, rows_ctx, lanes), lambda j: (j, 0, 0)),
                  pl.BlockSpec((2, rows_x, lanes), lambda j: (j, 0, 0)),
                  pl.BlockSpec((2, lanes, 512), lambda j: (j, 0, 0)),
                  pl.BlockSpec((2, lanes, lanes), lambda j: (j, 0, 0)),
                  pl.BlockSpec((2, 512, lanes), lambda j: (j, 0, 0)),
                  pl.BlockSpec((None, 1, 512), lambda j: (j, 0, 0))],
        out_specs=pl.BlockSpec((2, rows_x, lanes), lambda j: (j, 0, 0)),
        out_shape=jax.ShapeDtypeStruct((S5_GROUPS, rows_x, lanes), BF16),
        scratch_shapes=[pltpu.VMEM((4, rows_ctx + (rows_x // GRID_W) * S5_ROW_PITCH, 128), F32),
                        pltpu.VMEM((rows_x, 512), BF16)],
        compiler_params=pltpu.CompilerParams(dimension_semantics=("parallel",),
                                             vmem_limit_bytes=VMEM_LIMIT),
        name="s5",
    )(v_ctx, v_x, g_all, m_all, p_all, a16)


def _s5_prep_kernel(lr_ref, lc_ref, bt_ref, ct_ref, d_ref, m_ref, g_ref, p_ref, a_ref):
    hp = lax.Precision.HIGHEST
    n_s, lanes = S5_SUB, S5_SUB * S5_GC
    lane128 = lax.broadcasted_iota(jnp.int32, (n_s, 128), 1)
    blk_of_lane = lax.broadcasted_iota(jnp.int32, (128, lanes), 1) // S5_GC
    g_types, p_types, a16, k_rows = [], [], [], []
    for d in range(2):
        a_r, a_i, log_dt = lr_ref[d, 0:1, :], lr_ref[d, 1:2, :], lr_ref[d, 2:3, :]
        dt = jnp.exp(log_dt)
        lam_r, lam_i = a_r * dt, a_i * dt
        steps = lax.broadcasted_iota(jnp.int32, (24, 128), 0).astype(F32)
        mag = jnp.exp(lam_r * steps)
        pw_r, pw_i = mag * jnp.cos(lam_i * steps), mag * jnp.sin(lam_i * steps)
        nr, ni = pw_r[1:2] - 1.0, pw_i[1:2]
        den = a_r * a_r + a_i * a_i
        co_r, co_i = (nr * a_r + ni * a_i) / den, (ni * a_r - nr * a_i) / den
        b_r = jnp.concatenate([bt_ref[d, 0]] * n_s, axis=0)
        b_i = jnp.concatenate([bt_ref[d, 1]] * n_s, axis=0)
        bb_r, bb_i = co_r * b_r - co_i * b_i, co_r * b_i + co_i * b_r
        order = [n_s - 1 - i for i in range(n_s)] if d == 0 else list(range(n_s))
        pg_r = jnp.concatenate([jnp.broadcast_to(pw_r[n:n + 1], (S5_GC, 128)) for n in order], axis=0)
        pg_i = jnp.concatenate([jnp.broadcast_to(pw_i[n:n + 1], (S5_GC, 128)) for n in order], axis=0)
        g_types += [bb_r * pg_r - bb_i * pg_i, bb_r * pg_i + bb_i * pg_r]
        a16 += [pw_r[n_s:n_s + 1], pw_i[n_s:n_s + 1]]
        x0 = slice((n_s - 1) * S5_GC, n_s * S5_GC) if d == 0 else slice(0, S5_GC)
        x_r, x_i = bb_r[x0], bb_i[x0]
        lhs_r = jnp.concatenate([jnp.where(lane128 < S5_STATE, x_r, 0.0),
                                 jnp.where(lane128 < S5_STATE, 0.0, x_r)], axis=0)
        lhs_i = jnp.concatenate([jnp.where(lane128 < S5_STATE, x_i, 0.0),
                                 jnp.where(lane128 < S5_STATE, 0.0, x_i)], axis=0)
        dt_c = jnp.exp(lc_ref[d, 2])
        lam_rc, lam_ic = lc_ref[d, 0] * dt_c, lc_ref[d, 1] * dt_c
        n_y = (blk_of_lane if d == 0 else n_s - 1 - blk_of_lane).astype(F32)
        mag_y = jnp.exp(lam_rc * n_y)
        ypw_r, ypw_i = mag_y * jnp.cos(lam_ic * n_y), mag_y * jnp.sin(lam_ic * n_y)
        c_r, c_i = ct_ref[d, 0], ct_ref[d, 1]
        y_r, y_i = c_r * ypw_r - c_i * ypw_i, c_r * ypw_i + c_i * ypw_r
        k_rows.append(jnp.dot(lhs_r, y_r, preferred_element_type=F32, precision=hp)
                      - jnp.dot(lhs_i, y_i, preferred_element_type=F32, precision=hp))
        mag_1 = jnp.exp(lam_rc[:, 0:128])
        a1_r, a1_i = mag_1 * jnp.cos(lam_ic[:, 0:128]), mag_1 * jnp.sin(lam_ic[:, 0:128])
        a1_r = jnp.concatenate([a1_r, a1_r], axis=1)
        a1_i = jnp.concatenate([a1_i, a1_i], axis=1)
        p_types += [y_r * a1_r - y_i * a1_i, -(y_r * a1_i + y_i * a1_r)]

    a_ref[...] = jnp.concatenate(a16, axis=1)
    lane_g = lax.broadcasted_iota(jnp.int32, (lanes, 128), 1)
    row_p = lax.broadcasted_iota(jnp.int32, (128, lanes), 0)
    lane_k = lax.broadcasted_iota(jnp.int32, (S5_GC, lanes), 1)
    row_m = lax.broadcasted_iota(jnp.int32, (lanes, lanes), 0)
    lane_m = lax.broadcasted_iota(jnp.int32, (lanes, lanes), 1)
    for h in range(2):
        mine_l = (lane_g >= S5_STATE) == bool(h)
        g_ref[h] = jnp.concatenate([jnp.where(mine_l, t, 0.0) for t in g_types], axis=1).astype(BF16)
        mine_r = (row_p >= S5_STATE) == bool(h)
        p_ref[h] = jnp.concatenate([jnp.where(mine_r, t, 0.0) for t in p_types], axis=0).astype(BF16)
        k_f = k_rows[0][h * S5_GC:(h + 1) * S5_GC]
        k_b = k_rows[1][h * S5_GC:(h + 1) * S5_GC]
        blocks = []
        for i in range(n_s):
            up, down = S5_GC * i, S5_GC * (n_s - 1 - i)
            f = k_f if up == 0 else jnp.where(lane_k >= up, pltpu.roll(k_f, up, axis=1), 0.0)
            b = k_b if down == 0 else jnp.where(lane_k < lanes - down,
                                                pltpu.roll(k_b, lanes - down, axis=1), 0.0)
            blocks.append(f + b)
        m = jnp.concatenate(blocks, axis=0) + jnp.where(row_m == lane_m, d_ref[h], 0.0)
        m_ref[h] = m.astype(BF16)


def _s5_prep(a_re, a_im, log_step, b_re, b_im, c_re, c_im, d_skip):
    n_g, n_p, n_c, n_s = S5_GROUPS, S5_STATE, S5_GC, S5_SUB
    lanes = n_s * n_c
    pair = lambda a: jnp.transpose(a.astype(F32).reshape(2, n_g // 2, 2 * n_p), (1, 0, 2))
    lam_row = jnp.stack([pair(a_re), pair(a_im),
                         pair(jnp.broadcast_to(log_step[..., None], a_re.shape))], axis=2)
    lam_col = jnp.broadcast_to(lam_row[..., None], lam_row.shape + (lanes,))
    bt = lambda b: jnp.transpose(b.astype(F32).reshape(2, n_g // 2, 2, n_p, n_c),
                                 (1, 0, 4, 2, 3)).reshape(n_g // 2, 2, n_c, 2 * n_p)
    b_t = jnp.stack([bt(b_re), bt(b_im)], axis=2)
    ct = lambda c: jnp.tile(jnp.transpose(c.astype(F32).reshape(2, n_g // 2, 2, n_c, n_p),
                                          (1, 0, 2, 4, 3)).reshape(n_g // 2, 2, 2 * n_p, n_c),
                            (1, 1, 1, n_s))
    c_t = jnp.stack([ct(c_re), ct(c_im)], axis=2)
    d_row = jnp.tile(d_skip.astype(F32).reshape(n_g // 2, 2, 1, n_c), (1, 1, 1, n_s))
    blk = lambda *s: pl.BlockSpec((None,) + s, lambda j: (j,) + (0,) * len(s))
    grp = lambda *s: pl.BlockSpec((2,) + s, lambda j: (j,) + (0,) * len(s))
    return pl.pallas_call(
        _s5_prep_kernel,
        grid=(n_g // 2,),
        in_specs=[blk(2, 3, 2 * n_p), blk(2, 3, 2 * n_p, lanes), blk(2, 2, n_c, 2 * n_p),
                  blk(2, 2, 2 * n_p, lanes), blk(2, 1, lanes)],
        out_specs=[grp(lanes, lanes), grp(lanes, 8 * n_p), grp(8 * n_p, lanes), blk(1, 8 * n_p)],
        out_shape=[jax.ShapeDtypeStruct((n_g, lanes, lanes), BF16),
                   jax.ShapeDtypeStruct((n_g, lanes, 8 * n_p), BF16),
                   jax.ShapeDtypeStruct((n_g, 8 * n_p, lanes), BF16),
                   jax.ShapeDtypeStruct((n_g // 2, 1, 8 * n_p), F32)],
        compiler_params=pltpu.CompilerParams(dimension_semantics=("parallel",),
                                             vmem_limit_bytes=VMEM_LIMIT),
        name="s5_prep",
    )(lam_row, lam_col, b_t, c_t, d_row)


def _merge_kernel(hf_ref, hb_ref, o_ref, zm_ref, y_ref, zs_ref, x_ref, mod_ref, mhg_ref,
                  gluw_ref, glub_ref, wout_ref, fg_ref, out_ref, y_sc):
    for g8 in range(W_BRANCH // 128):
        rows = [y_ref[g8 * 8 + gp].astype(F32) for gp in range(8)]
        for i, blk in enumerate(_block_transpose8(rows)):
            y_sc[i * GRID_W:(i + 1) * GRID_W, g8 * 128:(g8 + 1) * 128] = blk

    hm = ((hf_ref[...] + hb_ref[...]) * jax.nn.sigmoid(o_ref[...])).astype(F32)
    mhg = mhg_ref[...]
    parts = []
    for hd in range(HEADS):
        sl = slice(hd * HEAD_DIM, (hd + 1) * HEAD_DIM)
        seg = hm[:, sl]
        mu = jnp.mean(seg, axis=-1, keepdims=True)
        dev = seg - mu
        var = jnp.mean(dev * dev, axis=-1, keepdims=True)
        parts.append(dev * lax.rsqrt(var + NORM_EPS) * mhg[:, sl])
    m_out = jnp.concatenate(parts, axis=-1).astype(BF16) * _silu(zm_ref[...])

    y = y_sc[...].astype(BF16)
    gl = 0.5 * y * (1.0 + jnp.tanh(0.7978845608028654 * (y + 0.044715 * (y * y * y))))
    gate = jax.nn.sigmoid((_dot(gl, gluw_ref[...]) + glub_ref[...]).astype(BF16))
    s_out = gl * gate * _silu(zs_ref[...])

    mixed = _dot(m_out, wout_ref[0:W_BRANCH, :]) + _dot(s_out, wout_ref[W_BRANCH:2 * W_BRANCH, :])
    xo = x_ref[...] + mod_ref[2:3, :] * mixed
    ms = jnp.mean(xo * xo, axis=-1, keepdims=True)
    out_ref[...] = xo * lax.rsqrt(ms + NORM_EPS) * fg_ref[...]


def _merge(hf, hb, o, zm, y, zs, x, mod, mh_g, glu_w, glu_b, w_out, final_g):
    bsz, t_len, _ = x.shape
    tm = 8 * GRID_W
    tile = pl.BlockSpec((None, tm, D_MODEL), lambda b, i: (b, i, 0))
    y_spec = pl.BlockSpec((S5_GROUPS, GRID_W, 128), lambda b, i: (0, (i // 2) * bsz + b, i % 2))
    const = lambda shape: pl.BlockSpec(shape, lambda b, i: (0,) * len(shape),
                                       pipeline_mode=pl.Buffered(1))
    return pl.pallas_call(
        _merge_kernel,
        grid=(bsz, t_len // tm),
        in_specs=[tile] * 4 + [y_spec, tile, tile,
                               pl.BlockSpec((None, 3, D_MODEL), lambda b, i: (b, 0, 0)),
                               const((1, W_BRANCH)), const((W_BRANCH, W_BRANCH)),
                               const((1, W_BRANCH)), const((2 * W_BRANCH, D_MODEL)),
                               const((1, D_MODEL))],
        out_specs=tile,
        out_shape=jax.ShapeDtypeStruct((bsz, t_len, D_MODEL), F32),
        scratch_shapes=[pltpu.VMEM((tm, W_BRANCH), F32)],
        compiler_params=pltpu.CompilerParams(dimension_semantics=("parallel", "arbitrary"),
                                             vmem_limit_bytes=VMEM_LIMIT),
        name="merge",
    )(hf, hb, o, zm, y, zs, x, mod, mh_g.reshape(1, -1), glu_w, glu_b.reshape(1, -1), w_out,
      final_g.reshape(1, -1))


def _s5_rows_ctx(u):
    bsz, t_len, _ = u.shape
    a = u.reshape(bsz, t_len // S5_SUB, S5_SUB, S5_GROUPS, S5_GC)
    a = jnp.transpose(a, (3, 1, 0, 2, 4))
    return a.reshape(S5_GROUPS, (t_len // S5_SUB) * bsz, S5_SUB * S5_GC)


def kernel(x, c, ctx, c_ctx, norm_g, ada_w, ada_b, w_in, b_gate, conv_qk, mh_g, s5_a_re, s5_a_im,
           s5_log_step, s5_b_re, s5_b_im, s5_c_re, s5_c_im, s5_d, glu_w, glu_b, w_out, final_g):
    bsz, t_len, _ = x.shape
    layer = 0

    cc = jnp.zeros((16, D_MODEL), F32).at[:bsz].set(c).at[bsz].set(c_ctx)
    mod = _ada(cc, ada_w[layer], ada_b[layer]).reshape(16, 3, D_MODEL)
    mod_x = mod[:bsz]
    mod_c = jnp.broadcast_to(mod[bsz][None], (bsz, 3, D_MODEL))

    w = w_in[layer]
    wb = W_BRANCH
    w_qk = w[:, 0:2 * wb].astype(BF16)
    gate0 = 5 * wb
    w_gate = jnp.pad(w[:, gate0:gate0 + N_GATES], ((0, 0), (0, GATE_PAD - N_GATES)))
    w_main = jnp.concatenate([w[:, 2 * wb:5 * wb], w[:, gate0 + N_GATES:], w_gate], axis=1).astype(BF16)
    b_gate_pad = jnp.pad(b_gate[layer].reshape(1, N_GATES), ((0, 0), (0, GATE_PAD - N_GATES)))

    proj = functools.partial(_inproj, norm_g=norm_g[layer], w_qk=w_qk, w_main=w_main,
                             conv_qk=conv_qk[layer], b_gate_pad=b_gate_pad)
    kt_c, v_c, u_c, gt_c, gx_c = proj(ctx, mod_c, tm=256, s5_layout=False, state_only=True)
    q_x, kt_x, v_x, o_x, zm_x, u_x, zs_x, gt_x, gx_x = proj(x, mod_x, tm=8 * GRID_W, s5_layout=True)

    zero_state = (jnp.zeros((bsz, 2, HEADS, HEAD_DIM, HEAD_DIM), F32),
                  jnp.zeros((bsz, 2, HEADS, 8, HEAD_DIM), F32),
                  jnp.zeros((bsz, 2, HEADS, 128), F32))
    _, ctx_state = _mlstm(None, kt_c, v_c, gt_c, gx_c, zero_state, with_output=False)
    (h_f, h_b), _ = _mlstm(q_x, kt_x, v_x, gt_x, gx_x, ctx_state, with_output=True)

    m_all, g_all, p_all, a16 = _s5_prep(
        s5_a_re[layer], s5_a_im[layer], s5_log_step[layer], s5_b_re[layer], s5_b_im[layer],
        s5_c_re[layer], s5_c_im[layer], s5_d[layer])
    y_x = _s5(_s5_rows_ctx(u_c), u_x, g_all, m_all, p_all, a16, bsz)

    return _merge(h_f, h_b, o_x, zm_x, y_x, zs_x, x, mod_x, mh_g[layer], glu_w[layer].astype(BF16),
                  glu_b[layer], w_out[layer].astype(BF16), final_g)
```

```python
import functools

import jax
import jax.numpy as jnp
from jax import lax
from jax.experimental import pallas as pl
from jax.experimental.pallas import tpu as pltpu

F32 = jnp.float32
BF16 = jnp.bfloat16

D_MODEL = 1024
HEADS = 4
HEAD_DIM = 256
W_BRANCH = 1024
S5_GROUPS = 64
S5_GC = 16
S5_STATE = 64
S5_SUB = 16
GRID_W = 64
S5_ROW_PITCH = 72
N_GATES = 16
GATE_PAD = 128
CONV_W = 3
NORM_EPS = 1e-6
MLSTM_CHUNK = 256
MLSTM_CHUNKS_PER_STEP = (4, 2, 1)
NEG_BIG = -1e30

VMEM_LIMIT = 56 * 1024 * 1024


def _silu(a):
    return a * jax.nn.sigmoid(a)


def _log_sigmoid(a):
    return jnp.minimum(a, 0.0) - jnp.log1p(jnp.exp(-jnp.abs(a)))


def _dot(a, b):
    return jnp.dot(a, b, preferred_element_type=F32)


def _split_bf16(a):
    hi = a.astype(BF16)
    lo = (a - hi.astype(F32)).astype(BF16)
    return hi, lo


def _block_transpose8(rows):
    lane = lax.broadcasted_iota(jnp.int32, rows[0].shape, 1)
    blk = lane // S5_GC
    for d in (4, 2, 1):
        keep = (blk & d) == 0
        new = list(rows)
        for i in range(8):
            if i & d == 0:
                a, b = rows[i], rows[i + d]
                new[i] = jnp.where(keep, a, pltpu.roll(b, d * S5_GC, axis=1))
                new[i + d] = jnp.where(keep, pltpu.roll(a, 128 - d * S5_GC, axis=1), b)
        rows = new
    return rows


def _chunk_gate_rows(gt_blk, reverse):
    chunk = gt_blk.shape[1]
    row = lax.broadcasted_iota(jnp.int32, (chunk, chunk), 0)
    col = lax.broadcasted_iota(jnp.int32, (chunk, chunk), 1)
    mask_st = (row >= col) if reverse else (row <= col)
    tri_st = jnp.where(mask_st, 1.0, 0.0).astype(BF16)
    hi, lo = _split_bf16(gt_blk)
    cum = _dot(hi, tri_st) + _dot(lo, tri_st)
    li0 = 8 if reverse else 0
    b_r = cum[li0 + HEADS:li0 + 2 * HEADS, :]
    run_max = gt_blk[li0:li0 + HEADS, :] - b_r
    lane = lax.broadcasted_iota(jnp.int32, run_max.shape, 1)
    shift = 1
    while shift < chunk:
        if reverse:
            moved = jnp.where(lane < chunk - shift, pltpu.roll(run_max, chunk - shift, axis=1), NEG_BIG)
        else:
            moved = jnp.where(lane >= shift, pltpu.roll(run_max, shift, axis=1), NEG_BIG)
        run_max = jnp.maximum(run_max, moved)
        shift *= 2
    return b_r, run_max


def _ada_kernel(c_ref, w_ref, b_ref, o_ref):
    s = _silu(c_ref[...])
    o_ref[...] = jnp.dot(s, w_ref[...], preferred_element_type=F32,
                         precision=lax.Precision.HIGHEST) + b_ref[...]


def _ada(cc, ada_w, ada_b):
    rows = cc.shape[0]
    n_out = ada_w.shape[1]
    tn = 1024
    return pl.pallas_call(
        _ada_kernel,
        grid=(n_out // tn,),
        in_specs=[pl.BlockSpec((rows, D_MODEL), lambda j: (0, 0)),
                  pl.BlockSpec((D_MODEL, tn), lambda j: (0, j)),
                  pl.BlockSpec((1, tn), lambda j: (0, j))],
        out_specs=pl.BlockSpec((rows, tn), lambda j: (0, j)),
        out_shape=jax.ShapeDtypeStruct((rows, n_out), F32),
        compiler_params=pltpu.CompilerParams(dimension_semantics=("arbitrary",),
                                             vmem_limit_bytes=VMEM_LIMIT),
        name="ada",
    )(cc, ada_w, ada_b.reshape(1, n_out))


HALO = 16


def _inproj_kernel(x_ref, xp_ref, xn_ref, mod_ref, ng_ref, wqk_ref, wm_ref, conv_ref, bg_ref,
                   *rest, tm, s5_layout, state_only):
    if state_only:
        kt_ref, v_ref, u_ref, gt_ref, gx_ref, lhs_sc = rest
    else:
        q_ref, kt_ref, v_ref, o_ref, zm_ref, u_ref, zs_ref, gt_ref, gx_ref, lhs_sc = rest
    i = pl.program_id(1)
    nt = pl.num_programs(1)
    shift = mod_ref[0:1, :]
    gain = ng_ref[...] * (1.0 + mod_ref[1:2, :])

    def norm_mod(a):
        ms = jnp.mean(a * a, axis=-1, keepdims=True)
        return a * lax.rsqrt(ms + NORM_EPS) * gain + shift

    lhs_sc[0:tm, :] = norm_mod(x_ref[...]).astype(BF16)
    lhs_sc[tm:tm + HALO, :] = norm_mod(xp_ref[...]).astype(BF16)
    lhs_sc[tm + HALO:tm + 2 * HALO, :] = norm_mod(xn_ref[...]).astype(BF16)

    h = lhs_sc[0:tm, :]
    zg = _dot(h, wm_ref[:, 5120:5120 + GATE_PAD]) + bg_ref[...]
    lane = lax.broadcasted_iota(jnp.int32, zg.shape, 1)
    is_forget = jnp.logical_and(lane < N_GATES, (lane % 8) >= 4)
    gates = jnp.where(is_forget, _log_sigmoid(zg), zg)
    gt = gates.T[0:N_GATES, :]
    gt_ref[...] = gt
    for c0 in range(0, tm, MLSTM_CHUNK):
        parts = []
        for reverse in (False, True):
            parts += list(_chunk_gate_rows(gt[:, c0:c0 + MLSTM_CHUNK], reverse))
        gx_ref[:, c0:c0 + MLSTM_CHUNK] = jnp.concatenate(parts, axis=0)

    u = _dot(h, wm_ref[:, 3072:4096])
    if s5_layout:
        for g8 in range(W_BRANCH // 128):
            rows = [u[i * GRID_W:(i + 1) * GRID_W, g8 * 128:(g8 + 1) * 128] for i in range(8)]
            for gp, blk in enumerate(_block_transpose8(rows)):
                u_ref[g8 * 8 + gp] = blk.astype(BF16)
    else:
        u_ref[...] = u.astype(BF16)

    valid_prev = (i > 0).astype(F32)
    valid_next = (i < nt - 1).astype(F32)

    def conv_stage(half):
        cols = slice(half * W_BRANCH, (half + 1) * W_BRANCH)
        z = _dot(lhs_sc[...], wqk_ref[:, cols])
        zc = z[0:tm]
        row8 = lax.broadcasted_iota(jnp.int32, (8, W_BRANCH), 0)
        before = pltpu.roll(zc, 1, axis=0)
        first = jnp.where(row8 == 0, z[tm + HALO - 1:tm + HALO] * valid_prev, before[0:8])
        before = jnp.concatenate([first, before[8:]], axis=0)
        after = pltpu.roll(zc, tm - 1, axis=0)
        final = jnp.where(row8 == 7, z[tm + HALO:tm + HALO + 1] * valid_next, after[tm - 8:tm])
        after = jnp.concatenate([after[:tm - 8], final], axis=0)
        cw = conv_ref[:, cols]
        conv = cw[0:1] * before + cw[1:2] * zc + cw[2:3] * after
        if half == 0:
            q_ref[...] = _silu(conv.astype(BF16))
        else:
            kt_ref[...] = (_silu(conv.astype(BF16)) * HEAD_DIM ** -0.5).T

    v_ref[...] = _dot(h, wm_ref[:, 0:1024]).astype(BF16)
    if not state_only:
        conv_stage(0)
        o_ref[...] = _dot(h, wm_ref[:, 1024:2048]).astype(BF16)
    conv_stage(1)
    if not state_only:
        zm_ref[...] = _dot(h, wm_ref[:, 2048:3072]).astype(BF16)
        zs_ref[...] = _dot(h, wm_ref[:, 4096:5120]).astype(BF16)


def _inproj(x, mod, norm_g, w_qk, w_main, conv_qk, b_gate_pad, tm, s5_layout, state_only=False):
    bsz, t_len, _ = x.shape
    nt = t_len // tm
    nhb = t_len // HALO
    tok = lambda w, dt: jax.ShapeDtypeStruct((bsz, t_len, w), dt)
    tile = lambda w: pl.BlockSpec((None, tm, w), lambda b, i: (b, i, 0))
    tile_t = lambda w: pl.BlockSpec((None, w, tm), lambda b, i: (b, 0, i))
    const = lambda shape: pl.BlockSpec(shape, lambda b, i: (0,) * len(shape),
                                       pipeline_mode=pl.Buffered(1))
    if s5_layout:
        assert tm == 8 * GRID_W and t_len % (2 * tm) == 0
        u_spec = pl.BlockSpec((S5_GROUPS, GRID_W, 128), lambda b, i: (0, (i // 2) * bsz + b, i % 2))
        u_shape = jax.ShapeDtypeStruct((S5_GROUPS, (t_len // (2 * tm)) * bsz * GRID_W, 256), BF16)
    else:
        u_spec, u_shape = tile(W_BRANCH), tok(W_BRANCH, BF16)
    kt_shape = jax.ShapeDtypeStruct((bsz, W_BRANCH, t_len), BF16)
    gt_shape = jax.ShapeDtypeStruct((bsz, N_GATES, t_len), F32)
    if state_only:
        out_specs = [tile_t(W_BRANCH), tile(W_BRANCH), u_spec, tile_t(N_GATES), tile_t(N_GATES)]
        out_shape = [kt_shape, tok(W_BRANCH, BF16), u_shape, gt_shape, gt_shape]
    else:
        out_specs = [tile(W_BRANCH), tile_t(W_BRANCH)] + [tile(W_BRANCH)] * 3 + [
            u_spec, tile(W_BRANCH), tile_t(N_GATES), tile_t(N_GATES)]
        out_shape = [tok(W_BRANCH, BF16), kt_shape] + [tok(W_BRANCH, BF16)] * 3 + [
            u_shape, tok(W_BRANCH, BF16), gt_shape, gt_shape]
    return pl.pallas_call(
        functools.partial(_inproj_kernel, tm=tm, s5_layout=s5_layout, state_only=state_only),
        grid=(bsz, nt),
        in_specs=[
            tile(D_MODEL),
            pl.BlockSpec((None, HALO, D_MODEL),
                         lambda b, i: (b, jnp.maximum(i * (tm // HALO) - 1, 0), 0)),
            pl.BlockSpec((None, HALO, D_MODEL),
                         lambda b, i: (b, jnp.minimum((i + 1) * (tm // HALO), nhb - 1), 0)),
            pl.BlockSpec((None, 3, D_MODEL), lambda b, i: (b, 0, 0)),
            const((1, D_MODEL)),
            const((D_MODEL, 2 * W_BRANCH)),
            const((D_MODEL, 5 * W_BRANCH + GATE_PAD)),
            const((CONV_W, 2 * W_BRANCH)),
            const((1, GATE_PAD)),
        ],
        out_specs=out_specs,
        out_shape=out_shape,
        scratch_shapes=[pltpu.VMEM((tm + 2 * HALO, D_MODEL), BF16)],
        compiler_params=pltpu.CompilerParams(dimension_semantics=("parallel", "arbitrary"),
                                             vmem_limit_bytes=VMEM_LIMIT),
        name="inproj",
    )(x, x, x, mod, norm_g.reshape(1, D_MODEL), w_qk, w_main, conv_qk, b_gate_pad)


def _mlstm_gates(gt_ref, gx_ref, m_sc, *, reverse, chunk, with_output):
    li0 = 8 if reverse else 0
    last = 0 if reverse else chunk - 1
    li_r = gt_ref[li0:li0 + HEADS, :]
    b_r = gx_ref[li0:li0 + HEADS, :]
    b_last = b_r[:, last:last + 1]
    m_old = m_sc[:, 0:1]
    g_r = b_last - b_r + li_r
    m_new = jnp.maximum(b_last + m_old, jnp.max(g_r, axis=1, keepdims=True))
    out = dict(m_old=m_old, m_new=m_new, decay=jnp.exp(b_last + m_old - m_new),
               k_scale=jnp.exp(g_r - m_new).astype(BF16))
    if not with_output:
        return out
    row = lax.broadcasted_iota(jnp.int32, (chunk, chunk), 0)
    col = lax.broadcasted_iota(jnp.int32, (chunk, chunk), 1)
    mask_ts = (col >= row) if reverse else (col <= row)
    mm_r = jnp.maximum(gx_ref[li0 + HEADS:li0 + 2 * HEADS, :], m_old).astype(BF16)
    b_hi, b_lo = _split_bf16(b_r)
    rows = jnp.concatenate([mm_r, b_hi, b_lo, jnp.zeros_like(b_hi)], axis=0)
    sel_r = lax.broadcasted_iota(jnp.int32, (4 * HEADS, 2 * HEADS * 128), 0)
    sel_c = lax.broadcasted_iota(jnp.int32, (4 * HEADS, 2 * HEADS * 128), 1) // 128
    head_r = sel_r % HEADS
    pick = jnp.logical_or(jnp.logical_and(sel_r < HEADS, sel_c % HEADS == head_r),
                          jnp.logical_and(jnp.logical_and(sel_r >= HEADS, sel_r < 3 * HEADS),
                                          sel_c == HEADS + head_r))
    sel = jnp.where(pick, 1.0, 0.0).astype(BF16)
    cols = lax.dot_general(rows, sel, (((0,), (0,)), ((), ())), preferred_element_type=F32)
    out.update(a_r=li_r - b_r, cols=cols, mask_ts=mask_ts)
    return out


def _mlstm_heads(q_ref, kt_ref, v_ref, h_ref, c_sc, n_sc, m_sc, gates, *, chunk):
    with_output = h_ref is not None
    m_old, m_new, decay, k_scale = gates["m_old"], gates["m_new"], gates["decay"], gates["k_scale"]
    if with_output:
        a_r, cols, mask_ts = gates["a_r"], gates["cols"], gates["mask_ts"]
    ones_rows = jnp.ones((8, chunk), BF16)
    wide = lambda a, n: jnp.concatenate([a] * (n // 128), axis=1)
    for hd in range(HEADS):
        sl = slice(hd * HEAD_DIM, (hd + 1) * HEAD_DIM)
        kt = kt_ref[sl, :]
        v = v_ref[:, sl]
        c_old = c_sc[hd]
        n_old = n_sc[hd]
        kw_t = kt * k_scale[hd:hd + 1, :]
        if with_output:
            q = q_ref[:, sl]
            mm_c = cols[:, hd * 128:(hd + 1) * 128]
            bm_c = cols[:, (HEADS + hd) * 128:(HEADS + hd + 1) * 128]
            decay_mat = jnp.exp(jnp.where(mask_ts, a_r[hd:hd + 1, :] - wide(mm_c, chunk), NEG_BIG))
            s_f = _dot(q, kt) * decay_mat
            w_inter = jnp.exp(m_old[hd:hd + 1, :] - mm_c)
            den = (w_inter * jnp.sum(q.astype(F32) * n_old[0:1, :], axis=1, keepdims=True)
                   + jnp.sum(s_f, axis=1, keepdims=True))
            q_w = q * wide(w_inter, HEAD_DIM).astype(BF16)
            num = _dot(jnp.concatenate([q_w, s_f.astype(BF16)], axis=1),
                       jnp.concatenate([c_old.astype(BF16), v], axis=0))
            inv = 1.0 / jnp.maximum(jnp.abs(den), jnp.exp(-bm_c))
            h_ref[:, sl] = (num * wide(inv, HEAD_DIM)).astype(BF16)
        c_sc[hd] = decay[hd:hd + 1, :] * c_old + _dot(kw_t, v)
        n_sc[hd] = decay[hd:hd + 1, :] * n_old + lax.dot_general(
            ones_rows, kw_t, (((1,), (1,)), ((), ())), preferred_element_type=F32)
        m_sc[hd:hd + 1, :] = jnp.broadcast_to(m_new[hd:hd + 1, :], (1, 128))


def _mlstm_kernel(*refs, with_output, chunk, subs):
    n_in = 5 if with_output else 4
    ins = [refs[0:n_in], refs[n_in:2 * n_in]]
    c0_ref, n0_ref, m0_ref = refs[2 * n_in:2 * n_in + 3]
    rest = refs[2 * n_in + 3:]
    if with_output:
        h_refs, rest = rest[0:2], rest[2:]
    else:
        h_refs = (None, None)
        ins = [(None,) + tuple(r) for r in ins]
    co_ref, no_ref, mo_ref, c_sc, n_sc, m_sc = rest
    i = pl.program_id(1)
    nc = pl.num_programs(1)

    @pl.when(i == 0)
    def _():
        c_sc[...] = c0_ref[...]
        n_sc[...] = n0_ref[...]
        m_sc[...] = m0_ref[...]

    for sub in range(subs):
        pos = (sub, subs - 1 - sub)
        rows = [pl.ds(pos[d] * chunk, chunk) for d in range(2)]
        view = lambda ref, d: None if ref is None else ref.at[rows[d], :]
        view_t = lambda ref, d: ref.at[:, rows[d]]
        gates = [_mlstm_gates(view_t(ins[d][3], d), view_t(ins[d][4], d), m_sc.at[d], reverse=bool(d),
                              chunk=chunk, with_output=with_output) for d in range(2)]
        for d in range(2):
            _mlstm_heads(view(ins[d][0], d), view_t(ins[d][1], d), view(ins[d][2], d), view(h_refs[d], d),
                         c_sc.at[d], n_sc.at[d], m_sc.at[d], gates[d], chunk=chunk)

    @pl.when(i == nc - 1)
    def _():
        co_ref[...] = c_sc[...]
        no_ref[...] = n_sc[...]
        mo_ref[...] = m_sc[...]


def _mlstm(q, kt, v, gt, gx, state, with_output):
    bsz, t_len, _ = v.shape
    chunk = MLSTM_CHUNK
    subs = next(n for n in MLSTM_CHUNKS_PER_STEP if t_len % (n * chunk) == 0)
    blk = subs * chunk
    nc = t_len // blk
    cidx = (lambda i: i, lambda i: nc - 1 - i)
    tile = lambda w, d: pl.BlockSpec((None, blk, w), lambda b, i: (b, cidx[d](i), 0))
    tile_t = lambda w, d: pl.BlockSpec((None, w, blk), lambda b, i: (b, 0, cidx[d](i)))
    st_dims = [(2, HEADS, HEAD_DIM, HEAD_DIM), (2, HEADS, 8, HEAD_DIM), (2, HEADS, 128)]
    st_specs = [pl.BlockSpec((None,) + s, lambda b, i, n=len(s): (b,) + (0,) * n) for s in st_dims]
    st_shapes = [jax.ShapeDtypeStruct((bsz,) + s, F32) for s in st_dims]
    in_specs, args = [], []
    for d in range(2):
        in_specs += ([tile(W_BRANCH, d)] if with_output else []) + [
            tile_t(W_BRANCH, d), tile(W_BRANCH, d), tile_t(N_GATES, d), tile_t(N_GATES, d)]
        args += ([q] if with_output else []) + [kt, v, gt, gx]
    out_specs, out_shape = list(st_specs), list(st_shapes)
    if with_output:
        out_specs = [tile(W_BRANCH, 0), tile(W_BRANCH, 1)] + out_specs
        out_shape = [jax.ShapeDtypeStruct((bsz, t_len, W_BRANCH), BF16)] * 2 + out_shape
    outs = pl.pallas_call(
        functools.partial(_mlstm_kernel, with_output=with_output, chunk=chunk, subs=subs),
        grid=(bsz, nc),
        in_specs=in_specs + st_specs,
        out_specs=out_specs,
        out_shape=out_shape,
        scratch_shapes=[pltpu.VMEM(s, F32) for s in st_dims],
        compiler_params=pltpu.CompilerParams(dimension_semantics=("parallel", "arbitrary"),
                                             vmem_limit_bytes=VMEM_LIMIT),
        name="mlstm_out" if with_output else "mlstm_state",
    )(*args, *state)
    if with_output:
        return (outs[0], outs[1]), tuple(outs[2:])
    return None, tuple(outs)


def _s5_kernel(vc_ref, vx_ref, g_ref, m_ref, p_ref, a_ref, y_ref, gu_sc, s_sc, *,
               nk_ctx, n_rc, bsz, rblk):
    rows_ctx = nk_ctx * bsz
    rows_x = n_rc * bsz * GRID_W

    def increments(v_ref, r0, r1):
        return _dot(v_ref[0, r0:r1, :], g_ref[0]) + _dot(v_ref[1, r0:r1, :], g_ref[1])

    inc = increments(vc_ref, 0, rows_ctx)
    for comp in range(4):
        gu_sc[comp, 0:rows_ctx, :] = inc[:, comp * 128:(comp + 1) * 128]
    for r0 in range(0, rows_x, rblk):
        inc = increments(vx_ref, r0, r0 + rblk)
        for run in range(rblk // GRID_W):
            dst = rows_ctx + (r0 // GRID_W + run) * S5_ROW_PITCH
            for comp in range(4):
                gu_sc[comp, dst:dst + GRID_W, :] = (
                    inc[run * GRID_W:(run + 1) * GRID_W, comp * 128:(comp + 1) * 128])

    a = a_ref[...]
    a_pow = [jnp.broadcast_to(a[:, comp * 128:(comp + 1) * 128], (bsz, 128)) for comp in range(4)]
    zero = jnp.zeros((bsz, 128), F32)

    def cmul(x_r, x_i, y_r, y_i):
        return x_r * y_r - x_i * y_i, x_r * y_i + x_i * y_r

    a_sq = [cmul(a_pow[2 * d], a_pow[2 * d + 1], a_pow[2 * d], a_pow[2 * d + 1]) for d in range(2)]

    def step(rows, carry, direction):
        s_r, s_i = carry
        inc_r = gu_sc[2 * direction, rows, :]
        inc_i = gu_sc[2 * direction + 1, rows, :]
        gu_sc[2 * direction, rows, :] = s_r
        gu_sc[2 * direction + 1, rows, :] = s_i
        p_r, p_i = cmul(a_pow[2 * direction], a_pow[2 * direction + 1], s_r, s_i)
        return p_r + inc_r, p_i + inc_i

    def step2(rows0, rows1, carry, direction):
        s_r, s_i = carry
        a_r, a_i = a_pow[2 * direction], a_pow[2 * direction + 1]
        inc0_r, inc0_i = gu_sc[2 * direction, rows0, :], gu_sc[2 * direction + 1, rows0, :]
        inc1_r, inc1_i = gu_sc[2 * direction, rows1, :], gu_sc[2 * direction + 1, rows1, :]
        gu_sc[2 * direction, rows0, :] = s_r
        gu_sc[2 * direction + 1, rows0, :] = s_i
        m_r, m_i = cmul(a_r, a_i, s_r, s_i)
        gu_sc[2 * direction, rows1, :] = m_r + inc0_r
        gu_sc[2 * direction + 1, rows1, :] = m_i + inc0_i
        c_r, c_i = cmul(a_r, a_i, inc0_r, inc0_i)
        q_r, q_i = cmul(a_sq[direction][0], a_sq[direction][1], s_r, s_i)
        return q_r + (c_r + inc1_r), q_i + (c_i + inc1_i)

    def ctx_rows(k):
        return pl.ds(pl.multiple_of(k * bsz, bsz), bsz)

    def x_rows(w, rc):
        return pl.ds(rows_ctx + rc * (bsz * S5_ROW_PITCH) + w, bsz, stride=S5_ROW_PITCH)

    def ctx_body(k, carry):
        return step(ctx_rows(k), carry[0], 0), step(ctx_rows(nk_ctx - 1 - k), carry[1], 1)

    def x_body(w, carry):
        c_f, c_b = carry
        w_b = GRID_W - 1 - w
        for rc in range(0, n_rc - 1, 2):
            c_f = step2(x_rows(w, rc), x_rows(w, rc + 1), c_f, 0)
            c_b = step2(x_rows(w_b, n_rc - 1 - rc), x_rows(w_b, n_rc - 2 - rc), c_b, 1)
        if n_rc % 2:
            c_f = step(x_rows(w, n_rc - 1), c_f, 0)
            c_b = step(x_rows(w_b, 0), c_b, 1)
        return c_f, c_b

    carry = lax.fori_loop(0, nk_ctx, ctx_body, ((zero, zero), (zero, zero)))
    lax.fori_loop(0, GRID_W, x_body, carry)

    for r0 in range(0, rows_x, rblk):
        r1 = r0 + rblk
        for run in range(rblk // GRID_W):
            src = rows_ctx + (r0 // GRID_W + run) * S5_ROW_PITCH
            for comp in range(4):
                s_sc[r0 + run * GRID_W:r0 + (run + 1) * GRID_W, comp * 128:(comp + 1) * 128] = (
                    gu_sc[comp, src:src + GRID_W, :].astype(BF16))
        for gg in range(2):
            y_ref[gg, r0:r1, :] = (_dot(vx_ref[gg, r0:r1, :], m_ref[gg])
                                   + _dot(s_sc[r0:r1, :], p_ref[gg])).astype(BF16)


def _s5(v_ctx, v_x, g_all, m_all, p_all, a16, bsz):
    rows_ctx, rows_x = v_ctx.shape[1], v_x.shape[1]
    lanes = S5_SUB * S5_GC
    return pl.pallas_call(
        functools.partial(_s5_kernel, nk_ctx=rows_ctx // bsz, n_rc=rows_x // (bsz * GRID_W),
                          bsz=bsz, rblk=512),
        grid=(S5_GROUPS // 2,),
        in_specs=[pl.BlockSpec((2, rows_ctx, lanes), lambda j: (j, 0, 0)),
                  pl.BlockSpec((2, rows_x, lanes), lambda j: (j, 0, 0)),
                  pl.BlockSpec((2, lanes, 512), lambda j: (j, 0, 0)),
                  pl.BlockSpec((2, lanes, lanes), lambda j: (j, 0, 0)),
                  pl.BlockSpec((2, 512, lanes), lambda j: (j, 0, 0)),
                  pl.BlockSpec((None, 1, 512), lambda j: (j, 0, 0))],
        out_specs=pl.BlockSpec((2, rows_x, lanes), lambda j: (j, 0, 0)),
        out_shape=jax.ShapeDtypeStruct((S5_GROUPS, rows_x, lanes), BF16),
        scratch_shapes=[pltpu.VMEM((4, rows_ctx + (rows_x // GRID_W) * S5_ROW_PITCH, 128), F32),
                        pltpu.VMEM((rows_x, 512), BF16)],
        compiler_params=pltpu.CompilerParams(dimension_semantics=("parallel",),
                                             vmem_limit_bytes=VMEM_LIMIT),
        name="s5",
    )(v_ctx, v_x, g_all, m_all, p_all, a16)


def _s5_prep_kernel(lr_ref, lc_ref, bt_ref, ct_ref, d_ref, m_ref, g_ref, p_ref, a_ref):
    hp = lax.Precision.HIGHEST
    n_s, lanes = S5_SUB, S5_SUB * S5_GC
    lane128 = lax.broadcasted_iota(jnp.int32, (n_s, 128), 1)
    blk_of_lane = lax.broadcasted_iota(jnp.int32, (128, lanes), 1) // S5_GC
    g_types, p_types, a16, k_rows = [], [], [], []
    for d in range(2):
        a_r, a_i, log_dt = lr_ref[d, 0:1, :], lr_ref[d, 1:2, :], lr_ref[d, 2:3, :]
        dt = jnp.exp(log_dt)
        lam_r, lam_i = a_r * dt, a_i * dt
        steps = lax.broadcasted_iota(jnp.int32, (24, 128), 0).astype(F32)
        mag = jnp.exp(lam_r * steps)
        pw_r, pw_i = mag * jnp.cos(lam_i * steps), mag * jnp.sin(lam_i * steps)
        nr, ni = pw_r[1:2] - 1.0, pw_i[1:2]
        den = a_r * a_r + a_i * a_i
        co_r, co_i = (nr * a_r + ni * a_i) / den, (ni * a_r - nr * a_i) / den
        b_r = jnp.concatenate([bt_ref[d, 0]] * n_s, axis=0)
        b_i = jnp.concatenate([bt_ref[d, 1]] * n_s, axis=0)
        bb_r, bb_i = co_r * b_r - co_i * b_i, co_r * b_i + co_i * b_r
        order = [n_s - 1 - i for i in range(n_s)] if d == 0 else list(range(n_s))
        pg_r = jnp.concatenate([jnp.broadcast_to(pw_r[n:n + 1], (S5_GC, 128)) for n in order], axis=0)
        pg_i = jnp.concatenate([jnp.broadcast_to(pw_i[n:n + 1], (S5_GC, 128)) for n in order], axis=0)
        g_types += [bb_r * pg_r - bb_i * pg_i, bb_r * pg_i + bb_i * pg_r]
        a16 += [pw_r[n_s:n_s + 1], pw_i[n_s:n_s + 1]]
        x0 = slice((n_s - 1) * S5_GC, n_s * S5_GC) if d == 0 else slice(0, S5_GC)
        x_r, x_i = bb_r[x0], bb_i[x0]
        lhs_r = jnp.concatenate([jnp.where(lane128 < S5_STATE, x_r, 0.0),
                                 jnp.where(lane128 < S5_STATE, 0.0, x_r)], axis=0)
        lhs_i = jnp.concatenate([jnp.where(lane128 < S5_STATE, x_i, 0.0),
                                 jnp.where(lane128 < S5_STATE, 0.0, x_i)], axis=0)
        dt_c = jnp.exp(lc_ref[d, 2])
        lam_rc, lam_ic = lc_ref[d, 0] * dt_c, lc_ref[d, 1] * dt_c
        n_y = (blk_of_lane if d == 0 else n_s - 1 - blk_of_lane).astype(F32)
        mag_y = jnp.exp(lam_rc * n_y)
        ypw_r, ypw_i = mag_y * jnp.cos(lam_ic * n_y), mag_y * jnp.sin(lam_ic * n_y)
        c_r, c_i = ct_ref[d, 0], ct_ref[d, 1]
        y_r, y_i = c_r * ypw_r - c_i * ypw_i, c_r * ypw_i + c_i * ypw_r
        k_rows.append(jnp.dot(lhs_r, y_r, preferred_element_type=F32, precision=hp)
                      - jnp.dot(lhs_i, y_i, preferred_element_type=F32, precision=hp))
        mag_1 = jnp.exp(lam_rc[:, 0:128])
        a1_r, a1_i = mag_1 * jnp.cos(lam_ic[:, 0:128]), mag_1 * jnp.sin(lam_ic[:, 0:128])
        a1_r = jnp.concatenate([a1_r, a1_r], axis=1)
        a1_i = jnp.concatenate([a1_i, a1_i], axis=1)
        p_types += [y_r * a1_r - y_i * a1_i, -(y_r * a1_i + y_i * a1_r)]

    a_ref[...] = jnp.concatenate(a16, axis=1)
    lane_g = lax.broadcasted_iota(jnp.int32, (lanes, 128), 1)
    row_p = lax.broadcasted_iota(jnp.int32, (128, lanes), 0)
    lane_k = lax.broadcasted_iota(jnp.int32, (S5_GC, lanes), 1)
    row_m = lax.broadcasted_iota(jnp.int32, (lanes, lanes), 0)
    lane_m = lax.broadcasted_iota(jnp.int32, (lanes, lanes), 1)
    for h in range(2):
        mine_l = (lane_g >= S5_STATE) == bool(h)
        g_ref[h] = jnp.concatenate([jnp.where(mine_l, t, 0.0) for t in g_types], axis=1).astype(BF16)
        mine_r = (row_p >= S5_STATE) == bool(h)
        p_ref[h] = jnp.concatenate([jnp.where(mine_r, t, 0.0) for t in p_types], axis=0).astype(BF16)
        k_f = k_rows[0][h * S5_GC:(h + 1) * S5_GC]
        k_b = k_rows[1][h * S5_GC:(h + 1) * S5_GC]
        blocks = []
        for i in range(n_s):
            up, down = S5_GC * i, S5_GC * (n_s - 1 - i)
            f = k_f if up == 0 else jnp.where(lane_k >= up, pltpu.roll(k_f, up, axis=1), 0.0)
            b = k_b if down == 0 else jnp.where(lane_k < lanes - down,
                                                pltpu.roll(k_b, lanes - down, axis=1), 0.0)
            blocks.append(f + b)
        m = jnp.concatenate(blocks, axis=0) + jnp.where(row_m == lane_m, d_ref[h], 0.0)
        m_ref[h] = m.astype(BF16)


def _s5_prep(a_re, a_im, log_step, b_re, b_im, c_re, c_im, d_skip):
    n_g, n_p, n_c, n_s = S5_GROUPS, S5_STATE, S5_GC, S5_SUB
    lanes = n_s * n_c
    pair = lambda a: jnp.transpose(a.astype(F32).reshape(2, n_g // 2, 2 * n_p), (1, 0, 2))
    lam_row = jnp.stack([pair(a_re), pair(a_im),
                         pair(jnp.broadcast_to(log_step[..., None], a_re.shape))], axis=2)
    lam_col = jnp.broadcast_to(lam_row[..., None], lam_row.shape + (lanes,))
    bt = lambda b: jnp.transpose(b.astype(F32).reshape(2, n_g // 2, 2, n_p, n_c),
                                 (1, 0, 4, 2, 3)).reshape(n_g // 2, 2, n_c, 2 * n_p)
    b_t = jnp.stack([bt(b_re), bt(b_im)], axis=2)
    ct = lambda c: jnp.tile(jnp.transpose(c.astype(F32).reshape(2, n_g // 2, 2, n_c, n_p),
                                          (1, 0, 2, 4, 3)).reshape(n_g // 2, 2, 2 * n_p, n_c),
                            (1, 1, 1, n_s))
    c_t = jnp.stack([ct(c_re), ct(c_im)], axis=2)
    d_row = jnp.tile(d_skip.astype(F32).reshape(n_g // 2, 2, 1, n_c), (1, 1, 1, n_s))
    blk = lambda *s: pl.BlockSpec((None,) + s, lambda j: (j,) + (0,) * len(s))
    grp = lambda *s: pl.BlockSpec((2,) + s, lambda j: (j,) + (0,) * len(s))
    return pl.pallas_call(
        _s5_prep_kernel,
        grid=(n_g // 2,),
        in_specs=[blk(2, 3, 2 * n_p), blk(2, 3, 2 * n_p, lanes), blk(2, 2, n_c, 2 * n_p),
                  blk(2, 2, 2 * n_p, lanes), blk(2, 1, lanes)],
        out_specs=[grp(lanes, lanes), grp(lanes, 8 * n_p), grp(8 * n_p, lanes), blk(1, 8 * n_p)],
        out_shape=[jax.ShapeDtypeStruct((n_g, lanes, lanes), BF16),
                   jax.ShapeDtypeStruct((n_g, lanes, 8 * n_p), BF16),
                   jax.ShapeDtypeStruct((n_g, 8 * n_p, lanes), BF16),
                   jax.ShapeDtypeStruct((n_g // 2, 1, 8 * n_p), F32)],
        compiler_params=pltpu.CompilerParams(dimension_semantics=("parallel",),
                                             vmem_limit_bytes=VMEM_LIMIT),
        name="s5_prep",
    )(lam_row, lam_col, b_t, c_t, d_row)


def _merge_kernel(hf_ref, hb_ref, o_ref, zm_ref, y_ref, zs_ref, x_ref, mod_ref, mhg_ref,
                  gluw_ref, glub_ref, wout_ref, fg_ref, out_ref, y_sc):
    for g8 in range(W_BRANCH // 128):
        rows = [y_ref[g8 * 8 + gp].astype(F32) for gp in range(8)]
        for i, blk in enumerate(_block_transpose8(rows)):
            y_sc[i * GRID_W:(i + 1) * GRID_W, g8 * 128:(g8 + 1) * 128] = blk

    hm = ((hf_ref[...] + hb_ref[...]) * jax.nn.sigmoid(o_ref[...])).astype(F32)
    mhg = mhg_ref[...]
    parts = []
    for hd in range(HEADS):
        sl = slice(hd * HEAD_DIM, (hd + 1) * HEAD_DIM)
        seg = hm[:, sl]
        mu = jnp.mean(seg, axis=-1, keepdims=True)
        dev = seg - mu
        var = jnp.mean(dev * dev, axis=-1, keepdims=True)
        parts.append(dev * lax.rsqrt(var + NORM_EPS) * mhg[:, sl])
    m_out = jnp.concatenate(parts, axis=-1).astype(BF16) * _silu(zm_ref[...])

    y = y_sc[...].astype(BF16)
    gl = 0.5 * y * (1.0 + jnp.tanh(0.7978845608028654 * (y + 0.044715 * (y * y * y))))
    gate = jax.nn.sigmoid((_dot(gl, gluw_ref[...]) + glub_ref[...]).astype(BF16))
    s_out = gl * gate * _silu(zs_ref[...])

    mixed = _dot(m_out, wout_ref[0:W_BRANCH, :]) + _dot(s_out, wout_ref[W_BRANCH:2 * W_BRANCH, :])
    xo = x_ref[...] + mod_ref[2:3, :] * mixed
    ms = jnp.mean(xo * xo, axis=-1, keepdims=True)
    out_ref[...] = xo * lax.rsqrt(ms + NORM_EPS) * fg_ref[...]


def _merge(hf, hb, o, zm, y, zs, x, mod, mh_g, glu_w, glu_b, w_out, final_g):
    bsz, t_len, _ = x.shape
    tm = 8 * GRID_W
    tile = pl.BlockSpec((None, tm, D_MODEL), lambda b, i: (b, i, 0))
    y_spec = pl.BlockSpec((S5_GROUPS, GRID_W, 128), lambda b, i: (0, (i // 2) * bsz + b, i % 2))
    const = lambda shape: pl.BlockSpec(shape, lambda b, i: (0,) * len(shape),
                                       pipeline_mode=pl.Buffered(1))
    return pl.pallas_call(
        _merge_kernel,
        grid=(bsz, t_len // tm),
        in_specs=[tile] * 4 + [y_spec, tile, tile,
                               pl.BlockSpec((None, 3, D_MODEL), lambda b, i: (b, 0, 0)),
                               const((1, W_BRANCH)), const((W_BRANCH, W_BRANCH)),
                               const((1, W_BRANCH)), const((2 * W_BRANCH, D_MODEL)),
                               const((1, D_MODEL))],
        out_specs=tile,
        out_shape=jax.ShapeDtypeStruct((bsz, t_len, D_MODEL), F32),
        scratch_shapes=[pltpu.VMEM((tm, W_BRANCH), F32)],
        compiler_params=pltpu.CompilerParams(dimension_semantics=("parallel", "arbitrary"),
                                             vmem_limit_bytes=VMEM_LIMIT),
        name="merge",
    )(hf, hb, o, zm, y, zs, x, mod, mh_g.reshape(1, -1), glu_w, glu_b.reshape(1, -1), w_out,
      final_g.reshape(1, -1))


def _s5_rows_ctx(u):
    bsz, t_len, _ = u.shape
    a = u.reshape(bsz, t_len // S5_SUB, S5_SUB, S5_GROUPS, S5_GC)
    a = jnp.transpose(a, (3, 1, 0, 2, 4))
    return a.reshape(S5_GROUPS, (t_len // S5_SUB) * bsz, S5_SUB * S5_GC)


def kernel(x, c, ctx, c_ctx, norm_g, ada_w, ada_b, w_in, b_gate, conv_qk, mh_g, s5_a_re, s5_a_im,
           s5_log_step, s5_b_re, s5_b_im, s5_c_re, s5_c_im, s5_d, glu_w, glu_b, w_out, final_g):
    bsz, t_len, _ = x.shape
    layer = 0

    cc = jnp.zeros((16, D_MODEL), F32).at[:bsz].set(c).at[bsz].set(c_ctx)
    mod = _ada(cc, ada_w[layer], ada_b[layer]).reshape(16, 3, D_MODEL)
    mod_x = mod[:bsz]
    mod_c = jnp.broadcast_to(mod[bsz][None], (bsz, 3, D_MODEL))

    w = w_in[layer]
    wb = W_BRANCH
    w_qk = w[:, 0:2 * wb].astype(BF16)
    gate0 = 5 * wb
    w_gate = jnp.pad(w[:, gate0:gate0 + N_GATES], ((0, 0), (0, GATE_PAD - N_GATES)))
    w_main = jnp.concatenate([w[:, 2 * wb:5 * wb], w[:, gate0 + N_GATES:], w_gate], axis=1).astype(BF16)
    b_gate_pad = jnp.pad(b_gate[layer].reshape(1, N_GATES), ((0, 0), (0, GATE_PAD - N_GATES)))

    proj = functools.partial(_inproj, norm_g=norm_g[layer], w_qk=w_qk, w_main=w_main,
                             conv_qk=conv_qk[layer], b_gate_pad=b_gate_pad)
    kt_c, v_c, u_c, gt_c, gx_c = proj(ctx, mod_c, tm=256, s5_layout=False, state_only=True)
    q_x, kt_x, v_x, o_x, zm_x, u_x, zs_x, gt_x, gx_x = proj(x, mod_x, tm=8 * GRID_W, s5_layout=True)

    zero_state = (jnp.zeros((bsz, 2, HEADS, HEAD_DIM, HEAD_DIM), F32),
                  jnp.zeros((bsz, 2, HEADS, 8, HEAD_DIM), F32),
                  jnp.zeros((bsz, 2, HEADS, 128), F32))
    _, ctx_state = _mlstm(None, kt_c, v_c, gt_c, gx_c, zero_state, with_output=False)
    (h_f, h_b), _ = _mlstm(q_x, kt_x, v_x, gt_x, gx_x, ctx_state, with_output=True)

    m_all, g_all, p_all, a16 = _s5_prep(
        s5_a_re[layer], s5_a_im[layer], s5_log_step[layer], s5_b_re[layer], s5_b_im[layer],
        s5_c_re[layer], s5_c_im[layer], s5_d[layer])
    y_x = _s5(_s5_rows_ctx(u_c), u_x, g_all, m_all, p_all, a16, bsz)

    return _merge(h_f, h_b, o_x, zm_x, y_x, zs_x, x, mod_x, mh_g[layer], glu_w[layer].astype(BF16),
                  glu_b[layer], w_out[layer].astype(BF16), final_g)
```

```python
import functools

import jax
import jax.numpy as jnp
from jax import lax
from jax.experimental import pallas as pl
from jax.experimental.pallas import tpu as pltpu

F32 = jnp.float32
BF16 = jnp.bfloat16

D_MODEL = 1024
HEADS = 4
HEAD_DIM = 256
W_BRANCH = 1024
S5_GROUPS = 64
S5_GC = 16
S5_STATE = 64
S5_SUB = 16
GRID_W = 64
S5_ROW_PITCH = 72
N_GATES = 16
GATE_PAD = 128
CONV_W = 3
NORM_EPS = 1e-6
MLSTM_CHUNK = 256
MLSTM_CHUNKS_PER_STEP = (4, 2, 1)
NEG_BIG = -1e30

VMEM_LIMIT = 56 * 1024 * 1024


def _silu(a):
    return a * jax.nn.sigmoid(a)


def _log_sigmoid(a):
    return jnp.minimum(a, 0.0) - jnp.log1p(jnp.exp(-jnp.abs(a)))


def _dot(a, b):
    return jnp.dot(a, b, preferred_element_type=F32)


def _split_bf16(a):
    hi = a.astype(BF16)
    lo = (a - hi.astype(F32)).astype(BF16)
    return hi, lo


def _block_transpose8(rows):
    lane = lax.broadcasted_iota(jnp.int32, rows[0].shape, 1)
    blk = lane // S5_GC
    for d in (4, 2, 1):
        keep = (blk & d) == 0
        new = list(rows)
        for i in range(8):
            if i & d == 0:
                a, b = rows[i], rows[i + d]
                new[i] = jnp.where(keep, a, pltpu.roll(b, d * S5_GC, axis=1))
                new[i + d] = jnp.where(keep, pltpu.roll(a, 128 - d * S5_GC, axis=1), b)
        rows = new
    return rows


def _chunk_gate_rows(gt_blk, reverse):
    chunk = gt_blk.shape[1]
    row = lax.broadcasted_iota(jnp.int32, (chunk, chunk), 0)
    col = lax.broadcasted_iota(jnp.int32, (chunk, chunk), 1)
    mask_st = (row >= col) if reverse else (row <= col)
    tri_st = jnp.where(mask_st, 1.0, 0.0).astype(BF16)
    hi, lo = _split_bf16(gt_blk)
    cum = _dot(hi, tri_st) + _dot(lo, tri_st)
    li0 = 8 if reverse else 0
    b_r = cum[li0 + HEADS:li0 + 2 * HEADS, :]
    run_max = gt_blk[li0:li0 + HEADS, :] - b_r
    lane = lax.broadcasted_iota(jnp.int32, run_max.shape, 1)
    shift = 1
    while shift < chunk:
        if reverse:
            moved = jnp.where(lane < chunk - shift, pltpu.roll(run_max, chunk - shift, axis=1), NEG_BIG)
        else:
            moved = jnp.where(lane >= shift, pltpu.roll(run_max, shift, axis=1), NEG_BIG)
        run_max = jnp.maximum(run_max, moved)
        shift *= 2
    return b_r, run_max


def _ada_kernel(c_ref, w_ref, b_ref, o_ref):
    s = _silu(c_ref[...])
    o_ref[...] = jnp.dot(s, w_ref[...], preferred_element_type=F32,
                         precision=lax.Precision.HIGHEST) + b_ref[...]


def _ada(cc, ada_w, ada_b):
    rows = cc.shape[0]
    n_out = ada_w.shape[1]
    tn = 1024
    return pl.pallas_call(
        _ada_kernel,
        grid=(n_out // tn,),
        in_specs=[pl.BlockSpec((rows, D_MODEL), lambda j: (0, 0)),
                  pl.BlockSpec((D_MODEL, tn), lambda j: (0, j)),
                  pl.BlockSpec((1, tn), lambda j: (0, j))],
        out_specs=pl.BlockSpec((rows, tn), lambda j: (0, j)),
        out_shape=jax.ShapeDtypeStruct((rows, n_out), F32),
        compiler_params=pltpu.CompilerParams(dimension_semantics=("arbitrary",),
                                             vmem_limit_bytes=VMEM_LIMIT),
        name="ada",
    )(cc, ada_w, ada_b.reshape(1, n_out))


HALO = 16


def _inproj_kernel(x_ref, xp_ref, xn_ref, mod_ref, ng_ref, wqk_ref, wm_ref, conv_ref, bg_ref,
                   *rest, tm, s5_layout, state_only):
    if state_only:
        kt_ref, v_ref, u_ref, gt_ref, gx_ref, lhs_sc = rest
    else:
        q_ref, kt_ref, v_ref, o_ref, zm_ref, u_ref, zs_ref, gt_ref, gx_ref, lhs_sc = rest
    i = pl.program_id(1)
    nt = pl.num_programs(1)
    shift = mod_ref[0:1, :]
    gain = ng_ref[...] * (1.0 + mod_ref[1:2, :])

    def norm_mod(a):
        ms = jnp.mean(a * a, axis=-1, keepdims=True)
        return a * lax.rsqrt(ms + NORM_EPS) * gain + shift

    lhs_sc[0:tm, :] = norm_mod(x_ref[...]).astype(BF16)
    lhs_sc[tm:tm + HALO, :] = norm_mod(xp_ref[...]).astype(BF16)
    lhs_sc[tm + HALO:tm + 2 * HALO, :] = norm_mod(xn_ref[...]).astype(BF16)

    h = lhs_sc[0:tm, :]
    zg = _dot(h, wm_ref[:, 5120:5120 + GATE_PAD]) + bg_ref[...]
    lane = lax.broadcasted_iota(jnp.int32, zg.shape, 1)
    is_forget = jnp.logical_and(lane < N_GATES, (lane % 8) >= 4)
    gates = jnp.where(is_forget, _log_sigmoid(zg), zg)
    gt = gates.T[0:N_GATES, :]
    gt_ref[...] = gt
    for c0 in range(0, tm, MLSTM_CHUNK):
        parts = []
        for reverse in (False, True):
            parts += list(_chunk_gate_rows(gt[:, c0:c0 + MLSTM_CHUNK], reverse))
        gx_ref[:, c0:c0 + MLSTM_CHUNK] = jnp.concatenate(parts, axis=0)

    u = _dot(h, wm_ref[:, 3072:4096])
    if s5_layout:
        for g8 in range(W_BRANCH // 128):
            rows = [u[i * GRID_W:(i + 1) * GRID_W, g8 * 128:(g8 + 1) * 128] for i in range(8)]
            for gp, blk in enumerate(_block_transpose8(rows)):
                u_ref[g8 * 8 + gp] = blk.astype(BF16)
    else:
        u_ref[...] = u.astype(BF16)

    valid_prev = (i > 0).astype(F32)
    valid_next = (i < nt - 1).astype(F32)

    def conv_stage(half):
        cols = slice(half * W_BRANCH, (half + 1) * W_BRANCH)
        z = _dot(lhs_sc[...], wqk_ref[:, cols])
        zc = z[0:tm]
        row8 = lax.broadcasted_iota(jnp.int32, (8, W_BRANCH), 0)
        before = pltpu.roll(zc, 1, axis=0)
        first = jnp.where(row8 == 0, z[tm + HALO - 1:tm + HALO] * valid_prev, before[0:8])
        before = jnp.concatenate([first, before[8:]], axis=0)
        after = pltpu.roll(zc, tm - 1, axis=0)
        final = jnp.where(row8 == 7, z[tm + HALO:tm + HALO + 1] * valid_next, after[tm - 8:tm])
        after = jnp.concatenate([after[:tm - 8], final], axis=0)
        cw = conv_ref[:, cols]
        conv = cw[0:1] * before + cw[1:2] * zc + cw[2:3] * after
        if half == 0:
            q_ref[...] = _silu(conv.astype(BF16))
        else:
            kt_ref[...] = (_silu(conv.astype(BF16)) * HEAD_DIM ** -0.5).T

    v_ref[...] = _dot(h, wm_ref[:, 0:1024]).astype(BF16)
    if not state_only:
        conv_stage(0)
        o_ref[...] = _dot(h, wm_ref[:, 1024:2048]).astype(BF16)
    conv_stage(1)
    if not state_only:
        zm_ref[...] = _dot(h, wm_ref[:, 2048:3072]).astype(BF16)
        zs_ref[...] = _dot(h, wm_ref[:, 4096:5120]).astype(BF16)


def _inproj(x, mod, norm_g, w_qk, w_main, conv_qk, b_gate_pad, tm, s5_layout, state_only=False):
    bsz, t_len, _ = x.shape
    nt = t_len // tm
    nhb = t_len // HALO
    tok = lambda w, dt: jax.ShapeDtypeStruct((bsz, t_len, w), dt)
    tile = lambda w: pl.BlockSpec((None, tm, w), lambda b, i: (b, i, 0))
    tile_t = lambda w: pl.BlockSpec((None, w, tm), lambda b, i: (b, 0, i))
    const = lambda shape: pl.BlockSpec(shape, lambda b, i: (0,) * len(shape),
                                       pipeline_mode=pl.Buffered(1))
    if s5_layout:
        assert tm == 8 * GRID_W and t_len % (2 * tm) == 0
        u_spec = pl.BlockSpec((S5_GROUPS, GRID_W, 128), lambda b, i: (0, (i // 2) * bsz + b, i % 2))
        u_shape = jax.ShapeDtypeStruct((S5_GROUPS, (t_len // (2 * tm)) * bsz * GRID_W, 256), BF16)
    else:
        u_spec, u_shape = tile(W_BRANCH), tok(W_BRANCH, BF16)
    kt_shape = jax.ShapeDtypeStruct((bsz, W_BRANCH, t_len), BF16)
    gt_shape = jax.ShapeDtypeStruct((bsz, N_GATES, t_len), F32)
    if state_only:
        out_specs = [tile_t(W_BRANCH), tile(W_BRANCH), u_spec, tile_t(N_GATES), tile_t(N_GATES)]
        out_shape = [kt_shape, tok(W_BRANCH, BF16), u_shape, gt_shape, gt_shape]
    else:
        out_specs = [tile(W_BRANCH), tile_t(W_BRANCH)] + [tile(W_BRANCH)] * 3 + [
            u_spec, tile(W_BRANCH), tile_t(N_GATES), tile_t(N_GATES)]
        out_shape = [tok(W_BRANCH, BF16), kt_shape] + [tok(W_BRANCH, BF16)] * 3 + [
            u_shape, tok(W_BRANCH, BF16), gt_shape, gt_shape]
    return pl.pallas_call(
        functools.partial(_inproj_kernel, tm=tm, s5_layout=s5_layout, state_only=state_only),
        grid=(bsz, nt),
        in_specs=[
            tile(D_MODEL),
            pl.BlockSpec((None, HALO, D_MODEL),
                         lambda b, i: (b, jnp.maximum(i * (tm // HALO) - 1, 0), 0)),
            pl.BlockSpec((None, HALO, D_MODEL),
                         lambda b, i: (b, jnp.minimum((i + 1) * (tm // HALO), nhb - 1), 0)),
            pl.BlockSpec((None, 3, D_MODEL), lambda b, i: (b, 0, 0)),
            const((1, D_MODEL)),
            const((D_MODEL, 2 * W_BRANCH)),
            const((D_MODEL, 5 * W_BRANCH + GATE_PAD)),
            const((CONV_W, 2 * W_BRANCH)),
            const((1, GATE_PAD)),
        ],
        out_specs=out_specs,
        out_shape=out_shape,
        scratch_shapes=[pltpu.VMEM((tm + 2 * HALO, D_MODEL), BF16)],
        compiler_params=pltpu.CompilerParams(dimension_semantics=("parallel", "arbitrary"),
                                             vmem_limit_bytes=VMEM_LIMIT),
        name="inproj",
    )(x, x, x, mod, norm_g.reshape(1, D_MODEL), w_qk, w_main, conv_qk, b_gate_pad)


def _mlstm_gates(gt_ref, gx_ref, m_sc, *, reverse, chunk, with_output):
    li0 = 8 if reverse else 0
    last = 0 if reverse else chunk - 1
    li_r = gt_ref[li0:li0 + HEADS, :]
    b_r = gx_ref[li0:li0 + HEADS, :]
    b_last = b_r[:, last:last + 1]
    m_old = m_sc[:, 0:1]
    g_r = b_last - b_r + li_r
    m_new = jnp.maximum(b_last + m_old, jnp.max(g_r, axis=1, keepdims=True))
    out = dict(m_old=m_old, m_new=m_new, decay=jnp.exp(b_last + m_old - m_new),
               k_scale=jnp.exp(g_r - m_new).astype(BF16))
    if not with_output:
        return out
    row = lax.broadcasted_iota(jnp.int32, (chunk, chunk), 0)
    col = lax.broadcasted_iota(jnp.int32, (chunk, chunk), 1)
    mask_ts = (col >= row) if reverse else (col <= row)
    mm_r = jnp.maximum(gx_ref[li0 + HEADS:li0 + 2 * HEADS, :], m_old).astype(BF16)
    b_hi, b_lo = _split_bf16(b_r)
    rows = jnp.concatenate([mm_r, b_hi, b_lo, jnp.zeros_like(b_hi)], axis=0)
    sel_r = lax.broadcasted_iota(jnp.int32, (4 * HEADS, 2 * HEADS * 128), 0)
    sel_c = lax.broadcasted_iota(jnp.int32, (4 * HEADS, 2 * HEADS * 128), 1) // 128
    head_r = sel_r % HEADS
    pick = jnp.logical_or(jnp.logical_and(sel_r < HEADS, sel_c % HEADS == head_r),
                          jnp.logical_and(jnp.logical_and(sel_r >= HEADS, sel_r < 3 * HEADS),
                                          sel_c == HEADS + head_r))
    sel = jnp.where(pick, 1.0, 0.0).astype(BF16)
    cols = lax.dot_general(rows, sel, (((0,), (0,)), ((), ())), preferred_element_type=F32)
    out.update(a_r=li_r - b_r, cols=cols, mask_ts=mask_ts)
    return out


def _mlstm_heads(q_ref, kt_ref, v_ref, h_ref, c_sc, n_sc, m_sc, gates, *, chunk):
    with_output = h_ref is not None
    m_old, m_new, decay, k_scale = gates["m_old"], gates["m_new"], gates["decay"], gates["k_scale"]
    if with_output:
        a_r, cols, mask_ts = gates["a_r"], gates["cols"], gates["mask_ts"]
    ones_rows = jnp.ones((8, chunk), BF16)
    wide = lambda a, n: jnp.concatenate([a] * (n // 128), axis=1)
    for hd in range(HEADS):
        sl = slice(hd * HEAD_DIM, (hd + 1) * HEAD_DIM)
        kt = kt_ref[sl, :]
        v = v_ref[:, sl]
        c_old = c_sc[hd]
        n_old = n_sc[hd]
        kw_t = kt * k_scale[hd:hd + 1, :]
        if with_output:
            q = q_ref[:, sl]
            mm_c = cols[:, hd * 128:(hd + 1) * 128]
            bm_c = cols[:, (HEADS + hd) * 128:(HEADS + hd + 1) * 128]
            decay_mat = jnp.exp(jnp.where(mask_ts, a_r[hd:hd + 1, :] - wide(mm_c, chunk), NEG_BIG))
            s_f = _dot(q, kt) * decay_mat
            w_inter = jnp.exp(m_old[hd:hd + 1, :] - mm_c)
            den = (w_inter * jnp.sum(q.astype(F32) * n_old[0:1, :], axis=1, keepdims=True)
                   + jnp.sum(s_f, axis=1, keepdims=True))
            q_w = q * wide(w_inter, HEAD_DIM).astype(BF16)
            num = _dot(jnp.concatenate([q_w, s_f.astype(BF16)], axis=1),
                       jnp.concatenate([c_old.astype(BF16), v], axis=0))
            inv = 1.0 / jnp.maximum(jnp.abs(den), jnp.exp(-bm_c))
            h_ref[:, sl] = (num * wide(inv, HEAD_DIM)).astype(BF16)
        c_sc[hd] = decay[hd:hd + 1, :] * c_old + _dot(kw_t, v)
        n_sc[hd] = decay[hd:hd + 1, :] * n_old + lax.dot_general(
            ones_rows, kw_t, (((1,), (1,)), ((), ())), preferred_element_type=F32)
        m_sc[hd:hd + 1, :] = jnp.broadcast_to(m_new[hd:hd + 1, :], (1, 128))


def _mlstm_kernel(*refs, with_output, chunk, subs):
    n_in = 5 if with_output else 4
    ins = [refs[0:n_in], refs[n_in:2 * n_in]]
    rest = refs[2 * n_in:]
    if with_output:
        c0_ref, n0_ref, m0_ref, hf_ref, hb_ref, c_sc, n_sc, m_sc = rest
        h_refs = (hf_ref, hb_ref)
    else:
        co_ref, no_ref, mo_ref, c_sc, n_sc, m_sc = rest
        h_refs = (None, None)
        ins = [(None,) + tuple(r) for r in ins]
    i = pl.program_id(1)
    nc = pl.num_programs(1)

    @pl.when(i == 0)
    def _():
        if with_output:
            c_sc[...] = c0_ref[...]
            n_sc[...] = n0_ref[...]
            m_sc[...] = m0_ref[...]
        else:
            c_sc[...] = jnp.zeros_like(c_sc)
            n_sc[...] = jnp.zeros_like(n_sc)
            m_sc[...] = jnp.zeros_like(m_sc)

    for sub in range(subs):
        pos = (sub, subs - 1 - sub)
        rows = [pl.ds(pos[d] * chunk, chunk) for d in range(2)]
        view = lambda ref, d: None if ref is None else ref.at[rows[d], :]
        view_t = lambda ref, d: ref.at[:, rows[d]]
        gates = [_mlstm_gates(view_t(ins[d][3], d), view_t(ins[d][4], d), m_sc.at[d], reverse=bool(d),
                              chunk=chunk, with_output=with_output) for d in range(2)]
        for d in range(2):
            _mlstm_heads(view(ins[d][0], d), view_t(ins[d][1], d), view(ins[d][2], d), view(h_refs[d], d),
                         c_sc.at[d], n_sc.at[d], m_sc.at[d], gates[d], chunk=chunk)

    if not with_output:
        @pl.when(i == nc - 1)
        def _():
            co_ref[...] = c_sc[...]
            no_ref[...] = n_sc[...]
            mo_ref[...] = m_sc[...]


def _mlstm(q, kt, v, gt, gx, state):
    with_output = q is not None
    bsz, t_len, _ = v.shape
    chunk = MLSTM_CHUNK
    subs = next(n for n in MLSTM_CHUNKS_PER_STEP if t_len % (n * chunk) == 0)
    blk = subs * chunk
    nc = t_len // blk
    cidx = (lambda i: i, lambda i: nc - 1 - i)
    tile = lambda w, d: pl.BlockSpec((None, blk, w), lambda b, i: (b, cidx[d](i), 0))
    tile_t = lambda w, d: pl.BlockSpec((None, w, blk), lambda b, i: (b, 0, cidx[d](i)))
    st_dims = [(2, HEADS, HEAD_DIM, HEAD_DIM), (2, HEADS, 8, HEAD_DIM), (2, HEADS, 128)]
    st_specs = [pl.BlockSpec((None,) + s, lambda b, i, n=len(s): (b,) + (0,) * n) for s in st_dims]
    st_shapes = [jax.ShapeDtypeStruct((bsz,) + s, F32) for s in st_dims]
    in_specs, args = [], []
    for d in range(2):
        in_specs += ([tile(W_BRANCH, d)] if with_output else []) + [
            tile_t(W_BRANCH, d), tile(W_BRANCH, d), tile_t(N_GATES, d), tile_t(N_GATES, d)]
        args += ([q] if with_output else []) + [kt, v, gt, gx]
    if with_output:
        in_specs, args = in_specs + st_specs, args + list(state)
        out_specs = [tile(W_BRANCH, 0), tile(W_BRANCH, 1)]
        out_shape = [jax.ShapeDtypeStruct((bsz, t_len, W_BRANCH), BF16)] * 2
    else:
        out_specs, out_shape = st_specs, st_shapes
    return pl.pallas_call(
        functools.partial(_mlstm_kernel, with_output=with_output, chunk=chunk, subs=subs),
        grid=(bsz, nc),
        in_specs=in_specs,
        out_specs=out_specs,
        out_shape=out_shape,
        scratch_shapes=[pltpu.VMEM(s, F32) for s in st_dims],
        compiler_params=pltpu.CompilerParams(dimension_semantics=("parallel", "arbitrary"),
                                             vmem_limit_bytes=VMEM_LIMIT),
        name="mlstm_out" if with_output else "mlstm_state",
    )(*args)


def _s5_kernel(vc_ref, vx_ref, g_ref, m_ref, p_ref, a_ref, y_ref, gu_sc, s_sc, *,
               nk_ctx, n_rc, bsz, rblk):
    rows_ctx = nk_ctx * bsz
    rows_x = n_rc * bsz * GRID_W

    def increments(v_ref, r0, r1):
        return _dot(v_ref[0, r0:r1, :], g_ref[0]) + _dot(v_ref[1, r0:r1, :], g_ref[1])

    inc = increments(vc_ref, 0, rows_ctx)
    for comp in range(4):
        gu_sc[comp, 0:rows_ctx, :] = inc[:, comp * 128:(comp + 1) * 128]
    for r0 in range(0, rows_x, rblk):
        inc = increments(vx_ref, r0, r0 + rblk)
        for run in range(rblk // GRID_W):
            dst = rows_ctx + (r0 // GRID_W + run) * S5_ROW_PITCH
            for comp in range(4):
                gu_sc[comp, dst:dst + GRID_W, :] = (
                    inc[run * GRID_W:(run + 1) * GRID_W, comp * 128:(comp + 1) * 128])

    a = a_ref[...]
    a_pow = [jnp.broadcast_to(a[:, comp * 128:(comp + 1) * 128], (bsz, 128)) for comp in range(4)]
    zero = jnp.zeros((bsz, 128), F32)

    def cmul(x_r, x_i, y_r, y_i):
        return x_r * y_r - x_i * y_i, x_r * y_i + x_i * y_r

    a_sq = [cmul(a_pow[2 * d], a_pow[2 * d + 1], a_pow[2 * d], a_pow[2 * d + 1]) for d in range(2)]

    def step(rows, carry, direction):
        s_r, s_i = carry
        inc_r = gu_sc[2 * direction, rows, :]
        inc_i = gu_sc[2 * direction + 1, rows, :]
        gu_sc[2 * direction, rows, :] = s_r
        gu_sc[2 * direction + 1, rows, :] = s_i
        p_r, p_i = cmul(a_pow[2 * direction], a_pow[2 * direction + 1], s_r, s_i)
        return p_r + inc_r, p_i + inc_i

    def step2(rows0, rows1, carry, direction):
        s_r, s_i = carry
        a_r, a_i = a_pow[2 * direction], a_pow[2 * direction + 1]
        inc0_r, inc0_i = gu_sc[2 * direction, rows0, :], gu_sc[2 * direction + 1, rows0, :]
        inc1_r, inc1_i = gu_sc[2 * direction, rows1, :], gu_sc[2 * direction + 1, rows1, :]
        gu_sc[2 * direction, rows0, :] = s_r
        gu_sc[2 * direction + 1, rows0, :] = s_i
        m_r, m_i = cmul(a_r, a_i, s_r, s_i)
        gu_sc[2 * direction, rows1, :] = m_r + inc0_r
        gu_sc[2 * direction + 1, rows1, :] = m_i + inc0_i
        c_r, c_i = cmul(a_r, a_i, inc0_r, inc0_i)
        q_r, q_i = cmul(a_sq[direction][0], a_sq[direction][1], s_r, s_i)
        return q_r + (c_r + inc1_r), q_i + (c_i + inc1_i)

    def ctx_rows(k):
        return pl.ds(pl.multiple_of(k * bsz, bsz), bsz)

    def x_rows(w, rc):
        return pl.ds(rows_ctx + rc * (bsz * S5_ROW_PITCH) + w, bsz, stride=S5_ROW_PITCH)

    def ctx_body(k, carry):
        return step(ctx_rows(k), carry[0], 0), step(ctx_rows(nk_ctx - 1 - k), carry[1], 1)

    def x_body(w, carry):
        c_f, c_b = carry
        w_b = GRID_W - 1 - w
        for rc in range(0, n_rc - 1, 2):
            c_f = step2(x_rows(w, rc), x_rows(w, rc + 1), c_f, 0)
            c_b = step2(x_rows(w_b, n_rc - 1 - rc), x_rows(w_b, n_rc - 2 - rc), c_b, 1)
        if n_rc % 2:
            c_f = step(x_rows(w, n_rc - 1), c_f, 0)
            c_b = step(x_rows(w_b, 0), c_b, 1)
        return c_f, c_b

    carry = lax.fori_loop(0, nk_ctx, ctx_body, ((zero, zero), (zero, zero)))
    lax.fori_loop(0, GRID_W, x_body, carry)

    for r0 in range(0, rows_x, rblk):
        r1 = r0 + rblk
        for run in range(rblk // GRID_W):
            src = rows_ctx + (r0 // GRID_W + run) * S5_ROW_PITCH
            for comp in range(4):
                s_sc[r0 + run * GRID_W:r0 + (run + 1) * GRID_W, comp * 128:(comp + 1) * 128] = (
                    gu_sc[comp, src:src + GRID_W, :].astype(BF16))
        for gg in range(2):
            y_ref[gg, r0:r1, :] = (_dot(vx_ref[gg, r0:r1, :], m_ref[gg])
                                   + _dot(s_sc[r0:r1, :], p_ref[gg])).astype(BF16)


def _s5(v_ctx, v_x, g_all, m_all, p_all, a16, bsz):
    rows_ctx, rows_x = v_ctx.shape[1], v_x.shape[1]
    lanes = S5_SUB * S5_GC
    return pl.pallas_call(
        functools.partial(_s5_kernel, nk_ctx=rows_ctx // bsz, n_rc=rows_x // (bsz * GRID_W),
                          bsz=bsz, rblk=512),
        grid=(S5_GROUPS // 2,),
        in_specs=[pl.BlockSpec((2, rows_ctx, lanes), lambda j: (j, 0, 0)),
                  pl.BlockSpec((2, rows_x, lanes), lambda j: (j, 0, 0)),
                  pl.BlockSpec((2, lanes, 512), lambda j: (j, 0, 0)),
                  pl.BlockSpec((2, lanes, lanes), lambda j: (j, 0, 0)),
                  pl.BlockSpec((2, 512, lanes), lambda j: (j, 0, 0)),
                  pl.BlockSpec((None, 1, 512), lambda j: (j, 0, 0))],
        out_specs=pl.BlockSpec((2, rows_x, lanes), lambda j: (j, 0, 0)),
        out_shape=jax.ShapeDtypeStruct((S5_GROUPS, rows_x, lanes), BF16),
        scratch_shapes=[pltpu.VMEM((4, rows_ctx + (rows_x // GRID_W) * S5_ROW_PITCH, 128), F32),
                        pltpu.VMEM((rows_x, 512), BF16)],
        compiler_params=pltpu.CompilerParams(dimension_semantics=("parallel",),
                                             vmem_limit_bytes=VMEM_LIMIT),
        name="s5",
    )(v_ctx, v_x, g_all, m_all, p_all, a16)


def _s5_prep_kernel(lr_ref, lc_ref, bt_ref, ct_ref, d_ref, m_ref, g_ref, p_ref, a_ref):
    hp = lax.Precision.HIGHEST
    n_s, lanes = S5_SUB, S5_SUB * S5_GC
    lane128 = lax.broadcasted_iota(jnp.int32, (n_s, 128), 1)
    blk_of_lane = lax.broadcasted_iota(jnp.int32, (128, lanes), 1) // S5_GC
    g_types, p_types, a16, k_rows = [], [], [], []
    for d in range(2):
        a_r, a_i, log_dt = lr_ref[d, 0:1, :], lr_ref[d, 1:2, :], lr_ref[d, 2:3, :]
        dt = jnp.exp(log_dt)
        lam_r, lam_i = a_r * dt, a_i * dt
        steps = lax.broadcasted_iota(jnp.int32, (24, 128), 0).astype(F32)
        mag = jnp.exp(lam_r * steps)
        pw_r, pw_i = mag * jnp.cos(lam_i * steps), mag * jnp.sin(lam_i * steps)
        nr, ni = pw_r[1:2] - 1.0, pw_i[1:2]
        den = a_r * a_r + a_i * a_i
        co_r, co_i = (nr * a_r + ni * a_i) / den, (ni * a_r - nr * a_i) / den
        b_r = jnp.concatenate([bt_ref[d, 0]] * n_s, axis=0)
        b_i = jnp.concatenate([bt_ref[d, 1]] * n_s, axis=0)
        bb_r, bb_i = co_r * b_r - co_i * b_i, co_r * b_i + co_i * b_r
        order = [n_s - 1 - i for i in range(n_s)] if d == 0 else list(range(n_s))
        pg_r = jnp.concatenate([jnp.broadcast_to(pw_r[n:n + 1], (S5_GC, 128)) for n in order], axis=0)
        pg_i = jnp.concatenate([jnp.broadcast_to(pw_i[n:n + 1], (S5_GC, 128)) for n in order], axis=0)
        g_types += [bb_r * pg_r - bb_i * pg_i, bb_r * pg_i + bb_i * pg_r]
        a16 += [pw_r[n_s:n_s + 1], pw_i[n_s:n_s + 1]]
        x0 = slice((n_s - 1) * S5_GC, n_s * S5_GC) if d == 0 else slice(0, S5_GC)
        x_r, x_i = bb_r[x0], bb_i[x0]
        lhs_r = jnp.concatenate([jnp.where(lane128 < S5_STATE, x_r, 0.0),
                                 jnp.where(lane128 < S5_STATE, 0.0, x_r)], axis=0)
        lhs_i = jnp.concatenate([jnp.where(lane128 < S5_STATE, x_i, 0.0),
                                 jnp.where(lane128 < S5_STATE, 0.0, x_i)], axis=0)
        dt_c = jnp.exp(lc_ref[d, 2])
        lam_rc, lam_ic = lc_ref[d, 0] * dt_c, lc_ref[d, 1] * dt_c
        mag_1 = jnp.exp(lam_rc)
        a1_r, a1_i = mag_1 * jnp.cos(lam_ic), mag_1 * jnp.sin(lam_ic)
        a1_r = jnp.concatenate([a1_r, a1_r], axis=1)
        a1_i = jnp.concatenate([a1_i, a1_i], axis=1)
        n_y = blk_of_lane if d == 0 else n_s - 1 - blk_of_lane
        ypw_r, ypw_i = jnp.ones_like(a1_r), jnp.zeros_like(a1_r)
        sq_r, sq_i = a1_r, a1_i
        for bit in (1, 2, 4, 8):
            on = (n_y & bit) != 0
            ypw_r, ypw_i = (jnp.where(on, ypw_r * sq_r - ypw_i * sq_i, ypw_r),
                            jnp.where(on, ypw_r * sq_i + ypw_i * sq_r, ypw_i))
            if bit < 8:
                sq_r, sq_i = sq_r * sq_r - sq_i * sq_i, 2.0 * sq_r * sq_i
        c_r, c_i = ct_ref[d, 0], ct_ref[d, 1]
        y_r, y_i = c_r * ypw_r - c_i * ypw_i, c_r * ypw_i + c_i * ypw_r
        k_rows.append(jnp.dot(lhs_r, y_r, preferred_element_type=F32, precision=hp)
                      - jnp.dot(lhs_i, y_i, preferred_element_type=F32, precision=hp))
        p_types += [y_r * a1_r - y_i * a1_i, -(y_r * a1_i + y_i * a1_r)]

    a_ref[...] = jnp.concatenate(a16, axis=1)
    lane_g = lax.broadcasted_iota(jnp.int32, (lanes, 128), 1)
    row_p = lax.broadcasted_iota(jnp.int32, (128, lanes), 0)
    lane_k = lax.broadcasted_iota(jnp.int32, (S5_GC, lanes), 1)
    row_m = lax.broadcasted_iota(jnp.int32, (lanes, lanes), 0)
    lane_m = lax.broadcasted_iota(jnp.int32, (lanes, lanes), 1)
    for h in range(2):
        mine_l = (lane_g >= S5_STATE) == bool(h)
        g_ref[h] = jnp.concatenate([jnp.where(mine_l, t, 0.0) for t in g_types], axis=1).astype(BF16)
        mine_r = (row_p >= S5_STATE) == bool(h)
        p_ref[h] = jnp.concatenate([jnp.where(mine_r, t, 0.0) for t in p_types], axis=0).astype(BF16)
        k_f = k_rows[0][h * S5_GC:(h + 1) * S5_GC]
        k_b = k_rows[1][h * S5_GC:(h + 1) * S5_GC]
        blocks = []
        for i in range(n_s):
            up, down = S5_GC * i, S5_GC * (n_s - 1 - i)
            f = k_f if up == 0 else jnp.where(lane_k >= up, pltpu.roll(k_f, up, axis=1), 0.0)
            b = k_b if down == 0 else jnp.where(lane_k < lanes - down,
                                                pltpu.roll(k_b, lanes - down, axis=1), 0.0)
            blocks.append(f + b)
        m = jnp.concatenate(blocks, axis=0) + jnp.where(row_m == lane_m, d_ref[h], 0.0)
        m_ref[h] = m.astype(BF16)


def _s5_prep(a_re, a_im, log_step, b_re, b_im, c_re, c_im, d_skip):
    n_g, n_p, n_c, n_s = S5_GROUPS, S5_STATE, S5_GC, S5_SUB
    lanes = n_s * n_c
    pair = lambda a: jnp.transpose(a.astype(F32).reshape(2, n_g // 2, 2 * n_p), (1, 0, 2))
    lam_row = jnp.stack([pair(a_re), pair(a_im),
                         pair(jnp.broadcast_to(log_step[..., None], a_re.shape))], axis=2)
    lam_col = jnp.broadcast_to(lam_row[..., None], lam_row.shape + (2 * n_p,))
    bt = lambda b: jnp.transpose(b.astype(F32).reshape(2, n_g // 2, 2, n_p, n_c),
                                 (1, 0, 4, 2, 3)).reshape(n_g // 2, 2, n_c, 2 * n_p)
    b_t = jnp.stack([bt(b_re), bt(b_im)], axis=2)
    ct = lambda c: jnp.tile(jnp.transpose(c.astype(F32).reshape(2, n_g // 2, 2, n_c, n_p),
                                          (1, 0, 2, 4, 3)).reshape(n_g // 2, 2, 2 * n_p, n_c),
                            (1, 1, 1, n_s))
    c_t = jnp.stack([ct(c_re), ct(c_im)], axis=2)
    d_row = jnp.tile(d_skip.astype(F32).reshape(n_g // 2, 2, 1, n_c), (1, 1, 1, n_s))
    blk = lambda *s: pl.BlockSpec((None,) + s, lambda j: (j,) + (0,) * len(s))
    grp = lambda *s: pl.BlockSpec((2,) + s, lambda j: (j,) + (0,) * len(s))
    return pl.pallas_call(
        _s5_prep_kernel,
        grid=(n_g // 2,),
        in_specs=[blk(2, 3, 2 * n_p), blk(2, 3, 2 * n_p, 2 * n_p), blk(2, 2, n_c, 2 * n_p),
                  blk(2, 2, 2 * n_p, lanes), blk(2, 1, lanes)],
        out_specs=[grp(lanes, lanes), grp(lanes, 8 * n_p), grp(8 * n_p, lanes), blk(1, 8 * n_p)],
        out_shape=[jax.ShapeDtypeStruct((n_g, lanes, lanes), BF16),
                   jax.ShapeDtypeStruct((n_g, lanes, 8 * n_p), BF16),
                   jax.ShapeDtypeStruct((n_g, 8 * n_p, lanes), BF16),
                   jax.ShapeDtypeStruct((n_g // 2, 1, 8 * n_p), F32)],
        compiler_params=pltpu.CompilerParams(dimension_semantics=("parallel",),
                                             vmem_limit_bytes=VMEM_LIMIT),
        name="s5_prep",
    )(lam_row, lam_col, b_t, c_t, d_row)


def _merge_kernel(hf_ref, hb_ref, o_ref, zm_ref, y_ref, zs_ref, x_ref, mod_ref, mhg_ref,
                  gluw_ref, glub_ref, wout_ref, fg_ref, out_ref, y_sc):
    for g8 in range(W_BRANCH // 128):
        rows = [y_ref[g8 * 8 + gp].astype(F32) for gp in range(8)]
        for i, blk in enumerate(_block_transpose8(rows)):
            y_sc[i * GRID_W:(i + 1) * GRID_W, g8 * 128:(g8 + 1) * 128] = blk.astype(BF16)

    hm = ((hf_ref[...] + hb_ref[...]) * jax.nn.sigmoid(o_ref[...])).astype(F32)
    mhg = mhg_ref[...]
    parts = []
    for hd in range(HEADS):
        sl = slice(hd * HEAD_DIM, (hd + 1) * HEAD_DIM)
        seg = hm[:, sl]
        mu = jnp.mean(seg, axis=-1, keepdims=True)
        dev = seg - mu
        var = jnp.mean(dev * dev, axis=-1, keepdims=True)
        parts.append(dev * lax.rsqrt(var + NORM_EPS) * mhg[:, sl])
    m_out = jnp.concatenate(parts, axis=-1).astype(BF16) * _silu(zm_ref[...])

    y = y_sc[...]
    gl = 0.5 * y * (1.0 + jnp.tanh(0.7978845608028654 * (y + 0.044715 * (y * y * y))))
    gate = jax.nn.sigmoid((_dot(gl, gluw_ref[...]) + glub_ref[...]).astype(BF16))
    s_out = gl * gate * _silu(zs_ref[...])

    mixed = _dot(m_out, wout_ref[0:W_BRANCH, :]) + _dot(s_out, wout_ref[W_BRANCH:2 * W_BRANCH, :])
    xo = x_ref[...] + mod_ref[2:3, :] * mixed
    ms = jnp.mean(xo * xo, axis=-1, keepdims=True)
    out_ref[...] = xo * lax.rsqrt(ms + NORM_EPS) * fg_ref[...]


def _merge(hf, hb, o, zm, y, zs, x, mod, mh_g, glu_w, glu_b, w_out, final_g):
    bsz, t_len, _ = x.shape
    tm = 8 * GRID_W
    tile = pl.BlockSpec((None, tm, D_MODEL), lambda b, i: (b, i, 0))
    y_spec = pl.BlockSpec((S5_GROUPS, GRID_W, 128), lambda b, i: (0, (i // 2) * bsz + b, i % 2))
    const = lambda shape: pl.BlockSpec(shape, lambda b, i: (0,) * len(shape),
                                       pipeline_mode=pl.Buffered(1))
    return pl.pallas_call(
        _merge_kernel,
        grid=(bsz, t_len // tm),
        in_specs=[tile] * 4 + [y_spec, tile, tile,
                               pl.BlockSpec((None, 3, D_MODEL), lambda b, i: (b, 0, 0)),
                               const((1, W_BRANCH)), const((W_BRANCH, W_BRANCH)),
                               const((1, W_BRANCH)), const((2 * W_BRANCH, D_MODEL)),
                               const((1, D_MODEL))],
        out_specs=tile,
        out_shape=jax.ShapeDtypeStruct((bsz, t_len, D_MODEL), F32),
        scratch_shapes=[pltpu.VMEM((tm, W_BRANCH), BF16)],
        compiler_params=pltpu.CompilerParams(dimension_semantics=("parallel", "arbitrary"),
                                             vmem_limit_bytes=VMEM_LIMIT),
        name="merge",
    )(hf, hb, o, zm, y, zs, x, mod, mh_g.reshape(1, -1), glu_w, glu_b.reshape(1, -1), w_out,
      final_g.reshape(1, -1))


def _s5_rows_ctx(u):
    bsz, t_len, _ = u.shape
    a = u.reshape(bsz, t_len // S5_SUB, S5_SUB, S5_GROUPS, S5_GC)
    a = jnp.transpose(a, (3, 1, 0, 2, 4))
    return a.reshape(S5_GROUPS, (t_len // S5_SUB) * bsz, S5_SUB * S5_GC)


def kernel(x, c, ctx, c_ctx, norm_g, ada_w, ada_b, w_in, b_gate, conv_qk, mh_g, s5_a_re, s5_a_im,
           s5_log_step, s5_b_re, s5_b_im, s5_c_re, s5_c_im, s5_d, glu_w, glu_b, w_out, final_g):
    bsz, t_len, _ = x.shape
    layer = 0

    cc = jnp.zeros((16, D_MODEL), F32).at[:bsz].set(c).at[bsz].set(c_ctx)
    mod = _ada(cc, ada_w[layer], ada_b[layer]).reshape(16, 3, D_MODEL)
    mod_x = mod[:bsz]
    mod_c = jnp.broadcast_to(mod[bsz][None], (bsz, 3, D_MODEL))

    w = w_in[layer]
    wb = W_BRANCH
    w_qk = w[:, 0:2 * wb].astype(BF16)
    gate0 = 5 * wb
    w_gate = jnp.pad(w[:, gate0:gate0 + N_GATES], ((0, 0), (0, GATE_PAD - N_GATES)))
    w_main = jnp.concatenate([w[:, 2 * wb:5 * wb], w[:, gate0 + N_GATES:], w_gate], axis=1).astype(BF16)
    b_gate_pad = jnp.pad(b_gate[layer].reshape(1, N_GATES), ((0, 0), (0, GATE_PAD - N_GATES)))

    proj = functools.partial(_inproj, norm_g=norm_g[layer], w_qk=w_qk, w_main=w_main,
                             conv_qk=conv_qk[layer], b_gate_pad=b_gate_pad)
    kt_c, v_c, u_c, gt_c, gx_c = proj(ctx, mod_c, tm=256, s5_layout=False, state_only=True)
    q_x, kt_x, v_x, o_x, zm_x, u_x, zs_x, gt_x, gx_x = proj(x, mod_x, tm=8 * GRID_W, s5_layout=True)

    ctx_state = _mlstm(None, kt_c, v_c, gt_c, gx_c, None)
    h_f, h_b = _mlstm(q_x, kt_x, v_x, gt_x, gx_x, ctx_state)

    m_all, g_all, p_all, a16 = _s5_prep(
        s5_a_re[layer], s5_a_im[layer], s5_log_step[layer], s5_b_re[layer], s5_b_im[layer],
        s5_c_re[layer], s5_c_im[layer], s5_d[layer])
    y_x = _s5(_s5_rows_ctx(u_c), u_x, g_all, m_all, p_all, a16, bsz)

    return _merge(h_f, h_b, o_x, zm_x, y_x, zs_x, x, mod_x, mh_g[layer], glu_w[layer].astype(BF16),
                  glu_b[layer], w_out[layer].astype(BF16), final_g)
```

```python
import functools

import jax
import jax.numpy as jnp
from jax import lax
from jax.experimental import pallas as pl
from jax.experimental.pallas import tpu as pltpu

F32 = jnp.float32
BF16 = jnp.bfloat16

D_MODEL = 1024
HEADS = 4
HEAD_DIM = 256
W_BRANCH = 1024
S5_GROUPS = 64
S5_GC = 16
S5_STATE = 64
S5_SUB = 16
GRID_W = 64
S5_ROW_PITCH = 72
N_GATES = 16
GATE_PAD = 128
CONV_W = 3
NORM_EPS = 1e-6
MLSTM_CHUNK = 256
MLSTM_CHUNKS_PER_STEP = (4, 2, 1)
NEG_BIG = -1e30

VMEM_LIMIT = 56 * 1024 * 1024


def _silu(a):
    return a * jax.nn.sigmoid(a)


def _log_sigmoid(a):
    return jnp.minimum(a, 0.0) - jnp.log1p(jnp.exp(-jnp.abs(a)))


def _dot(a, b):
    return jnp.dot(a, b, preferred_element_type=F32)


def _split_bf16(a):
    hi = a.astype(BF16)
    lo = (a - hi.astype(F32)).astype(BF16)
    return hi, lo


def _block_transpose8(rows):
    lane = lax.broadcasted_iota(jnp.int32, rows[0].shape, 1)
    blk = lane // S5_GC
    for d in (4, 2, 1):
        keep = (blk & d) == 0
        new = list(rows)
        for i in range(8):
            if i & d == 0:
                a, b = rows[i], rows[i + d]
                new[i] = jnp.where(keep, a, pltpu.roll(b, d * S5_GC, axis=1))
                new[i + d] = jnp.where(keep, pltpu.roll(a, 128 - d * S5_GC, axis=1), b)
        rows = new
    return rows


def _chunk_gate_rows(gt_blk, reverse):
    chunk = gt_blk.shape[1]
    row = lax.broadcasted_iota(jnp.int32, (chunk, chunk), 0)
    col = lax.broadcasted_iota(jnp.int32, (chunk, chunk), 1)
    mask_st = (row >= col) if reverse else (row <= col)
    tri_st = jnp.where(mask_st, 1.0, 0.0).astype(BF16)
    hi, lo = _split_bf16(gt_blk)
    cum = _dot(hi, tri_st) + _dot(lo, tri_st)
    li0 = 8 if reverse else 0
    b_r = cum[li0 + HEADS:li0 + 2 * HEADS, :]
    run_max = gt_blk[li0:li0 + HEADS, :] - b_r
    lane = lax.broadcasted_iota(jnp.int32, run_max.shape, 1)
    shift = 1
    while shift < chunk:
        if reverse:
            moved = jnp.where(lane < chunk - shift, pltpu.roll(run_max, chunk - shift, axis=1), NEG_BIG)
        else:
            moved = jnp.where(lane >= shift, pltpu.roll(run_max, shift, axis=1), NEG_BIG)
        run_max = jnp.maximum(run_max, moved)
        shift *= 2
    return b_r, run_max


def _ada_kernel(c_ref, w_ref, b_ref, o_ref):
    s = _silu(c_ref[...])
    o_ref[...] = jnp.dot(s, w_ref[...], preferred_element_type=F32,
                         precision=lax.Precision.HIGHEST) + b_ref[...]


def _ada(cc, ada_w, ada_b):
    rows = cc.shape[0]
    n_out = ada_w.shape[1]
    tn = 1024
    return pl.pallas_call(
        _ada_kernel,
        grid=(n_out // tn,),
        in_specs=[pl.BlockSpec((rows, D_MODEL), lambda j: (0, 0)),
                  pl.BlockSpec((D_MODEL, tn), lambda j: (0, j)),
                  pl.BlockSpec((1, tn), lambda j: (0, j))],
        out_specs=pl.BlockSpec((rows, tn), lambda j: (0, j)),
        out_shape=jax.ShapeDtypeStruct((rows, n_out), F32),
        compiler_params=pltpu.CompilerParams(dimension_semantics=("arbitrary",),
                                             vmem_limit_bytes=VMEM_LIMIT),
        name="ada",
    )(cc, ada_w, ada_b.reshape(1, n_out))


HALO = 16


def _inproj_kernel(x_ref, xp_ref, xn_ref, mod_ref, ng_ref, wqk_ref, wm_ref, conv_ref, bg_ref,
                   *rest, tm, s5_layout, state_only):
    if state_only:
        kt_ref, v_ref, u_ref, gt_ref, gx_ref, lhs_sc = rest
    else:
        q_ref, kt_ref, v_ref, o_ref, zm_ref, u_ref, zs_ref, gt_ref, gx_ref, lhs_sc = rest
    i = pl.program_id(1)
    nt = pl.num_programs(1)
    shift = mod_ref[0:1, :]
    gain = ng_ref[...] * (1.0 + mod_ref[1:2, :])

    def norm_mod(a):
        ms = jnp.mean(a * a, axis=-1, keepdims=True)
        return a * lax.rsqrt(ms + NORM_EPS) * gain + shift

    lhs_sc[0:tm, :] = norm_mod(x_ref[...]).astype(BF16)
    lhs_sc[tm:tm + HALO, :] = norm_mod(xp_ref[...]).astype(BF16)
    lhs_sc[tm + HALO:tm + 2 * HALO, :] = norm_mod(xn_ref[...]).astype(BF16)

    h = lhs_sc[0:tm, :]
    zg = _dot(h, wm_ref[:, 5120:5120 + GATE_PAD]) + bg_ref[...]
    lane = lax.broadcasted_iota(jnp.int32, zg.shape, 1)
    is_forget = jnp.logical_and(lane < N_GATES, (lane % 8) >= 4)
    gates = jnp.where(is_forget, _log_sigmoid(zg), zg)
    gt = gates.T[0:N_GATES, :]
    gt_ref[...] = gt
    for c0 in range(0, tm, MLSTM_CHUNK):
        parts = []
        for reverse in (False, True):
            parts += list(_chunk_gate_rows(gt[:, c0:c0 + MLSTM_CHUNK], reverse))
        gx_ref[:, c0:c0 + MLSTM_CHUNK] = jnp.concatenate(parts, axis=0)

    u = _dot(h, wm_ref[:, 3072:4096])
    if s5_layout:
        for g8 in range(W_BRANCH // 128):
            rows = [u[i * GRID_W:(i + 1) * GRID_W, g8 * 128:(g8 + 1) * 128] for i in range(8)]
            for gp, blk in enumerate(_block_transpose8(rows)):
                u_ref[g8 * 8 + gp] = blk.astype(BF16)
    else:
        u_ref[...] = u.astype(BF16)

    valid_prev = (i > 0).astype(F32)
    valid_next = (i < nt - 1).astype(F32)

    def conv_stage(half):
        cols = slice(half * W_BRANCH, (half + 1) * W_BRANCH)
        z = _dot(lhs_sc[...], wqk_ref[:, cols])
        zc = z[0:tm]
        row8 = lax.broadcasted_iota(jnp.int32, (8, W_BRANCH), 0)
        before = pltpu.roll(zc, 1, axis=0)
        first = jnp.where(row8 == 0, z[tm + HALO - 1:tm + HALO] * valid_prev, before[0:8])
        before = jnp.concatenate([first, before[8:]], axis=0)
        after = pltpu.roll(zc, tm - 1, axis=0)
        final = jnp.where(row8 == 7, z[tm + HALO:tm + HALO + 1] * valid_next, after[tm - 8:tm])
        after = jnp.concatenate([after[:tm - 8], final], axis=0)
        cw = conv_ref[:, cols]
        conv = cw[0:1] * before + cw[1:2] * zc + cw[2:3] * after
        if half == 0:
            q_ref[...] = _silu(conv.astype(BF16))
        else:
            kt_ref[...] = (_silu(conv.astype(BF16)) * HEAD_DIM ** -0.5).T

    v_ref[...] = _dot(h, wm_ref[:, 0:1024]).astype(BF16)
    if not state_only:
        conv_stage(0)
        o_ref[...] = _dot(h, wm_ref[:, 1024:2048]).astype(BF16)
    conv_stage(1)
    if not state_only:
        zm_ref[...] = _dot(h, wm_ref[:, 2048:3072]).astype(BF16)
        zs_ref[...] = _dot(h, wm_ref[:, 4096:5120]).astype(BF16)


def _inproj(x, mod, norm_g, w_qk, w_main, conv_qk, b_gate_pad, tm, s5_layout, state_only=False):
    bsz, t_len, _ = x.shape
    nt = t_len // tm
    nhb = t_len // HALO
    tok = lambda w, dt: jax.ShapeDtypeStruct((bsz, t_len, w), dt)
    tile = lambda w: pl.BlockSpec((None, tm, w), lambda b, i: (b, i, 0))
    tile_t = lambda w: pl.BlockSpec((None, w, tm), lambda b, i: (b, 0, i))
    const = lambda shape: pl.BlockSpec(shape, lambda b, i: (0,) * len(shape),
                                       pipeline_mode=pl.Buffered(1))
    if s5_layout:
        assert tm == 8 * GRID_W and t_len % (2 * tm) == 0
        u_spec = pl.BlockSpec((S5_GROUPS, GRID_W, 128), lambda b, i: (0, (i // 2) * bsz + b, i % 2))
        u_shape = jax.ShapeDtypeStruct((S5_GROUPS, (t_len // (2 * tm)) * bsz * GRID_W, 256), BF16)
    else:
        u_spec, u_shape = tile(W_BRANCH), tok(W_BRANCH, BF16)
    kt_shape = jax.ShapeDtypeStruct((bsz, W_BRANCH, t_len), BF16)
    gt_shape = jax.ShapeDtypeStruct((bsz, N_GATES, t_len), F32)
    if state_only:
        out_specs = [tile_t(W_BRANCH), tile(W_BRANCH), u_spec, tile_t(N_GATES), tile_t(N_GATES)]
        out_shape = [kt_shape, tok(W_BRANCH, BF16), u_shape, gt_shape, gt_shape]
    else:
        out_specs = [tile(W_BRANCH), tile_t(W_BRANCH)] + [tile(W_BRANCH)] * 3 + [
            u_spec, tile(W_BRANCH), tile_t(N_GATES), tile_t(N_GATES)]
        out_shape = [tok(W_BRANCH, BF16), kt_shape] + [tok(W_BRANCH, BF16)] * 3 + [
            u_shape, tok(W_BRANCH, BF16), gt_shape, gt_shape]
    return pl.pallas_call(
        functools.partial(_inproj_kernel, tm=tm, s5_layout=s5_layout, state_only=state_only),
        grid=(bsz, nt),
        in_specs=[
            tile(D_MODEL),
            pl.BlockSpec((None, HALO, D_MODEL),
                         lambda b, i: (b, jnp.maximum(i * (tm // HALO) - 1, 0), 0)),
            pl.BlockSpec((None, HALO, D_MODEL),
                         lambda b, i: (b, jnp.minimum((i + 1) * (tm // HALO), nhb - 1), 0)),
            pl.BlockSpec((None, 3, D_MODEL), lambda b, i: (b, 0, 0)),
            const((1, D_MODEL)),
            const((D_MODEL, 2 * W_BRANCH)),
            const((D_MODEL, 5 * W_BRANCH + GATE_PAD)),
            const((CONV_W, 2 * W_BRANCH)),
            const((1, GATE_PAD)),
        ],
        out_specs=out_specs,
        out_shape=out_shape,
        scratch_shapes=[pltpu.VMEM((tm + 2 * HALO, D_MODEL), BF16)],
        compiler_params=pltpu.CompilerParams(dimension_semantics=("parallel", "arbitrary"),
                                             vmem_limit_bytes=VMEM_LIMIT),
        name="inproj",
    )(x, x, x, mod, norm_g.reshape(1, D_MODEL), w_qk, w_main, conv_qk, b_gate_pad)


def _mlstm_gates(gt_ref, gx_ref, m_sc, *, reverse, chunk, with_output):
    li0 = 8 if reverse else 0
    last = 0 if reverse else chunk - 1
    li_r = gt_ref[li0:li0 + HEADS, :]
    b_r = gx_ref[li0:li0 + HEADS, :]
    b_last = b_r[:, last:last + 1]
    m_old = m_sc[:, 0:1]
    g_r = b_last - b_r + li_r
    m_new = jnp.maximum(b_last + m_old, jnp.max(g_r, axis=1, keepdims=True))
    out = dict(m_old=m_old, m_new=m_new, decay=jnp.exp(b_last + m_old - m_new),
               k_scale=jnp.exp(g_r - m_new).astype(BF16))
    if not with_output:
        return out
    row = lax.broadcasted_iota(jnp.int32, (chunk, chunk), 0)
    col = lax.broadcasted_iota(jnp.int32, (chunk, chunk), 1)
    mask_ts = (col >= row) if reverse else (col <= row)
    mm_r = jnp.maximum(gx_ref[li0 + HEADS:li0 + 2 * HEADS, :], m_old).astype(BF16)
    b_hi, b_lo = _split_bf16(b_r)
    rows = jnp.concatenate([mm_r, b_hi, b_lo, jnp.zeros_like(b_hi)], axis=0)
    sel_r = lax.broadcasted_iota(jnp.int32, (4 * HEADS, 2 * HEADS * 128), 0)
    sel_c = lax.broadcasted_iota(jnp.int32, (4 * HEADS, 2 * HEADS * 128), 1) // 128
    head_r = sel_r % HEADS
    pick = jnp.logical_or(jnp.logical_and(sel_r < HEADS, sel_c % HEADS == head_r),
                          jnp.logical_and(jnp.logical_and(sel_r >= HEADS, sel_r < 3 * HEADS),
                                          sel_c == HEADS + head_r))
    sel = jnp.where(pick, 1.0, 0.0).astype(BF16)
    cols = lax.dot_general(rows, sel, (((0,), (0,)), ((), ())), preferred_element_type=F32)
    out.update(a_r=li_r - b_r, cols=cols, mask_ts=mask_ts)
    return out


def _mlstm_heads(q_ref, kt_ref, v_ref, h_ref, c_sc, n_sc, m_sc, gates, *, chunk):
    with_output = h_ref is not None
    m_old, m_new, decay, k_scale = gates["m_old"], gates["m_new"], gates["decay"], gates["k_scale"]
    if with_output:
        a_r, cols, mask_ts = gates["a_r"], gates["cols"], gates["mask_ts"]
    ones_rows = jnp.ones((8, chunk), BF16)
    wide = lambda a, n: jnp.concatenate([a] * (n // 128), axis=1)
    for hd in range(HEADS):
        sl = slice(hd * HEAD_DIM, (hd + 1) * HEAD_DIM)
        kt = kt_ref[sl, :]
        v = v_ref[:, sl]
        c_old = c_sc[hd]
        n_old = n_sc[hd]
        kw_t = kt * k_scale[hd:hd + 1, :]
        if with_output:
            q = q_ref[:, sl]
            mm_c = cols[:, hd * 128:(hd + 1) * 128]
            bm_c = cols[:, (HEADS + hd) * 128:(HEADS + hd + 1) * 128]
            decay_mat = jnp.exp(jnp.where(mask_ts, a_r[hd:hd + 1, :] - wide(mm_c, chunk), NEG_BIG))
            s_f = _dot(q, kt) * decay_mat
            w_inter = jnp.exp(m_old[hd:hd + 1, :] - mm_c)
            den = (w_inter * jnp.sum(q.astype(F32) * n_old[0:1, :], axis=1, keepdims=True)
                   + jnp.sum(s_f, axis=1, keepdims=True))
            q_w = q * wide(w_inter, HEAD_DIM).astype(BF16)
            num = _dot(jnp.concatenate([q_w, s_f.astype(BF16)], axis=1),
                       jnp.concatenate([c_old.astype(BF16), v], axis=0))
            inv = 1.0 / jnp.maximum(jnp.abs(den), jnp.exp(-bm_c))
            h_ref[:, sl] = (num * wide(inv, HEAD_DIM)).astype(BF16)
        c_sc[hd] = decay[hd:hd + 1, :] * c_old + _dot(kw_t, v)
        n_sc[hd] = decay[hd:hd + 1, :] * n_old + lax.dot_general(
            ones_rows, kw_t, (((1,), (1,)), ((), ())), preferred_element_type=F32)
        m_sc[hd:hd + 1, :] = jnp.broadcast_to(m_new[hd:hd + 1, :], (1, 128))


def _mlstm_kernel(*refs, with_output, chunk, subs):
    n_in = 5 if with_output else 4
    ins = [refs[0:n_in], refs[n_in:2 * n_in]]
    rest = refs[2 * n_in:]
    if with_output:
        c0_ref, n0_ref, m0_ref, hf_ref, hb_ref, c_sc, n_sc, m_sc = rest
        h_refs = (hf_ref, hb_ref)
    else:
        co_ref, no_ref, mo_ref, c_sc, n_sc, m_sc = rest
        h_refs = (None, None)
        ins = [(None,) + tuple(r) for r in ins]
    i = pl.program_id(1)
    nc = pl.num_programs(1)

    @pl.when(i == 0)
    def _():
        if with_output:
            c_sc[...] = c0_ref[...]
            n_sc[...] = n0_ref[...]
            m_sc[...] = m0_ref[...]
        else:
            c_sc[...] = jnp.zeros_like(c_sc)
            n_sc[...] = jnp.zeros_like(n_sc)
            m_sc[...] = jnp.zeros_like(m_sc)

    for sub in range(subs):
        pos = (sub, subs - 1 - sub)
        rows = [pl.ds(pos[d] * chunk, chunk) for d in range(2)]
        view = lambda ref, d: None if ref is None else ref.at[rows[d], :]
        view_t = lambda ref, d: ref.at[:, rows[d]]
        gates = [_mlstm_gates(view_t(ins[d][3], d), view_t(ins[d][4], d), m_sc.at[d], reverse=bool(d),
                              chunk=chunk, with_output=with_output) for d in range(2)]
        for d in range(2):
            _mlstm_heads(view(ins[d][0], d), view_t(ins[d][1], d), view(ins[d][2], d), view(h_refs[d], d),
                         c_sc.at[d], n_sc.at[d], m_sc.at[d], gates[d], chunk=chunk)

    if not with_output:
        @pl.when(i == nc - 1)
        def _():
            co_ref[...] = c_sc[...]
            no_ref[...] = n_sc[...]
            mo_ref[...] = m_sc[...]


def _mlstm(q, kt, v, gt, gx, state):
    with_output = q is not None
    bsz, t_len, _ = v.shape
    chunk = MLSTM_CHUNK
    subs = next(n for n in MLSTM_CHUNKS_PER_STEP if t_len % (n * chunk) == 0)
    blk = subs * chunk
    nc = t_len // blk
    cidx = (lambda i: i, lambda i: nc - 1 - i)
    tile = lambda w, d: pl.BlockSpec((None, blk, w), lambda b, i: (b, cidx[d](i), 0))
    tile_t = lambda w, d: pl.BlockSpec((None, w, blk), lambda b, i: (b, 0, cidx[d](i)))
    st_dims = [(2, HEADS, HEAD_DIM, HEAD_DIM), (2, HEADS, 8, HEAD_DIM), (2, HEADS, 128)]
    st_specs = [pl.BlockSpec((None,) + s, lambda b, i, n=len(s): (b,) + (0,) * n) for s in st_dims]
    st_shapes = [jax.ShapeDtypeStruct((bsz,) + s, F32) for s in st_dims]
    in_specs, args = [], []
    for d in range(2):
        in_specs += ([tile(W_BRANCH, d)] if with_output else []) + [
            tile_t(W_BRANCH, d), tile(W_BRANCH, d), tile_t(N_GATES, d), tile_t(N_GATES, d)]
        args += ([q] if with_output else []) + [kt, v, gt, gx]
    if with_output:
        in_specs, args = in_specs + st_specs, args + list(state)
        out_specs = [tile(W_BRANCH, 0), tile(W_BRANCH, 1)]
        out_shape = [jax.ShapeDtypeStruct((bsz, t_len, W_BRANCH), BF16)] * 2
    else:
        out_specs, out_shape = st_specs, st_shapes
    return pl.pallas_call(
        functools.partial(_mlstm_kernel, with_output=with_output, chunk=chunk, subs=subs),
        grid=(bsz, nc),
        in_specs=in_specs,
        out_specs=out_specs,
        out_shape=out_shape,
        scratch_shapes=[pltpu.VMEM(s, F32) for s in st_dims],
        compiler_params=pltpu.CompilerParams(dimension_semantics=("parallel", "arbitrary"),
                                             vmem_limit_bytes=VMEM_LIMIT),
        name="mlstm_out" if with_output else "mlstm_state",
    )(*args)


def _s5_kernel(vc_ref, vx_ref, g_ref, m_ref, p_ref, a_ref, y_ref, gu_sc, s_sc, *,
               nk_ctx, n_rc, bsz, rblk):
    rows_ctx = nk_ctx * bsz
    rows_x = n_rc * bsz * GRID_W

    def increments(v_ref, r0, r1):
        return _dot(v_ref[0, r0:r1, :], g_ref[0]) + _dot(v_ref[1, r0:r1, :], g_ref[1])

    inc = increments(vc_ref, 0, rows_ctx)
    for comp in range(4):
        gu_sc[comp, 0:rows_ctx, :] = inc[:, comp * 128:(comp + 1) * 128]
    for r0 in range(0, rows_x, rblk):
        inc = increments(vx_ref, r0, r0 + rblk)
        for run in range(rblk // GRID_W):
            dst = rows_ctx + (r0 // GRID_W + run) * S5_ROW_PITCH
            for comp in range(4):
                gu_sc[comp, dst:dst + GRID_W, :] = (
                    inc[run * GRID_W:(run + 1) * GRID_W, comp * 128:(comp + 1) * 128])

    a = a_ref[...]
    a_pow = [jnp.broadcast_to(a[:, comp * 128:(comp + 1) * 128], (bsz, 128)) for comp in range(4)]
    zero = jnp.zeros((bsz, 128), F32)

    def cmul(x_r, x_i, y_r, y_i):
        return x_r * y_r - x_i * y_i, x_r * y_i + x_i * y_r

    a_sq = [cmul(a_pow[2 * d], a_pow[2 * d + 1], a_pow[2 * d], a_pow[2 * d + 1]) for d in range(2)]

    def step(rows, carry, direction):
        s_r, s_i = carry
        inc_r = gu_sc[2 * direction, rows, :]
        inc_i = gu_sc[2 * direction + 1, rows, :]
        gu_sc[2 * direction, rows, :] = s_r
        gu_sc[2 * direction + 1, rows, :] = s_i
        p_r, p_i = cmul(a_pow[2 * direction], a_pow[2 * direction + 1], s_r, s_i)
        return p_r + inc_r, p_i + inc_i

    def step2(rows0, rows1, carry, direction):
        s_r, s_i = carry
        a_r, a_i = a_pow[2 * direction], a_pow[2 * direction + 1]
        inc0_r, inc0_i = gu_sc[2 * direction, rows0, :], gu_sc[2 * direction + 1, rows0, :]
        inc1_r, inc1_i = gu_sc[2 * direction, rows1, :], gu_sc[2 * direction + 1, rows1, :]
        gu_sc[2 * direction, rows0, :] = s_r
        gu_sc[2 * direction + 1, rows0, :] = s_i
        m_r, m_i = cmul(a_r, a_i, s_r, s_i)
        gu_sc[2 * direction, rows1, :] = m_r + inc0_r
        gu_sc[2 * direction + 1, rows1, :] = m_i + inc0_i
        c_r, c_i = cmul(a_r, a_i, inc0_r, inc0_i)
        q_r, q_i = cmul(a_sq[direction][0], a_sq[direction][1], s_r, s_i)
        return q_r + (c_r + inc1_r), q_i + (c_i + inc1_i)

    def ctx_rows(k):
        return pl.ds(pl.multiple_of(k * bsz, bsz), bsz)

    def x_rows(w, rc):
        return pl.ds(rows_ctx + rc * (bsz * S5_ROW_PITCH) + w, bsz, stride=S5_ROW_PITCH)

    def ctx_body(k, carry):
        return step(ctx_rows(k), carry[0], 0), step(ctx_rows(nk_ctx - 1 - k), carry[1], 1)

    def x_body(w, carry):
        c_f, c_b = carry
        w_b = GRID_W - 1 - w
        for rc in range(0, n_rc - 1, 2):
            c_f = step2(x_rows(w, rc), x_rows(w, rc + 1), c_f, 0)
            c_b = step2(x_rows(w_b, n_rc - 1 - rc), x_rows(w_b, n_rc - 2 - rc), c_b, 1)
        if n_rc % 2:
            c_f = step(x_rows(w, n_rc - 1), c_f, 0)
            c_b = step(x_rows(w_b, 0), c_b, 1)
        return c_f, c_b

    carry = lax.fori_loop(0, nk_ctx, ctx_body, ((zero, zero), (zero, zero)))
    lax.fori_loop(0, GRID_W, x_body, carry)

    for r0 in range(0, rows_x, rblk):
        r1 = r0 + rblk
        for run in range(rblk // GRID_W):
            src = rows_ctx + (r0 // GRID_W + run) * S5_ROW_PITCH
            for comp in range(4):
                s_sc[r0 + run * GRID_W:r0 + (run + 1) * GRID_W, comp * 128:(comp + 1) * 128] = (
                    gu_sc[comp, src:src + GRID_W, :].astype(BF16))
        for gg in range(2):
            y_ref[gg, r0:r1, :] = (_dot(vx_ref[gg, r0:r1, :], m_ref[gg])
                                   + _dot(s_sc[r0:r1, :], p_ref[gg])).astype(BF16)


def _s5(v_ctx, v_x, g_all, m_all, p_all, a16, bsz):
    rows_ctx, rows_x = v_ctx.shape[1], v_x.shape[1]
    lanes = S5_SUB * S5_GC
    return pl.pallas_call(
        functools.partial(_s5_kernel, nk_ctx=rows_ctx // bsz, n_rc=rows_x // (bsz * GRID_W),
                          bsz=bsz, rblk=512),
        grid=(S5_GROUPS // 2,),
        in_specs=[pl.BlockSpec((2, rows_ctx, lanes), lambda j: (j, 0, 0)),
                  pl.BlockSpec((2, rows_x, lanes), lambda j: (j, 0, 0)),
                  pl.BlockSpec((2, lanes, 512), lambda j: (j, 0, 0)),
                  pl.BlockSpec((2, lanes, lanes), lambda j: (j, 0, 0)),
                  pl.BlockSpec((2, 512, lanes), lambda j: (j, 0, 0)),
                  pl.BlockSpec((None, 1, 512), lambda j: (j, 0, 0))],
        out_specs=pl.BlockSpec((2, rows_x, lanes), lambda j: (j, 0, 0)),
        out_shape=jax.ShapeDtypeStruct((S5_GROUPS, rows_x, lanes), BF16),
        scratch_shapes=[pltpu.VMEM((4, rows_ctx + (rows_x // GRID_W) * S5_ROW_PITCH, 128), F32),
                        pltpu.VMEM((rows_x, 512), BF16)],
        compiler_params=pltpu.CompilerParams(dimension_semantics=("parallel",),
                                             vmem_limit_bytes=VMEM_LIMIT),
        name="s5",
    )(v_ctx, v_x, g_all, m_all, p_all, a16)


def _s5_prep_kernel(lr_ref, lc_ref, bt_ref, ct_ref, d_ref, m_ref, g_ref, p_ref, a_ref):
    hp = lax.Precision.HIGHEST
    n_s, lanes = S5_SUB, S5_SUB * S5_GC
    lane128 = lax.broadcasted_iota(jnp.int32, (n_s, 128), 1)
    blk_of_lane = lax.broadcasted_iota(jnp.int32, (128, lanes), 1) // S5_GC
    g_types, p_types, a16, k_rows = [], [], [], []
    for d in range(2):
        a_r, a_i, log_dt = lr_ref[d, 0:1, :], lr_ref[d, 1:2, :], lr_ref[d, 2:3, :]
        dt = jnp.exp(log_dt)
        lam_r, lam_i = a_r * dt, a_i * dt
        steps = lax.broadcasted_iota(jnp.int32, (24, 128), 0).astype(F32)
        mag = jnp.exp(lam_r * steps)
        pw_r, pw_i = mag * jnp.cos(lam_i * steps), mag * jnp.sin(lam_i * steps)
        nr, ni = pw_r[1:2] - 1.0, pw_i[1:2]
        den = a_r * a_r + a_i * a_i
        co_r, co_i = (nr * a_r + ni * a_i) / den, (ni * a_r - nr * a_i) / den
        b_r = jnp.concatenate([bt_ref[d, 0]] * n_s, axis=0)
        b_i = jnp.concatenate([bt_ref[d, 1]] * n_s, axis=0)
        bb_r, bb_i = co_r * b_r - co_i * b_i, co_r * b_i + co_i * b_r
        order = [n_s - 1 - i for i in range(n_s)] if d == 0 else list(range(n_s))
        pg_r = jnp.concatenate([jnp.broadcast_to(pw_r[n:n + 1], (S5_GC, 128)) for n in order], axis=0)
        pg_i = jnp.concatenate([jnp.broadcast_to(pw_i[n:n + 1], (S5_GC, 128)) for n in order], axis=0)
        g_types += [bb_r * pg_r - bb_i * pg_i, bb_r * pg_i + bb_i * pg_r]
        a16 += [pw_r[n_s:n_s + 1], pw_i[n_s:n_s + 1]]
        x0 = slice((n_s - 1) * S5_GC, n_s * S5_GC) if d == 0 else slice(0, S5_GC)
        x_r, x_i = bb_r[x0], bb_i[x0]
        lhs_r = jnp.concatenate([jnp.where(lane128 < S5_STATE, x_r, 0.0),
                                 jnp.where(lane128 < S5_STATE, 0.0, x_r)], axis=0)
        lhs_i = jnp.concatenate([jnp.where(lane128 < S5_STATE, x_i, 0.0),
                                 jnp.where(lane128 < S5_STATE, 0.0, x_i)], axis=0)
        dt_c = jnp.exp(lc_ref[d, 2])
        lam_rc, lam_ic = lc_ref[d, 0] * dt_c, lc_ref[d, 1] * dt_c
        mag_1 = jnp.exp(lam_rc)
        a1_r, a1_i = mag_1 * jnp.cos(lam_ic), mag_1 * jnp.sin(lam_ic)
        a1_r = jnp.concatenate([a1_r, a1_r], axis=1)
        a1_i = jnp.concatenate([a1_i, a1_i], axis=1)
        n_y = blk_of_lane if d == 0 else n_s - 1 - blk_of_lane
        ypw_r, ypw_i = jnp.ones_like(a1_r), jnp.zeros_like(a1_r)
        sq_r, sq_i = a1_r, a1_i
        for bit in (1, 2, 4, 8):
            on = (n_y & bit) != 0
            ypw_r, ypw_i = (jnp.where(on, ypw_r * sq_r - ypw_i * sq_i, ypw_r),
                            jnp.where(on, ypw_r * sq_i + ypw_i * sq_r, ypw_i))
            if bit < 8:
                sq_r, sq_i = sq_r * sq_r - sq_i * sq_i, 2.0 * sq_r * sq_i
        c_r, c_i = ct_ref[d, 0], ct_ref[d, 1]
        y_r, y_i = c_r * ypw_r - c_i * ypw_i, c_r * ypw_i + c_i * ypw_r
        k_rows.append(jnp.dot(lhs_r, y_r, preferred_element_type=F32, precision=hp)
                      - jnp.dot(lhs_i, y_i, preferred_element_type=F32, precision=hp))
        p_types += [y_r * a1_r - y_i * a1_i, -(y_r * a1_i + y_i * a1_r)]

    a_ref[...] = jnp.concatenate(a16, axis=1)
    lane_g = lax.broadcasted_iota(jnp.int32, (lanes, 128), 1)
    row_p = lax.broadcasted_iota(jnp.int32, (128, lanes), 0)
    lane_k = lax.broadcasted_iota(jnp.int32, (S5_GC, lanes), 1)
    row_m = lax.broadcasted_iota(jnp.int32, (lanes, lanes), 0)
    lane_m = lax.broadcasted_iota(jnp.int32, (lanes, lanes), 1)
    for h in range(2):
        mine_l = (lane_g >= S5_STATE) == bool(h)
        g_ref[h] = jnp.concatenate([jnp.where(mine_l, t, 0.0) for t in g_types], axis=1).astype(BF16)
        mine_r = (row_p >= S5_STATE) == bool(h)
        p_ref[h] = jnp.concatenate([jnp.where(mine_r, t, 0.0) for t in p_types], axis=0).astype(BF16)
        k_f = k_rows[0][h * S5_GC:(h + 1) * S5_GC]
        k_b = k_rows[1][h * S5_GC:(h + 1) * S5_GC]
        blocks = []
        for i in range(n_s):
            up, down = S5_GC * i, S5_GC * (n_s - 1 - i)
            f = k_f if up == 0 else jnp.where(lane_k >= up, pltpu.roll(k_f, up, axis=1), 0.0)
            b = k_b if down == 0 else jnp.where(lane_k < lanes - down,
                                                pltpu.roll(k_b, lanes - down, axis=1), 0.0)
            blocks.append(f + b)
        m = jnp.concatenate(blocks, axis=0) + jnp.where(row_m == lane_m, d_ref[h], 0.0)
        m_ref[h] = m.astype(BF16)


def _s5_prep(a_re, a_im, log_step, b_re, b_im, c_re, c_im, d_skip):
    n_g, n_p, n_c, n_s = S5_GROUPS, S5_STATE, S5_GC, S5_SUB
    lanes = n_s * n_c
    pair = lambda a: jnp.transpose(a.astype(F32).reshape(2, n_g // 2, 2 * n_p), (1, 0, 2))
    lam_row = jnp.stack([pair(a_re), pair(a_im),
                         pair(jnp.broadcast_to(log_step[..., None], a_re.shape))], axis=2)
    lam_col = jnp.broadcast_to(lam_row[..., None], lam_row.shape + (2 * n_p,))
    bt = lambda b: jnp.transpose(b.astype(F32).reshape(2, n_g // 2, 2, n_p, n_c),
                                 (1, 0, 4, 2, 3)).reshape(n_g // 2, 2, n_c, 2 * n_p)
    b_t = jnp.stack([bt(b_re), bt(b_im)], axis=2)
    ct = lambda c: jnp.tile(jnp.transpose(c.astype(F32).reshape(2, n_g // 2, 2, n_c, n_p),
                                          (1, 0, 2, 4, 3)).reshape(n_g // 2, 2, 2 * n_p, n_c),
                            (1, 1, 1, n_s))
    c_t = jnp.stack([ct(c_re), ct(c_im)], axis=2)
    d_row = jnp.tile(d_skip.astype(F32).reshape(n_g // 2, 2, 1, n_c), (1, 1, 1, n_s))
    blk = lambda *s: pl.BlockSpec((None,) + s, lambda j: (j,) + (0,) * len(s))
    grp = lambda *s: pl.BlockSpec((2,) + s, lambda j: (j,) + (0,) * len(s))
    return pl.pallas_call(
        _s5_prep_kernel,
        grid=(n_g // 2,),
        in_specs=[blk(2, 3, 2 * n_p), blk(2, 3, 2 * n_p, 2 * n_p), blk(2, 2, n_c, 2 * n_p),
                  blk(2, 2, 2 * n_p, lanes), blk(2, 1, lanes)],
        out_specs=[grp(lanes, lanes), grp(lanes, 8 * n_p), grp(8 * n_p, lanes), blk(1, 8 * n_p)],
        out_shape=[jax.ShapeDtypeStruct((n_g, lanes, lanes), BF16),
                   jax.ShapeDtypeStruct((n_g, lanes, 8 * n_p), BF16),
                   jax.ShapeDtypeStruct((n_g, 8 * n_p, lanes), BF16),
                   jax.ShapeDtypeStruct((n_g // 2, 1, 8 * n_p), F32)],
        compiler_params=pltpu.CompilerParams(dimension_semantics=("parallel",),
                                             vmem_limit_bytes=VMEM_LIMIT),
        name="s5_prep",
    )(lam_row, lam_col, b_t, c_t, d_row)


def _merge_kernel(hf_ref, hb_ref, o_ref, zm_ref, y_ref, zs_ref, x_ref, mod_ref, mhg_ref,
                  gluw_ref, glub_ref, wout_ref, fg_ref, out_ref, y_sc):
    for g8 in range(W_BRANCH // 128):
        rows = [y_ref[g8 * 8 + gp].astype(F32) for gp in range(8)]
        for i, blk in enumerate(_block_transpose8(rows)):
            y_sc[i * GRID_W:(i + 1) * GRID_W, g8 * 128:(g8 + 1) * 128] = blk.astype(BF16)

    mhg = mhg_ref[...]
    half = x_ref.shape[0] // 2
    for rs in (slice(0, half), slice(half, 2 * half)):
        hm = ((hf_ref[rs, :] + hb_ref[rs, :]) * jax.nn.sigmoid(o_ref[rs, :])).astype(F32)
        parts = []
        for hd in range(HEADS):
            sl = slice(hd * HEAD_DIM, (hd + 1) * HEAD_DIM)
            seg = hm[:, sl]
            mu = jnp.mean(seg, axis=-1, keepdims=True)
            dev = seg - mu
            var = jnp.mean(dev * dev, axis=-1, keepdims=True)
            parts.append(dev * lax.rsqrt(var + NORM_EPS) * mhg[:, sl])
        m_out = jnp.concatenate(parts, axis=-1).astype(BF16) * _silu(zm_ref[rs, :])

        y = y_sc[rs, :]
        gl = 0.5 * y * (1.0 + jnp.tanh(0.7978845608028654 * (y + 0.044715 * (y * y * y))))
        gate = jax.nn.sigmoid((_dot(gl, gluw_ref[...]) + glub_ref[...]).astype(BF16))
        s_out = gl * gate * _silu(zs_ref[rs, :])

        mixed = _dot(m_out, wout_ref[0:W_BRANCH, :]) + _dot(s_out, wout_ref[W_BRANCH:2 * W_BRANCH, :])
        xo = x_ref[rs, :] + mod_ref[2:3, :] * mixed
        ms = jnp.mean(xo * xo, axis=-1, keepdims=True)
        out_ref[rs, :] = xo * lax.rsqrt(ms + NORM_EPS) * fg_ref[...]


def _merge(hf, hb, o, zm, y, zs, x, mod, mh_g, glu_w, glu_b, w_out, final_g):
    bsz, t_len, _ = x.shape
    tm = 8 * GRID_W
    tile = pl.BlockSpec((None, tm, D_MODEL), lambda b, i: (b, i, 0))
    y_spec = pl.BlockSpec((S5_GROUPS, GRID_W, 128), lambda b, i: (0, (i // 2) * bsz + b, i % 2))
    const = lambda shape: pl.BlockSpec(shape, lambda b, i: (0,) * len(shape),
                                       pipeline_mode=pl.Buffered(1))
    return pl.pallas_call(
        _merge_kernel,
        grid=(bsz, t_len // tm),
        in_specs=[tile] * 4 + [y_spec, tile, tile,
                               pl.BlockSpec((None, 3, D_MODEL), lambda b, i: (b, 0, 0)),
                               const((1, W_BRANCH)), const((W_BRANCH, W_BRANCH)),
                               const((1, W_BRANCH)), const((2 * W_BRANCH, D_MODEL)),
                               const((1, D_MODEL))],
        out_specs=tile,
        out_shape=jax.ShapeDtypeStruct((bsz, t_len, D_MODEL), F32),
        scratch_shapes=[pltpu.VMEM((tm, W_BRANCH), BF16)],
        compiler_params=pltpu.CompilerParams(dimension_semantics=("parallel", "arbitrary"),
                                             vmem_limit_bytes=VMEM_LIMIT),
        name="merge",
    )(hf, hb, o, zm, y, zs, x, mod, mh_g.reshape(1, -1), glu_w, glu_b.reshape(1, -1), w_out,
      final_g.reshape(1, -1))


def _s5_rows_ctx(u):
    bsz, t_len, _ = u.shape
    a = u.reshape(bsz, t_len // S5_SUB, S5_SUB, S5_GROUPS, S5_GC)
    a = jnp.transpose(a, (3, 1, 0, 2, 4))
    return a.reshape(S5_GROUPS, (t_len // S5_SUB) * bsz, S5_SUB * S5_GC)


def kernel(x, c, ctx, c_ctx, norm_g, ada_w, ada_b, w_in, b_gate, conv_qk, mh_g, s5_a_re, s5_a_im,
           s5_log_step, s5_b_re, s5_b_im, s5_c_re, s5_c_im, s5_d, glu_w, glu_b, w_out, final_g):
    bsz, t_len, _ = x.shape
    layer = 0

    cc = jnp.zeros((16, D_MODEL), F32).at[:bsz].set(c).at[bsz].set(c_ctx)
    mod = _ada(cc, ada_w[layer], ada_b[layer]).reshape(16, 3, D_MODEL)
    mod_x = mod[:bsz]
    mod_c = jnp.broadcast_to(mod[bsz][None], (bsz, 3, D_MODEL))

    w = w_in[layer]
    wb = W_BRANCH
    w_qk = w[:, 0:2 * wb].astype(BF16)
    gate0 = 5 * wb
    w_gate = jnp.pad(w[:, gate0:gate0 + N_GATES], ((0, 0), (0, GATE_PAD - N_GATES)))
    w_main = jnp.concatenate([w[:, 2 * wb:5 * wb], w[:, gate0 + N_GATES:], w_gate], axis=1).astype(BF16)
    b_gate_pad = jnp.pad(b_gate[layer].reshape(1, N_GATES), ((0, 0), (0, GATE_PAD - N_GATES)))

    proj = functools.partial(_inproj, norm_g=norm_g[layer], w_qk=w_qk, w_main=w_main,
                             conv_qk=conv_qk[layer], b_gate_pad=b_gate_pad)
    kt_c, v_c, u_c, gt_c, gx_c = proj(ctx, mod_c, tm=256, s5_layout=False, state_only=True)
    q_x, kt_x, v_x, o_x, zm_x, u_x, zs_x, gt_x, gx_x = proj(x, mod_x, tm=8 * GRID_W, s5_layout=True)

    ctx_state = _mlstm(None, kt_c, v_c, gt_c, gx_c, None)
    h_f, h_b = _mlstm(q_x, kt_x, v_x, gt_x, gx_x, ctx_state)

    m_all, g_all, p_all, a16 = _s5_prep(
        s5_a_re[layer], s5_a_im[layer], s5_log_step[layer], s5_b_re[layer], s5_b_im[layer],
        s5_c_re[layer], s5_c_im[layer], s5_d[layer])
    y_x = _s5(_s5_rows_ctx(u_c), u_x, g_all, m_all, p_all, a16, bsz)

    return _merge(h_f, h_b, o_x, zm_x, y_x, zs_x, x, mod_x, mh_g[layer], glu_w[layer].astype(BF16),
                  glu_b[layer], w_out[layer].astype(BF16), final_g)
```

```python
import functools

import jax
import jax.numpy as jnp
from jax import lax
from jax.experimental import pallas as pl
from jax.experimental.pallas import tpu as pltpu

F32 = jnp.float32
BF16 = jnp.bfloat16

D_MODEL = 1024
HEADS = 4
HEAD_DIM = 256
W_BRANCH = 1024
S5_GROUPS = 64
S5_GC = 16
S5_STATE = 64
S5_SUB = 16
GRID_W = 64
S5_ROW_PITCH = 72
N_GATES = 16
GATE_PAD = 128
CONV_W = 3
NORM_EPS = 1e-6
MLSTM_CHUNK = 256
MLSTM_CHUNKS_PER_STEP = (4, 2, 1)
MERGE_ROWS = 256
NEG_BIG = -1e30

VMEM_LIMIT = 56 * 1024 * 1024


def _silu(a):
    return a * jax.nn.sigmoid(a)


def _log_sigmoid(a):
    return jnp.minimum(a, 0.0) - jnp.log1p(jnp.exp(-jnp.abs(a)))


def _dot(a, b):
    return jnp.dot(a, b, preferred_element_type=F32)


def _split_bf16(a):
    hi = a.astype(BF16)
    lo = (a - hi.astype(F32)).astype(BF16)
    return hi, lo


def _block_transpose8(rows):
    lane = lax.broadcasted_iota(jnp.int32, rows[0].shape, 1)
    blk = lane // S5_GC
    for d in (4, 2, 1):
        keep = (blk & d) == 0
        new = list(rows)
        for i in range(8):
            if i & d == 0:
                a, b = rows[i], rows[i + d]
                new[i] = jnp.where(keep, a, pltpu.roll(b, d * S5_GC, axis=1))
                new[i + d] = jnp.where(keep, pltpu.roll(a, 128 - d * S5_GC, axis=1), b)
        rows = new
    return rows


def _chunk_gate_rows(gt_blk, reverse):
    chunk = gt_blk.shape[1]
    row = lax.broadcasted_iota(jnp.int32, (chunk, chunk), 0)
    col = lax.broadcasted_iota(jnp.int32, (chunk, chunk), 1)
    mask_st = (row >= col) if reverse else (row <= col)
    tri_st = jnp.where(mask_st, 1.0, 0.0).astype(BF16)
    hi, lo = _split_bf16(gt_blk)
    cum = _dot(hi, tri_st) + _dot(lo, tri_st)
    li0 = 8 if reverse else 0
    b_r = cum[li0 + HEADS:li0 + 2 * HEADS, :]
    run_max = gt_blk[li0:li0 + HEADS, :] - b_r
    lane = lax.broadcasted_iota(jnp.int32, run_max.shape, 1)
    shift = 1
    while shift < chunk:
        if reverse:
            moved = jnp.where(lane < chunk - shift, pltpu.roll(run_max, chunk - shift, axis=1), NEG_BIG)
        else:
            moved = jnp.where(lane >= shift, pltpu.roll(run_max, shift, axis=1), NEG_BIG)
        run_max = jnp.maximum(run_max, moved)
        shift *= 2
    return b_r, run_max


def _ada_kernel(c_ref, w_ref, b_ref, o_ref):
    s = _silu(c_ref[...])
    o_ref[...] = jnp.dot(s, w_ref[...], preferred_element_type=F32,
                         precision=lax.Precision.HIGHEST) + b_ref[...]


def _ada(cc, ada_w, ada_b):
    rows = cc.shape[0]
    n_out = ada_w.shape[1]
    tn = 1024
    return pl.pallas_call(
        _ada_kernel,
        grid=(n_out // tn,),
        in_specs=[pl.BlockSpec((rows, D_MODEL), lambda j: (0, 0)),
                  pl.BlockSpec((D_MODEL, tn), lambda j: (0, j)),
                  pl.BlockSpec((1, tn), lambda j: (0, j))],
        out_specs=pl.BlockSpec((rows, tn), lambda j: (0, j)),
        out_shape=jax.ShapeDtypeStruct((rows, n_out), F32),
        compiler_params=pltpu.CompilerParams(dimension_semantics=("arbitrary",),
                                             vmem_limit_bytes=VMEM_LIMIT),
        name="ada",
    )(cc, ada_w, ada_b.reshape(1, n_out))


HALO = 16


def _inproj_kernel(x_ref, xp_ref, xn_ref, mod_ref, ng_ref, wqk_ref, wm_ref, conv_ref, bg_ref,
                   *rest, tm, s5_layout, state_only):
    if state_only:
        kt_ref, v_ref, u_ref, gt_ref, gx_ref, lhs_sc = rest
    else:
        q_ref, kt_ref, v_ref, o_ref, zm_ref, u_ref, zs_ref, gt_ref, gx_ref, lhs_sc = rest
    i = pl.program_id(1)
    nt = pl.num_programs(1)
    shift = mod_ref[0:1, :]
    gain = ng_ref[...] * (1.0 + mod_ref[1:2, :])

    def norm_mod(a):
        ms = jnp.mean(a * a, axis=-1, keepdims=True)
        return a * lax.rsqrt(ms + NORM_EPS) * gain + shift

    lhs_sc[0:tm, :] = norm_mod(x_ref[...]).astype(BF16)
    lhs_sc[tm:tm + HALO, :] = norm_mod(xp_ref[...]).astype(BF16)
    lhs_sc[tm + HALO:tm + 2 * HALO, :] = norm_mod(xn_ref[...]).astype(BF16)

    h = lhs_sc[0:tm, :]
    zg = _dot(h, wm_ref[:, 5120:5120 + GATE_PAD]) + bg_ref[...]
    lane = lax.broadcasted_iota(jnp.int32, zg.shape, 1)
    is_forget = jnp.logical_and(lane < N_GATES, (lane % 8) >= 4)
    gates = jnp.where(is_forget, _log_sigmoid(zg), zg)
    gt = gates.T[0:N_GATES, :]
    gt_ref[...] = gt
    for c0 in range(0, tm, MLSTM_CHUNK):
        parts = []
        for reverse in (False, True):
            parts += list(_chunk_gate_rows(gt[:, c0:c0 + MLSTM_CHUNK], reverse))
        gx_ref[:, c0:c0 + MLSTM_CHUNK] = jnp.concatenate(parts, axis=0)

    u = _dot(h, wm_ref[:, 3072:4096])
    if s5_layout:
        for g8 in range(W_BRANCH // 128):
            rows = [u[i * GRID_W:(i + 1) * GRID_W, g8 * 128:(g8 + 1) * 128] for i in range(8)]
            for gp, blk in enumerate(_block_transpose8(rows)):
                u_ref[g8 * 8 + gp] = blk.astype(BF16)
    else:
        u_ref[...] = u.astype(BF16)

    valid_prev = (i > 0).astype(F32)
    valid_next = (i < nt - 1).astype(F32)

    def conv_stage(half):
        cols = slice(half * W_BRANCH, (half + 1) * W_BRANCH)
        z = _dot(lhs_sc[...], wqk_ref[:, cols])
        zc = z[0:tm]
        row8 = lax.broadcasted_iota(jnp.int32, (8, W_BRANCH), 0)
        before = pltpu.roll(zc, 1, axis=0)
        first = jnp.where(row8 == 0, z[tm + HALO - 1:tm + HALO] * valid_prev, before[0:8])
        before = jnp.concatenate([first, before[8:]], axis=0)
        after = pltpu.roll(zc, tm - 1, axis=0)
        final = jnp.where(row8 == 7, z[tm + HALO:tm + HALO + 1] * valid_next, after[tm - 8:tm])
        after = jnp.concatenate([after[:tm - 8], final], axis=0)
        cw = conv_ref[:, cols]
        conv = cw[0:1] * before + cw[1:2] * zc + cw[2:3] * after
        if half == 0:
            q_ref[...] = _silu(conv.astype(BF16))
        else:
            kt_ref[...] = (_silu(conv.astype(BF16)) * HEAD_DIM ** -0.5).T

    v_ref[...] = _dot(h, wm_ref[:, 0:1024]).astype(BF16)
    if not state_only:
        conv_stage(0)
        o_ref[...] = _dot(h, wm_ref[:, 1024:2048]).astype(BF16)
    conv_stage(1)
    if not state_only:
        zm_ref[...] = _dot(h, wm_ref[:, 2048:3072]).astype(BF16)
        zs_ref[...] = _dot(h, wm_ref[:, 4096:5120]).astype(BF16)


def _inproj(x, mod, norm_g, w_qk, w_main, conv_qk, b_gate_pad, tm, s5_layout, state_only=False):
    bsz, t_len, _ = x.shape
    nt = t_len // tm
    nhb = t_len // HALO
    tok = lambda w, dt: jax.ShapeDtypeStruct((bsz, t_len, w), dt)
    tile = lambda w: pl.BlockSpec((None, tm, w), lambda b, i: (b, i, 0))
    tile_t = lambda w: pl.BlockSpec((None, w, tm), lambda b, i: (b, 0, i))
    const = lambda shape: pl.BlockSpec(shape, lambda b, i: (0,) * len(shape),
                                       pipeline_mode=pl.Buffered(1))
    if s5_layout:
        assert tm == 8 * GRID_W and t_len % (2 * tm) == 0
        u_spec = pl.BlockSpec((S5_GROUPS, GRID_W, 128), lambda b, i: (0, (i // 2) * bsz + b, i % 2))
        u_shape = jax.ShapeDtypeStruct((S5_GROUPS, (t_len // (2 * tm)) * bsz * GRID_W, 256), BF16)
    else:
        u_spec, u_shape = tile(W_BRANCH), tok(W_BRANCH, BF16)
    kt_shape = jax.ShapeDtypeStruct((bsz, W_BRANCH, t_len), BF16)
    gt_shape = jax.ShapeDtypeStruct((bsz, N_GATES, t_len), F32)
    if state_only:
        out_specs = [tile_t(W_BRANCH), tile(W_BRANCH), u_spec, tile_t(N_GATES), tile_t(N_GATES)]
        out_shape = [kt_shape, tok(W_BRANCH, BF16), u_shape, gt_shape, gt_shape]
    else:
        out_specs = [tile(W_BRANCH), tile_t(W_BRANCH)] + [tile(W_BRANCH)] * 3 + [
            u_spec, tile(W_BRANCH), tile_t(N_GATES), tile_t(N_GATES)]
        out_shape = [tok(W_BRANCH, BF16), kt_shape] + [tok(W_BRANCH, BF16)] * 3 + [
            u_shape, tok(W_BRANCH, BF16), gt_shape, gt_shape]
    return pl.pallas_call(
        functools.partial(_inproj_kernel, tm=tm, s5_layout=s5_layout, state_only=state_only),
        grid=(bsz, nt),
        in_specs=[
            tile(D_MODEL),
            pl.BlockSpec((None, HALO, D_MODEL),
                         lambda b, i: (b, jnp.maximum(i * (tm // HALO) - 1, 0), 0)),
            pl.BlockSpec((None, HALO, D_MODEL),
                         lambda b, i: (b, jnp.minimum((i + 1) * (tm // HALO), nhb - 1), 0)),
            pl.BlockSpec((None, 3, D_MODEL), lambda b, i: (b, 0, 0)),
            const((1, D_MODEL)),
            const((D_MODEL, 2 * W_BRANCH)),
            const((D_MODEL, 5 * W_BRANCH + GATE_PAD)),
            const((CONV_W, 2 * W_BRANCH)),
            const((1, GATE_PAD)),
        ],
        out_specs=out_specs,
        out_shape=out_shape,
        scratch_shapes=[pltpu.VMEM((tm + 2 * HALO, D_MODEL), BF16)],
        compiler_params=pltpu.CompilerParams(dimension_semantics=("parallel", "arbitrary"),
                                             vmem_limit_bytes=VMEM_LIMIT),
        name="inproj",
    )(x, x, x, mod, norm_g.reshape(1, D_MODEL), w_qk, w_main, conv_qk, b_gate_pad)


def _mlstm_gates(gt_ref, gx_ref, m_sc, *, reverse, chunk, with_output):
    li0 = 8 if reverse else 0
    last = 0 if reverse else chunk - 1
    li_r = gt_ref[li0:li0 + HEADS, :]
    b_r = gx_ref[li0:li0 + HEADS, :]
    b_last = b_r[:, last:last + 1]
    m_old = m_sc[:, 0:1]
    g_r = b_last - b_r + li_r
    m_new = jnp.maximum(b_last + m_old, jnp.max(g_r, axis=1, keepdims=True))
    out = dict(m_old=m_old, m_new=m_new, decay=jnp.exp(b_last + m_old - m_new),
               k_scale=jnp.exp(g_r - m_new).astype(BF16))
    if not with_output:
        return out
    row = lax.broadcasted_iota(jnp.int32, (chunk, chunk), 0)
    col = lax.broadcasted_iota(jnp.int32, (chunk, chunk), 1)
    mask_ts = (col >= row) if reverse else (col <= row)
    mm_r = jnp.maximum(gx_ref[li0 + HEADS:li0 + 2 * HEADS, :], m_old).astype(BF16)
    b_hi, b_lo = _split_bf16(b_r)
    rows = jnp.concatenate([mm_r, b_hi, b_lo, jnp.zeros_like(b_hi)], axis=0)
    sel_r = lax.broadcasted_iota(jnp.int32, (4 * HEADS, 2 * HEADS * 128), 0)
    sel_c = lax.broadcasted_iota(jnp.int32, (4 * HEADS, 2 * HEADS * 128), 1) // 128
    head_r = sel_r % HEADS
    pick = jnp.logical_or(jnp.logical_and(sel_r < HEADS, sel_c % HEADS == head_r),
                          jnp.logical_and(jnp.logical_and(sel_r >= HEADS, sel_r < 3 * HEADS),
                                          sel_c == HEADS + head_r))
    sel = jnp.where(pick, 1.0, 0.0).astype(BF16)
    cols = lax.dot_general(rows, sel, (((0,), (0,)), ((), ())), preferred_element_type=F32)
    out.update(a_r=li_r - b_r, cols=cols, mask_ts=mask_ts)
    return out


def _mlstm_heads(q_ref, kt_ref, v_ref, h_ref, c_sc, n_sc, m_sc, gates, *, chunk):
    with_output = h_ref is not None
    m_old, m_new, decay, k_scale = gates["m_old"], gates["m_new"], gates["decay"], gates["k_scale"]
    if with_output:
        a_r, cols, mask_ts = gates["a_r"], gates["cols"], gates["mask_ts"]
    ones_rows = jnp.ones((8, chunk), BF16)
    wide = lambda a, n: jnp.concatenate([a] * (n // 128), axis=1)
    for hd in range(HEADS):
        sl = slice(hd * HEAD_DIM, (hd + 1) * HEAD_DIM)
        kt = kt_ref[sl, :]
        v = v_ref[:, sl]
        c_old = c_sc[hd]
        n_old = n_sc[hd]
        kw_t = kt * k_scale[hd:hd + 1, :]
        if with_output:
            q = q_ref[:, sl]
            mm_c = cols[:, hd * 128:(hd + 1) * 128]
            bm_c = cols[:, (HEADS + hd) * 128:(HEADS + hd + 1) * 128]
            decay_mat = jnp.exp(jnp.where(mask_ts, a_r[hd:hd + 1, :] - wide(mm_c, chunk), NEG_BIG))
            s_f = _dot(q, kt) * decay_mat
            w_inter = jnp.exp(m_old[hd:hd + 1, :] - mm_c)
            den = (w_inter * jnp.sum(q.astype(F32) * n_old[0:1, :], axis=1, keepdims=True)
                   + jnp.sum(s_f, axis=1, keepdims=True))
            q_w = q * wide(w_inter, HEAD_DIM).astype(BF16)
            num = _dot(jnp.concatenate([q_w, s_f.astype(BF16)], axis=1),
                       jnp.concatenate([c_old.astype(BF16), v], axis=0))
            inv = 1.0 / jnp.maximum(jnp.abs(den), jnp.exp(-bm_c))
            h_ref[:, sl] = (num * wide(inv, HEAD_DIM)).astype(BF16)
        c_sc[hd] = decay[hd:hd + 1, :] * c_old + _dot(kw_t, v)
        n_sc[hd] = decay[hd:hd + 1, :] * n_old + lax.dot_general(
            ones_rows, kw_t, (((1,), (1,)), ((), ())), preferred_element_type=F32)
        m_sc[hd:hd + 1, :] = jnp.broadcast_to(m_new[hd:hd + 1, :], (1, 128))


def _mlstm_kernel(*refs, with_output, chunk, subs):
    n_in = 5 if with_output else 4
    ins = [refs[0:n_in], refs[n_in:2 * n_in]]
    rest = refs[2 * n_in:]
    if with_output:
        c0_ref, n0_ref, m0_ref, hf_ref, hb_ref, c_sc, n_sc, m_sc = rest
        h_refs = (hf_ref, hb_ref)
    else:
        co_ref, no_ref, mo_ref, c_sc, n_sc, m_sc = rest
        h_refs = (None, None)
        ins = [(None,) + tuple(r) for r in ins]
    i = pl.program_id(1)
    nc = pl.num_programs(1)

    @pl.when(i == 0)
    def _():
        if with_output:
            c_sc[...] = c0_ref[...]
            n_sc[...] = n0_ref[...]
            m_sc[...] = m0_ref[...]
        else:
            c_sc[...] = jnp.zeros_like(c_sc)
            n_sc[...] = jnp.zeros_like(n_sc)
            m_sc[...] = jnp.zeros_like(m_sc)

    for sub in range(subs):
        pos = (sub, subs - 1 - sub)
        rows = [pl.ds(pos[d] * chunk, chunk) for d in range(2)]
        view = lambda ref, d: None if ref is None else ref.at[rows[d], :]
        view_t = lambda ref, d: ref.at[:, rows[d]]
        gates = [_mlstm_gates(view_t(ins[d][3], d), view_t(ins[d][4], d), m_sc.at[d], reverse=bool(d),
                              chunk=chunk, with_output=with_output) for d in range(2)]
        for d in range(2):
            _mlstm_heads(view(ins[d][0], d), view_t(ins[d][1], d), view(ins[d][2], d), view(h_refs[d], d),
                         c_sc.at[d], n_sc.at[d], m_sc.at[d], gates[d], chunk=chunk)

    if not with_output:
        @pl.when(i == nc - 1)
        def _():
            co_ref[...] = c_sc[...]
            no_ref[...] = n_sc[...]
            mo_ref[...] = m_sc[...]


def _mlstm(q, kt, v, gt, gx, state):
    with_output = q is not None
    bsz, t_len, _ = v.shape
    chunk = MLSTM_CHUNK
    subs = next(n for n in MLSTM_CHUNKS_PER_STEP if t_len % (n * chunk) == 0)
    blk = subs * chunk
    nc = t_len // blk
    cidx = (lambda i: i, lambda i: nc - 1 - i)
    tile = lambda w, d: pl.BlockSpec((None, blk, w), lambda b, i: (b, cidx[d](i), 0))
    tile_t = lambda w, d: pl.BlockSpec((None, w, blk), lambda b, i: (b, 0, cidx[d](i)))
    st_dims = [(2, HEADS, HEAD_DIM, HEAD_DIM), (2, HEADS, 8, HEAD_DIM), (2, HEADS, 128)]
    st_specs = [pl.BlockSpec((None,) + s, lambda b, i, n=len(s): (b,) + (0,) * n) for s in st_dims]
    st_shapes = [jax.ShapeDtypeStruct((bsz,) + s, F32) for s in st_dims]
    in_specs, args = [], []
    for d in range(2):
        in_specs += ([tile(W_BRANCH, d)] if with_output else []) + [
            tile_t(W_BRANCH, d), tile(W_BRANCH, d), tile_t(N_GATES, d), tile_t(N_GATES, d)]
        args += ([q] if with_output else []) + [kt, v, gt, gx]
    if with_output:
        in_specs, args = in_specs + st_specs, args + list(state)
        out_specs = [tile(W_BRANCH, 0), tile(W_BRANCH, 1)]
        out_shape = [jax.ShapeDtypeStruct((bsz, t_len, W_BRANCH), BF16)] * 2
    else:
        out_specs, out_shape = st_specs, st_shapes
    return pl.pallas_call(
        functools.partial(_mlstm_kernel, with_output=with_output, chunk=chunk, subs=subs),
        grid=(bsz, nc),
        in_specs=in_specs,
        out_specs=out_specs,
        out_shape=out_shape,
        scratch_shapes=[pltpu.VMEM(s, F32) for s in st_dims],
        compiler_params=pltpu.CompilerParams(dimension_semantics=("parallel", "arbitrary"),
                                             vmem_limit_bytes=VMEM_LIMIT),
        name="mlstm_out" if with_output else "mlstm_state",
    )(*args)


def _s5_kernel(vc_ref, vx_ref, g_ref, m_ref, p_ref, a_ref, y_ref, gu_sc, s_sc, *,
               nk_ctx, n_rc, bsz, rblk):
    rows_ctx = nk_ctx * bsz
    rows_x = n_rc * bsz * GRID_W

    def increments(v_ref, r0, r1):
        return _dot(v_ref[0, r0:r1, :], g_ref[0]) + _dot(v_ref[1, r0:r1, :], g_ref[1])

    inc = increments(vc_ref, 0, rows_ctx)
    for comp in range(4):
        gu_sc[comp, 0:rows_ctx, :] = inc[:, comp * 128:(comp + 1) * 128]
    for r0 in range(0, rows_x, rblk):
        inc = increments(vx_ref, r0, r0 + rblk)
        for run in range(rblk // GRID_W):
            dst = rows_ctx + (r0 // GRID_W + run) * S5_ROW_PITCH
            for comp in range(4):
                gu_sc[comp, dst:dst + GRID_W, :] = (
                    inc[run * GRID_W:(run + 1) * GRID_W, comp * 128:(comp + 1) * 128])

    a = a_ref[...]
    a_pow = [jnp.broadcast_to(a[:, comp * 128:(comp + 1) * 128], (bsz, 128)) for comp in range(4)]
    zero = jnp.zeros((bsz, 128), F32)

    def cmul(x_r, x_i, y_r, y_i):
        return x_r * y_r - x_i * y_i, x_r * y_i + x_i * y_r

    a_sq = [cmul(a_pow[2 * d], a_pow[2 * d + 1], a_pow[2 * d], a_pow[2 * d + 1]) for d in range(2)]

    def step(rows, carry, direction):
        s_r, s_i = carry
        inc_r = gu_sc[2 * direction, rows, :]
        inc_i = gu_sc[2 * direction + 1, rows, :]
        gu_sc[2 * direction, rows, :] = s_r
        gu_sc[2 * direction + 1, rows, :] = s_i
        p_r, p_i = cmul(a_pow[2 * direction], a_pow[2 * direction + 1], s_r, s_i)
        return p_r + inc_r, p_i + inc_i

    def step2(rows0, rows1, carry, direction):
        s_r, s_i = carry
        a_r, a_i = a_pow[2 * direction], a_pow[2 * direction + 1]
        inc0_r, inc0_i = gu_sc[2 * direction, rows0, :], gu_sc[2 * direction + 1, rows0, :]
        inc1_r, inc1_i = gu_sc[2 * direction, rows1, :], gu_sc[2 * direction + 1, rows1, :]
        gu_sc[2 * direction, rows0, :] = s_r
        gu_sc[2 * direction + 1, rows0, :] = s_i
        m_r, m_i = cmul(a_r, a_i, s_r, s_i)
        gu_sc[2 * direction, rows1, :] = m_r + inc0_r
        gu_sc[2 * direction + 1, rows1, :] = m_i + inc0_i
        c_r, c_i = cmul(a_r, a_i, inc0_r, inc0_i)
        q_r, q_i = cmul(a_sq[direction][0], a_sq[direction][1], s_r, s_i)
        return q_r + (c_r + inc1_r), q_i + (c_i + inc1_i)

    def ctx_rows(k):
        return pl.ds(pl.multiple_of(k * bsz, bsz), bsz)

    def x_rows(w, rc):
        return pl.ds(rows_ctx + rc * (bsz * S5_ROW_PITCH) + w, bsz, stride=S5_ROW_PITCH)

    def ctx_body(k, carry):
        return step(ctx_rows(k), carry[0], 0), step(ctx_rows(nk_ctx - 1 - k), carry[1], 1)

    def x_body(w, carry):
        c_f, c_b = carry
        w_b = GRID_W - 1 - w
        for rc in range(0, n_rc - 1, 2):
            c_f = step2(x_rows(w, rc), x_rows(w, rc + 1), c_f, 0)
            c_b = step2(x_rows(w_b, n_rc - 1 - rc), x_rows(w_b, n_rc - 2 - rc), c_b, 1)
        if n_rc % 2:
            c_f = step(x_rows(w, n_rc - 1), c_f, 0)
            c_b = step(x_rows(w_b, 0), c_b, 1)
        return c_f, c_b

    carry = lax.fori_loop(0, nk_ctx, ctx_body, ((zero, zero), (zero, zero)))
    lax.fori_loop(0, GRID_W, x_body, carry)

    for r0 in range(0, rows_x, rblk):
        r1 = r0 + rblk
        for run in range(rblk // GRID_W):
            src = rows_ctx + (r0 // GRID_W + run) * S5_ROW_PITCH
            for comp in range(4):
                s_sc[r0 + run * GRID_W:r0 + (run + 1) * GRID_W, comp * 128:(comp + 1) * 128] = (
                    gu_sc[comp, src:src + GRID_W, :].astype(BF16))
        for gg in range(2):
            y_ref[gg, r0:r1, :] = (_dot(vx_ref[gg, r0:r1, :], m_ref[gg])
                                   + _dot(s_sc[r0:r1, :], p_ref[gg])).astype(BF16)


def _s5(v_ctx, v_x, g_all, m_all, p_all, a16, bsz):
    rows_ctx, rows_x = v_ctx.shape[1], v_x.shape[1]
    lanes = S5_SUB * S5_GC
    return pl.pallas_call(
        functools.partial(_s5_kernel, nk_ctx=rows_ctx // bsz, n_rc=rows_x // (bsz * GRID_W),
                          bsz=bsz, rblk=512),
        grid=(S5_GROUPS // 2,),
        in_specs=[pl.BlockSpec((2, rows_ctx, lanes), lambda j: (j, 0, 0)),
                  pl.BlockSpec((2, rows_x, lanes), lambda j: (j, 0, 0)),
                  pl.BlockSpec((2, lanes, 512), lambda j: (j, 0, 0)),
                  pl.BlockSpec((2, lanes, lanes), lambda j: (j, 0, 0)),
                  pl.BlockSpec((2, 512, lanes), lambda j: (j, 0, 0)),
                  pl.BlockSpec((None, 1, 512), lambda j: (j, 0, 0))],
        out_specs=pl.BlockSpec((2, rows_x, lanes), lambda j: (j, 0, 0)),
        out_shape=jax.ShapeDtypeStruct((S5_GROUPS, rows_x, lanes), BF16),
        scratch_shapes=[pltpu.VMEM((4, rows_ctx + (rows_x // GRID_W) * S5_ROW_PITCH, 128), F32),
                        pltpu.VMEM((rows_x, 512), BF16)],
        compiler_params=pltpu.CompilerParams(dimension_semantics=("parallel",),
                                             vmem_limit_bytes=VMEM_LIMIT),
        name="s5",
    )(v_ctx, v_x, g_all, m_all, p_all, a16)


def _s5_prep_kernel(lr_ref, lc_ref, bt_ref, ct_ref, d_ref, m_ref, g_ref, p_ref, a_ref):
    hp = lax.Precision.HIGHEST
    n_s, lanes = S5_SUB, S5_SUB * S5_GC
    lane128 = lax.broadcasted_iota(jnp.int32, (n_s, 128), 1)
    blk_of_lane = lax.broadcasted_iota(jnp.int32, (128, lanes), 1) // S5_GC
    g_types, p_types, a16, k_rows = [], [], [], []
    for d in range(2):
        a_r, a_i, log_dt = lr_ref[d, 0:1, :], lr_ref[d, 1:2, :], lr_ref[d, 2:3, :]
        dt = jnp.exp(log_dt)
        lam_r, lam_i = a_r * dt, a_i * dt
        steps = lax.broadcasted_iota(jnp.int32, (24, 128), 0).astype(F32)
        mag = jnp.exp(lam_r * steps)
        pw_r, pw_i = mag * jnp.cos(lam_i * steps), mag * jnp.sin(lam_i * steps)
        nr, ni = pw_r[1:2] - 1.0, pw_i[1:2]
        den = a_r * a_r + a_i * a_i
        co_r, co_i = (nr * a_r + ni * a_i) / den, (ni * a_r - nr * a_i) / den
        b_r = jnp.concatenate([bt_ref[d, 0]] * n_s, axis=0)
        b_i = jnp.concatenate([bt_ref[d, 1]] * n_s, axis=0)
        bb_r, bb_i = co_r * b_r - co_i * b_i, co_r * b_i + co_i * b_r
        order = [n_s - 1 - i for i in range(n_s)] if d == 0 else list(range(n_s))
        pg_r = jnp.concatenate([jnp.broadcast_to(pw_r[n:n + 1], (S5_GC, 128)) for n in order], axis=0)
        pg_i = jnp.concatenate([jnp.broadcast_to(pw_i[n:n + 1], (S5_GC, 128)) for n in order], axis=0)
        g_types += [bb_r * pg_r - bb_i * pg_i, bb_r * pg_i + bb_i * pg_r]
        a16 += [pw_r[n_s:n_s + 1], pw_i[n_s:n_s + 1]]
        x0 = slice((n_s - 1) * S5_GC, n_s * S5_GC) if d == 0 else slice(0, S5_GC)
        x_r, x_i = bb_r[x0], bb_i[x0]
        lhs_r = jnp.concatenate([jnp.where(lane128 < S5_STATE, x_r, 0.0),
                                 jnp.where(lane128 < S5_STATE, 0.0, x_r)], axis=0)
        lhs_i = jnp.concatenate([jnp.where(lane128 < S5_STATE, x_i, 0.0),
                                 jnp.where(lane128 < S5_STATE, 0.0, x_i)], axis=0)
        dt_c = jnp.exp(lc_ref[d, 2])
        lam_rc, lam_ic = lc_ref[d, 0] * dt_c, lc_ref[d, 1] * dt_c
        mag_1 = jnp.exp(lam_rc)
        a1_r, a1_i = mag_1 * jnp.cos(lam_ic), mag_1 * jnp.sin(lam_ic)
        a1_r = jnp.concatenate([a1_r, a1_r], axis=1)
        a1_i = jnp.concatenate([a1_i, a1_i], axis=1)
        n_y = blk_of_lane if d == 0 else n_s - 1 - blk_of_lane
        ypw_r, ypw_i = jnp.ones_like(a1_r), jnp.zeros_like(a1_r)
        sq_r, sq_i = a1_r, a1_i
        for bit in (1, 2, 4, 8):
            on = (n_y & bit) != 0
            ypw_r, ypw_i = (jnp.where(on, ypw_r * sq_r - ypw_i * sq_i, ypw_r),
                            jnp.where(on, ypw_r * sq_i + ypw_i * sq_r, ypw_i))
            if bit < 8:
                sq_r, sq_i = sq_r * sq_r - sq_i * sq_i, 2.0 * sq_r * sq_i
        c_r, c_i = ct_ref[d, 0], ct_ref[d, 1]
        y_r, y_i = c_r * ypw_r - c_i * ypw_i, c_r * ypw_i + c_i * ypw_r
        k_rows.append(jnp.dot(lhs_r, y_r, preferred_element_type=F32, precision=hp)
                      - jnp.dot(lhs_i, y_i, preferred_element_type=F32, precision=hp))
        p_types += [y_r * a1_r - y_i * a1_i, -(y_r * a1_i + y_i * a1_r)]

    a_ref[...] = jnp.concatenate(a16, axis=1)
    lane_g = lax.broadcasted_iota(jnp.int32, (lanes, 128), 1)
    row_p = lax.broadcasted_iota(jnp.int32, (128, lanes), 0)
    lane_k = lax.broadcasted_iota(jnp.int32, (S5_GC, lanes), 1)
    row_m = lax.broadcasted_iota(jnp.int32, (lanes, lanes), 0)
    lane_m = lax.broadcasted_iota(jnp.int32, (lanes, lanes), 1)
    for h in range(2):
        mine_l = (lane_g >= S5_STATE) == bool(h)
        g_ref[h] = jnp.concatenate([jnp.where(mine_l, t, 0.0) for t in g_types], axis=1).astype(BF16)
        mine_r = (row_p >= S5_STATE) == bool(h)
        p_ref[h] = jnp.concatenate([jnp.where(mine_r, t, 0.0) for t in p_types], axis=0).astype(BF16)
        k_f = k_rows[0][h * S5_GC:(h + 1) * S5_GC]
        k_b = k_rows[1][h * S5_GC:(h + 1) * S5_GC]
        blocks = []
        for i in range(n_s):
            up, down = S5_GC * i, S5_GC * (n_s - 1 - i)
            f = k_f if up == 0 else jnp.where(lane_k >= up, pltpu.roll(k_f, up, axis=1), 0.0)
            b = k_b if down == 0 else jnp.where(lane_k < lanes - down,
                                                pltpu.roll(k_b, lanes - down, axis=1), 0.0)
            blocks.append(f + b)
        m = jnp.concatenate(blocks, axis=0) + jnp.where(row_m == lane_m, d_ref[h], 0.0)
        m_ref[h] = m.astype(BF16)


def _s5_prep(a_re, a_im, log_step, b_re, b_im, c_re, c_im, d_skip):
    n_g, n_p, n_c, n_s = S5_GROUPS, S5_STATE, S5_GC, S5_SUB
    lanes = n_s * n_c
    pair = lambda a: jnp.transpose(a.astype(F32).reshape(2, n_g // 2, 2 * n_p), (1, 0, 2))
    lam_row = jnp.stack([pair(a_re), pair(a_im),
                         pair(jnp.broadcast_to(log_step[..., None], a_re.shape))], axis=2)
    lam_col = jnp.broadcast_to(lam_row[..., None], lam_row.shape + (2 * n_p,))
    bt = lambda b: jnp.transpose(b.astype(F32).reshape(2, n_g // 2, 2, n_p, n_c),
                                 (1, 0, 4, 2, 3)).reshape(n_g // 2, 2, n_c, 2 * n_p)
    b_t = jnp.stack([bt(b_re), bt(b_im)], axis=2)
    ct = lambda c: jnp.tile(jnp.transpose(c.astype(F32).reshape(2, n_g // 2, 2, n_c, n_p),
                                          (1, 0, 2, 4, 3)).reshape(n_g // 2, 2, 2 * n_p, n_c),
                            (1, 1, 1, n_s))
    c_t = jnp.stack([ct(c_re), ct(c_im)], axis=2)
    d_row = jnp.tile(d_skip.astype(F32).reshape(n_g // 2, 2, 1, n_c), (1, 1, 1, n_s))
    blk = lambda *s: pl.BlockSpec((None,) + s, lambda j: (j,) + (0,) * len(s))
    grp = lambda *s: pl.BlockSpec((2,) + s, lambda j: (j,) + (0,) * len(s))
    return pl.pallas_call(
        _s5_prep_kernel,
        grid=(n_g // 2,),
        in_specs=[blk(2, 3, 2 * n_p), blk(2, 3, 2 * n_p, 2 * n_p), blk(2, 2, n_c, 2 * n_p),
                  blk(2, 2, 2 * n_p, lanes), blk(2, 1, lanes)],
        out_specs=[grp(lanes, lanes), grp(lanes, 8 * n_p), grp(8 * n_p, lanes), blk(1, 8 * n_p)],
        out_shape=[jax.ShapeDtypeStruct((n_g, lanes, lanes), BF16),
                   jax.ShapeDtypeStruct((n_g, lanes, 8 * n_p), BF16),
                   jax.ShapeDtypeStruct((n_g, 8 * n_p, lanes), BF16),
                   jax.ShapeDtypeStruct((n_g // 2, 1, 8 * n_p), F32)],
        compiler_params=pltpu.CompilerParams(dimension_semantics=("parallel",),
                                             vmem_limit_bytes=VMEM_LIMIT),
        name="s5_prep",
    )(lam_row, lam_col, b_t, c_t, d_row)


def _merge_kernel(hf_ref, hb_ref, o_ref, zm_ref, y_ref, zs_ref, x_ref, mod_ref, mhg_ref,
                  gluw_ref, glub_ref, wout_ref, fg_ref, out_ref, y_sc):
    for lh in range(2):
        for g8 in range(W_BRANCH // 128):
            rows = [y_ref[g8 * 8 + gp, :, lh * 128:(lh + 1) * 128].astype(F32) for gp in range(8)]
            for i, blk in enumerate(_block_transpose8(rows)):
                r0 = (lh * 8 + i) * GRID_W
                y_sc[r0:r0 + GRID_W, g8 * 128:(g8 + 1) * 128] = blk.astype(BF16)

    mhg = mhg_ref[...]
    for r0 in range(0, x_ref.shape[0], MERGE_ROWS):
        rs = slice(r0, r0 + MERGE_ROWS)
        hm = ((hf_ref[rs, :] + hb_ref[rs, :]) * jax.nn.sigmoid(o_ref[rs, :])).astype(F32)
        parts = []
        for hd in range(HEADS):
            sl = slice(hd * HEAD_DIM, (hd + 1) * HEAD_DIM)
            seg = hm[:, sl]
            mu = jnp.mean(seg, axis=-1, keepdims=True)
            dev = seg - mu
            var = jnp.mean(dev * dev, axis=-1, keepdims=True)
            parts.append(dev * lax.rsqrt(var + NORM_EPS) * mhg[:, sl])
        m_out = jnp.concatenate(parts, axis=-1).astype(BF16) * _silu(zm_ref[rs, :])

        y = y_sc[rs, :]
        gl = 0.5 * y * (1.0 + jnp.tanh(0.7978845608028654 * (y + 0.044715 * (y * y * y))))
        gate = jax.nn.sigmoid((_dot(gl, gluw_ref[...]) + glub_ref[...]).astype(BF16))
        s_out = gl * gate * _silu(zs_ref[rs, :])

        mixed = _dot(m_out, wout_ref[0:W_BRANCH, :]) + _dot(s_out, wout_ref[W_BRANCH:2 * W_BRANCH, :])
        xo = x_ref[rs, :] + mod_ref[2:3, :] * mixed
        ms = jnp.mean(xo * xo, axis=-1, keepdims=True)
        out_ref[rs, :] = xo * lax.rsqrt(ms + NORM_EPS) * fg_ref[...]


def _merge(hf, hb, o, zm, y, zs, x, mod, mh_g, glu_w, glu_b, w_out, final_g):
    bsz, t_len, _ = x.shape
    tm = S5_SUB * GRID_W
    tile = pl.BlockSpec((None, tm, D_MODEL), lambda b, i: (b, i, 0))
    y_spec = pl.BlockSpec((S5_GROUPS, GRID_W, S5_SUB * S5_GC), lambda b, i: (0, i * bsz + b, 0))
    const = lambda shape: pl.BlockSpec(shape, lambda b, i: (0,) * len(shape),
                                       pipeline_mode=pl.Buffered(1))
    return pl.pallas_call(
        _merge_kernel,
        grid=(bsz, t_len // tm),
        in_specs=[tile] * 4 + [y_spec, tile, tile,
                               pl.BlockSpec((None, 3, D_MODEL), lambda b, i: (b, 0, 0)),
                               const((1, W_BRANCH)), const((W_BRANCH, W_BRANCH)),
                               const((1, W_BRANCH)), const((2 * W_BRANCH, D_MODEL)),
                               const((1, D_MODEL))],
        out_specs=tile,
        out_shape=jax.ShapeDtypeStruct((bsz, t_len, D_MODEL), F32),
        scratch_shapes=[pltpu.VMEM((tm, W_BRANCH), BF16)],
        compiler_params=pltpu.CompilerParams(dimension_semantics=("parallel", "arbitrary"),
                                             vmem_limit_bytes=VMEM_LIMIT),
        name="merge",
    )(hf, hb, o, zm, y, zs, x, mod, mh_g.reshape(1, -1), glu_w, glu_b.reshape(1, -1), w_out,
      final_g.reshape(1, -1))


def _s5_rows_ctx(u):
    bsz, t_len, _ = u.shape
    a = u.reshape(bsz, t_len // S5_SUB, S5_SUB, S5_GROUPS, S5_GC)
    a = jnp.transpose(a, (3, 1, 0, 2, 4))
    return a.reshape(S5_GROUPS, (t_len // S5_SUB) * bsz, S5_SUB * S5_GC)


def kernel(x, c, ctx, c_ctx, norm_g, ada_w, ada_b, w_in, b_gate, conv_qk, mh_g, s5_a_re, s5_a_im,
           s5_log_step, s5_b_re, s5_b_im, s5_c_re, s5_c_im, s5_d, glu_w, glu_b, w_out, final_g):
    bsz, t_len, _ = x.shape
    layer = 0

    cc = jnp.zeros((16, D_MODEL), F32).at[:bsz].set(c).at[bsz].set(c_ctx)
    mod = _ada(cc, ada_w[layer], ada_b[layer]).reshape(16, 3, D_MODEL)
    mod_x = mod[:bsz]
    mod_c = jnp.broadcast_to(mod[bsz][None], (bsz, 3, D_MODEL))

    w = w_in[layer]
    wb = W_BRANCH
    w_qk = w[:, 0:2 * wb].astype(BF16)
    gate0 = 5 * wb
    w_gate = jnp.pad(w[:, gate0:gate0 + N_GATES], ((0, 0), (0, GATE_PAD - N_GATES)))
    w_main = jnp.concatenate([w[:, 2 * wb:5 * wb], w[:, gate0 + N_GATES:], w_gate], axis=1).astype(BF16)
    b_gate_pad = jnp.pad(b_gate[layer].reshape(1, N_GATES), ((0, 0), (0, GATE_PAD - N_GATES)))

    proj = functools.partial(_inproj, norm_g=norm_g[layer], w_qk=w_qk, w_main=w_main,
                             conv_qk=conv_qk[layer], b_gate_pad=b_gate_pad)
    kt_c, v_c, u_c, gt_c, gx_c = proj(ctx, mod_c, tm=256, s5_layout=False, state_only=True)
    q_x, kt_x, v_x, o_x, zm_x, u_x, zs_x, gt_x, gx_x = proj(x, mod_x, tm=8 * GRID_W, s5_layout=True)

    ctx_state = _mlstm(None, kt_c, v_c, gt_c, gx_c, None)
    h_f, h_b = _mlstm(q_x, kt_x, v_x, gt_x, gx_x, ctx_state)

    m_all, g_all, p_all, a16 = _s5_prep(
        s5_a_re[layer], s5_a_im[layer], s5_log_step[layer], s5_b_re[layer], s5_b_im[layer],
        s5_c_re[layer], s5_c_im[layer], s5_d[layer])
    y_x = _s5(_s5_rows_ctx(u_c), u_x, g_all, m_all, p_all, a16, bsz)

    return _merge(h_f, h_b, o_x, zm_x, y_x, zs_x, x, mod_x, mh_g[layer], glu_w[layer].astype(BF16),
                  glu_b[layer], w_out[layer].astype(BF16), final_g)
```

```python
import functools

import jax
import jax.numpy as jnp
from jax import lax
from jax.experimental import pallas as pl
from jax.experimental.pallas import tpu as pltpu

F32 = jnp.float32
BF16 = jnp.bfloat16

D_MODEL = 1024
HEADS = 4
HEAD_DIM = 256
W_BRANCH = 1024
S5_GROUPS = 64
S5_GC = 16
S5_STATE = 64
S5_SUB = 16
GRID_W = 64
S5_ROW_PITCH = 72
N_GATES = 16
GATE_PAD = 128
CONV_W = 3
NORM_EPS = 1e-6
MLSTM_CHUNK = 256
MLSTM_CHUNKS_PER_STEP = (4, 2, 1)
MERGE_ROWS = 256
NEG_BIG = -1e30

VMEM_LIMIT = 56 * 1024 * 1024


def _silu(a):
    return a * jax.nn.sigmoid(a)


def _log_sigmoid(a):
    return jnp.minimum(a, 0.0) - jnp.log1p(jnp.exp(-jnp.abs(a)))


def _dot(a, b):
    return jnp.dot(a, b, preferred_element_type=F32)


def _split_bf16(a):
    hi = a.astype(BF16)
    lo = (a - hi.astype(F32)).astype(BF16)
    return hi, lo


def _block_transpose8(rows):
    lane = lax.broadcasted_iota(jnp.int32, rows[0].shape, 1)
    blk = lane // S5_GC
    for d in (4, 2, 1):
        keep = (blk & d) == 0
        new = list(rows)
        for i in range(8):
            if i & d == 0:
                a, b = rows[i], rows[i + d]
                new[i] = jnp.where(keep, a, pltpu.roll(b, d * S5_GC, axis=1))
                new[i + d] = jnp.where(keep, pltpu.roll(a, 128 - d * S5_GC, axis=1), b)
        rows = new
    return rows


def _chunk_gate_rows(gt_blk, reverse):
    chunk = gt_blk.shape[1]
    row = lax.broadcasted_iota(jnp.int32, (chunk, chunk), 0)
    col = lax.broadcasted_iota(jnp.int32, (chunk, chunk), 1)
    mask_st = (row >= col) if reverse else (row <= col)
    tri_st = jnp.where(mask_st, 1.0, 0.0).astype(BF16)
    hi, lo = _split_bf16(gt_blk)
    cum = _dot(hi, tri_st) + _dot(lo, tri_st)
    li0 = 8 if reverse else 0
    b_r = cum[li0 + HEADS:li0 + 2 * HEADS, :]
    run_max = gt_blk[li0:li0 + HEADS, :] - b_r
    lane = lax.broadcasted_iota(jnp.int32, run_max.shape, 1)
    shift = 1
    while shift < chunk:
        if reverse:
            moved = jnp.where(lane < chunk - shift, pltpu.roll(run_max, chunk - shift, axis=1), NEG_BIG)
        else:
            moved = jnp.where(lane >= shift, pltpu.roll(run_max, shift, axis=1), NEG_BIG)
        run_max = jnp.maximum(run_max, moved)
        shift *= 2
    return b_r, run_max


def _ada_kernel(c_ref, w_ref, b_ref, o_ref):
    s = _silu(c_ref[...])
    o_ref[...] = jnp.dot(s, w_ref[...], preferred_element_type=F32,
                         precision=lax.Precision.HIGHEST) + b_ref[...]


def _ada(cc, ada_w, ada_b):
    rows = cc.shape[0]
    n_out = ada_w.shape[1]
    tn = 1024
    return pl.pallas_call(
        _ada_kernel,
        grid=(n_out // tn,),
        in_specs=[pl.BlockSpec((rows, D_MODEL), lambda j: (0, 0)),
                  pl.BlockSpec((D_MODEL, tn), lambda j: (0, j)),
                  pl.BlockSpec((1, tn), lambda j: (0, j))],
        out_specs=pl.BlockSpec((rows, tn), lambda j: (0, j)),
        out_shape=jax.ShapeDtypeStruct((rows, n_out), F32),
        compiler_params=pltpu.CompilerParams(dimension_semantics=("arbitrary",),
                                             vmem_limit_bytes=VMEM_LIMIT),
        name="ada",
    )(cc, ada_w, ada_b.reshape(1, n_out))


HALO = 16


def _inproj_kernel(x_ref, xp_ref, xn_ref, mod_ref, ng_ref, wqk_ref, wm_ref, conv_ref, bg_ref,
                   *rest, tm, s5_layout, state_only):
    if state_only:
        kt_ref, v_ref, u_ref, gt_ref, gx_ref, lhs_sc = rest
    else:
        q_ref, kt_ref, v_ref, o_ref, zm_ref, u_ref, zs_ref, gt_ref, gx_ref, lhs_sc = rest
    i = pl.program_id(1)
    nt = pl.num_programs(1)
    shift = mod_ref[0:1, :]
    gain = ng_ref[...] * (1.0 + mod_ref[1:2, :])

    def norm_mod(a):
        ms = jnp.mean(a * a, axis=-1, keepdims=True)
        return a * lax.rsqrt(ms + NORM_EPS) * gain + shift

    lhs_sc[0:tm, :] = norm_mod(x_ref[...]).astype(BF16)
    lhs_sc[tm:tm + HALO, :] = norm_mod(xp_ref[...]).astype(BF16)
    lhs_sc[tm + HALO:tm + 2 * HALO, :] = norm_mod(xn_ref[...]).astype(BF16)

    h = lhs_sc[0:tm, :]
    zg = _dot(h, wm_ref[:, 5120:5120 + GATE_PAD]) + bg_ref[...]
    lane = lax.broadcasted_iota(jnp.int32, zg.shape, 1)
    is_forget = jnp.logical_and(lane < N_GATES, (lane % 8) >= 4)
    gates = jnp.where(is_forget, _log_sigmoid(zg), zg)
    gt = gates.T[0:N_GATES, :]
    gt_ref[...] = gt
    for c0 in range(0, tm, MLSTM_CHUNK):
        parts = []
        for reverse in (False, True):
            parts += list(_chunk_gate_rows(gt[:, c0:c0 + MLSTM_CHUNK], reverse))
        gx_ref[:, c0:c0 + MLSTM_CHUNK] = jnp.concatenate(parts, axis=0)

    u = _dot(h, wm_ref[:, 3072:4096])
    if s5_layout:
        for g8 in range(W_BRANCH // 128):
            rows = [u[i * GRID_W:(i + 1) * GRID_W, g8 * 128:(g8 + 1) * 128] for i in range(8)]
            for gp, blk in enumerate(_block_transpose8(rows)):
                u_ref[g8 * 8 + gp] = blk.astype(BF16)
    else:
        u_ref[...] = u.astype(BF16)

    valid_prev = (i > 0).astype(F32)
    valid_next = (i < nt - 1).astype(F32)

    def conv_stage(half):
        cols = slice(half * W_BRANCH, (half + 1) * W_BRANCH)
        z = _dot(lhs_sc[...], wqk_ref[:, cols])
        zc = z[0:tm]
        row8 = lax.broadcasted_iota(jnp.int32, (8, W_BRANCH), 0)
        before = pltpu.roll(zc, 1, axis=0)
        first = jnp.where(row8 == 0, z[tm + HALO - 1:tm + HALO] * valid_prev, before[0:8])
        before = jnp.concatenate([first, before[8:]], axis=0)
        after = pltpu.roll(zc, tm - 1, axis=0)
        final = jnp.where(row8 == 7, z[tm + HALO:tm + HALO + 1] * valid_next, after[tm - 8:tm])
        after = jnp.concatenate([after[:tm - 8], final], axis=0)
        cw = conv_ref[:, cols]
        conv = cw[0:1] * before + cw[1:2] * zc + cw[2:3] * after
        if half == 0:
            q_ref[...] = _silu(conv.astype(BF16))
        else:
            kt_ref[...] = (_silu(conv.astype(BF16)) * HEAD_DIM ** -0.5).T

    v_ref[...] = _dot(h, wm_ref[:, 0:1024]).astype(BF16)
    if not state_only:
        conv_stage(0)
        o_ref[...] = _dot(h, wm_ref[:, 1024:2048]).astype(BF16)
    conv_stage(1)
    if not state_only:
        zm_ref[...] = _dot(h, wm_ref[:, 2048:3072]).astype(BF16)
        zs_ref[...] = _dot(h, wm_ref[:, 4096:5120]).astype(BF16)


def _inproj(x, mod, norm_g, w_qk, w_main, conv_qk, b_gate_pad, tm, s5_layout, state_only=False):
    bsz, t_len, _ = x.shape
    nt = t_len // tm
    nhb = t_len // HALO
    tok = lambda w, dt: jax.ShapeDtypeStruct((bsz, t_len, w), dt)
    tile = lambda w: pl.BlockSpec((None, tm, w), lambda b, i: (b, i, 0))
    tile_t = lambda w: pl.BlockSpec((None, w, tm), lambda b, i: (b, 0, i))
    const = lambda shape: pl.BlockSpec(shape, lambda b, i: (0,) * len(shape),
                                       pipeline_mode=pl.Buffered(1))
    if s5_layout:
        assert tm == 8 * GRID_W and t_len % (2 * tm) == 0
        u_spec = pl.BlockSpec((S5_GROUPS, GRID_W, 128), lambda b, i: (0, (i // 2) * bsz + b, i % 2))
        u_shape = jax.ShapeDtypeStruct((S5_GROUPS, (t_len // (2 * tm)) * bsz * GRID_W, 256), BF16)
    else:
        u_spec, u_shape = tile(W_BRANCH), tok(W_BRANCH, BF16)
    kt_shape = jax.ShapeDtypeStruct((bsz, W_BRANCH, t_len), BF16)
    gt_shape = jax.ShapeDtypeStruct((bsz, N_GATES, t_len), F32)
    if state_only:
        out_specs = [tile_t(W_BRANCH), tile(W_BRANCH), u_spec, tile_t(N_GATES), tile_t(N_GATES)]
        out_shape = [kt_shape, tok(W_BRANCH, BF16), u_shape, gt_shape, gt_shape]
    else:
        out_specs = [tile(W_BRANCH), tile_t(W_BRANCH)] + [tile(W_BRANCH)] * 3 + [
            u_spec, tile(W_BRANCH), tile_t(N_GATES), tile_t(N_GATES)]
        out_shape = [tok(W_BRANCH, BF16), kt_shape] + [tok(W_BRANCH, BF16)] * 3 + [
            u_shape, tok(W_BRANCH, BF16), gt_shape, gt_shape]
    return pl.pallas_call(
        functools.partial(_inproj_kernel, tm=tm, s5_layout=s5_layout, state_only=state_only),
        grid=(bsz, nt),
        in_specs=[
            tile(D_MODEL),
            pl.BlockSpec((None, HALO, D_MODEL),
                         lambda b, i: (b, jnp.maximum(i * (tm // HALO) - 1, 0), 0)),
            pl.BlockSpec((None, HALO, D_MODEL),
                         lambda b, i: (b, jnp.minimum((i + 1) * (tm // HALO), nhb - 1), 0)),
            pl.BlockSpec((None, 3, D_MODEL), lambda b, i: (b, 0, 0)),
            const((1, D_MODEL)),
            const((D_MODEL, 2 * W_BRANCH)),
            const((D_MODEL, 5 * W_BRANCH + GATE_PAD)),
            const((CONV_W, 2 * W_BRANCH)),
            const((1, GATE_PAD)),
        ],
        out_specs=out_specs,
        out_shape=out_shape,
        scratch_shapes=[pltpu.VMEM((tm + 2 * HALO, D_MODEL), BF16)],
        compiler_params=pltpu.CompilerParams(dimension_semantics=("parallel", "arbitrary"),
                                             vmem_limit_bytes=VMEM_LIMIT),
        name="inproj",
    )(x, x, x, mod, norm_g.reshape(1, D_MODEL), w_qk, w_main, conv_qk, b_gate_pad)


def _mlstm_gates(gt_ref, gx_ref, m_sc, *, reverse, chunk, with_output):
    li0 = 8 if reverse else 0
    last = 0 if reverse else chunk - 1
    li_r = gt_ref[li0:li0 + HEADS, :]
    b_r = gx_ref[li0:li0 + HEADS, :]
    b_last = b_r[:, last:last + 1]
    m_old = m_sc[:, 0:1]
    g_r = b_last - b_r + li_r
    m_new = jnp.maximum(b_last + m_old, jnp.max(g_r, axis=1, keepdims=True))
    out = dict(m_old=m_old, m_new=m_new, decay=jnp.exp(b_last + m_old - m_new),
               k_scale=jnp.exp(g_r - m_new).astype(BF16))
    if not with_output:
        return out
    row = lax.broadcasted_iota(jnp.int32, (chunk, chunk), 0)
    col = lax.broadcasted_iota(jnp.int32, (chunk, chunk), 1)
    mask_ts = (col >= row) if reverse else (col <= row)
    mm_r = jnp.maximum(gx_ref[li0 + HEADS:li0 + 2 * HEADS, :], m_old).astype(BF16)
    b_hi, b_lo = _split_bf16(b_r)
    rows = jnp.concatenate([mm_r, b_hi, b_lo, jnp.zeros_like(b_hi)], axis=0)
    sel_r = lax.broadcasted_iota(jnp.int32, (4 * HEADS, 2 * HEADS * 128), 0)
    sel_c = lax.broadcasted_iota(jnp.int32, (4 * HEADS, 2 * HEADS * 128), 1) // 128
    head_r = sel_r % HEADS
    pick = jnp.logical_or(jnp.logical_and(sel_r < HEADS, sel_c % HEADS == head_r),
                          jnp.logical_and(jnp.logical_and(sel_r >= HEADS, sel_r < 3 * HEADS),
                                          sel_c == HEADS + head_r))
    sel = jnp.where(pick, 1.0, 0.0).astype(BF16)
    cols = lax.dot_general(rows, sel, (((0,), (0,)), ((), ())), preferred_element_type=F32)
    out.update(a_r=li_r - b_r, cols=cols, mask_ts=mask_ts)
    return out


def _mlstm_heads(q_ref, kt_ref, v_ref, h_ref, c_sc, n_sc, m_sc, gates, *, chunk):
    with_output = h_ref is not None
    m_old, m_new, decay, k_scale = gates["m_old"], gates["m_new"], gates["decay"], gates["k_scale"]
    if with_output:
        a_r, cols, mask_ts = gates["a_r"], gates["cols"], gates["mask_ts"]
    ones_rows = jnp.ones((8, chunk), BF16)
    wide = lambda a, n: jnp.concatenate([a] * (n // 128), axis=1)
    for hd in range(HEADS):
        sl = slice(hd * HEAD_DIM, (hd + 1) * HEAD_DIM)
        kt = kt_ref[sl, :]
        v = v_ref[:, sl]
        c_old = c_sc[hd]
        n_old = n_sc[hd]
        kw_t = kt * k_scale[hd:hd + 1, :]
        if with_output:
            q = q_ref[:, sl]
            mm_c = cols[:, hd * 128:(hd + 1) * 128]
            bm_c = cols[:, (HEADS + hd) * 128:(HEADS + hd + 1) * 128]
            decay_mat = jnp.exp(jnp.where(mask_ts, a_r[hd:hd + 1, :] - wide(mm_c, chunk), NEG_BIG))
            s_f = _dot(q, kt) * decay_mat
            w_inter = jnp.exp(m_old[hd:hd + 1, :] - mm_c)
            den = (w_inter * jnp.sum(q.astype(F32) * n_old[0:1, :], axis=1, keepdims=True)
                   + jnp.sum(s_f, axis=1, keepdims=True))
            q_w = q * wide(w_inter, HEAD_DIM).astype(BF16)
            num = _dot(jnp.concatenate([q_w, s_f.astype(BF16)], axis=1),
                       jnp.concatenate([c_old.astype(BF16), v], axis=0))
            inv = 1.0 / jnp.maximum(jnp.abs(den), jnp.exp(-bm_c))
            h_ref[:, sl] = (num * wide(inv, HEAD_DIM)).astype(BF16)
        c_sc[hd] = decay[hd:hd + 1, :] * c_old + _dot(kw_t, v)
        n_sc[hd] = decay[hd:hd + 1, :] * n_old + lax.dot_general(
            ones_rows, kw_t, (((1,), (1,)), ((), ())), preferred_element_type=F32)
        m_sc[hd:hd + 1, :] = jnp.broadcast_to(m_new[hd:hd + 1, :], (1, 128))


def _mlstm_kernel(*refs, with_output, chunk, subs):
    n_in = 5 if with_output else 4
    ins = [refs[0:n_in], refs[n_in:2 * n_in]]
    rest = refs[2 * n_in:]
    if with_output:
        c0_ref, n0_ref, m0_ref, hf_ref, hb_ref, c_sc, n_sc, m_sc = rest
        h_refs = (hf_ref, hb_ref)
    else:
        co_ref, no_ref, mo_ref, c_sc, n_sc, m_sc = rest
        h_refs = (None, None)
        ins = [(None,) + tuple(r) for r in ins]
    i = pl.program_id(1)
    nc = pl.num_programs(1)

    @pl.when(i == 0)
    def _():
        if with_output:
            c_sc[...] = c0_ref[...]
            n_sc[...] = n0_ref[...]
            m_sc[...] = m0_ref[...]
        else:
            c_sc[...] = jnp.zeros_like(c_sc)
            n_sc[...] = jnp.zeros_like(n_sc)
            m_sc[...] = jnp.zeros_like(m_sc)

    for sub in range(subs):
        pos = (sub, subs - 1 - sub)
        rows = [pl.ds(pos[d] * chunk, chunk) for d in range(2)]
        view = lambda ref, d: None if ref is None else ref.at[rows[d], :]
        view_t = lambda ref, d: ref.at[:, rows[d]]
        gates = [_mlstm_gates(view_t(ins[d][3], d), view_t(ins[d][4], d), m_sc.at[d], reverse=bool(d),
                              chunk=chunk, with_output=with_output) for d in range(2)]
        for d in range(2):
            _mlstm_heads(view(ins[d][0], d), view_t(ins[d][1], d), view(ins[d][2], d), view(h_refs[d], d),
                         c_sc.at[d], n_sc.at[d], m_sc.at[d], gates[d], chunk=chunk)

    if not with_output:
        @pl.when(i == nc - 1)
        def _():
            co_ref[...] = c_sc[...]
            no_ref[...] = n_sc[...]
            mo_ref[...] = m_sc[...]


def _mlstm(q, kt, v, gt, gx, state):
    with_output = q is not None
    bsz, t_len, _ = v.shape
    chunk = MLSTM_CHUNK
    subs = next(n for n in MLSTM_CHUNKS_PER_STEP if t_len % (n * chunk) == 0)
    blk = subs * chunk
    nc = t_len // blk
    cidx = (lambda i: i, lambda i: nc - 1 - i)
    tile = lambda w, d: pl.BlockSpec((None, blk, w), lambda b, i: (b, cidx[d](i), 0))
    tile_t = lambda w, d: pl.BlockSpec((None, w, blk), lambda b, i: (b, 0, cidx[d](i)))
    st_dims = [(2, HEADS, HEAD_DIM, HEAD_DIM), (2, HEADS, 8, HEAD_DIM), (2, HEADS, 128)]
    st_specs = [pl.BlockSpec((None,) + s, lambda b, i, n=len(s): (b,) + (0,) * n) for s in st_dims]
    st_shapes = [jax.ShapeDtypeStruct((bsz,) + s, F32) for s in st_dims]
    in_specs, args = [], []
    for d in range(2):
        in_specs += ([tile(W_BRANCH, d)] if with_output else []) + [
            tile_t(W_BRANCH, d), tile(W_BRANCH, d), tile_t(N_GATES, d), tile_t(N_GATES, d)]
        args += ([q] if with_output else []) + [kt, v, gt, gx]
    if with_output:
        in_specs, args = in_specs + st_specs, args + list(state)
        out_specs = [tile(W_BRANCH, 0), tile(W_BRANCH, 1)]
        out_shape = [jax.ShapeDtypeStruct((bsz, t_len, W_BRANCH), BF16)] * 2
    else:
        out_specs, out_shape = st_specs, st_shapes
    return pl.pallas_call(
        functools.partial(_mlstm_kernel, with_output=with_output, chunk=chunk, subs=subs),
        grid=(bsz, nc),
        in_specs=in_specs,
        out_specs=out_specs,
        out_shape=out_shape,
        scratch_shapes=[pltpu.VMEM(s, F32) for s in st_dims],
        compiler_params=pltpu.CompilerParams(dimension_semantics=("parallel", "arbitrary"),
                                             vmem_limit_bytes=VMEM_LIMIT),
        name="mlstm_out" if with_output else "mlstm_state",
    )(*args)


def _s5_kernel(vc_ref, vx_ref, g_ref, m_ref, p_ref, a_ref, y_ref, gu_sc, s_sc, *,
               nk_ctx, n_rc, bsz, rblk):
    rows_ctx = nk_ctx * bsz
    rows_x = n_rc * bsz * GRID_W

    def increments(v_ref, r0, r1):
        return _dot(v_ref[0, r0:r1, :], g_ref[0]) + _dot(v_ref[1, r0:r1, :], g_ref[1])

    inc = increments(vc_ref, 0, rows_ctx)
    for comp in range(4):
        gu_sc[comp, 0:rows_ctx, :] = inc[:, comp * 128:(comp + 1) * 128]
    for r0 in range(0, rows_x, rblk):
        inc = increments(vx_ref, r0, r0 + rblk)
        for run in range(rblk // GRID_W):
            dst = rows_ctx + (r0 // GRID_W + run) * S5_ROW_PITCH
            for comp in range(4):
                gu_sc[comp, dst:dst + GRID_W, :] = (
                    inc[run * GRID_W:(run + 1) * GRID_W, comp * 128:(comp + 1) * 128])

    a = a_ref[...]
    a_pow = [jnp.broadcast_to(a[:, comp * 128:(comp + 1) * 128], (bsz, 128)) for comp in range(4)]
    zero = jnp.zeros((bsz, 128), F32)

    def cmul(x_r, x_i, y_r, y_i):
        return x_r * y_r - x_i * y_i, x_r * y_i + x_i * y_r

    a_sq = [cmul(a_pow[2 * d], a_pow[2 * d + 1], a_pow[2 * d], a_pow[2 * d + 1]) for d in range(2)]

    def step(rows, carry, direction):
        s_r, s_i = carry
        inc_r = gu_sc[2 * direction, rows, :]
        inc_i = gu_sc[2 * direction + 1, rows, :]
        gu_sc[2 * direction, rows, :] = s_r
        gu_sc[2 * direction + 1, rows, :] = s_i
        p_r, p_i = cmul(a_pow[2 * direction], a_pow[2 * direction + 1], s_r, s_i)
        return p_r + inc_r, p_i + inc_i

    def step2(rows0, rows1, carry, direction):
        s_r, s_i = carry
        a_r, a_i = a_pow[2 * direction], a_pow[2 * direction + 1]
        inc0_r, inc0_i = gu_sc[2 * direction, rows0, :], gu_sc[2 * direction + 1, rows0, :]
        inc1_r, inc1_i = gu_sc[2 * direction, rows1, :], gu_sc[2 * direction + 1, rows1, :]
        gu_sc[2 * direction, rows0, :] = s_r
        gu_sc[2 * direction + 1, rows0, :] = s_i
        m_r, m_i = cmul(a_r, a_i, s_r, s_i)
        gu_sc[2 * direction, rows1, :] = m_r + inc0_r
        gu_sc[2 * direction + 1, rows1, :] = m_i + inc0_i
        c_r, c_i = cmul(a_r, a_i, inc0_r, inc0_i)
        q_r, q_i = cmul(a_sq[direction][0], a_sq[direction][1], s_r, s_i)
        return q_r + (c_r + inc1_r), q_i + (c_i + inc1_i)

    def ctx_rows(k):
        return pl.ds(pl.multiple_of(k * bsz, bsz), bsz)

    def x_rows(w, rc):
        return pl.ds(rows_ctx + rc * (bsz * S5_ROW_PITCH) + w, bsz, stride=S5_ROW_PITCH)

    def ctx_body(k, carry):
        return step(ctx_rows(k), carry[0], 0), step(ctx_rows(nk_ctx - 1 - k), carry[1], 1)

    def x_body(w, carry):
        c_f, c_b = carry
        w_b = GRID_W - 1 - w
        for rc in range(0, n_rc - 1, 2):
            c_f = step2(x_rows(w, rc), x_rows(w, rc + 1), c_f, 0)
            c_b = step2(x_rows(w_b, n_rc - 1 - rc), x_rows(w_b, n_rc - 2 - rc), c_b, 1)
        if n_rc % 2:
            c_f = step(x_rows(w, n_rc - 1), c_f, 0)
            c_b = step(x_rows(w_b, 0), c_b, 1)
        return c_f, c_b

    carry = lax.fori_loop(0, nk_ctx, ctx_body, ((zero, zero), (zero, zero)))
    lax.fori_loop(0, GRID_W, x_body, carry)

    for r0 in range(0, rows_x, rblk):
        r1 = r0 + rblk
        for run in range(rblk // GRID_W):
            src = rows_ctx + (r0 // GRID_W + run) * S5_ROW_PITCH
            for comp in range(4):
                s_sc[r0 + run * GRID_W:r0 + (run + 1) * GRID_W, comp * 128:(comp + 1) * 128] = (
                    gu_sc[comp, src:src + GRID_W, :].astype(BF16))
        for gg in range(2):
            y_ref[gg, r0:r1, :] = (_dot(vx_ref[gg, r0:r1, :], m_ref[gg])
                                   + _dot(s_sc[r0:r1, :], p_ref[gg])).astype(BF16)


def _s5(v_ctx, v_x, g_all, m_all, p_all, a16, bsz):
    rows_ctx, rows_x = v_ctx.shape[1], v_x.shape[1]
    lanes = S5_SUB * S5_GC
    return pl.pallas_call(
        functools.partial(_s5_kernel, nk_ctx=rows_ctx // bsz, n_rc=rows_x // (bsz * GRID_W),
                          bsz=bsz, rblk=512),
        grid=(S5_GROUPS // 2,),
        in_specs=[pl.BlockSpec((2, rows_ctx, lanes), lambda j: (j, 0, 0)),
                  pl.BlockSpec((2, rows_x, lanes), lambda j: (j, 0, 0)),
                  pl.BlockSpec((2, lanes, 512), lambda j: (j, 0, 0)),
                  pl.BlockSpec((2, lanes, lanes), lambda j: (j, 0, 0)),
                  pl.BlockSpec((2, 512, lanes), lambda j: (j, 0, 0)),
                  pl.BlockSpec((None, 1, 512), lambda j: (j, 0, 0))],
        out_specs=pl.BlockSpec((2, rows_x, lanes), lambda j: (j, 0, 0)),
        out_shape=jax.ShapeDtypeStruct((S5_GROUPS, rows_x, lanes), BF16),
        scratch_shapes=[pltpu.VMEM((4, rows_ctx + (rows_x // GRID_W) * S5_ROW_PITCH, 128), F32),
                        pltpu.VMEM((rows_x, 512), BF16)],
        compiler_params=pltpu.CompilerParams(dimension_semantics=("parallel",),
                                             vmem_limit_bytes=VMEM_LIMIT),
        name="s5",
    )(v_ctx, v_x, g_all, m_all, p_all, a16)


def _s5_prep_kernel(lr_ref, lc_ref, bt_ref, ct_ref, d_ref, m_ref, g_ref, p_ref, a_ref):
    hp = lax.Precision.HIGHEST
    n_s, lanes = S5_SUB, S5_SUB * S5_GC
    lane128 = lax.broadcasted_iota(jnp.int32, (n_s, 128), 1)
    blk_of_lane = lax.broadcasted_iota(jnp.int32, (128, lanes), 1) // S5_GC
    g_types, p_types, a16, k_rows = [], [], [], []
    for d in range(2):
        a_r, a_i, log_dt = lr_ref[d, 0:1, :], lr_ref[d, 1:2, :], lr_ref[d, 2:3, :]
        dt = jnp.exp(log_dt)
        lam_r, lam_i = a_r * dt, a_i * dt
        steps = lax.broadcasted_iota(jnp.int32, (24, 128), 0).astype(F32)
        mag = jnp.exp(lam_r * steps)
        pw_r, pw_i = mag * jnp.cos(lam_i * steps), mag * jnp.sin(lam_i * steps)
        nr, ni = pw_r[1:2] - 1.0, pw_i[1:2]
        den = a_r * a_r + a_i * a_i
        co_r, co_i = (nr * a_r + ni * a_i) / den, (ni * a_r - nr * a_i) / den
        b_r = jnp.concatenate([bt_ref[d, 0]] * n_s, axis=0)
        b_i = jnp.concatenate([bt_ref[d, 1]] * n_s, axis=0)
        bb_r, bb_i = co_r * b_r - co_i * b_i, co_r * b_i + co_i * b_r
        order = [n_s - 1 - i for i in range(n_s)] if d == 0 else list(range(n_s))
        pg_r = jnp.concatenate([jnp.broadcast_to(pw_r[n:n + 1], (S5_GC, 128)) for n in order], axis=0)
        pg_i = jnp.concatenate([jnp.broadcast_to(pw_i[n:n + 1], (S5_GC, 128)) for n in order], axis=0)
        g_types += [bb_r * pg_r - bb_i * pg_i, bb_r * pg_i + bb_i * pg_r]
        a16 += [pw_r[n_s:n_s + 1], pw_i[n_s:n_s + 1]]
        x0 = slice((n_s - 1) * S5_GC, n_s * S5_GC) if d == 0 else slice(0, S5_GC)
        x_r, x_i = bb_r[x0], bb_i[x0]
        lhs_r = jnp.concatenate([jnp.where(lane128 < S5_STATE, x_r, 0.0),
                                 jnp.where(lane128 < S5_STATE, 0.0, x_r)], axis=0)
        lhs_i = jnp.concatenate([jnp.where(lane128 < S5_STATE, x_i, 0.0),
                                 jnp.where(lane128 < S5_STATE, 0.0, x_i)], axis=0)
        dt_c = jnp.exp(lc_ref[d, 2])
        lam_rc, lam_ic = lc_ref[d, 0] * dt_c, lc_ref[d, 1] * dt_c
        mag_1 = jnp.exp(lam_rc)
        a1_r, a1_i = mag_1 * jnp.cos(lam_ic), mag_1 * jnp.sin(lam_ic)
        a1_r = jnp.concatenate([a1_r, a1_r], axis=1)
        a1_i = jnp.concatenate([a1_i, a1_i], axis=1)
        n_y = blk_of_lane if d == 0 else n_s - 1 - blk_of_lane
        ypw_r, ypw_i = jnp.ones_like(a1_r), jnp.zeros_like(a1_r)
        sq_r, sq_i = a1_r, a1_i
        for bit in (1, 2, 4, 8):
            on = (n_y & bit) != 0
            ypw_r, ypw_i = (jnp.where(on, ypw_r * sq_r - ypw_i * sq_i, ypw_r),
                            jnp.where(on, ypw_r * sq_i + ypw_i * sq_r, ypw_i))
            if bit < 8:
                sq_r, sq_i = sq_r * sq_r - sq_i * sq_i, 2.0 * sq_r * sq_i
        c_r, c_i = ct_ref[d, 0], ct_ref[d, 1]
        y_r, y_i = c_r * ypw_r - c_i * ypw_i, c_r * ypw_i + c_i * ypw_r
        k_rows.append(jnp.dot(lhs_r, y_r, preferred_element_type=F32, precision=hp)
                      - jnp.dot(lhs_i, y_i, preferred_element_type=F32, precision=hp))
        p_types += [y_r * a1_r - y_i * a1_i, -(y_r * a1_i + y_i * a1_r)]

    a_ref[...] = jnp.concatenate(a16, axis=1)
    lane_g = lax.broadcasted_iota(jnp.int32, (lanes, 128), 1)
    row_p = lax.broadcasted_iota(jnp.int32, (128, lanes), 0)
    lane_k = lax.broadcasted_iota(jnp.int32, (S5_GC, lanes), 1)
    row_m = lax.broadcasted_iota(jnp.int32, (lanes, lanes), 0)
    lane_m = lax.broadcasted_iota(jnp.int32, (lanes, lanes), 1)
    for h in range(2):
        mine_l = (lane_g >= S5_STATE) == bool(h)
        g_ref[h] = jnp.concatenate([jnp.where(mine_l, t, 0.0) for t in g_types], axis=1).astype(BF16)
        mine_r = (row_p >= S5_STATE) == bool(h)
        p_ref[h] = jnp.concatenate([jnp.where(mine_r, t, 0.0) for t in p_types], axis=0).astype(BF16)
        k_f = k_rows[0][h * S5_GC:(h + 1) * S5_GC]
        k_b = k_rows[1][h * S5_GC:(h + 1) * S5_GC]
        blocks = []
        for i in range(n_s):
            up, down = S5_GC * i, S5_GC * (n_s - 1 - i)
            f = k_f if up == 0 else jnp.where(lane_k >= up, pltpu.roll(k_f, up, axis=1), 0.0)
            b = k_b if down == 0 else jnp.where(lane_k < lanes - down,
                                                pltpu.roll(k_b, lanes - down, axis=1), 0.0)
            blocks.append(f + b)
        m = jnp.concatenate(blocks, axis=0) + jnp.where(row_m == lane_m, d_ref[h], 0.0)
        m_ref[h] = m.astype(BF16)


def _s5_prep(a_re, a_im, log_step, b_re, b_im, c_re, c_im, d_skip):
    n_g, n_p, n_c, n_s = S5_GROUPS, S5_STATE, S5_GC, S5_SUB
    lanes = n_s * n_c
    pair = lambda a: jnp.transpose(a.astype(F32).reshape(2, n_g // 2, 2 * n_p), (1, 0, 2))
    lam_row = jnp.stack([pair(a_re), pair(a_im),
                         pair(jnp.broadcast_to(log_step[..., None], a_re.shape))], axis=2)
    lam_col = jnp.broadcast_to(lam_row[..., None], lam_row.shape + (2 * n_p,))
    bt = lambda b: jnp.transpose(b.astype(F32).reshape(2, n_g // 2, 2, n_p, n_c),
                                 (1, 0, 4, 2, 3)).reshape(n_g // 2, 2, n_c, 2 * n_p)
    b_t = jnp.stack([bt(b_re), bt(b_im)], axis=2)
    ct = lambda c: jnp.tile(jnp.transpose(c.astype(F32).reshape(2, n_g // 2, 2, n_c, n_p),
                                          (1, 0, 2, 4, 3)).reshape(n_g // 2, 2, 2 * n_p, n_c),
                            (1, 1, 1, n_s))
    c_t = jnp.stack([ct(c_re), ct(c_im)], axis=2)
    d_row = jnp.tile(d_skip.astype(F32).reshape(n_g // 2, 2, 1, n_c), (1, 1, 1, n_s))
    blk = lambda *s: pl.BlockSpec((None,) + s, lambda j: (j,) + (0,) * len(s))
    grp = lambda *s: pl.BlockSpec((2,) + s, lambda j: (j,) + (0,) * len(s))
    return pl.pallas_call(
        _s5_prep_kernel,
        grid=(n_g // 2,),
        in_specs=[blk(2, 3, 2 * n_p), blk(2, 3, 2 * n_p, 2 * n_p), blk(2, 2, n_c, 2 * n_p),
                  blk(2, 2, 2 * n_p, lanes), blk(2, 1, lanes)],
        out_specs=[grp(lanes, lanes), grp(lanes, 8 * n_p), grp(8 * n_p, lanes), blk(1, 8 * n_p)],
        out_shape=[jax.ShapeDtypeStruct((n_g, lanes, lanes), BF16),
                   jax.ShapeDtypeStruct((n_g, lanes, 8 * n_p), BF16),
                   jax.ShapeDtypeStruct((n_g, 8 * n_p, lanes), BF16),
                   jax.ShapeDtypeStruct((n_g // 2, 1, 8 * n_p), F32)],
        compiler_params=pltpu.CompilerParams(dimension_semantics=("parallel",),
                                             vmem_limit_bytes=VMEM_LIMIT),
        name="s5_prep",
    )(lam_row, lam_col, b_t, c_t, d_row)


def _merge_kernel(hf_ref, hb_ref, o_ref, zm_ref, y_ref, zs_ref, x_ref, mod_ref, mhg_ref,
                  gluw_ref, glub_ref, wout_ref, fg_ref, out_ref, y_sc):
    for lh in range(2):
        for g8 in range(W_BRANCH // 128):
            rows = [y_ref[g8 * 8 + gp, :, lh * 128:(lh + 1) * 128].astype(F32) for gp in range(8)]
            for i, blk in enumerate(_block_transpose8(rows)):
                r0 = (lh * 8 + i) * GRID_W
                y_sc[r0:r0 + GRID_W, g8 * 128:(g8 + 1) * 128] = blk.astype(BF16)

    mhg = mhg_ref[...]
    for r0 in range(0, x_ref.shape[0], MERGE_ROWS):
        rs = slice(r0, r0 + MERGE_ROWS)
        hm = ((hf_ref[rs, :] + hb_ref[rs, :]) * jax.nn.sigmoid(o_ref[rs, :])).astype(F32)
        parts = []
        for hd in range(HEADS):
            sl = slice(hd * HEAD_DIM, (hd + 1) * HEAD_DIM)
            seg = hm[:, sl]
            mu = jnp.mean(seg, axis=-1, keepdims=True)
            dev = seg - mu
            var = jnp.mean(dev * dev, axis=-1, keepdims=True)
            parts.append(dev * lax.rsqrt(var + NORM_EPS) * mhg[:, sl])
        m_out = jnp.concatenate(parts, axis=-1).astype(BF16) * _silu(zm_ref[rs, :])

        y = y_sc[rs, :]
        gl = 0.5 * y * (1.0 + jnp.tanh(0.7978845608028654 * (y + 0.044715 * (y * y * y))))
        gate = jax.nn.sigmoid((_dot(gl, gluw_ref[...]) + glub_ref[...]).astype(BF16))
        s_out = gl * gate * _silu(zs_ref[rs, :])

        mixed = _dot(m_out, wout_ref[0:W_BRANCH, :]) + _dot(s_out, wout_ref[W_BRANCH:2 * W_BRANCH, :])
        xo = x_ref[rs, :] + mod_ref[2:3, :] * mixed
        ms = jnp.mean(xo * xo, axis=-1, keepdims=True)
        out_ref[rs, :] = xo * lax.rsqrt(ms + NORM_EPS) * fg_ref[...]


def _merge(hf, hb, o, zm, y, zs, x, mod, mh_g, glu_w, glu_b, w_out, final_g):
    bsz, t_len, _ = x.shape
    tm = S5_SUB * GRID_W
    tile = pl.BlockSpec((None, tm, D_MODEL), lambda b, i: (b, i, 0))
    y_spec = pl.BlockSpec((S5_GROUPS, GRID_W, S5_SUB * S5_GC), lambda b, i: (0, i * bsz + b, 0))
    const = lambda shape: pl.BlockSpec(shape, lambda b, i: (0,) * len(shape),
                                       pipeline_mode=pl.Buffered(1))
    return pl.pallas_call(
        _merge_kernel,
        grid=(bsz, t_len // tm),
        in_specs=[tile] * 4 + [y_spec, tile, tile,
                               pl.BlockSpec((None, 3, D_MODEL), lambda b, i: (b, 0, 0)),
                               const((1, W_BRANCH)), const((W_BRANCH, W_BRANCH)),
                               const((1, W_BRANCH)), const((2 * W_BRANCH, D_MODEL)),
                               const((1, D_MODEL))],
        out_specs=tile,
        out_shape=jax.ShapeDtypeStruct((bsz, t_len, D_MODEL), F32),
        scratch_shapes=[pltpu.VMEM((tm, W_BRANCH), BF16)],
        compiler_params=pltpu.CompilerParams(dimension_semantics=("parallel", "arbitrary"),
                                             vmem_limit_bytes=VMEM_LIMIT),
        name="merge",
    )(hf, hb, o, zm, y, zs, x, mod, mh_g.reshape(1, -1), glu_w, glu_b.reshape(1, -1), w_out,
      final_g.reshape(1, -1))


def _s5_rows_ctx(u):
    bsz, t_len, _ = u.shape
    a = u.reshape(bsz, t_len // S5_SUB, S5_SUB, S5_GROUPS, S5_GC)
    a = jnp.transpose(a, (3, 1, 0, 2, 4))
    return a.reshape(S5_GROUPS, (t_len // S5_SUB) * bsz, S5_SUB * S5_GC)


def kernel(x, c, ctx, c_ctx, norm_g, ada_w, ada_b, w_in, b_gate, conv_qk, mh_g, s5_a_re, s5_a_im,
           s5_log_step, s5_b_re, s5_b_im, s5_c_re, s5_c_im, s5_d, glu_w, glu_b, w_out, final_g):
    bsz = x.shape[0]
    layer = 0

    cc = jnp.zeros((16, D_MODEL), F32).at[:bsz].set(c).at[bsz].set(c_ctx)
    mod = _ada(cc, ada_w[layer], ada_b[layer]).reshape(16, 3, D_MODEL)
    mod_x = mod[:bsz]
    mod_c = jnp.broadcast_to(mod[bsz][None], (bsz, 3, D_MODEL))

    w = w_in[layer]
    wb = W_BRANCH
    w_qk = w[:, 0:2 * wb].astype(BF16)
    gate0 = 5 * wb
    w_main = jnp.concatenate([w[:, 2 * wb:5 * wb], w[:, gate0 + N_GATES:],
                              w[:, gate0:gate0 + GATE_PAD]], axis=1).astype(BF16)
    b_gate_pad = jnp.pad(b_gate[layer].reshape(1, N_GATES), ((0, 0), (0, GATE_PAD - N_GATES)))

    proj = functools.partial(_inproj, norm_g=norm_g[layer], w_qk=w_qk, w_main=w_main,
                             conv_qk=conv_qk[layer], b_gate_pad=b_gate_pad)
    kt_c, v_c, u_c, gt_c, gx_c = proj(ctx, mod_c, tm=MLSTM_CHUNK, s5_layout=False, state_only=True)
    q_x, kt_x, v_x, o_x, zm_x, u_x, zs_x, gt_x, gx_x = proj(x, mod_x, tm=8 * GRID_W, s5_layout=True)

    ctx_state = _mlstm(None, kt_c, v_c, gt_c, gx_c, None)
    h_f, h_b = _mlstm(q_x, kt_x, v_x, gt_x, gx_x, ctx_state)

    m_all, g_all, p_all, a16 = _s5_prep(
        s5_a_re[layer], s5_a_im[layer], s5_log_step[layer], s5_b_re[layer], s5_b_im[layer],
        s5_c_re[layer], s5_c_im[layer], s5_d[layer])
    y_x = _s5(_s5_rows_ctx(u_c), u_x, g_all, m_all, p_all, a16, bsz)

    return _merge(h_f, h_b, o_x, zm_x, y_x, zs_x, x, mod_x, mh_g[layer], glu_w[layer].astype(BF16),
                  glu_b[layer], w_out[layer].astype(BF16), final_g)
```

```python
import functools

import jax
import jax.numpy as jnp
from jax import lax
from jax.experimental import pallas as pl
from jax.experimental.pallas import tpu as pltpu

F32 = jnp.float32
BF16 = jnp.bfloat16

D_MODEL = 1024
HEADS = 4
HEAD_DIM = 256
W_BRANCH = 1024
S5_GROUPS = 64
S5_GC = 16
S5_STATE = 64
S5_SUB = 16
GRID_W = 64
S5_ROW_PITCH = 72
N_GATES = 16
GATE_PAD = 128
CONV_W = 3
NORM_EPS = 1e-6
MLSTM_CHUNK = 256
MLSTM_CHUNKS_PER_STEP = (4, 2, 1)
MERGE_ROWS = 256
NEG_BIG = -1e30

VMEM_LIMIT = 56 * 1024 * 1024


def _silu(a):
    return a * jax.nn.sigmoid(a)


def _log_sigmoid(a):
    return jnp.minimum(a, 0.0) - jnp.log1p(jnp.exp(-jnp.abs(a)))


def _dot(a, b):
    return jnp.dot(a, b, preferred_element_type=F32)


def _split_bf16(a):
    hi = a.astype(BF16)
    lo = (a - hi.astype(F32)).astype(BF16)
    return hi, lo


def _block_transpose8(rows):
    lane = lax.broadcasted_iota(jnp.int32, rows[0].shape, 1)
    blk = lane // S5_GC
    for d in (4, 2, 1):
        keep = (blk & d) == 0
        new = list(rows)
        for i in range(8):
            if i & d == 0:
                a, b = rows[i], rows[i + d]
                new[i] = jnp.where(keep, a, pltpu.roll(b, d * S5_GC, axis=1))
                new[i + d] = jnp.where(keep, pltpu.roll(a, 128 - d * S5_GC, axis=1), b)
        rows = new
    return rows


def _chunk_gate_rows(gt_blk, reverse):
    chunk = gt_blk.shape[1]
    row = lax.broadcasted_iota(jnp.int32, (chunk, chunk), 0)
    col = lax.broadcasted_iota(jnp.int32, (chunk, chunk), 1)
    mask_st = (row >= col) if reverse else (row <= col)
    tri_st = jnp.where(mask_st, 1.0, 0.0).astype(BF16)
    hi, lo = _split_bf16(gt_blk)
    cum = _dot(hi, tri_st) + _dot(lo, tri_st)
    li0 = 8 if reverse else 0
    b_r = cum[li0 + HEADS:li0 + 2 * HEADS, :]
    run_max = gt_blk[li0:li0 + HEADS, :] - b_r
    lane = lax.broadcasted_iota(jnp.int32, run_max.shape, 1)
    shift = 1
    while shift < chunk:
        if reverse:
            moved = jnp.where(lane < chunk - shift, pltpu.roll(run_max, chunk - shift, axis=1), NEG_BIG)
        else:
            moved = jnp.where(lane >= shift, pltpu.roll(run_max, shift, axis=1), NEG_BIG)
        run_max = jnp.maximum(run_max, moved)
        shift *= 2
    return b_r, run_max


def _ada_kernel(c_ref, w_ref, b_ref, o_ref):
    s = _silu(c_ref[...])
    o_ref[...] = jnp.dot(s, w_ref[...], preferred_element_type=F32,
                         precision=lax.Precision.HIGHEST) + b_ref[...]


def _ada(cc, ada_w, ada_b):
    rows = cc.shape[0]
    n_out = ada_w.shape[1]
    tn = 1024
    return pl.pallas_call(
        _ada_kernel,
        grid=(n_out // tn,),
        in_specs=[pl.BlockSpec((rows, D_MODEL), lambda j: (0, 0)),
                  pl.BlockSpec((D_MODEL, tn), lambda j: (0, j)),
                  pl.BlockSpec((1, tn), lambda j: (0, j))],
        out_specs=pl.BlockSpec((rows, tn), lambda j: (0, j)),
        out_shape=jax.ShapeDtypeStruct((rows, n_out), F32),
        compiler_params=pltpu.CompilerParams(dimension_semantics=("arbitrary",),
                                             vmem_limit_bytes=VMEM_LIMIT),
        name="ada",
    )(cc, ada_w, ada_b.reshape(1, n_out))


HALO = 8


def _inproj_kernel(x_ref, xp_ref, xn_ref, mod_ref, ng_ref, wqk_ref, wm_ref, conv_ref, bg_ref,
                   *rest, tm, s5_layout, state_only):
    if state_only:
        kt_ref, v_ref, u_ref, gt_ref, gx_ref, lhs_sc = rest
    else:
        q_ref, kt_ref, v_ref, o_ref, zm_ref, u_ref, zs_ref, gt_ref, gx_ref, lhs_sc = rest
    i = pl.program_id(1)
    nt = pl.num_programs(1)
    shift = mod_ref[0:1, :]
    gain = ng_ref[...] * (1.0 + mod_ref[1:2, :])

    def norm_mod(a):
        ms = jnp.mean(a * a, axis=-1, keepdims=True)
        return a * lax.rsqrt(ms + NORM_EPS) * gain + shift

    lhs_sc[0:tm, :] = norm_mod(x_ref[...]).astype(BF16)
    lhs_sc[tm:tm + HALO, :] = norm_mod(xp_ref[...]).astype(BF16)
    lhs_sc[tm + HALO:tm + 2 * HALO, :] = norm_mod(xn_ref[...]).astype(BF16)

    h = lhs_sc[0:tm, :]
    zg = _dot(h, wm_ref[:, 5120:5120 + GATE_PAD]) + bg_ref[...]
    lane = lax.broadcasted_iota(jnp.int32, zg.shape, 1)
    is_forget = jnp.logical_and(lane < N_GATES, (lane % 8) >= 4)
    gates = jnp.where(is_forget, _log_sigmoid(zg), zg)
    gt = gates.T[0:N_GATES, :]
    gt_ref[...] = gt
    for c0 in range(0, tm, MLSTM_CHUNK):
        parts = []
        for reverse in (False, True):
            parts += list(_chunk_gate_rows(gt[:, c0:c0 + MLSTM_CHUNK], reverse))
        gx_ref[:, c0:c0 + MLSTM_CHUNK] = jnp.concatenate(parts, axis=0)

    u = _dot(h, wm_ref[:, 3072:4096])
    if s5_layout:
        for g8 in range(W_BRANCH // 128):
            rows = [u[i * GRID_W:(i + 1) * GRID_W, g8 * 128:(g8 + 1) * 128] for i in range(8)]
            for gp, blk in enumerate(_block_transpose8(rows)):
                u_ref[g8 * 8 + gp] = blk.astype(BF16)
    else:
        u_ref[...] = u.astype(BF16)

    valid_prev = (i > 0).astype(F32)
    valid_next = (i < nt - 1).astype(F32)

    def conv_stage(half):
        cols = slice(half * W_BRANCH, (half + 1) * W_BRANCH)
        z = _dot(lhs_sc[...], wqk_ref[:, cols])
        zc = z[0:tm]
        row8 = lax.broadcasted_iota(jnp.int32, (8, W_BRANCH), 0)
        before = pltpu.roll(zc, 1, axis=0)
        first = jnp.where(row8 == 0, z[tm + HALO - 1:tm + HALO] * valid_prev, before[0:8])
        before = jnp.concatenate([first, before[8:]], axis=0)
        after = pltpu.roll(zc, tm - 1, axis=0)
        final = jnp.where(row8 == 7, z[tm + HALO:tm + HALO + 1] * valid_next, after[tm - 8:tm])
        after = jnp.concatenate([after[:tm - 8], final], axis=0)
        cw = conv_ref[:, cols]
        conv = cw[0:1] * before + cw[1:2] * zc + cw[2:3] * after
        if half == 0:
            q_ref[...] = _silu(conv.astype(BF16))
        else:
            kt_ref[...] = (_silu(conv.astype(BF16)) * HEAD_DIM ** -0.5).T

    v_ref[...] = _dot(h, wm_ref[:, 0:1024]).astype(BF16)
    if not state_only:
        conv_stage(0)
        o_ref[...] = _dot(h, wm_ref[:, 1024:2048]).astype(BF16)
    conv_stage(1)
    if not state_only:
        zm_ref[...] = _dot(h, wm_ref[:, 2048:3072]).astype(BF16)
        zs_ref[...] = _dot(h, wm_ref[:, 4096:5120]).astype(BF16)


def _inproj(x, mod, norm_g, w_qk, w_main, conv_qk, b_gate_pad, tm, s5_layout, state_only=False):
    bsz, t_len, _ = x.shape
    nt = t_len // tm
    nhb = t_len // HALO
    tok = lambda w, dt: jax.ShapeDtypeStruct((bsz, t_len, w), dt)
    tile = lambda w: pl.BlockSpec((None, tm, w), lambda b, i: (b, i, 0))
    tile_t = lambda w: pl.BlockSpec((None, w, tm), lambda b, i: (b, 0, i))
    const = lambda shape: pl.BlockSpec(shape, lambda b, i: (0,) * len(shape),
                                       pipeline_mode=pl.Buffered(1))
    if s5_layout:
        assert tm == 8 * GRID_W and t_len % (2 * tm) == 0
        u_spec = pl.BlockSpec((S5_GROUPS, GRID_W, 128), lambda b, i: (0, (i // 2) * bsz + b, i % 2))
        u_shape = jax.ShapeDtypeStruct((S5_GROUPS, (t_len // (2 * tm)) * bsz * GRID_W, 256), BF16)
    else:
        u_spec, u_shape = tile(W_BRANCH), tok(W_BRANCH, BF16)
    kt_shape = jax.ShapeDtypeStruct((bsz, W_BRANCH, t_len), BF16)
    gt_shape = jax.ShapeDtypeStruct((bsz, N_GATES, t_len), F32)
    if state_only:
        out_specs = [tile_t(W_BRANCH), tile(W_BRANCH), u_spec, tile_t(N_GATES), tile_t(N_GATES)]
        out_shape = [kt_shape, tok(W_BRANCH, BF16), u_shape, gt_shape, gt_shape]
    else:
        out_specs = [tile(W_BRANCH), tile_t(W_BRANCH)] + [tile(W_BRANCH)] * 3 + [
            u_spec, tile(W_BRANCH), tile_t(N_GATES), tile_t(N_GATES)]
        out_shape = [tok(W_BRANCH, BF16), kt_shape] + [tok(W_BRANCH, BF16)] * 3 + [
            u_shape, tok(W_BRANCH, BF16), gt_shape, gt_shape]
    return pl.pallas_call(
        functools.partial(_inproj_kernel, tm=tm, s5_layout=s5_layout, state_only=state_only),
        grid=(bsz, nt),
        in_specs=[
            tile(D_MODEL),
            pl.BlockSpec((None, HALO, D_MODEL),
                         lambda b, i: (b, jnp.maximum(i * (tm // HALO) - 1, 0), 0)),
            pl.BlockSpec((None, HALO, D_MODEL),
                         lambda b, i: (b, jnp.minimum((i + 1) * (tm // HALO), nhb - 1), 0)),
            pl.BlockSpec((None, 3, D_MODEL), lambda b, i: (b, 0, 0)),
            const((1, D_MODEL)),
            const((D_MODEL, 2 * W_BRANCH)),
            const((D_MODEL, 5 * W_BRANCH + GATE_PAD)),
            const((CONV_W, 2 * W_BRANCH)),
            const((1, GATE_PAD)),
        ],
        out_specs=out_specs,
        out_shape=out_shape,
        scratch_shapes=[pltpu.VMEM((tm + 2 * HALO, D_MODEL), BF16)],
        compiler_params=pltpu.CompilerParams(dimension_semantics=("parallel", "arbitrary"),
                                             vmem_limit_bytes=VMEM_LIMIT),
        name="inproj",
    )(x, x, x, mod, norm_g.reshape(1, D_MODEL), w_qk, w_main, conv_qk, b_gate_pad)


def _mlstm_gates(gt_ref, gx_ref, m_sc, *, reverse, chunk, with_output):
    li0 = 8 if reverse else 0
    last = 0 if reverse else chunk - 1
    li_r = gt_ref[li0:li0 + HEADS, :]
    b_r = gx_ref[li0:li0 + HEADS, :]
    b_last = b_r[:, last:last + 1]
    m_old = m_sc[:, 0:1]
    g_r = b_last - b_r + li_r
    m_new = jnp.maximum(b_last + m_old, jnp.max(g_r, axis=1, keepdims=True))
    out = dict(m_old=m_old, m_new=m_new, decay=jnp.exp(b_last + m_old - m_new),
               k_scale=jnp.exp(g_r - m_new).astype(BF16))
    if not with_output:
        return out
    row = lax.broadcasted_iota(jnp.int32, (chunk, chunk), 0)
    col = lax.broadcasted_iota(jnp.int32, (chunk, chunk), 1)
    mask_ts = (col >= row) if reverse else (col <= row)
    mm_r = jnp.maximum(gx_ref[li0 + HEADS:li0 + 2 * HEADS, :], m_old).astype(BF16)
    b_hi, b_lo = _split_bf16(b_r)
    rows = jnp.concatenate([mm_r, b_hi, b_lo, jnp.zeros_like(b_hi)], axis=0)
    sel_r = lax.broadcasted_iota(jnp.int32, (4 * HEADS, 2 * HEADS * 128), 0)
    sel_c = lax.broadcasted_iota(jnp.int32, (4 * HEADS, 2 * HEADS * 128), 1) // 128
    head_r = sel_r % HEADS
    pick = jnp.logical_or(jnp.logical_and(sel_r < HEADS, sel_c % HEADS == head_r),
                          jnp.logical_and(jnp.logical_and(sel_r >= HEADS, sel_r < 3 * HEADS),
                                          sel_c == HEADS + head_r))
    sel = jnp.where(pick, 1.0, 0.0).astype(BF16)
    cols = lax.dot_general(rows, sel, (((0,), (0,)), ((), ())), preferred_element_type=F32)
    out.update(a_r=li_r - b_r, cols=cols, mask_ts=mask_ts)
    return out


def _mlstm_heads(q_ref, kt_ref, v_ref, h_ref, c_sc, n_sc, m_sc, gates, *, chunk):
    with_output = h_ref is not None
    m_old, m_new, decay, k_scale = gates["m_old"], gates["m_new"], gates["decay"], gates["k_scale"]
    if with_output:
        a_r, cols, mask_ts = gates["a_r"], gates["cols"], gates["mask_ts"]
    ones_rows = jnp.ones((8, chunk), BF16)
    wide = lambda a, n: jnp.concatenate([a] * (n // 128), axis=1)
    for hd in range(HEADS):
        sl = slice(hd * HEAD_DIM, (hd + 1) * HEAD_DIM)
        kt = kt_ref[sl, :]
        v = v_ref[:, sl]
        c_old = c_sc[hd]
        n_old = n_sc[hd]
        kw_t = kt * k_scale[hd:hd + 1, :]
        if with_output:
            q = q_ref[:, sl]
            mm_c = cols[:, hd * 128:(hd + 1) * 128]
            bm_c = cols[:, (HEADS + hd) * 128:(HEADS + hd + 1) * 128]
            decay_mat = jnp.exp(jnp.where(mask_ts, a_r[hd:hd + 1, :] - wide(mm_c, chunk), NEG_BIG))
            s_f = _dot(q, kt) * decay_mat
            w_inter = jnp.exp(m_old[hd:hd + 1, :] - mm_c)
            den = (w_inter * jnp.sum(q.astype(F32) * n_old[0:1, :], axis=1, keepdims=True)
                   + jnp.sum(s_f, axis=1, keepdims=True))
            q_w = q * wide(w_inter, HEAD_DIM).astype(BF16)
            num = _dot(jnp.concatenate([q_w, s_f.astype(BF16)], axis=1),
                       jnp.concatenate([c_old.astype(BF16), v], axis=0))
            inv = 1.0 / jnp.maximum(jnp.abs(den), jnp.exp(-bm_c))
            h_ref[:, sl] = (num * wide(inv, HEAD_DIM)).astype(BF16)
        c_sc[hd] = decay[hd:hd + 1, :] * c_old + _dot(kw_t, v)
        n_sc[hd] = decay[hd:hd + 1, :] * n_old + lax.dot_general(
            ones_rows, kw_t, (((1,), (1,)), ((), ())), preferred_element_type=F32)
        m_sc[hd:hd + 1, :] = jnp.broadcast_to(m_new[hd:hd + 1, :], (1, 128))


def _mlstm_kernel(*refs, with_output, chunk, subs):
    n_in = 5 if with_output else 4
    ins = [refs[0:n_in], refs[n_in:2 * n_in]]
    rest = refs[2 * n_in:]
    if with_output:
        c0_ref, n0_ref, m0_ref, hf_ref, hb_ref, c_sc, n_sc, m_sc = rest
        h_refs = (hf_ref, hb_ref)
    else:
        co_ref, no_ref, mo_ref, c_sc, n_sc, m_sc = rest
        h_refs = (None, None)
        ins = [(None,) + tuple(r) for r in ins]
    i = pl.program_id(1)
    nc = pl.num_programs(1)

    @pl.when(i == 0)
    def _():
        if with_output:
            c_sc[...] = c0_ref[...]
            n_sc[...] = n0_ref[...]
            m_sc[...] = m0_ref[...]
        else:
            c_sc[...] = jnp.zeros_like(c_sc)
            n_sc[...] = jnp.zeros_like(n_sc)
            m_sc[...] = jnp.zeros_like(m_sc)

    for sub in range(subs):
        pos = (sub, subs - 1 - sub)
        rows = [pl.ds(pos[d] * chunk, chunk) for d in range(2)]
        view = lambda ref, d: None if ref is None else ref.at[rows[d], :]
        view_t = lambda ref, d: ref.at[:, rows[d]]
        gates = [_mlstm_gates(view_t(ins[d][3], d), view_t(ins[d][4], d), m_sc.at[d], reverse=bool(d),
                              chunk=chunk, with_output=with_output) for d in range(2)]
        for d in range(2):
            _mlstm_heads(view(ins[d][0], d), view_t(ins[d][1], d), view(ins[d][2], d), view(h_refs[d], d),
                         c_sc.at[d], n_sc.at[d], m_sc.at[d], gates[d], chunk=chunk)

    if not with_output:
        @pl.when(i == nc - 1)
        def _():
            co_ref[...] = c_sc[...]
            no_ref[...] = n_sc[...]
            mo_ref[...] = m_sc[...]


def _mlstm(q, kt, v, gt, gx, state):
    with_output = q is not None
    bsz, t_len, _ = v.shape
    chunk = MLSTM_CHUNK
    subs = next(n for n in MLSTM_CHUNKS_PER_STEP if t_len % (n * chunk) == 0)
    blk = subs * chunk
    nc = t_len // blk
    cidx = (lambda i: i, lambda i: nc - 1 - i)
    tile = lambda w, d: pl.BlockSpec((None, blk, w), lambda b, i: (b, cidx[d](i), 0))
    tile_t = lambda w, d: pl.BlockSpec((None, w, blk), lambda b, i: (b, 0, cidx[d](i)))
    st_dims = [(2, HEADS, HEAD_DIM, HEAD_DIM), (2, HEADS, 8, HEAD_DIM), (2, HEADS, 128)]
    st_specs = [pl.BlockSpec((None,) + s, lambda b, i, n=len(s): (b,) + (0,) * n) for s in st_dims]
    st_shapes = [jax.ShapeDtypeStruct((bsz,) + s, F32) for s in st_dims]
    in_specs, args = [], []
    for d in range(2):
        in_specs += ([tile(W_BRANCH, d)] if with_output else []) + [
            tile_t(W_BRANCH, d), tile(W_BRANCH, d), tile_t(N_GATES, d), tile_t(N_GATES, d)]
        args += ([q] if with_output else []) + [kt, v, gt, gx]
    if with_output:
        in_specs, args = in_specs + st_specs, args + list(state)
        out_specs = [tile(W_BRANCH, 0), tile(W_BRANCH, 1)]
        out_shape = [jax.ShapeDtypeStruct((bsz, t_len, W_BRANCH), BF16)] * 2
    else:
        out_specs, out_shape = st_specs, st_shapes
    return pl.pallas_call(
        functools.partial(_mlstm_kernel, with_output=with_output, chunk=chunk, subs=subs),
        grid=(bsz, nc),
        in_specs=in_specs,
        out_specs=out_specs,
        out_shape=out_shape,
        scratch_shapes=[pltpu.VMEM(s, F32) for s in st_dims],
        compiler_params=pltpu.CompilerParams(dimension_semantics=("parallel", "arbitrary"),
                                             vmem_limit_bytes=VMEM_LIMIT),
        name="mlstm_out" if with_output else "mlstm_state",
    )(*args)


def _s5_kernel(vc_ref, vx_ref, g_ref, m_ref, p_ref, a_ref, y_ref, gu_sc, s_sc, *,
               nk_ctx, n_rc, bsz, rblk):
    rows_ctx = nk_ctx * bsz
    rows_x = n_rc * bsz * GRID_W

    def increments(v_ref, r0, r1):
        return _dot(v_ref[0, r0:r1, :], g_ref[0]) + _dot(v_ref[1, r0:r1, :], g_ref[1])

    inc = increments(vc_ref, 0, rows_ctx)
    for comp in range(4):
        gu_sc[comp, 0:rows_ctx, :] = inc[:, comp * 128:(comp + 1) * 128]
    for r0 in range(0, rows_x, rblk):
        inc = increments(vx_ref, r0, r0 + rblk)
        for run in range(rblk // GRID_W):
            dst = rows_ctx + (r0 // GRID_W + run) * S5_ROW_PITCH
            for comp in range(4):
                gu_sc[comp, dst:dst + GRID_W, :] = (
                    inc[run * GRID_W:(run + 1) * GRID_W, comp * 128:(comp + 1) * 128])

    a = a_ref[...]
    a_pow = [jnp.broadcast_to(a[:, comp * 128:(comp + 1) * 128], (bsz, 128)) for comp in range(4)]
    zero = jnp.zeros((bsz, 128), F32)

    def cmul(x_r, x_i, y_r, y_i):
        return x_r * y_r - x_i * y_i, x_r * y_i + x_i * y_r

    a_sq = [cmul(a_pow[2 * d], a_pow[2 * d + 1], a_pow[2 * d], a_pow[2 * d + 1]) for d in range(2)]

    def step(rows, carry, direction):
        s_r, s_i = carry
        inc_r = gu_sc[2 * direction, rows, :]
        inc_i = gu_sc[2 * direction + 1, rows, :]
        gu_sc[2 * direction, rows, :] = s_r
        gu_sc[2 * direction + 1, rows, :] = s_i
        p_r, p_i = cmul(a_pow[2 * direction], a_pow[2 * direction + 1], s_r, s_i)
        return p_r + inc_r, p_i + inc_i

    def step2(rows0, rows1, carry, direction):
        s_r, s_i = carry
        a_r, a_i = a_pow[2 * direction], a_pow[2 * direction + 1]
        inc0_r, inc0_i = gu_sc[2 * direction, rows0, :], gu_sc[2 * direction + 1, rows0, :]
        inc1_r, inc1_i = gu_sc[2 * direction, rows1, :], gu_sc[2 * direction + 1, rows1, :]
        gu_sc[2 * direction, rows0, :] = s_r
        gu_sc[2 * direction + 1, rows0, :] = s_i
        m_r, m_i = cmul(a_r, a_i, s_r, s_i)
        gu_sc[2 * direction, rows1, :] = m_r + inc0_r
        gu_sc[2 * direction + 1, rows1, :] = m_i + inc0_i
        c_r, c_i = cmul(a_r, a_i, inc0_r, inc0_i)
        q_r, q_i = cmul(a_sq[direction][0], a_sq[direction][1], s_r, s_i)
        return q_r + (c_r + inc1_r), q_i + (c_i + inc1_i)

    def ctx_rows(k):
        return pl.ds(pl.multiple_of(k * bsz, bsz), bsz)

    def x_rows(w, rc):
        return pl.ds(rows_ctx + rc * (bsz * S5_ROW_PITCH) + w, bsz, stride=S5_ROW_PITCH)

    def ctx_body(k, carry):
        return step(ctx_rows(k), carry[0], 0), step(ctx_rows(nk_ctx - 1 - k), carry[1], 1)

    def x_body(w, carry):
        c_f, c_b = carry
        w_b = GRID_W - 1 - w
        for rc in range(0, n_rc - 1, 2):
            c_f = step2(x_rows(w, rc), x_rows(w, rc + 1), c_f, 0)
            c_b = step2(x_rows(w_b, n_rc - 1 - rc), x_rows(w_b, n_rc - 2 - rc), c_b, 1)
        if n_rc % 2:
            c_f = step(x_rows(w, n_rc - 1), c_f, 0)
            c_b = step(x_rows(w_b, 0), c_b, 1)
        return c_f, c_b

    carry = lax.fori_loop(0, nk_ctx, ctx_body, ((zero, zero), (zero, zero)))
    lax.fori_loop(0, GRID_W, x_body, carry)

    for r0 in range(0, rows_x, rblk):
        r1 = r0 + rblk
        for run in range(rblk // GRID_W):
            src = rows_ctx + (r0 // GRID_W + run) * S5_ROW_PITCH
            for comp in range(4):
                s_sc[r0 + run * GRID_W:r0 + (run + 1) * GRID_W, comp * 128:(comp + 1) * 128] = (
                    gu_sc[comp, src:src + GRID_W, :].astype(BF16))
        for gg in range(2):
            y_ref[gg, r0:r1, :] = (_dot(vx_ref[gg, r0:r1, :], m_ref[gg])
                                   + _dot(s_sc[r0:r1, :], p_ref[gg])).astype(BF16)


def _s5(v_ctx, v_x, g_all, m_all, p_all, a16, bsz):
    rows_ctx, rows_x = v_ctx.shape[1], v_x.shape[1]
    lanes = S5_SUB * S5_GC
    return pl.pallas_call(
        functools.partial(_s5_kernel, nk_ctx=rows_ctx // bsz, n_rc=rows_x // (bsz * GRID_W),
                          bsz=bsz, rblk=512),
        grid=(S5_GROUPS // 2,),
        in_specs=[pl.BlockSpec((2, rows_ctx, lanes), lambda j: (j, 0, 0)),
                  pl.BlockSpec((2, rows_x, lanes), lambda j: (j, 0, 0)),
                  pl.BlockSpec((2, lanes, 512), lambda j: (j, 0, 0)),
                  pl.BlockSpec((2, lanes, lanes), lambda j: (j, 0, 0)),
                  pl.BlockSpec((2, 512, lanes), lambda j: (j, 0, 0)),
                  pl.BlockSpec((None, 1, 512), lambda j: (j, 0, 0))],
        out_specs=pl.BlockSpec((2, rows_x, lanes), lambda j: (j, 0, 0)),
        out_shape=jax.ShapeDtypeStruct((S5_GROUPS, rows_x, lanes), BF16),
        scratch_shapes=[pltpu.VMEM((4, rows_ctx + (rows_x // GRID_W) * S5_ROW_PITCH, 128), F32),
                        pltpu.VMEM((rows_x, 512), BF16)],
        compiler_params=pltpu.CompilerParams(dimension_semantics=("parallel",),
                                             vmem_limit_bytes=VMEM_LIMIT),
        name="s5",
    )(v_ctx, v_x, g_all, m_all, p_all, a16)


def _s5_prep_kernel(lr_ref, lc_ref, bt_ref, ct_ref, d_ref, m_ref, g_ref, p_ref, a_ref):
    hp = lax.Precision.HIGHEST
    n_s, lanes = S5_SUB, S5_SUB * S5_GC
    lane128 = lax.broadcasted_iota(jnp.int32, (n_s, 128), 1)
    blk_of_lane = lax.broadcasted_iota(jnp.int32, (128, lanes), 1) // S5_GC
    g_types, p_types, a16, k_rows = [], [], [], []
    for d in range(2):
        a_r, a_i, log_dt = lr_ref[d, 0:1, :], lr_ref[d, 1:2, :], lr_ref[d, 2:3, :]
        dt = jnp.exp(log_dt)
        lam_r, lam_i = a_r * dt, a_i * dt
        steps = lax.broadcasted_iota(jnp.int32, (24, 128), 0).astype(F32)
        mag = jnp.exp(lam_r * steps)
        pw_r, pw_i = mag * jnp.cos(lam_i * steps), mag * jnp.sin(lam_i * steps)
        nr, ni = pw_r[1:2] - 1.0, pw_i[1:2]
        den = a_r * a_r + a_i * a_i
        co_r, co_i = (nr * a_r + ni * a_i) / den, (ni * a_r - nr * a_i) / den
        b_r = jnp.concatenate([bt_ref[d, 0]] * n_s, axis=0)
        b_i = jnp.concatenate([bt_ref[d, 1]] * n_s, axis=0)
        bb_r, bb_i = co_r * b_r - co_i * b_i, co_r * b_i + co_i * b_r
        order = [n_s - 1 - i for i in range(n_s)] if d == 0 else list(range(n_s))
        pg_r = jnp.concatenate([jnp.broadcast_to(pw_r[n:n + 1], (S5_GC, 128)) for n in order], axis=0)
        pg_i = jnp.concatenate([jnp.broadcast_to(pw_i[n:n + 1], (S5_GC, 128)) for n in order], axis=0)
        g_types += [bb_r * pg_r - bb_i * pg_i, bb_r * pg_i + bb_i * pg_r]
        a16 += [pw_r[n_s:n_s + 1], pw_i[n_s:n_s + 1]]
        x0 = slice((n_s - 1) * S5_GC, n_s * S5_GC) if d == 0 else slice(0, S5_GC)
        x_r, x_i = bb_r[x0], bb_i[x0]
        lhs_r = jnp.concatenate([jnp.where(lane128 < S5_STATE, x_r, 0.0),
                                 jnp.where(lane128 < S5_STATE, 0.0, x_r)], axis=0)
        lhs_i = jnp.concatenate([jnp.where(lane128 < S5_STATE, x_i, 0.0),
                                 jnp.where(lane128 < S5_STATE, 0.0, x_i)], axis=0)
        dt_c = jnp.exp(lc_ref[d, 2])
        lam_rc, lam_ic = lc_ref[d, 0] * dt_c, lc_ref[d, 1] * dt_c
        mag_1 = jnp.exp(lam_rc)
        a1_r, a1_i = mag_1 * jnp.cos(lam_ic), mag_1 * jnp.sin(lam_ic)
        a1_r = jnp.concatenate([a1_r, a1_r], axis=1)
        a1_i = jnp.concatenate([a1_i, a1_i], axis=1)
        n_y = blk_of_lane if d == 0 else n_s - 1 - blk_of_lane
        ypw_r, ypw_i = jnp.ones_like(a1_r), jnp.zeros_like(a1_r)
        sq_r, sq_i = a1_r, a1_i
        for bit in (1, 2, 4, 8):
            on = (n_y & bit) != 0
            ypw_r, ypw_i = (jnp.where(on, ypw_r * sq_r - ypw_i * sq_i, ypw_r),
                            jnp.where(on, ypw_r * sq_i + ypw_i * sq_r, ypw_i))
            if bit < 8:
                sq_r, sq_i = sq_r * sq_r - sq_i * sq_i, 2.0 * sq_r * sq_i
        c_r, c_i = ct_ref[d, 0], ct_ref[d, 1]
        y_r, y_i = c_r * ypw_r - c_i * ypw_i, c_r * ypw_i + c_i * ypw_r
        k_rows.append(jnp.dot(lhs_r, y_r, preferred_element_type=F32, precision=hp)
                      - jnp.dot(lhs_i, y_i, preferred_element_type=F32, precision=hp))
        p_types += [y_r * a1_r - y_i * a1_i, -(y_r * a1_i + y_i * a1_r)]

    a_ref[...] = jnp.concatenate(a16, axis=1)
    lane_g = lax.broadcasted_iota(jnp.int32, (lanes, 128), 1)
    row_p = lax.broadcasted_iota(jnp.int32, (128, lanes), 0)
    lane_k = lax.broadcasted_iota(jnp.int32, (S5_GC, lanes), 1)
    row_m = lax.broadcasted_iota(jnp.int32, (lanes, lanes), 0)
    lane_m = lax.broadcasted_iota(jnp.int32, (lanes, lanes), 1)
    for h in range(2):
        mine_l = (lane_g >= S5_STATE) == bool(h)
        g_ref[h] = jnp.concatenate([jnp.where(mine_l, t, 0.0) for t in g_types], axis=1).astype(BF16)
        mine_r = (row_p >= S5_STATE) == bool(h)
        p_ref[h] = jnp.concatenate([jnp.where(mine_r, t, 0.0) for t in p_types], axis=0).astype(BF16)
        k_f = k_rows[0][h * S5_GC:(h + 1) * S5_GC]
        k_b = k_rows[1][h * S5_GC:(h + 1) * S5_GC]
        blocks = []
        for i in range(n_s):
            up, down = S5_GC * i, S5_GC * (n_s - 1 - i)
            f = k_f if up == 0 else jnp.where(lane_k >= up, pltpu.roll(k_f, up, axis=1), 0.0)
            b = k_b if down == 0 else jnp.where(lane_k < lanes - down,
                                                pltpu.roll(k_b, lanes - down, axis=1), 0.0)
            blocks.append(f + b)
        m = jnp.concatenate(blocks, axis=0) + jnp.where(row_m == lane_m, d_ref[h], 0.0)
        m_ref[h] = m.astype(BF16)


def _s5_prep(a_re, a_im, log_step, b_re, b_im, c_re, c_im, d_skip):
    n_g, n_p, n_c, n_s = S5_GROUPS, S5_STATE, S5_GC, S5_SUB
    lanes = n_s * n_c
    pair = lambda a: jnp.transpose(a.astype(F32).reshape(2, n_g // 2, 2 * n_p), (1, 0, 2))
    lam_row = jnp.stack([pair(a_re), pair(a_im),
                         pair(jnp.broadcast_to(log_step[..., None], a_re.shape))], axis=2)
    lam_col = jnp.broadcast_to(lam_row[..., None], lam_row.shape + (2 * n_p,))
    bt = lambda b: jnp.transpose(b.astype(F32).reshape(2, n_g // 2, 2, n_p, n_c),
                                 (1, 0, 4, 2, 3)).reshape(n_g // 2, 2, n_c, 2 * n_p)
    b_t = jnp.stack([bt(b_re), bt(b_im)], axis=2)
    ct = lambda c: jnp.tile(jnp.transpose(c.astype(F32).reshape(2, n_g // 2, 2, n_c, n_p),
                                          (1, 0, 2, 4, 3)).reshape(n_g // 2, 2, 2 * n_p, n_c),
                            (1, 1, 1, n_s))
    c_t = jnp.stack([ct(c_re), ct(c_im)], axis=2)
    d_row = jnp.tile(d_skip.astype(F32).reshape(n_g // 2, 2, 1, n_c), (1, 1, 1, n_s))
    blk = lambda *s: pl.BlockSpec((None,) + s, lambda j: (j,) + (0,) * len(s))
    grp = lambda *s: pl.BlockSpec((2,) + s, lambda j: (j,) + (0,) * len(s))
    return pl.pallas_call(
        _s5_prep_kernel,
        grid=(n_g // 2,),
        in_specs=[blk(2, 3, 2 * n_p), blk(2, 3, 2 * n_p, 2 * n_p), blk(2, 2, n_c, 2 * n_p),
                  blk(2, 2, 2 * n_p, lanes), blk(2, 1, lanes)],
        out_specs=[grp(lanes, lanes), grp(lanes, 8 * n_p), grp(8 * n_p, lanes), blk(1, 8 * n_p)],
        out_shape=[jax.ShapeDtypeStruct((n_g, lanes, lanes), BF16),
                   jax.ShapeDtypeStruct((n_g, lanes, 8 * n_p), BF16),
                   jax.ShapeDtypeStruct((n_g, 8 * n_p, lanes), BF16),
                   jax.ShapeDtypeStruct((n_g // 2, 1, 8 * n_p), F32)],
        compiler_params=pltpu.CompilerParams(dimension_semantics=("parallel",),
                                             vmem_limit_bytes=VMEM_LIMIT),
        name="s5_prep",
    )(lam_row, lam_col, b_t, c_t, d_row)


def _merge_kernel(hf_ref, hb_ref, o_ref, zm_ref, y_ref, zs_ref, x_ref, mod_ref, mhg_ref,
                  gluw_ref, glub_ref, wout_ref, fg_ref, out_ref, y_sc):
    for lh in range(2):
        for g8 in range(W_BRANCH // 128):
            rows = [y_ref[g8 * 8 + gp, :, lh * 128:(lh + 1) * 128].astype(F32) for gp in range(8)]
            for i, blk in enumerate(_block_transpose8(rows)):
                r0 = (lh * 8 + i) * GRID_W
                y_sc[r0:r0 + GRID_W, g8 * 128:(g8 + 1) * 128] = blk.astype(BF16)

    mhg = mhg_ref[...]
    for r0 in range(0, x_ref.shape[0], MERGE_ROWS):
        rs = slice(r0, r0 + MERGE_ROWS)
        hm = ((hf_ref[rs, :] + hb_ref[rs, :]) * jax.nn.sigmoid(o_ref[rs, :])).astype(F32)
        parts = []
        for hd in range(HEADS):
            sl = slice(hd * HEAD_DIM, (hd + 1) * HEAD_DIM)
            seg = hm[:, sl]
            mu = jnp.mean(seg, axis=-1, keepdims=True)
            dev = seg - mu
            var = jnp.mean(dev * dev, axis=-1, keepdims=True)
            parts.append(dev * lax.rsqrt(var + NORM_EPS) * mhg[:, sl])
        m_out = jnp.concatenate(parts, axis=-1).astype(BF16) * _silu(zm_ref[rs, :])

        y = y_sc[rs, :]
        gl = 0.5 * y * (1.0 + jnp.tanh(0.7978845608028654 * (y + 0.044715 * (y * y * y))))
        gate = jax.nn.sigmoid((_dot(gl, gluw_ref[...]) + glub_ref[...]).astype(BF16))
        s_out = gl * gate * _silu(zs_ref[rs, :])

        mixed = _dot(m_out, wout_ref[0:W_BRANCH, :]) + _dot(s_out, wout_ref[W_BRANCH:2 * W_BRANCH, :])
        xo = x_ref[rs, :] + mod_ref[2:3, :] * mixed
        ms = jnp.mean(xo * xo, axis=-1, keepdims=True)
        out_ref[rs, :] = xo * lax.rsqrt(ms + NORM_EPS) * fg_ref[...]


def _merge(hf, hb, o, zm, y, zs, x, mod, mh_g, glu_w, glu_b, w_out, final_g):
    bsz, t_len, _ = x.shape
    tm = S5_SUB * GRID_W
    tile = pl.BlockSpec((None, tm, D_MODEL), lambda b, i: (b, i, 0))
    y_spec = pl.BlockSpec((S5_GROUPS, GRID_W, S5_SUB * S5_GC), lambda b, i: (0, i * bsz + b, 0))
    const = lambda shape: pl.BlockSpec(shape, lambda b, i: (0,) * len(shape),
                                       pipeline_mode=pl.Buffered(1))
    return pl.pallas_call(
        _merge_kernel,
        grid=(bsz, t_len // tm),
        in_specs=[tile] * 4 + [y_spec, tile, tile,
                               pl.BlockSpec((None, 3, D_MODEL), lambda b, i: (b, 0, 0)),
                               const((1, W_BRANCH)), const((W_BRANCH, W_BRANCH)),
                               const((1, W_BRANCH)), const((2 * W_BRANCH, D_MODEL)),
                               const((1, D_MODEL))],
        out_specs=tile,
        out_shape=jax.ShapeDtypeStruct((bsz, t_len, D_MODEL), F32),
        scratch_shapes=[pltpu.VMEM((tm, W_BRANCH), BF16)],
        compiler_params=pltpu.CompilerParams(dimension_semantics=("parallel", "arbitrary"),
                                             vmem_limit_bytes=VMEM_LIMIT),
        name="merge",
    )(hf, hb, o, zm, y, zs, x, mod, mh_g.reshape(1, -1), glu_w, glu_b.reshape(1, -1), w_out,
      final_g.reshape(1, -1))


def _s5_rows_ctx(u):
    bsz, t_len, _ = u.shape
    a = u.reshape(bsz, t_len // S5_SUB, S5_SUB, S5_GROUPS, S5_GC)
    a = jnp.transpose(a, (3, 1, 0, 2, 4))
    return a.reshape(S5_GROUPS, (t_len // S5_SUB) * bsz, S5_SUB * S5_GC)


def kernel(x, c, ctx, c_ctx, norm_g, ada_w, ada_b, w_in, b_gate, conv_qk, mh_g, s5_a_re, s5_a_im,
           s5_log_step, s5_b_re, s5_b_im, s5_c_re, s5_c_im, s5_d, glu_w, glu_b, w_out, final_g):
    bsz = x.shape[0]
    layer = 0

    cc = jnp.zeros((16, D_MODEL), F32).at[:bsz].set(c).at[bsz].set(c_ctx)
    mod = _ada(cc, ada_w[layer], ada_b[layer]).reshape(16, 3, D_MODEL)
    mod_x = mod[:bsz]
    mod_c = jnp.broadcast_to(mod[bsz][None], (bsz, 3, D_MODEL))

    w = w_in[layer]
    wb = W_BRANCH
    w_qk = w[:, 0:2 * wb].astype(BF16)
    gate0 = 5 * wb
    w_main = jnp.concatenate([w[:, 2 * wb:5 * wb], w[:, gate0 + N_GATES:],
                              w[:, gate0:gate0 + GATE_PAD]], axis=1).astype(BF16)
    b_gate_pad = jnp.pad(b_gate[layer].reshape(1, N_GATES), ((0, 0), (0, GATE_PAD - N_GATES)))

    proj = functools.partial(_inproj, norm_g=norm_g[layer], w_qk=w_qk, w_main=w_main,
                             conv_qk=conv_qk[layer], b_gate_pad=b_gate_pad)
    kt_c, v_c, u_c, gt_c, gx_c = proj(ctx, mod_c, tm=MLSTM_CHUNK, s5_layout=False, state_only=True)
    q_x, kt_x, v_x, o_x, zm_x, u_x, zs_x, gt_x, gx_x = proj(x, mod_x, tm=8 * GRID_W, s5_layout=True)

    ctx_state = _mlstm(None, kt_c, v_c, gt_c, gx_c, None)
    h_f, h_b = _mlstm(q_x, kt_x, v_x, gt_x, gx_x, ctx_state)

    m_all, g_all, p_all, a16 = _s5_prep(
        s5_a_re[layer], s5_a_im[layer], s5_log_step[layer], s5_b_re[layer], s5_b_im[layer],
        s5_c_re[layer], s5_c_im[layer], s5_d[layer])
    y_x = _s5(_s5_rows_ctx(u_c), u_x, g_all, m_all, p_all, a16, bsz)

    return _merge(h_f, h_b, o_x, zm_x, y_x, zs_x, x, mod_x, mh_g[layer], glu_w[layer].astype(BF16),
                  glu_b[layer], w_out[layer].astype(BF16), final_g)
```

```python
import functools

import jax
import jax.numpy as jnp
from jax import lax
from jax.experimental import pallas as pl
from jax.experimental.pallas import tpu as pltpu

F32 = jnp.float32
BF16 = jnp.bfloat16

D_MODEL = 1024
HEADS = 4
HEAD_DIM = 256
W_BRANCH = 1024
S5_GROUPS = 64
S5_GC = 16
S5_STATE = 64
S5_SUB = 16
GRID_W = 64
S5_ROW_PITCH = 72
N_GATES = 16
GATE_PAD = 128
CONV_W = 3
NORM_EPS = 1e-6
MLSTM_CHUNK = 256
MLSTM_CHUNKS_PER_STEP = (4, 2, 1)
MERGE_ROWS = 256
NEG_BIG = -1e30

VMEM_LIMIT = 56 * 1024 * 1024


def _silu(a):
    return a * jax.nn.sigmoid(a)


def _log_sigmoid(a):
    return jnp.minimum(a, 0.0) - jnp.log1p(jnp.exp(-jnp.abs(a)))


def _dot(a, b):
    return jnp.dot(a, b, preferred_element_type=F32)


def _split_bf16(a):
    hi = a.astype(BF16)
    lo = (a - hi.astype(F32)).astype(BF16)
    return hi, lo


def _block_transpose8(rows):
    lane = lax.broadcasted_iota(jnp.int32, rows[0].shape, 1)
    blk = lane // S5_GC
    for d in (4, 2, 1):
        keep = (blk & d) == 0
        new = list(rows)
        for i in range(8):
            if i & d == 0:
                a, b = rows[i], rows[i + d]
                new[i] = jnp.where(keep, a, pltpu.roll(b, d * S5_GC, axis=1))
                new[i + d] = jnp.where(keep, pltpu.roll(a, 128 - d * S5_GC, axis=1), b)
        rows = new
    return rows


def _chunk_gate_rows(gt_blk, reverse):
    chunk = gt_blk.shape[1]
    row = lax.broadcasted_iota(jnp.int32, (chunk, chunk), 0)
    col = lax.broadcasted_iota(jnp.int32, (chunk, chunk), 1)
    mask_st = (row >= col) if reverse else (row <= col)
    tri_st = jnp.where(mask_st, 1.0, 0.0).astype(BF16)
    hi, lo = _split_bf16(gt_blk)
    cum = _dot(hi, tri_st) + _dot(lo, tri_st)
    li0 = 8 if reverse else 0
    b_r = cum[li0 + HEADS:li0 + 2 * HEADS, :]
    run_max = gt_blk[li0:li0 + HEADS, :] - b_r
    lane = lax.broadcasted_iota(jnp.int32, run_max.shape, 1)
    shift = 1
    while shift < chunk:
        if reverse:
            moved = jnp.where(lane < chunk - shift, pltpu.roll(run_max, chunk - shift, axis=1), NEG_BIG)
        else:
            moved = jnp.where(lane >= shift, pltpu.roll(run_max, shift, axis=1), NEG_BIG)
        run_max = jnp.maximum(run_max, moved)
        shift *= 2
    return b_r, run_max


def _ada_kernel(c_ref, w_ref, b_ref, o_ref):
    s = _silu(c_ref[...])
    o_ref[...] = jnp.dot(s, w_ref[...], preferred_element_type=F32,
                         precision=lax.Precision.HIGHEST) + b_ref[...]


def _ada(cc, ada_w, ada_b):
    rows = cc.shape[0]
    n_out = ada_w.shape[1]
    tn = 1024
    return pl.pallas_call(
        _ada_kernel,
        grid=(n_out // tn,),
        in_specs=[pl.BlockSpec((rows, D_MODEL), lambda j: (0, 0)),
                  pl.BlockSpec((D_MODEL, tn), lambda j: (0, j)),
                  pl.BlockSpec((1, tn), lambda j: (0, j))],
        out_specs=pl.BlockSpec((rows, tn), lambda j: (0, j)),
        out_shape=jax.ShapeDtypeStruct((rows, n_out), F32),
        compiler_params=pltpu.CompilerParams(dimension_semantics=("arbitrary",),
                                             vmem_limit_bytes=VMEM_LIMIT),
        name="ada",
    )(cc, ada_w, ada_b.reshape(1, n_out))


def _wprep_kernel(w_ref, wqk_ref, wm_ref):
    wb = W_BRANCH
    gate0 = 5 * wb
    for r0 in range(0, w_ref.shape[0], 128):
        rs = slice(r0, r0 + 128)
        wqk_ref[rs, :] = w_ref[rs, 0:2 * wb].astype(BF16)
        wm_ref[rs, 0:3 * wb] = w_ref[rs, 2 * wb:5 * wb].astype(BF16)
        wm_ref[rs, 3 * wb:5 * wb] = w_ref[rs, gate0 + N_GATES:gate0 + N_GATES + 2 * wb].astype(BF16)
        wm_ref[rs, 5 * wb:5 * wb + GATE_PAD] = w_ref[rs, gate0:gate0 + GATE_PAD].astype(BF16)


def _wprep(w):
    d, n = w.shape
    whole = lambda shape: pl.BlockSpec(shape, lambda i: (0, 0), pipeline_mode=pl.Buffered(1))
    return pl.pallas_call(
        _wprep_kernel,
        grid=(1,),
        in_specs=[whole((d, n))],
        out_specs=[whole((d, 2 * W_BRANCH)), whole((d, 5 * W_BRANCH + GATE_PAD))],
        out_shape=[jax.ShapeDtypeStruct((d, 2 * W_BRANCH), BF16),
                   jax.ShapeDtypeStruct((d, 5 * W_BRANCH + GATE_PAD), BF16)],
        compiler_params=pltpu.CompilerParams(dimension_semantics=("arbitrary",),
                                             vmem_limit_bytes=VMEM_LIMIT),
        name="wprep",
    )(w)


HALO = 16


def _inproj_kernel(x_ref, xp_ref, xn_ref, mod_ref, ng_ref, wqk_ref, wm_ref, conv_ref, bg_ref,
                   *rest, tm, s5_layout, state_only):
    if state_only:
        kt_ref, v_ref, u_ref, gt_ref, gx_ref, lhs_sc = rest
    else:
        q_ref, kt_ref, v_ref, o_ref, zm_ref, u_ref, zs_ref, gt_ref, gx_ref, lhs_sc = rest
    i = pl.program_id(1)
    nt = pl.num_programs(1)
    shift = mod_ref[0:1, :]
    gain = ng_ref[...] * (1.0 + mod_ref[1:2, :])

    def norm_mod(a):
        ms = jnp.mean(a * a, axis=-1, keepdims=True)
        return a * lax.rsqrt(ms + NORM_EPS) * gain + shift

    lhs_sc[0:tm, :] = norm_mod(x_ref[...]).astype(BF16)
    lhs_sc[tm:tm + HALO, :] = norm_mod(xp_ref[...]).astype(BF16)
    lhs_sc[tm + HALO:tm + 2 * HALO, :] = norm_mod(xn_ref[...]).astype(BF16)

    h = lhs_sc[0:tm, :]
    zg = _dot(h, wm_ref[:, 5120:5120 + GATE_PAD]) + bg_ref[...]
    lane = lax.broadcasted_iota(jnp.int32, zg.shape, 1)
    is_forget = jnp.logical_and(lane < N_GATES, (lane % 8) >= 4)
    gates = jnp.where(is_forget, _log_sigmoid(zg), zg)
    gt = gates.T[0:N_GATES, :]
    gt_ref[...] = gt
    for c0 in range(0, tm, MLSTM_CHUNK):
        parts = []
        for reverse in (False, True):
            parts += list(_chunk_gate_rows(gt[:, c0:c0 + MLSTM_CHUNK], reverse))
        gx_ref[:, c0:c0 + MLSTM_CHUNK] = jnp.concatenate(parts, axis=0)

    u = _dot(h, wm_ref[:, 3072:4096])
    if s5_layout:
        for g8 in range(W_BRANCH // 128):
            rows = [u[i * GRID_W:(i + 1) * GRID_W, g8 * 128:(g8 + 1) * 128] for i in range(8)]
            for gp, blk in enumerate(_block_transpose8(rows)):
                u_ref[g8 * 8 + gp] = blk.astype(BF16)
    else:
        u_ref[...] = u.astype(BF16)

    valid_prev = (i > 0).astype(F32)
    valid_next = (i < nt - 1).astype(F32)

    def conv_stage(half):
        cols = slice(half * W_BRANCH, (half + 1) * W_BRANCH)
        z = _dot(lhs_sc[...], wqk_ref[:, cols])
        zc = z[0:tm]
        row8 = lax.broadcasted_iota(jnp.int32, (8, W_BRANCH), 0)
        before = pltpu.roll(zc, 1, axis=0)
        first = jnp.where(row8 == 0, z[tm + HALO - 1:tm + HALO] * valid_prev, before[0:8])
        before = jnp.concatenate([first, before[8:]], axis=0)
        after = pltpu.roll(zc, tm - 1, axis=0)
        final = jnp.where(row8 == 7, z[tm + HALO:tm + HALO + 1] * valid_next, after[tm - 8:tm])
        after = jnp.concatenate([after[:tm - 8], final], axis=0)
        cw = conv_ref[:, cols]
        conv = cw[0:1] * before + cw[1:2] * zc + cw[2:3] * after
        if half == 0:
            q_ref[...] = _silu(conv.astype(BF16))
        else:
            kt_ref[...] = (_silu(conv.astype(BF16)) * HEAD_DIM ** -0.5).T

    v_ref[...] = _dot(h, wm_ref[:, 0:1024]).astype(BF16)
    if not state_only:
        conv_stage(0)
        o_ref[...] = _dot(h, wm_ref[:, 1024:2048]).astype(BF16)
    conv_stage(1)
    if not state_only:
        zm_ref[...] = _dot(h, wm_ref[:, 2048:3072]).astype(BF16)
        zs_ref[...] = _dot(h, wm_ref[:, 4096:5120]).astype(BF16)


def _inproj(x, mod, norm_g, w_qk, w_main, conv_qk, b_gate_pad, tm, s5_layout, state_only=False):
    bsz, t_len, _ = x.shape
    nt = t_len // tm
    nhb = t_len // HALO
    tok = lambda w, dt: jax.ShapeDtypeStruct((bsz, t_len, w), dt)
    tile = lambda w: pl.BlockSpec((None, tm, w), lambda b, i: (b, i, 0))
    tile_t = lambda w: pl.BlockSpec((None, w, tm), lambda b, i: (b, 0, i))
    const = lambda shape: pl.BlockSpec(shape, lambda b, i: (0,) * len(shape),
                                       pipeline_mode=pl.Buffered(1))
    if s5_layout:
        assert tm == 8 * GRID_W and t_len % (2 * tm) == 0
        u_spec = pl.BlockSpec((S5_GROUPS, GRID_W, 128), lambda b, i: (0, (i // 2) * bsz + b, i % 2))
        u_shape = jax.ShapeDtypeStruct((S5_GROUPS, (t_len // (2 * tm)) * bsz * GRID_W, 256), BF16)
    else:
        u_spec, u_shape = tile(W_BRANCH), tok(W_BRANCH, BF16)
    kt_shape = jax.ShapeDtypeStruct((bsz, W_BRANCH, t_len), BF16)
    gt_shape = jax.ShapeDtypeStruct((bsz, N_GATES, t_len), F32)
    if state_only:
        out_specs = [tile_t(W_BRANCH), tile(W_BRANCH), u_spec, tile_t(N_GATES), tile_t(N_GATES)]
        out_shape = [kt_shape, tok(W_BRANCH, BF16), u_shape, gt_shape, gt_shape]
    else:
        out_specs = [tile(W_BRANCH), tile_t(W_BRANCH)] + [tile(W_BRANCH)] * 3 + [
            u_spec, tile(W_BRANCH), tile_t(N_GATES), tile_t(N_GATES)]
        out_shape = [tok(W_BRANCH, BF16), kt_shape] + [tok(W_BRANCH, BF16)] * 3 + [
            u_shape, tok(W_BRANCH, BF16), gt_shape, gt_shape]
    return pl.pallas_call(
        functools.partial(_inproj_kernel, tm=tm, s5_layout=s5_layout, state_only=state_only),
        grid=(bsz, nt),
        in_specs=[
            tile(D_MODEL),
            pl.BlockSpec((None, HALO, D_MODEL),
                         lambda b, i: (b, jnp.maximum(i * (tm // HALO) - 1, 0), 0)),
            pl.BlockSpec((None, HALO, D_MODEL),
                         lambda b, i: (b, jnp.minimum((i + 1) * (tm // HALO), nhb - 1), 0)),
            pl.BlockSpec((None, 3, D_MODEL), lambda b, i: (b, 0, 0)),
            const((1, D_MODEL)),
            const((D_MODEL, 2 * W_BRANCH)),
            const((D_MODEL, 5 * W_BRANCH + GATE_PAD)),
            const((CONV_W, 2 * W_BRANCH)),
            const((1, GATE_PAD)),
        ],
        out_specs=out_specs,
        out_shape=out_shape,
        scratch_shapes=[pltpu.VMEM((tm + 2 * HALO, D_MODEL), BF16)],
        compiler_params=pltpu.CompilerParams(dimension_semantics=("parallel", "arbitrary"),
                                             vmem_limit_bytes=VMEM_LIMIT),
        name="inproj",
    )(x, x, x, mod, norm_g.reshape(1, D_MODEL), w_qk, w_main, conv_qk, b_gate_pad)


def _mlstm_gates(gt_ref, gx_ref, m_sc, *, reverse, chunk, with_output):
    li0 = 8 if reverse else 0
    last = 0 if reverse else chunk - 1
    li_r = gt_ref[li0:li0 + HEADS, :]
    b_r = gx_ref[li0:li0 + HEADS, :]
    b_last = b_r[:, last:last + 1]
    m_old = m_sc[:, 0:1]
    g_r = b_last - b_r + li_r
    m_new = jnp.maximum(b_last + m_old, jnp.max(g_r, axis=1, keepdims=True))
    out = dict(m_old=m_old, m_new=m_new, decay=jnp.exp(b_last + m_old - m_new),
               k_scale=jnp.exp(g_r - m_new).astype(BF16))
    if not with_output:
        return out
    row = lax.broadcasted_iota(jnp.int32, (chunk, chunk), 0)
    col = lax.broadcasted_iota(jnp.int32, (chunk, chunk), 1)
    mask_ts = (col >= row) if reverse else (col <= row)
    mm_r = jnp.maximum(gx_ref[li0 + HEADS:li0 + 2 * HEADS, :], m_old).astype(BF16)
    b_hi, b_lo = _split_bf16(b_r)
    rows = jnp.concatenate([mm_r, b_hi, b_lo, jnp.zeros_like(b_hi)], axis=0)
    sel_r = lax.broadcasted_iota(jnp.int32, (4 * HEADS, 2 * HEADS * 128), 0)
    sel_c = lax.broadcasted_iota(jnp.int32, (4 * HEADS, 2 * HEADS * 128), 1) // 128
    head_r = sel_r % HEADS
    pick = jnp.logical_or(jnp.logical_and(sel_r < HEADS, sel_c % HEADS == head_r),
                          jnp.logical_and(jnp.logical_and(sel_r >= HEADS, sel_r < 3 * HEADS),
                                          sel_c == HEADS + head_r))
    sel = jnp.where(pick, 1.0, 0.0).astype(BF16)
    cols = lax.dot_general(rows, sel, (((0,), (0,)), ((), ())), preferred_element_type=F32)
    out.update(a_r=li_r - b_r, cols=cols, mask_ts=mask_ts)
    return out


def _mlstm_heads(q_ref, kt_ref, v_ref, h_ref, c_sc, n_sc, m_sc, gates, *, chunk):
    with_output = h_ref is not None
    m_old, m_new, decay, k_scale = gates["m_old"], gates["m_new"], gates["decay"], gates["k_scale"]
    if with_output:
        a_r, cols, mask_ts = gates["a_r"], gates["cols"], gates["mask_ts"]
    ones_rows = jnp.ones((8, chunk), BF16)
    wide = lambda a, n: jnp.concatenate([a] * (n // 128), axis=1)
    for hd in range(HEADS):
        sl = slice(hd * HEAD_DIM, (hd + 1) * HEAD_DIM)
        kt = kt_ref[sl, :]
        v = v_ref[:, sl]
        c_old = c_sc[hd]
        n_old = n_sc[hd]
        kw_t = kt * k_scale[hd:hd + 1, :]
        if with_output:
            q = q_ref[:, sl]
            mm_c = cols[:, hd * 128:(hd + 1) * 128]
            bm_c = cols[:, (HEADS + hd) * 128:(HEADS + hd + 1) * 128]
            decay_mat = jnp.exp(jnp.where(mask_ts, a_r[hd:hd + 1, :] - wide(mm_c, chunk), NEG_BIG))
            s_f = _dot(q, kt) * decay_mat
            w_inter = jnp.exp(m_old[hd:hd + 1, :] - mm_c)
            den = (w_inter * jnp.sum(q.astype(F32) * n_old[0:1, :], axis=1, keepdims=True)
                   + jnp.sum(s_f, axis=1, keepdims=True))
            q_w = q * wide(w_inter, HEAD_DIM).astype(BF16)
            num = _dot(jnp.concatenate([q_w, s_f.astype(BF16)], axis=1),
                       jnp.concatenate([c_old.astype(BF16), v], axis=0))
            inv = 1.0 / jnp.maximum(jnp.abs(den), jnp.exp(-bm_c))
            h_ref[:, sl] = (num * wide(inv, HEAD_DIM)).astype(BF16)
        c_sc[hd] = decay[hd:hd + 1, :] * c_old + _dot(kw_t, v)
        n_sc[hd] = decay[hd:hd + 1, :] * n_old + lax.dot_general(
            ones_rows, kw_t, (((1,), (1,)), ((), ())), preferred_element_type=F32)
        m_sc[hd:hd + 1, :] = jnp.broadcast_to(m_new[hd:hd + 1, :], (1, 128))


def _mlstm_kernel(*refs, with_output, chunk, subs):
    n_in = 5 if with_output else 4
    ins = [refs[0:n_in], refs[n_in:2 * n_in]]
    rest = refs[2 * n_in:]
    if with_output:
        c0_ref, n0_ref, m0_ref, hf_ref, hb_ref, c_sc, n_sc, m_sc = rest
        h_refs = (hf_ref, hb_ref)
    else:
        co_ref, no_ref, mo_ref, c_sc, n_sc, m_sc = rest
        h_refs = (None, None)
        ins = [(None,) + tuple(r) for r in ins]
    i = pl.program_id(1)
    nc = pl.num_programs(1)

    @pl.when(i == 0)
    def _():
        if with_output:
            c_sc[...] = c0_ref[...]
            n_sc[...] = n0_ref[...]
            m_sc[...] = m0_ref[...]
        else:
            c_sc[...] = jnp.zeros_like(c_sc)
            n_sc[...] = jnp.zeros_like(n_sc)
            m_sc[...] = jnp.zeros_like(m_sc)

    for sub in range(subs):
        pos = (sub, subs - 1 - sub)
        rows = [pl.ds(pos[d] * chunk, chunk) for d in range(2)]
        view = lambda ref, d: None if ref is None else ref.at[rows[d], :]
        view_t = lambda ref, d: ref.at[:, rows[d]]
        gates = [_mlstm_gates(view_t(ins[d][3], d), view_t(ins[d][4], d), m_sc.at[d], reverse=bool(d),
                              chunk=chunk, with_output=with_output) for d in range(2)]
        for d in range(2):
            _mlstm_heads(view(ins[d][0], d), view_t(ins[d][1], d), view(ins[d][2], d), view(h_refs[d], d),
                         c_sc.at[d], n_sc.at[d], m_sc.at[d], gates[d], chunk=chunk)

    if not with_output:
        @pl.when(i == nc - 1)
        def _():
            co_ref[...] = c_sc[...]
            no_ref[...] = n_sc[...]
            mo_ref[...] = m_sc[...]


def _mlstm(q, kt, v, gt, gx, state):
    with_output = q is not None
    bsz, t_len, _ = v.shape
    chunk = MLSTM_CHUNK
    subs = next(n for n in MLSTM_CHUNKS_PER_STEP if t_len % (n * chunk) == 0)
    blk = subs * chunk
    nc = t_len // blk
    cidx = (lambda i: i, lambda i: nc - 1 - i)
    tile = lambda w, d: pl.BlockSpec((None, blk, w), lambda b, i: (b, cidx[d](i), 0))
    tile_t = lambda w, d: pl.BlockSpec((None, w, blk), lambda b, i: (b, 0, cidx[d](i)))
    st_dims = [(2, HEADS, HEAD_DIM, HEAD_DIM), (2, HEADS, 8, HEAD_DIM), (2, HEADS, 128)]
    st_specs = [pl.BlockSpec((None,) + s, lambda b, i, n=len(s): (b,) + (0,) * n) for s in st_dims]
    st_shapes = [jax.ShapeDtypeStruct((bsz,) + s, F32) for s in st_dims]
    in_specs, args = [], []
    for d in range(2):
        in_specs += ([tile(W_BRANCH, d)] if with_output else []) + [
            tile_t(W_BRANCH, d), tile(W_BRANCH, d), tile_t(N_GATES, d), tile_t(N_GATES, d)]
        args += ([q] if with_output else []) + [kt, v, gt, gx]
    if with_output:
        in_specs, args = in_specs + st_specs, args + list(state)
        out_specs = [tile(W_BRANCH, 0), tile(W_BRANCH, 1)]
        out_shape = [jax.ShapeDtypeStruct((bsz, t_len, W_BRANCH), BF16)] * 2
    else:
        out_specs, out_shape = st_specs, st_shapes
    return pl.pallas_call(
        functools.partial(_mlstm_kernel, with_output=with_output, chunk=chunk, subs=subs),
        grid=(bsz, nc),
        in_specs=in_specs,
        out_specs=out_specs,
        out_shape=out_shape,
        scratch_shapes=[pltpu.VMEM(s, F32) for s in st_dims],
        compiler_params=pltpu.CompilerParams(dimension_semantics=("parallel", "arbitrary"),
                                             vmem_limit_bytes=VMEM_LIMIT),
        name="mlstm_out" if with_output else "mlstm_state",
    )(*args)


def _s5_kernel(vc_ref, vx_ref, g_ref, m_ref, p_ref, a_ref, y_ref, gu_sc, s_sc, *,
               nk_ctx, n_rc, bsz, rblk):
    rows_ctx = nk_ctx * bsz
    rows_x = n_rc * bsz * GRID_W

    def increments(v_ref, r0, r1):
        return _dot(v_ref[0, r0:r1, :], g_ref[0]) + _dot(v_ref[1, r0:r1, :], g_ref[1])

    inc = increments(vc_ref, 0, rows_ctx)
    for comp in range(4):
        gu_sc[comp, 0:rows_ctx, :] = inc[:, comp * 128:(comp + 1) * 128]
    for r0 in range(0, rows_x, rblk):
        inc = increments(vx_ref, r0, r0 + rblk)
        for run in range(rblk // GRID_W):
            dst = rows_ctx + (r0 // GRID_W + run) * S5_ROW_PITCH
            for comp in range(4):
                gu_sc[comp, dst:dst + GRID_W, :] = (
                    inc[run * GRID_W:(run + 1) * GRID_W, comp * 128:(comp + 1) * 128])

    a = a_ref[...]
    a_pow = [jnp.broadcast_to(a[:, comp * 128:(comp + 1) * 128], (bsz, 128)) for comp in range(4)]
    zero = jnp.zeros((bsz, 128), F32)

    def cmul(x_r, x_i, y_r, y_i):
        return x_r * y_r - x_i * y_i, x_r * y_i + x_i * y_r

    a_sq = [cmul(a_pow[2 * d], a_pow[2 * d + 1], a_pow[2 * d], a_pow[2 * d + 1]) for d in range(2)]

    def step(rows, carry, direction):
        s_r, s_i = carry
        inc_r = gu_sc[2 * direction, rows, :]
        inc_i = gu_sc[2 * direction + 1, rows, :]
        gu_sc[2 * direction, rows, :] = s_r
        gu_sc[2 * direction + 1, rows, :] = s_i
        p_r, p_i = cmul(a_pow[2 * direction], a_pow[2 * direction + 1], s_r, s_i)
        return p_r + inc_r, p_i + inc_i

    def step2(rows0, rows1, carry, direction):
        s_r, s_i = carry
        a_r, a_i = a_pow[2 * direction], a_pow[2 * direction + 1]
        inc0_r, inc0_i = gu_sc[2 * direction, rows0, :], gu_sc[2 * direction + 1, rows0, :]
        inc1_r, inc1_i = gu_sc[2 * direction, rows1, :], gu_sc[2 * direction + 1, rows1, :]
        gu_sc[2 * direction, rows0, :] = s_r
        gu_sc[2 * direction + 1, rows0, :] = s_i
        m_r, m_i = cmul(a_r, a_i, s_r, s_i)
        gu_sc[2 * direction, rows1, :] = m_r + inc0_r
        gu_sc[2 * direction + 1, rows1, :] = m_i + inc0_i
        c_r, c_i = cmul(a_r, a_i, inc0_r, inc0_i)
        q_r, q_i = cmul(a_sq[direction][0], a_sq[direction][1], s_r, s_i)
        return q_r + (c_r + inc1_r), q_i + (c_i + inc1_i)

    def ctx_rows(k):
        return pl.ds(pl.multiple_of(k * bsz, bsz), bsz)

    def x_rows(w, rc):
        return pl.ds(rows_ctx + rc * (bsz * S5_ROW_PITCH) + w, bsz, stride=S5_ROW_PITCH)

    def ctx_body(k, carry):
        return step(ctx_rows(k), carry[0], 0), step(ctx_rows(nk_ctx - 1 - k), carry[1], 1)

    def x_body(w, carry):
        c_f, c_b = carry
        w_b = GRID_W - 1 - w
        for rc in range(0, n_rc - 1, 2):
            c_f = step2(x_rows(w, rc), x_rows(w, rc + 1), c_f, 0)
            c_b = step2(x_rows(w_b, n_rc - 1 - rc), x_rows(w_b, n_rc - 2 - rc), c_b, 1)
        if n_rc % 2:
            c_f = step(x_rows(w, n_rc - 1), c_f, 0)
            c_b = step(x_rows(w_b, 0), c_b, 1)
        return c_f, c_b

    carry = lax.fori_loop(0, nk_ctx, ctx_body, ((zero, zero), (zero, zero)))
    lax.fori_loop(0, GRID_W, x_body, carry)

    for r0 in range(0, rows_x, rblk):
        r1 = r0 + rblk
        for run in range(rblk // GRID_W):
            src = rows_ctx + (r0 // GRID_W + run) * S5_ROW_PITCH
            for comp in range(4):
                s_sc[r0 + run * GRID_W:r0 + (run + 1) * GRID_W, comp * 128:(comp + 1) * 128] = (
                    gu_sc[comp, src:src + GRID_W, :].astype(BF16))
        for gg in range(2):
            y_ref[gg, r0:r1, :] = (_dot(vx_ref[gg, r0:r1, :], m_ref[gg])
                                   + _dot(s_sc[r0:r1, :], p_ref[gg])).astype(BF16)


def _s5(v_ctx, v_x, g_all, m_all, p_all, a16, bsz):
    rows_ctx, rows_x = v_ctx.shape[1], v_x.shape[1]
    lanes = S5_SUB * S5_GC
    return pl.pallas_call(
        functools.partial(_s5_kernel, nk_ctx=rows_ctx // bsz, n_rc=rows_x // (bsz * GRID_W),
                          bsz=bsz, rblk=512),
        grid=(S5_GROUPS // 2,),
        in_specs=[pl.BlockSpec((2, rows_ctx, lanes), lambda j: (j, 0, 0)),
                  pl.BlockSpec((2, rows_x, lanes), lambda j: (j, 0, 0)),
                  pl.BlockSpec((2, lanes, 512), lambda j: (j, 0, 0)),
                  pl.BlockSpec((2, lanes, lanes), lambda j: (j, 0, 0)),
                  pl.BlockSpec((2, 512, lanes), lambda j: (j, 0, 0)),
                  pl.BlockSpec((None, 1, 512), lambda j: (j, 0, 0))],
        out_specs=pl.BlockSpec((2, rows_x, lanes), lambda j: (j, 0, 0)),
        out_shape=jax.ShapeDtypeStruct((S5_GROUPS, rows_x, lanes), BF16),
        scratch_shapes=[pltpu.VMEM((4, rows_ctx + (rows_x // GRID_W) * S5_ROW_PITCH, 128), F32),
                        pltpu.VMEM((rows_x, 512), BF16)],
        compiler_params=pltpu.CompilerParams(dimension_semantics=("parallel",),
                                             vmem_limit_bytes=VMEM_LIMIT),
        name="s5",
    )(v_ctx, v_x, g_all, m_all, p_all, a16)


def _s5_prep_kernel(lr_ref, lc_ref, bt_ref, ct_ref, d_ref, m_ref, g_ref, p_ref, a_ref):
    hp = lax.Precision.HIGHEST
    n_s, lanes = S5_SUB, S5_SUB * S5_GC
    lane128 = lax.broadcasted_iota(jnp.int32, (n_s, 128), 1)
    blk_of_lane = lax.broadcasted_iota(jnp.int32, (128, lanes), 1) // S5_GC
    g_types, p_types, a16, k_rows = [], [], [], []
    for d in range(2):
        a_r, a_i, log_dt = lr_ref[d, 0:1, :], lr_ref[d, 1:2, :], lr_ref[d, 2:3, :]
        dt = jnp.exp(log_dt)
        lam_r, lam_i = a_r * dt, a_i * dt
        steps = lax.broadcasted_iota(jnp.int32, (24, 128), 0).astype(F32)
        mag = jnp.exp(lam_r * steps)
        pw_r, pw_i = mag * jnp.cos(lam_i * steps), mag * jnp.sin(lam_i * steps)
        nr, ni = pw_r[1:2] - 1.0, pw_i[1:2]
        den = a_r * a_r + a_i * a_i
        co_r, co_i = (nr * a_r + ni * a_i) / den, (ni * a_r - nr * a_i) / den
        b_r = jnp.concatenate([bt_ref[d, 0]] * n_s, axis=0)
        b_i = jnp.concatenate([bt_ref[d, 1]] * n_s, axis=0)
        bb_r, bb_i = co_r * b_r - co_i * b_i, co_r * b_i + co_i * b_r
        order = [n_s - 1 - i for i in range(n_s)] if d == 0 else list(range(n_s))
        pg_r = jnp.concatenate([jnp.broadcast_to(pw_r[n:n + 1], (S5_GC, 128)) for n in order], axis=0)
        pg_i = jnp.concatenate([jnp.broadcast_to(pw_i[n:n + 1], (S5_GC, 128)) for n in order], axis=0)
        g_types += [bb_r * pg_r - bb_i * pg_i, bb_r * pg_i + bb_i * pg_r]
        a16 += [pw_r[n_s:n_s + 1], pw_i[n_s:n_s + 1]]
        x0 = slice((n_s - 1) * S5_GC, n_s * S5_GC) if d == 0 else slice(0, S5_GC)
        x_r, x_i = bb_r[x0], bb_i[x0]
        lhs_r = jnp.concatenate([jnp.where(lane128 < S5_STATE, x_r, 0.0),
                                 jnp.where(lane128 < S5_STATE, 0.0, x_r)], axis=0)
        lhs_i = jnp.concatenate([jnp.where(lane128 < S5_STATE, x_i, 0.0),
                                 jnp.where(lane128 < S5_STATE, 0.0, x_i)], axis=0)
        dt_c = jnp.exp(lc_ref[d, 2])
        lam_rc, lam_ic = lc_ref[d, 0] * dt_c, lc_ref[d, 1] * dt_c
        mag_1 = jnp.exp(lam_rc)
        a1_r, a1_i = mag_1 * jnp.cos(lam_ic), mag_1 * jnp.sin(lam_ic)
        a1_r = jnp.concatenate([a1_r, a1_r], axis=1)
        a1_i = jnp.concatenate([a1_i, a1_i], axis=1)
        n_y = blk_of_lane if d == 0 else n_s - 1 - blk_of_lane
        ypw_r, ypw_i = jnp.ones_like(a1_r), jnp.zeros_like(a1_r)
        sq_r, sq_i = a1_r, a1_i
        for bit in (1, 2, 4, 8):
            on = (n_y & bit) != 0
            ypw_r, ypw_i = (jnp.where(on, ypw_r * sq_r - ypw_i * sq_i, ypw_r),
                            jnp.where(on, ypw_r * sq_i + ypw_i * sq_r, ypw_i))
            if bit < 8:
                sq_r, sq_i = sq_r * sq_r - sq_i * sq_i, 2.0 * sq_r * sq_i
        c_r, c_i = ct_ref[d, 0], ct_ref[d, 1]
        y_r, y_i = c_r * ypw_r - c_i * ypw_i, c_r * ypw_i + c_i * ypw_r
        k_rows.append(jnp.dot(lhs_r, y_r, preferred_element_type=F32, precision=hp)
                      - jnp.dot(lhs_i, y_i, preferred_element_type=F32, precision=hp))
        p_types += [y_r * a1_r - y_i * a1_i, -(y_r * a1_i + y_i * a1_r)]

    a_ref[...] = jnp.concatenate(a16, axis=1)
    lane_g = lax.broadcasted_iota(jnp.int32, (lanes, 128), 1)
    row_p = lax.broadcasted_iota(jnp.int32, (128, lanes), 0)
    lane_k = lax.broadcasted_iota(jnp.int32, (S5_GC, lanes), 1)
    row_m = lax.broadcasted_iota(jnp.int32, (lanes, lanes), 0)
    lane_m = lax.broadcasted_iota(jnp.int32, (lanes, lanes), 1)
    for h in range(2):
        mine_l = (lane_g >= S5_STATE) == bool(h)
        g_ref[h] = jnp.concatenate([jnp.where(mine_l, t, 0.0) for t in g_types], axis=1).astype(BF16)
        mine_r = (row_p >= S5_STATE) == bool(h)
        p_ref[h] = jnp.concatenate([jnp.where(mine_r, t, 0.0) for t in p_types], axis=0).astype(BF16)
        k_f = k_rows[0][h * S5_GC:(h + 1) * S5_GC]
        k_b = k_rows[1][h * S5_GC:(h + 1) * S5_GC]
        blocks = []
        for i in range(n_s):
            up, down = S5_GC * i, S5_GC * (n_s - 1 - i)
            f = k_f if up == 0 else jnp.where(lane_k >= up, pltpu.roll(k_f, up, axis=1), 0.0)
            b = k_b if down == 0 else jnp.where(lane_k < lanes - down,
                                                pltpu.roll(k_b, lanes - down, axis=1), 0.0)
            blocks.append(f + b)
        m = jnp.concatenate(blocks, axis=0) + jnp.where(row_m == lane_m, d_ref[h], 0.0)
        m_ref[h] = m.astype(BF16)


def _s5_prep(a_re, a_im, log_step, b_re, b_im, c_re, c_im, d_skip):
    n_g, n_p, n_c, n_s = S5_GROUPS, S5_STATE, S5_GC, S5_SUB
    lanes = n_s * n_c
    pair = lambda a: jnp.transpose(a.astype(F32).reshape(2, n_g // 2, 2 * n_p), (1, 0, 2))
    lam_row = jnp.stack([pair(a_re), pair(a_im),
                         pair(jnp.broadcast_to(log_step[..., None], a_re.shape))], axis=2)
    lam_col = jnp.broadcast_to(lam_row[..., None], lam_row.shape + (2 * n_p,))
    bt = lambda b: jnp.transpose(b.astype(F32).reshape(2, n_g // 2, 2, n_p, n_c),
                                 (1, 0, 4, 2, 3)).reshape(n_g // 2, 2, n_c, 2 * n_p)
    b_t = jnp.stack([bt(b_re), bt(b_im)], axis=2)
    ct = lambda c: jnp.tile(jnp.transpose(c.astype(F32).reshape(2, n_g // 2, 2, n_c, n_p),
                                          (1, 0, 2, 4, 3)).reshape(n_g // 2, 2, 2 * n_p, n_c),
                            (1, 1, 1, n_s))
    c_t = jnp.stack([ct(c_re), ct(c_im)], axis=2)
    d_row = jnp.tile(d_skip.astype(F32).reshape(n_g // 2, 2, 1, n_c), (1, 1, 1, n_s))
    blk = lambda *s: pl.BlockSpec((None,) + s, lambda j: (j,) + (0,) * len(s))
    grp = lambda *s: pl.BlockSpec((2,) + s, lambda j: (j,) + (0,) * len(s))
    return pl.pallas_call(
        _s5_prep_kernel,
        grid=(n_g // 2,),
        in_specs=[blk(2, 3, 2 * n_p), blk(2, 3, 2 * n_p, 2 * n_p), blk(2, 2, n_c, 2 * n_p),
                  blk(2, 2, 2 * n_p, lanes), blk(2, 1, lanes)],
        out_specs=[grp(lanes, lanes), grp(lanes, 8 * n_p), grp(8 * n_p, lanes), blk(1, 8 * n_p)],
        out_shape=[jax.ShapeDtypeStruct((n_g, lanes, lanes), BF16),
                   jax.ShapeDtypeStruct((n_g, lanes, 8 * n_p), BF16),
                   jax.ShapeDtypeStruct((n_g, 8 * n_p, lanes), BF16),
                   jax.ShapeDtypeStruct((n_g // 2, 1, 8 * n_p), F32)],
        compiler_params=pltpu.CompilerParams(dimension_semantics=("parallel",),
                                             vmem_limit_bytes=VMEM_LIMIT),
        name="s5_prep",
    )(lam_row, lam_col, b_t, c_t, d_row)


def _merge_kernel(hf_ref, hb_ref, o_ref, zm_ref, y_ref, zs_ref, x_ref, mod_ref, mhg_ref,
                  gluw_ref, glub_ref, wout_ref, fg_ref, out_ref, y_sc):
    for lh in range(2):
        for g8 in range(W_BRANCH // 128):
            rows = [y_ref[g8 * 8 + gp, :, lh * 128:(lh + 1) * 128].astype(F32) for gp in range(8)]
            for i, blk in enumerate(_block_transpose8(rows)):
                r0 = (lh * 8 + i) * GRID_W
                y_sc[r0:r0 + GRID_W, g8 * 128:(g8 + 1) * 128] = blk.astype(BF16)

    mhg = mhg_ref[...]
    for r0 in range(0, x_ref.shape[0], MERGE_ROWS):
        rs = slice(r0, r0 + MERGE_ROWS)
        hm = ((hf_ref[rs, :] + hb_ref[rs, :]) * jax.nn.sigmoid(o_ref[rs, :])).astype(F32)
        parts = []
        for hd in range(HEADS):
            sl = slice(hd * HEAD_DIM, (hd + 1) * HEAD_DIM)
            seg = hm[:, sl]
            mu = jnp.mean(seg, axis=-1, keepdims=True)
            dev = seg - mu
            var = jnp.mean(dev * dev, axis=-1, keepdims=True)
            parts.append(dev * lax.rsqrt(var + NORM_EPS) * mhg[:, sl])
        m_out = jnp.concatenate(parts, axis=-1).astype(BF16) * _silu(zm_ref[rs, :])

        y = y_sc[rs, :]
        gl = 0.5 * y * (1.0 + jnp.tanh(0.7978845608028654 * (y + 0.044715 * (y * y * y))))
        gate = jax.nn.sigmoid((_dot(gl, gluw_ref[...]) + glub_ref[...]).astype(BF16))
        s_out = gl * gate * _silu(zs_ref[rs, :])

        mixed = _dot(m_out, wout_ref[0:W_BRANCH, :]) + _dot(s_out, wout_ref[W_BRANCH:2 * W_BRANCH, :])
        xo = x_ref[rs, :] + mod_ref[2:3, :] * mixed
        ms = jnp.mean(xo * xo, axis=-1, keepdims=True)
        out_ref[rs, :] = xo * lax.rsqrt(ms + NORM_EPS) * fg_ref[...]


def _merge(hf, hb, o, zm, y, zs, x, mod, mh_g, glu_w, glu_b, w_out, final_g):
    bsz, t_len, _ = x.shape
    tm = S5_SUB * GRID_W
    tile = pl.BlockSpec((None, tm, D_MODEL), lambda b, i: (b, i, 0))
    y_spec = pl.BlockSpec((S5_GROUPS, GRID_W, S5_SUB * S5_GC), lambda b, i: (0, i * bsz + b, 0))
    const = lambda shape: pl.BlockSpec(shape, lambda b, i: (0,) * len(shape),
                                       pipeline_mode=pl.Buffered(1))
    return pl.pallas_call(
        _merge_kernel,
        grid=(bsz, t_len // tm),
        in_specs=[tile] * 4 + [y_spec, tile, tile,
                               pl.BlockSpec((None, 3, D_MODEL), lambda b, i: (b, 0, 0)),
                               const((1, W_BRANCH)), const((W_BRANCH, W_BRANCH)),
                               const((1, W_BRANCH)), const((2 * W_BRANCH, D_MODEL)),
                               const((1, D_MODEL))],
        out_specs=tile,
        out_shape=jax.ShapeDtypeStruct((bsz, t_len, D_MODEL), F32),
        scratch_shapes=[pltpu.VMEM((tm, W_BRANCH), BF16)],
        compiler_params=pltpu.CompilerParams(dimension_semantics=("parallel", "arbitrary"),
                                             vmem_limit_bytes=VMEM_LIMIT),
        name="merge",
    )(hf, hb, o, zm, y, zs, x, mod, mh_g.reshape(1, -1), glu_w, glu_b.reshape(1, -1), w_out,
      final_g.reshape(1, -1))


def _s5_rows_ctx(u):
    bsz, t_len, _ = u.shape
    a = u.reshape(bsz, t_len // S5_SUB, S5_SUB, S5_GROUPS, S5_GC)
    a = jnp.transpose(a, (3, 1, 0, 2, 4))
    return a.reshape(S5_GROUPS, (t_len // S5_SUB) * bsz, S5_SUB * S5_GC)


def kernel(x, c, ctx, c_ctx, norm_g, ada_w, ada_b, w_in, b_gate, conv_qk, mh_g, s5_a_re, s5_a_im,
           s5_log_step, s5_b_re, s5_b_im, s5_c_re, s5_c_im, s5_d, glu_w, glu_b, w_out, final_g):
    bsz = x.shape[0]
    layer = 0

    cc = jnp.zeros((16, D_MODEL), F32).at[:bsz].set(c).at[bsz].set(c_ctx)
    mod = _ada(cc, ada_w[layer], ada_b[layer]).reshape(16, 3, D_MODEL)
    mod_x = mod[:bsz]
    mod_c = jnp.broadcast_to(mod[bsz][None], (bsz, 3, D_MODEL))

    w_qk, w_main = _wprep(w_in[layer])
    b_gate_pad = jnp.pad(b_gate[layer].reshape(1, N_GATES), ((0, 0), (0, GATE_PAD - N_GATES)))

    proj = functools.partial(_inproj, norm_g=norm_g[layer], w_qk=w_qk, w_main=w_main,
                             conv_qk=conv_qk[layer], b_gate_pad=b_gate_pad)
    kt_c, v_c, u_c, gt_c, gx_c = proj(ctx, mod_c, tm=MLSTM_CHUNK, s5_layout=False, state_only=True)
    q_x, kt_x, v_x, o_x, zm_x, u_x, zs_x, gt_x, gx_x = proj(x, mod_x, tm=8 * GRID_W, s5_layout=True)

    ctx_state = _mlstm(None, kt_c, v_c, gt_c, gx_c, None)
    h_f, h_b = _mlstm(q_x, kt_x, v_x, gt_x, gx_x, ctx_state)

    m_all, g_all, p_all, a16 = _s5_prep(
        s5_a_re[layer], s5_a_im[layer], s5_log_step[layer], s5_b_re[layer], s5_b_im[layer],
        s5_c_re[layer], s5_c_im[layer], s5_d[layer])
    y_x = _s5(_s5_rows_ctx(u_c), u_x, g_all, m_all, p_all, a16, bsz)

    return _merge(h_f, h_b, o_x, zm_x, y_x, zs_x, x, mod_x, mh_g[layer], glu_w[layer].astype(BF16),
                  glu_b[layer], w_out[layer].astype(BF16), final_g)
```

```python
import functools

import jax
import jax.numpy as jnp
from jax import lax
from jax.experimental import pallas as pl
from jax.experimental.pallas import tpu as pltpu

F32 = jnp.float32
BF16 = jnp.bfloat16

D_MODEL = 1024
HEADS = 4
HEAD_DIM = 256
W_BRANCH = 1024
S5_GROUPS = 64
S5_GC = 16
S5_STATE = 64
S5_SUB = 16
GRID_W = 64
S5_ROW_PITCH = 72
N_GATES = 16
GATE_PAD = 128
CONV_W = 3
NORM_EPS = 1e-6
MLSTM_CHUNK = 256
MLSTM_CHUNKS_PER_STEP = (4, 2, 1)
MERGE_ROWS = 256
NEG_BIG = -1e30

VMEM_LIMIT = 56 * 1024 * 1024


def _silu(a):
    return a * jax.nn.sigmoid(a)


def _log_sigmoid(a):
    return jnp.minimum(a, 0.0) - jnp.log1p(jnp.exp(-jnp.abs(a)))


def _dot(a, b):
    return jnp.dot(a, b, preferred_element_type=F32)


def _split_bf16(a):
    hi = a.astype(BF16)
    lo = (a - hi.astype(F32)).astype(BF16)
    return hi, lo


def _block_transpose8(rows):
    lane = lax.broadcasted_iota(jnp.int32, rows[0].shape, 1)
    blk = lane // S5_GC
    for d in (4, 2, 1):
        keep = (blk & d) == 0
        new = list(rows)
        for i in range(8):
            if i & d == 0:
                a, b = rows[i], rows[i + d]
                new[i] = jnp.where(keep, a, pltpu.roll(b, d * S5_GC, axis=1))
                new[i + d] = jnp.where(keep, pltpu.roll(a, 128 - d * S5_GC, axis=1), b)
        rows = new
    return rows


def _chunk_gate_rows(gt_blk, reverse):
    chunk = gt_blk.shape[1]
    row = lax.broadcasted_iota(jnp.int32, (chunk, chunk), 0)
    col = lax.broadcasted_iota(jnp.int32, (chunk, chunk), 1)
    mask_st = (row >= col) if reverse else (row <= col)
    tri_st = jnp.where(mask_st, 1.0, 0.0).astype(BF16)
    hi, lo = _split_bf16(gt_blk)
    cum = _dot(hi, tri_st) + _dot(lo, tri_st)
    li0 = 8 if reverse else 0
    b_r = cum[li0 + HEADS:li0 + 2 * HEADS, :]
    run_max = gt_blk[li0:li0 + HEADS, :] - b_r
    lane = lax.broadcasted_iota(jnp.int32, run_max.shape, 1)
    shift = 1
    while shift < chunk:
        if reverse:
            moved = jnp.where(lane < chunk - shift, pltpu.roll(run_max, chunk - shift, axis=1), NEG_BIG)
        else:
            moved = jnp.where(lane >= shift, pltpu.roll(run_max, shift, axis=1), NEG_BIG)
        run_max = jnp.maximum(run_max, moved)
        shift *= 2
    return b_r, run_max


def _ada_kernel(c_ref, w_ref, b_ref, o_ref):
    s = _silu(c_ref[...])
    o_ref[...] = jnp.dot(s, w_ref[...], preferred_element_type=F32,
                         precision=lax.Precision.HIGHEST) + b_ref[...]


def _ada(cc, ada_w, ada_b):
    rows = cc.shape[0]
    n_out = ada_w.shape[1]
    tn = 1024
    return pl.pallas_call(
        _ada_kernel,
        grid=(n_out // tn,),
        in_specs=[pl.BlockSpec((rows, D_MODEL), lambda j: (0, 0)),
                  pl.BlockSpec((D_MODEL, tn), lambda j: (0, j)),
                  pl.BlockSpec((1, tn), lambda j: (0, j))],
        out_specs=pl.BlockSpec((rows, tn), lambda j: (0, j)),
        out_shape=jax.ShapeDtypeStruct((rows, n_out), F32),
        compiler_params=pltpu.CompilerParams(dimension_semantics=("arbitrary",),
                                             vmem_limit_bytes=VMEM_LIMIT),
        name="ada",
    )(cc, ada_w, ada_b.reshape(1, n_out))


HALO = 16


def _inproj_kernel(x_ref, xp_ref, xn_ref, mod_ref, ng_ref, wqk_ref, wm_ref, conv_ref, bg_ref,
                   *rest, tm, s5_layout, state_only):
    if state_only:
        kt_ref, v_ref, u_ref, gt_ref, gx_ref, lhs_sc, uc_sc = rest
    else:
        q_ref, kt_ref, v_ref, o_ref, zm_ref, u_ref, zs_ref, gt_ref, gx_ref, lhs_sc = rest
    i = pl.program_id(1)
    nt = pl.num_programs(1)
    shift = mod_ref[0:1, :]
    gain = ng_ref[...] * (1.0 + mod_ref[1:2, :])

    def norm_mod(a):
        ms = jnp.mean(a * a, axis=-1, keepdims=True)
        return a * lax.rsqrt(ms + NORM_EPS) * gain + shift

    lhs_sc[0:tm, :] = norm_mod(x_ref[...]).astype(BF16)
    lhs_sc[tm:tm + HALO, :] = norm_mod(xp_ref[...]).astype(BF16)
    lhs_sc[tm + HALO:tm + 2 * HALO, :] = norm_mod(xn_ref[...]).astype(BF16)

    h = lhs_sc[0:tm, :]
    zg = _dot(h, wm_ref[:, 5120:5120 + GATE_PAD]) + bg_ref[...]
    lane = lax.broadcasted_iota(jnp.int32, zg.shape, 1)
    is_forget = jnp.logical_and(lane < N_GATES, (lane % 8) >= 4)
    gates = jnp.where(is_forget, _log_sigmoid(zg), zg)
    gt = gates.T[0:N_GATES, :]
    gt_ref[...] = gt
    for c0 in range(0, tm, MLSTM_CHUNK):
        parts = []
        for reverse in (False, True):
            parts += list(_chunk_gate_rows(gt[:, c0:c0 + MLSTM_CHUNK], reverse))
        gx_ref[:, c0:c0 + MLSTM_CHUNK] = jnp.concatenate(parts, axis=0)

    u = _dot(h, wm_ref[:, 3072:4096])
    if s5_layout:
        for g8 in range(W_BRANCH // 128):
            rows = [u[i * GRID_W:(i + 1) * GRID_W, g8 * 128:(g8 + 1) * 128] for i in range(8)]
            for gp, blk in enumerate(_block_transpose8(rows)):
                u_ref[g8 * 8 + gp] = blk.astype(BF16)
    else:
        for g8 in range(W_BRANCH // 128):
            uc_sc[g8] = u[:, g8 * 128:(g8 + 1) * 128]
        for g8 in range(W_BRANCH // 128):
            for lh in range(2):
                rows = [uc_sc[g8, pl.ds(lh * 8 + i, tm // S5_SUB, stride=S5_SUB), :] for i in range(8)]
                for gp, blk in enumerate(_block_transpose8(rows)):
                    u_ref[g8 * 8 + gp, :, lh * 128:(lh + 1) * 128] = blk.astype(BF16)

    valid_prev = (i > 0).astype(F32)
    valid_next = (i < nt - 1).astype(F32)

    def conv_stage(half):
        cols = slice(half * W_BRANCH, (half + 1) * W_BRANCH)
        z = _dot(lhs_sc[...], wqk_ref[:, cols])
        zc = z[0:tm]
        row8 = lax.broadcasted_iota(jnp.int32, (8, W_BRANCH), 0)
        before = pltpu.roll(zc, 1, axis=0)
        first = jnp.where(row8 == 0, z[tm + HALO - 1:tm + HALO] * valid_prev, before[0:8])
        before = jnp.concatenate([first, before[8:]], axis=0)
        after = pltpu.roll(zc, tm - 1, axis=0)
        final = jnp.where(row8 == 7, z[tm + HALO:tm + HALO + 1] * valid_next, after[tm - 8:tm])
        after = jnp.concatenate([after[:tm - 8], final], axis=0)
        cw = conv_ref[:, cols]
        conv = cw[0:1] * before + cw[1:2] * zc + cw[2:3] * after
        if half == 0:
            q_ref[...] = _silu(conv.astype(BF16))
        else:
            kt_ref[...] = (_silu(conv.astype(BF16)) * HEAD_DIM ** -0.5).T

    v_ref[...] = _dot(h, wm_ref[:, 0:1024]).astype(BF16)
    if not state_only:
        conv_stage(0)
        o_ref[...] = _dot(h, wm_ref[:, 1024:2048]).astype(BF16)
    conv_stage(1)
    if not state_only:
        zm_ref[...] = _dot(h, wm_ref[:, 2048:3072]).astype(BF16)
        zs_ref[...] = _dot(h, wm_ref[:, 4096:5120]).astype(BF16)


def _inproj(x, mod, norm_g, w_qk, w_main, conv_qk, b_gate_pad, tm, s5_layout, state_only=False):
    bsz, t_len, _ = x.shape
    nt = t_len // tm
    nhb = t_len // HALO
    tok = lambda w, dt: jax.ShapeDtypeStruct((bsz, t_len, w), dt)
    tile = lambda w: pl.BlockSpec((None, tm, w), lambda b, i: (b, i, 0))
    tile_t = lambda w: pl.BlockSpec((None, w, tm), lambda b, i: (b, 0, i))
    const = lambda shape: pl.BlockSpec(shape, lambda b, i: (0,) * len(shape),
                                       pipeline_mode=pl.Buffered(1))
    if s5_layout:
        assert tm == 8 * GRID_W and t_len % (2 * tm) == 0
        u_spec = pl.BlockSpec((S5_GROUPS, GRID_W, 128), lambda b, i: (0, (i // 2) * bsz + b, i % 2))
        u_shape = jax.ShapeDtypeStruct((S5_GROUPS, (t_len // (2 * tm)) * bsz * GRID_W, 256), BF16)
    else:
        assert state_only and nt == 1 and tm % (16 * S5_SUB) == 0
        u_spec = pl.BlockSpec((S5_GROUPS, tm // S5_SUB, S5_SUB * S5_GC), lambda b, i: (0, b, 0))
        u_shape = jax.ShapeDtypeStruct((S5_GROUPS, bsz * (tm // S5_SUB), S5_SUB * S5_GC), BF16)
    kt_shape = jax.ShapeDtypeStruct((bsz, W_BRANCH, t_len), BF16)
    gt_shape = jax.ShapeDtypeStruct((bsz, N_GATES, t_len), F32)
    if state_only:
        out_specs = [tile_t(W_BRANCH), tile(W_BRANCH), u_spec, tile_t(N_GATES), tile_t(N_GATES)]
        out_shape = [kt_shape, tok(W_BRANCH, BF16), u_shape, gt_shape, gt_shape]
    else:
        out_specs = [tile(W_BRANCH), tile_t(W_BRANCH)] + [tile(W_BRANCH)] * 3 + [
            u_spec, tile(W_BRANCH), tile_t(N_GATES), tile_t(N_GATES)]
        out_shape = [tok(W_BRANCH, BF16), kt_shape] + [tok(W_BRANCH, BF16)] * 3 + [
            u_shape, tok(W_BRANCH, BF16), gt_shape, gt_shape]
    return pl.pallas_call(
        functools.partial(_inproj_kernel, tm=tm, s5_layout=s5_layout, state_only=state_only),
        grid=(bsz, nt),
        in_specs=[
            tile(D_MODEL),
            pl.BlockSpec((None, HALO, D_MODEL),
                         lambda b, i: (b, jnp.maximum(i * (tm // HALO) - 1, 0), 0)),
            pl.BlockSpec((None, HALO, D_MODEL),
                         lambda b, i: (b, jnp.minimum((i + 1) * (tm // HALO), nhb - 1), 0)),
            pl.BlockSpec((None, 3, D_MODEL), lambda b, i: (b, 0, 0)),
            const((1, D_MODEL)),
            const((D_MODEL, 2 * W_BRANCH)),
            const((D_MODEL, 5 * W_BRANCH + GATE_PAD)),
            const((CONV_W, 2 * W_BRANCH)),
            const((1, GATE_PAD)),
        ],
        out_specs=out_specs,
        out_shape=out_shape,
        scratch_shapes=[pltpu.VMEM((tm + 2 * HALO, D_MODEL), BF16)] + (
            [] if s5_layout else [pltpu.VMEM((W_BRANCH // 128, tm, 128), F32)]),
        compiler_params=pltpu.CompilerParams(dimension_semantics=("parallel", "arbitrary"),
                                             vmem_limit_bytes=VMEM_LIMIT),
        name="inproj",
    )(x, x, x, mod, norm_g.reshape(1, D_MODEL), w_qk, w_main, conv_qk, b_gate_pad)


def _mlstm_gates(gt_ref, gx_ref, m_sc, *, reverse, chunk, with_output):
    li0 = 8 if reverse else 0
    last = 0 if reverse else chunk - 1
    li_r = gt_ref[li0:li0 + HEADS, :]
    b_r = gx_ref[li0:li0 + HEADS, :]
    b_last = b_r[:, last:last + 1]
    m_old = m_sc[:, 0:1]
    g_r = b_last - b_r + li_r
    m_new = jnp.maximum(b_last + m_old, jnp.max(g_r, axis=1, keepdims=True))
    out = dict(m_old=m_old, m_new=m_new, decay=jnp.exp(b_last + m_old - m_new),
               k_scale=jnp.exp(g_r - m_new).astype(BF16))
    if not with_output:
        return out
    row = lax.broadcasted_iota(jnp.int32, (chunk, chunk), 0)
    col = lax.broadcasted_iota(jnp.int32, (chunk, chunk), 1)
    mask_ts = (col >= row) if reverse else (col <= row)
    mm_r = jnp.maximum(gx_ref[li0 + HEADS:li0 + 2 * HEADS, :], m_old).astype(BF16)
    b_hi, b_lo = _split_bf16(b_r)
    rows = jnp.concatenate([mm_r, b_hi, b_lo, jnp.zeros_like(b_hi)], axis=0)
    sel_r = lax.broadcasted_iota(jnp.int32, (4 * HEADS, 2 * HEADS * 128), 0)
    sel_c = lax.broadcasted_iota(jnp.int32, (4 * HEADS, 2 * HEADS * 128), 1) // 128
    head_r = sel_r % HEADS
    pick = jnp.logical_or(jnp.logical_and(sel_r < HEADS, sel_c % HEADS == head_r),
                          jnp.logical_and(jnp.logical_and(sel_r >= HEADS, sel_r < 3 * HEADS),
                                          sel_c == HEADS + head_r))
    sel = jnp.where(pick, 1.0, 0.0).astype(BF16)
    cols = lax.dot_general(rows, sel, (((0,), (0,)), ((), ())), preferred_element_type=F32)
    out.update(a_r=li_r - b_r, cols=cols, mask_ts=mask_ts)
    return out


def _mlstm_heads(q_ref, kt_ref, v_ref, h_ref, c_sc, n_sc, m_sc, gates, *, chunk):
    with_output = h_ref is not None
    m_old, m_new, decay, k_scale = gates["m_old"], gates["m_new"], gates["decay"], gates["k_scale"]
    if with_output:
        a_r, cols, mask_ts = gates["a_r"], gates["cols"], gates["mask_ts"]
    ones_rows = jnp.ones((8, chunk), BF16)
    wide = lambda a, n: jnp.concatenate([a] * (n // 128), axis=1)
    for hd in range(HEADS):
        sl = slice(hd * HEAD_DIM, (hd + 1) * HEAD_DIM)
        kt = kt_ref[sl, :]
        v = v_ref[:, sl]
        c_old = c_sc[hd]
        n_old = n_sc[hd]
        kw_t = kt * k_scale[hd:hd + 1, :]
        if with_output:
            q = q_ref[:, sl]
            mm_c = cols[:, hd * 128:(hd + 1) * 128]
            bm_c = cols[:, (HEADS + hd) * 128:(HEADS + hd + 1) * 128]
            decay_mat = jnp.exp(jnp.where(mask_ts, a_r[hd:hd + 1, :] - wide(mm_c, chunk), NEG_BIG))
            s_f = _dot(q, kt) * decay_mat
            w_inter = jnp.exp(m_old[hd:hd + 1, :] - mm_c)
            den = (w_inter * jnp.sum(q.astype(F32) * n_old[0:1, :], axis=1, keepdims=True)
                   + jnp.sum(s_f, axis=1, keepdims=True))
            q_w = q * wide(w_inter, HEAD_DIM).astype(BF16)
            num = _dot(jnp.concatenate([q_w, s_f.astype(BF16)], axis=1),
                       jnp.concatenate([c_old.astype(BF16), v], axis=0))
            inv = 1.0 / jnp.maximum(jnp.abs(den), jnp.exp(-bm_c))
            h_ref[:, sl] = (num * wide(inv, HEAD_DIM)).astype(BF16)
        c_sc[hd] = decay[hd:hd + 1, :] * c_old + _dot(kw_t, v)
        n_sc[hd] = decay[hd:hd + 1, :] * n_old + lax.dot_general(
            ones_rows, kw_t, (((1,), (1,)), ((), ())), preferred_element_type=F32)
        m_sc[hd:hd + 1, :] = jnp.broadcast_to(m_new[hd:hd + 1, :], (1, 128))


def _mlstm_kernel(*refs, with_output, chunk, subs):
    n_in = 5 if with_output else 4
    ins = [refs[0:n_in], refs[n_in:2 * n_in]]
    rest = refs[2 * n_in:]
    if with_output:
        c0_ref, n0_ref, m0_ref, hf_ref, hb_ref, c_sc, n_sc, m_sc = rest
        h_refs = (hf_ref, hb_ref)
    else:
        co_ref, no_ref, mo_ref, c_sc, n_sc, m_sc = rest
        h_refs = (None, None)
        ins = [(None,) + tuple(r) for r in ins]
    i = pl.program_id(1)
    nc = pl.num_programs(1)

    @pl.when(i == 0)
    def _():
        if with_output:
            c_sc[...] = c0_ref[...]
            n_sc[...] = n0_ref[...]
            m_sc[...] = m0_ref[...]
        else:
            c_sc[...] = jnp.zeros_like(c_sc)
            n_sc[...] = jnp.zeros_like(n_sc)
            m_sc[...] = jnp.zeros_like(m_sc)

    for sub in range(subs):
        pos = (sub, subs - 1 - sub)
        rows = [pl.ds(pos[d] * chunk, chunk) for d in range(2)]
        view = lambda ref, d: None if ref is None else ref.at[rows[d], :]
        view_t = lambda ref, d: ref.at[:, rows[d]]
        gates = [_mlstm_gates(view_t(ins[d][3], d), view_t(ins[d][4], d), m_sc.at[d], reverse=bool(d),
                              chunk=chunk, with_output=with_output) for d in range(2)]
        for d in range(2):
            _mlstm_heads(view(ins[d][0], d), view_t(ins[d][1], d), view(ins[d][2], d), view(h_refs[d], d),
                         c_sc.at[d], n_sc.at[d], m_sc.at[d], gates[d], chunk=chunk)

    if not with_output:
        @pl.when(i == nc - 1)
        def _():
            co_ref[...] = c_sc[...]
            no_ref[...] = n_sc[...]
            mo_ref[...] = m_sc[...]


def _mlstm(q, kt, v, gt, gx, state):
    with_output = q is not None
    bsz, t_len, _ = v.shape
    chunk = MLSTM_CHUNK
    subs = next(n for n in MLSTM_CHUNKS_PER_STEP if t_len % (n * chunk) == 0)
    blk = subs * chunk
    nc = t_len // blk
    cidx = (lambda i: i, lambda i: nc - 1 - i)
    tile = lambda w, d: pl.BlockSpec((None, blk, w), lambda b, i: (b, cidx[d](i), 0))
    tile_t = lambda w, d: pl.BlockSpec((None, w, blk), lambda b, i: (b, 0, cidx[d](i)))
    st_dims = [(2, HEADS, HEAD_DIM, HEAD_DIM), (2, HEADS, 8, HEAD_DIM), (2, HEADS, 128)]
    st_specs = [pl.BlockSpec((None,) + s, lambda b, i, n=len(s): (b,) + (0,) * n) for s in st_dims]
    st_shapes = [jax.ShapeDtypeStruct((bsz,) + s, F32) for s in st_dims]
    in_specs, args = [], []
    for d in range(2):
        in_specs += ([tile(W_BRANCH, d)] if with_output else []) + [
            tile_t(W_BRANCH, d), tile(W_BRANCH, d), tile_t(N_GATES, d), tile_t(N_GATES, d)]
        args += ([q] if with_output else []) + [kt, v, gt, gx]
    if with_output:
        in_specs, args = in_specs + st_specs, args + list(state)
        out_specs = [tile(W_BRANCH, 0), tile(W_BRANCH, 1)]
        out_shape = [jax.ShapeDtypeStruct((bsz, t_len, W_BRANCH), BF16)] * 2
    else:
        out_specs, out_shape = st_specs, st_shapes
    return pl.pallas_call(
        functools.partial(_mlstm_kernel, with_output=with_output, chunk=chunk, subs=subs),
        grid=(bsz, nc),
        in_specs=in_specs,
        out_specs=out_specs,
        out_shape=out_shape,
        scratch_shapes=[pltpu.VMEM(s, F32) for s in st_dims],
        compiler_params=pltpu.CompilerParams(dimension_semantics=("parallel", "arbitrary"),
                                             vmem_limit_bytes=VMEM_LIMIT),
        name="mlstm_out" if with_output else "mlstm_state",
    )(*args)


def _s5_kernel(vc_ref, vx_ref, g_ref, m_ref, p_ref, a_ref, y_ref, gu_sc, s_sc, *,
               nk_ctx, n_rc, bsz, rblk):
    rows_ctx = nk_ctx * bsz
    rows_x = n_rc * bsz * GRID_W
    ctx_pitch = nk_ctx + 8
    x_base = bsz * ctx_pitch

    def increments(v_ref, r0, r1):
        return _dot(v_ref[0, r0:r1, :], g_ref[0]) + _dot(v_ref[1, r0:r1, :], g_ref[1])

    inc = increments(vc_ref, 0, rows_ctx)
    for b in range(bsz):
        for comp in range(4):
            gu_sc[comp, b * ctx_pitch:b * ctx_pitch + nk_ctx, :] = (
                inc[b * nk_ctx:(b + 1) * nk_ctx, comp * 128:(comp + 1) * 128])
    for r0 in range(0, rows_x, rblk):
        inc = increments(vx_ref, r0, r0 + rblk)
        for run in range(rblk // GRID_W):
            dst = x_base + (r0 // GRID_W + run) * S5_ROW_PITCH
            for comp in range(4):
                gu_sc[comp, dst:dst + GRID_W, :] = (
                    inc[run * GRID_W:(run + 1) * GRID_W, comp * 128:(comp + 1) * 128])

    a = a_ref[...]
    a_pow = [jnp.broadcast_to(a[:, comp * 128:(comp + 1) * 128], (bsz, 128)) for comp in range(4)]
    zero = jnp.zeros((bsz, 128), F32)

    def cmul(x_r, x_i, y_r, y_i):
        return x_r * y_r - x_i * y_i, x_r * y_i + x_i * y_r

    a_sq = [cmul(a_pow[2 * d], a_pow[2 * d + 1], a_pow[2 * d], a_pow[2 * d + 1]) for d in range(2)]

    def step(rows, carry, direction):
        s_r, s_i = carry
        inc_r = gu_sc[2 * direction, rows, :]
        inc_i = gu_sc[2 * direction + 1, rows, :]
        gu_sc[2 * direction, rows, :] = s_r
        gu_sc[2 * direction + 1, rows, :] = s_i
        p_r, p_i = cmul(a_pow[2 * direction], a_pow[2 * direction + 1], s_r, s_i)
        return p_r + inc_r, p_i + inc_i

    def step2(rows0, rows1, carry, direction):
        s_r, s_i = carry
        a_r, a_i = a_pow[2 * direction], a_pow[2 * direction + 1]
        inc0_r, inc0_i = gu_sc[2 * direction, rows0, :], gu_sc[2 * direction + 1, rows0, :]
        inc1_r, inc1_i = gu_sc[2 * direction, rows1, :], gu_sc[2 * direction + 1, rows1, :]
        gu_sc[2 * direction, rows0, :] = s_r
        gu_sc[2 * direction + 1, rows0, :] = s_i
        m_r, m_i = cmul(a_r, a_i, s_r, s_i)
        gu_sc[2 * direction, rows1, :] = m_r + inc0_r
        gu_sc[2 * direction + 1, rows1, :] = m_i + inc0_i
        c_r, c_i = cmul(a_r, a_i, inc0_r, inc0_i)
        q_r, q_i = cmul(a_sq[direction][0], a_sq[direction][1], s_r, s_i)
        return q_r + (c_r + inc1_r), q_i + (c_i + inc1_i)

    def ctx_rows(k):
        return pl.ds(k, bsz, stride=ctx_pitch)

    def x_rows(w, rc):
        return pl.ds(x_base + rc * (bsz * S5_ROW_PITCH) + w, bsz, stride=S5_ROW_PITCH)

    def ctx_body(k, carry):
        return step(ctx_rows(k), carry[0], 0), step(ctx_rows(nk_ctx - 1 - k), carry[1], 1)

    def x_body(w, carry):
        c_f, c_b = carry
        w_b = GRID_W - 1 - w
        for rc in range(0, n_rc - 1, 2):
            c_f = step2(x_rows(w, rc), x_rows(w, rc + 1), c_f, 0)
            c_b = step2(x_rows(w_b, n_rc - 1 - rc), x_rows(w_b, n_rc - 2 - rc), c_b, 1)
        if n_rc % 2:
            c_f = step(x_rows(w, n_rc - 1), c_f, 0)
            c_b = step(x_rows(w_b, 0), c_b, 1)
        return c_f, c_b

    carry = lax.fori_loop(0, nk_ctx, ctx_body, ((zero, zero), (zero, zero)))
    lax.fori_loop(0, GRID_W, x_body, carry)

    for r0 in range(0, rows_x, rblk):
        r1 = r0 + rblk
        for run in range(rblk // GRID_W):
            src = x_base + (r0 // GRID_W + run) * S5_ROW_PITCH
            for comp in range(4):
                s_sc[r0 + run * GRID_W:r0 + (run + 1) * GRID_W, comp * 128:(comp + 1) * 128] = (
                    gu_sc[comp, src:src + GRID_W, :].astype(BF16))
        for gg in range(2):
            y_ref[gg, r0:r1, :] = (_dot(vx_ref[gg, r0:r1, :], m_ref[gg])
                                   + _dot(s_sc[r0:r1, :], p_ref[gg])).astype(BF16)


def _s5(v_ctx, v_x, g_all, m_all, p_all, a16, bsz):
    rows_ctx, rows_x = v_ctx.shape[1], v_x.shape[1]
    lanes = S5_SUB * S5_GC
    return pl.pallas_call(
        functools.partial(_s5_kernel, nk_ctx=rows_ctx // bsz, n_rc=rows_x // (bsz * GRID_W),
                          bsz=bsz, rblk=512),
        grid=(S5_GROUPS // 2,),
        in_specs=[pl.BlockSpec((2, rows_ctx, lanes), lambda j: (j, 0, 0)),
                  pl.BlockSpec((2, rows_x, lanes), lambda j: (j, 0, 0)),
                  pl.BlockSpec((2, lanes, 512), lambda j: (j, 0, 0)),
                  pl.BlockSpec((2, lanes, lanes), lambda j: (j, 0, 0)),
                  pl.BlockSpec((2, 512, lanes), lambda j: (j, 0, 0)),
                  pl.BlockSpec((None, 1, 512), lambda j: (j, 0, 0))],
        out_specs=pl.BlockSpec((2, rows_x, lanes), lambda j: (j, 0, 0)),
        out_shape=jax.ShapeDtypeStruct((S5_GROUPS, rows_x, lanes), BF16),
        scratch_shapes=[pltpu.VMEM((4, bsz * (rows_ctx // bsz + 8) + (rows_x // GRID_W) * S5_ROW_PITCH, 128), F32),
                        pltpu.VMEM((rows_x, 512), BF16)],
        compiler_params=pltpu.CompilerParams(dimension_semantics=("parallel",),
                                             vmem_limit_bytes=VMEM_LIMIT),
        name="s5",
    )(v_ctx, v_x, g_all, m_all, p_all, a16)


def _s5_prep_kernel(lr_ref, lc_ref, bt_ref, ct_ref, d_ref, m_ref, g_ref, p_ref, a_ref):
    hp = lax.Precision.HIGHEST
    n_s, lanes = S5_SUB, S5_SUB * S5_GC
    lane128 = lax.broadcasted_iota(jnp.int32, (n_s, 128), 1)
    blk_of_lane = lax.broadcasted_iota(jnp.int32, (128, lanes), 1) // S5_GC
    g_types, p_types, a16, k_rows = [], [], [], []
    for d in range(2):
        a_r, a_i, log_dt = lr_ref[d, 0:1, :], lr_ref[d, 1:2, :], lr_ref[d, 2:3, :]
        dt = jnp.exp(log_dt)
        lam_r, lam_i = a_r * dt, a_i * dt
        steps = lax.broadcasted_iota(jnp.int32, (24, 128), 0).astype(F32)
        mag = jnp.exp(lam_r * steps)
        pw_r, pw_i = mag * jnp.cos(lam_i * steps), mag * jnp.sin(lam_i * steps)
        nr, ni = pw_r[1:2] - 1.0, pw_i[1:2]
        den = a_r * a_r + a_i * a_i
        co_r, co_i = (nr * a_r + ni * a_i) / den, (ni * a_r - nr * a_i) / den
        b_r = jnp.concatenate([bt_ref[d, 0]] * n_s, axis=0)
        b_i = jnp.concatenate([bt_ref[d, 1]] * n_s, axis=0)
        bb_r, bb_i = co_r * b_r - co_i * b_i, co_r * b_i + co_i * b_r
        order = [n_s - 1 - i for i in range(n_s)] if d == 0 else list(range(n_s))
        pg_r = jnp.concatenate([jnp.broadcast_to(pw_r[n:n + 1], (S5_GC, 128)) for n in order], axis=0)
        pg_i = jnp.concatenate([jnp.broadcast_to(pw_i[n:n + 1], (S5_GC, 128)) for n in order], axis=0)
        g_types += [bb_r * pg_r - bb_i * pg_i, bb_r * pg_i + bb_i * pg_r]
        a16 += [pw_r[n_s:n_s + 1], pw_i[n_s:n_s + 1]]
        x0 = slice((n_s - 1) * S5_GC, n_s * S5_GC) if d == 0 else slice(0, S5_GC)
        x_r, x_i = bb_r[x0], bb_i[x0]
        lhs_r = jnp.concatenate([jnp.where(lane128 < S5_STATE, x_r, 0.0),
                                 jnp.where(lane128 < S5_STATE, 0.0, x_r)], axis=0)
        lhs_i = jnp.concatenate([jnp.where(lane128 < S5_STATE, x_i, 0.0),
                                 jnp.where(lane128 < S5_STATE, 0.0, x_i)], axis=0)
        dt_c = jnp.exp(lc_ref[d, 2])
        lam_rc, lam_ic = lc_ref[d, 0] * dt_c, lc_ref[d, 1] * dt_c
        mag_1 = jnp.exp(lam_rc)
        a1_r, a1_i = mag_1 * jnp.cos(lam_ic), mag_1 * jnp.sin(lam_ic)
        a1_r = jnp.concatenate([a1_r, a1_r], axis=1)
        a1_i = jnp.concatenate([a1_i, a1_i], axis=1)
        n_y = blk_of_lane if d == 0 else n_s - 1 - blk_of_lane
        ypw_r, ypw_i = jnp.ones_like(a1_r), jnp.zeros_like(a1_r)
        sq_r, sq_i = a1_r, a1_i
        for bit in (1, 2, 4, 8):
            on = (n_y & bit) != 0
            ypw_r, ypw_i = (jnp.where(on, ypw_r * sq_r - ypw_i * sq_i, ypw_r),
                            jnp.where(on, ypw_r * sq_i + ypw_i * sq_r, ypw_i))
            if bit < 8:
                sq_r, sq_i = sq_r * sq_r - sq_i * sq_i, 2.0 * sq_r * sq_i
        c_r, c_i = ct_ref[d, 0], ct_ref[d, 1]
        y_r, y_i = c_r * ypw_r - c_i * ypw_i, c_r * ypw_i + c_i * ypw_r
        k_rows.append(jnp.dot(lhs_r, y_r, preferred_element_type=F32, precision=hp)
                      - jnp.dot(lhs_i, y_i, preferred_element_type=F32, precision=hp))
        p_types += [y_r * a1_r - y_i * a1_i, -(y_r * a1_i + y_i * a1_r)]

    a_ref[...] = jnp.concatenate(a16, axis=1)
    lane_g = lax.broadcasted_iota(jnp.int32, (lanes, 128), 1)
    row_p = lax.broadcasted_iota(jnp.int32, (128, lanes), 0)
    lane_k = lax.broadcasted_iota(jnp.int32, (S5_GC, lanes), 1)
    row_m = lax.broadcasted_iota(jnp.int32, (lanes, lanes), 0)
    lane_m = lax.broadcasted_iota(jnp.int32, (lanes, lanes), 1)
    for h in range(2):
        mine_l = (lane_g >= S5_STATE) == bool(h)
        g_ref[h] = jnp.concatenate([jnp.where(mine_l, t, 0.0) for t in g_types], axis=1).astype(BF16)
        mine_r = (row_p >= S5_STATE) == bool(h)
        p_ref[h] = jnp.concatenate([jnp.where(mine_r, t, 0.0) for t in p_types], axis=0).astype(BF16)
        k_f = k_rows[0][h * S5_GC:(h + 1) * S5_GC]
        k_b = k_rows[1][h * S5_GC:(h + 1) * S5_GC]
        blocks = []
        for i in range(n_s):
            up, down = S5_GC * i, S5_GC * (n_s - 1 - i)
            f = k_f if up == 0 else jnp.where(lane_k >= up, pltpu.roll(k_f, up, axis=1), 0.0)
            b = k_b if down == 0 else jnp.where(lane_k < lanes - down,
                                                pltpu.roll(k_b, lanes - down, axis=1), 0.0)
            blocks.append(f + b)
        m = jnp.concatenate(blocks, axis=0) + jnp.where(row_m == lane_m, d_ref[h], 0.0)
        m_ref[h] = m.astype(BF16)


def _s5_prep(a_re, a_im, log_step, b_re, b_im, c_re, c_im, d_skip):
    n_g, n_p, n_c, n_s = S5_GROUPS, S5_STATE, S5_GC, S5_SUB
    lanes = n_s * n_c
    pair = lambda a: jnp.transpose(a.astype(F32).reshape(2, n_g // 2, 2 * n_p), (1, 0, 2))
    lam_row = jnp.stack([pair(a_re), pair(a_im),
                         pair(jnp.broadcast_to(log_step[..., None], a_re.shape))], axis=2)
    lam_col = jnp.broadcast_to(lam_row[..., None], lam_row.shape + (2 * n_p,))
    bt = lambda b: jnp.transpose(b.astype(F32).reshape(2, n_g // 2, 2, n_p, n_c),
                                 (1, 0, 4, 2, 3)).reshape(n_g // 2, 2, n_c, 2 * n_p)
    b_t = jnp.stack([bt(b_re), bt(b_im)], axis=2)
    ct = lambda c: jnp.tile(jnp.transpose(c.astype(F32).reshape(2, n_g // 2, 2, n_c, n_p),
                                          (1, 0, 2, 4, 3)).reshape(n_g // 2, 2, 2 * n_p, n_c),
                            (1, 1, 1, n_s))
    c_t = jnp.stack([ct(c_re), ct(c_im)], axis=2)
    d_row = jnp.tile(d_skip.astype(F32).reshape(n_g // 2, 2, 1, n_c), (1, 1, 1, n_s))
    blk = lambda *s: pl.BlockSpec((None,) + s, lambda j: (j,) + (0,) * len(s))
    grp = lambda *s: pl.BlockSpec((2,) + s, lambda j: (j,) + (0,) * len(s))
    return pl.pallas_call(
        _s5_prep_kernel,
        grid=(n_g // 2,),
        in_specs=[blk(2, 3, 2 * n_p), blk(2, 3, 2 * n_p, 2 * n_p), blk(2, 2, n_c, 2 * n_p),
                  blk(2, 2, 2 * n_p, lanes), blk(2, 1, lanes)],
        out_specs=[grp(lanes, lanes), grp(lanes, 8 * n_p), grp(8 * n_p, lanes), blk(1, 8 * n_p)],
        out_shape=[jax.ShapeDtypeStruct((n_g, lanes, lanes), BF16),
                   jax.ShapeDtypeStruct((n_g, lanes, 8 * n_p), BF16),
                   jax.ShapeDtypeStruct((n_g, 8 * n_p, lanes), BF16),
                   jax.ShapeDtypeStruct((n_g // 2, 1, 8 * n_p), F32)],
        compiler_params=pltpu.CompilerParams(dimension_semantics=("parallel",),
                                             vmem_limit_bytes=VMEM_LIMIT),
        name="s5_prep",
    )(lam_row, lam_col, b_t, c_t, d_row)


def _merge_kernel(hf_ref, hb_ref, o_ref, zm_ref, y_ref, zs_ref, x_ref, mod_ref, mhg_ref,
                  gluw_ref, glub_ref, wout_ref, fg_ref, out_ref, y_sc):
    for lh in range(2):
        for g8 in range(W_BRANCH // 128):
            rows = [y_ref[g8 * 8 + gp, :, lh * 128:(lh + 1) * 128].astype(F32) for gp in range(8)]
            for i, blk in enumerate(_block_transpose8(rows)):
                r0 = (lh * 8 + i) * GRID_W
                y_sc[r0:r0 + GRID_W, g8 * 128:(g8 + 1) * 128] = blk.astype(BF16)

    mhg = mhg_ref[...]
    for r0 in range(0, x_ref.shape[0], MERGE_ROWS):
        rs = slice(r0, r0 + MERGE_ROWS)
        hm = ((hf_ref[rs, :] + hb_ref[rs, :]) * jax.nn.sigmoid(o_ref[rs, :])).astype(F32)
        parts = []
        for hd in range(HEADS):
            sl = slice(hd * HEAD_DIM, (hd + 1) * HEAD_DIM)
            seg = hm[:, sl]
            mu = jnp.mean(seg, axis=-1, keepdims=True)
            dev = seg - mu
            var = jnp.mean(dev * dev, axis=-1, keepdims=True)
            parts.append(dev * lax.rsqrt(var + NORM_EPS) * mhg[:, sl])
        m_out = jnp.concatenate(parts, axis=-1).astype(BF16) * _silu(zm_ref[rs, :])

        y = y_sc[rs, :]
        gl = 0.5 * y * (1.0 + jnp.tanh(0.7978845608028654 * (y + 0.044715 * (y * y * y))))
        gate = jax.nn.sigmoid((_dot(gl, gluw_ref[...]) + glub_ref[...]).astype(BF16))
        s_out = gl * gate * _silu(zs_ref[rs, :])

        mixed = _dot(m_out, wout_ref[0:W_BRANCH, :]) + _dot(s_out, wout_ref[W_BRANCH:2 * W_BRANCH, :])
        xo = x_ref[rs, :] + mod_ref[2:3, :] * mixed
        ms = jnp.mean(xo * xo, axis=-1, keepdims=True)
        out_ref[rs, :] = xo * lax.rsqrt(ms + NORM_EPS) * fg_ref[...]


def _merge(hf, hb, o, zm, y, zs, x, mod, mh_g, glu_w, glu_b, w_out, final_g):
    bsz, t_len, _ = x.shape
    tm = S5_SUB * GRID_W
    tile = pl.BlockSpec((None, tm, D_MODEL), lambda b, i: (b, i, 0))
    y_spec = pl.BlockSpec((S5_GROUPS, GRID_W, S5_SUB * S5_GC), lambda b, i: (0, i * bsz + b, 0))
    const = lambda shape: pl.BlockSpec(shape, lambda b, i: (0,) * len(shape),
                                       pipeline_mode=pl.Buffered(1))
    return pl.pallas_call(
        _merge_kernel,
        grid=(bsz, t_len // tm),
        in_specs=[tile] * 4 + [y_spec, tile, tile,
                               pl.BlockSpec((None, 3, D_MODEL), lambda b, i: (b, 0, 0)),
                               const((1, W_BRANCH)), const((W_BRANCH, W_BRANCH)),
                               const((1, W_BRANCH)), const((2 * W_BRANCH, D_MODEL)),
                               const((1, D_MODEL))],
        out_specs=tile,
        out_shape=jax.ShapeDtypeStruct((bsz, t_len, D_MODEL), F32),
        scratch_shapes=[pltpu.VMEM((tm, W_BRANCH), BF16)],
        compiler_params=pltpu.CompilerParams(dimension_semantics=("parallel", "arbitrary"),
                                             vmem_limit_bytes=VMEM_LIMIT),
        name="merge",
    )(hf, hb, o, zm, y, zs, x, mod, mh_g.reshape(1, -1), glu_w, glu_b.reshape(1, -1), w_out,
      final_g.reshape(1, -1))


def _s5_rows_ctx(u):
    bsz, t_len, _ = u.shape
    a = u.reshape(bsz, t_len // S5_SUB, S5_SUB, S5_GROUPS, S5_GC)
    a = jnp.transpose(a, (3, 1, 0, 2, 4))
    return a.reshape(S5_GROUPS, (t_len // S5_SUB) * bsz, S5_SUB * S5_GC)


def kernel(x, c, ctx, c_ctx, norm_g, ada_w, ada_b, w_in, b_gate, conv_qk, mh_g, s5_a_re, s5_a_im,
           s5_log_step, s5_b_re, s5_b_im, s5_c_re, s5_c_im, s5_d, glu_w, glu_b, w_out, final_g):
    bsz = x.shape[0]
    layer = 0

    cc = jnp.zeros((16, D_MODEL), F32).at[:bsz].set(c).at[bsz].set(c_ctx)
    mod = _ada(cc, ada_w[layer], ada_b[layer]).reshape(16, 3, D_MODEL)
    mod_x = mod[:bsz]
    mod_c = jnp.broadcast_to(mod[bsz][None], (bsz, 3, D_MODEL))

    w = w_in[layer]
    wb = W_BRANCH
    w_qk = w[:, 0:2 * wb].astype(BF16)
    gate0 = 5 * wb
    w_main = jnp.concatenate([w[:, 2 * wb:5 * wb], w[:, gate0 + N_GATES:],
                              w[:, gate0:gate0 + GATE_PAD]], axis=1).astype(BF16)
    b_gate_pad = jnp.pad(b_gate[layer].reshape(1, N_GATES), ((0, 0), (0, GATE_PAD - N_GATES)))

    proj = functools.partial(_inproj, norm_g=norm_g[layer], w_qk=w_qk, w_main=w_main,
                             conv_qk=conv_qk[layer], b_gate_pad=b_gate_pad)
    kt_c, v_c, u_c, gt_c, gx_c = proj(ctx, mod_c, tm=MLSTM_CHUNK, s5_layout=False, state_only=True)
    q_x, kt_x, v_x, o_x, zm_x, u_x, zs_x, gt_x, gx_x = proj(x, mod_x, tm=8 * GRID_W, s5_layout=True)

    ctx_state = _mlstm(None, kt_c, v_c, gt_c, gx_c, None)
    h_f, h_b = _mlstm(q_x, kt_x, v_x, gt_x, gx_x, ctx_state)

    m_all, g_all, p_all, a16 = _s5_prep(
        s5_a_re[layer], s5_a_im[layer], s5_log_step[layer], s5_b_re[layer], s5_b_im[layer],
        s5_c_re[layer], s5_c_im[layer], s5_d[layer])
    y_x = _s5(u_c, u_x, g_all, m_all, p_all, a16, bsz)

    return _merge(h_f, h_b, o_x, zm_x, y_x, zs_x, x, mod_x, mh_g[layer], glu_w[layer].astype(BF16),
                  glu_b[layer], w_out[layer].astype(BF16), final_g)
```
